```python
import math
import jax, jax.numpy as jnp
from jax import lax
import numpy as np

D_MODEL = 1024
BATCH = 8
SEQ = 2048
DEPTH = 4

HEAD_DIM = 64
N_HEADS = D_MODEL // HEAD_DIM
MIX_WIDTH = N_HEADS * HEAD_DIM
NSA_HEADS = N_HEADS // 2
NSA_KV = 2
NSA_GROUP = NSA_HEADS // NSA_KV
FOX_HEADS = N_HEADS // 4
SB_HEADS = N_HEADS - NSA_HEADS - FOX_HEADS
ROPE_DIM = HEAD_DIM // 4
ROPE_THETA = 500000.0
CMP_LEN = 32
CMP_STRIDE = 16
CMP_HIDDEN = 2 * HEAD_DIM
SEL_LEN = 64
SEL_TOPN = 16
WINDOW = 512
FORCE_SCORE = 1.0e4
Q_BLOCK = 128
SEL_Q_BLOCK = 64
N_GROUPS = 4
EXPERTS_PER_GROUP = 8
N_EXPERTS = N_GROUPS * EXPERTS_PER_GROUP
EXPERT_TOPK = 2
D_EXPERT = D_MODEL // 2
MOE_CHUNK = 256
EPS = 1e-6
NEG = -1e30

KV_W = NSA_KV * HEAD_DIM
IN_SIZES = (NSA_HEADS * HEAD_DIM, KV_W, KV_W, KV_W, KV_W, KV_W, KV_W, NSA_HEADS * 3,
            FOX_HEADS * HEAD_DIM, FOX_HEADS * HEAD_DIM, FOX_HEADS * HEAD_DIM, FOX_HEADS,
            SB_HEADS * HEAD_DIM, SB_HEADS * HEAD_DIM, SB_HEADS * HEAD_DIM)
D_IN = sum(IN_SIZES)

kernel_name = 'hybrid_nsa_fox_stickbreak_hmoe_adaln'


def rmsnorm(x, g):
    xf = x.astype(jnp.float32)
    y = xf * lax.rsqrt(jnp.mean(xf * xf, axis=-1, keepdims=True) + EPS)
    return (y * g.astype(jnp.float32)).astype(x.dtype)


def rope_partial(x, pos):
    half = ROPE_DIM // 2
    inv = jnp.exp(jnp.arange(half, dtype=jnp.float32) * (-2.0 * math.log(ROPE_THETA) / ROPE_DIM))
    ang = pos.astype(jnp.float32)[:, None] * inv[None, :]
    shp = (pos.shape[0],) + (1,) * (x.ndim - 3) + (half,)
    cos = jnp.cos(ang).reshape(shp).astype(x.dtype)
    sin = jnp.sin(ang).reshape(shp).astype(x.dtype)
    x1 = x[..., :half]
    x2 = x[..., half:ROPE_DIM]
    return jnp.concatenate([x1 * cos - x2 * sin, x2 * cos + x1 * sin, x[..., ROPE_DIM:]], axis=-1)


def masked_softmax(s, m):
    return jax.nn.softmax(jnp.where(m, s, NEG), axis=-1)


def sweep_blocks(fn, n_blocks):
    out = lax.map(fn, jnp.arange(n_blocks))
    out = jnp.moveaxis(out, 0, 1)
    return out.reshape(out.shape[:1] + (-1,) + out.shape[3:])


def compress_blocks(k, blk_idx, pe, w1, w2):
    b, _, g, dh = k.shape
    n = blk_idx.shape[0]
    blk = k[:, blk_idx] + pe[:, None, :]
    flat = jnp.moveaxis(blk, 3, 2).reshape(b, n, g, CMP_LEN * dh)
    return jax.nn.silu(flat @ w1) @ w2


def nsa_mixer(q, kc, vc, ks, vs, kw, vw, gates, pos_k, w1_k, w2_k, pos_v, w1_v, w2_v):
    b, s_len, _ = q.shape
    scale = HEAD_DIM ** -0.5
    pos = jnp.arange(s_len)
    kv_shape = (b, s_len, NSA_KV, HEAD_DIM)
    q = rope_partial(q.reshape(b, s_len, NSA_KV, NSA_GROUP, HEAD_DIM), pos)
    kc, vc, vs, vw = [t.reshape(kv_shape) for t in (kc, vc, vs, vw)]
    ks = rope_partial(ks.reshape(kv_shape), pos)
    kw = rope_partial(kw.reshape(kv_shape), pos)

    n_cmp = (s_len - CMP_LEN) // CMP_STRIDE + 1
    cmp_start = jnp.arange(n_cmp) * CMP_STRIDE
    cmp_end = cmp_start + CMP_LEN - 1
    blk_idx = cmp_start[:, None] + jnp.arange(CMP_LEN)[None, :]
    ck = rope_partial(compress_blocks(kc, blk_idx, pos_k, w1_k, w2_k), cmp_end)
    cv = compress_blocks(vc, blk_idx, pos_v, w1_v, w2_v)
    s_cmp = jnp.einsum('bqghd,bngd->bqghn', q, ck).astype(jnp.float32) * scale
    valid = (cmp_end[None, :] <= pos[:, None])[None, :, None, None, :]
    p_cmp = masked_softmax(s_cmp, valid) * valid
    o_cmp = jnp.einsum('bqghn,bngd->bqghd', p_cmp.astype(cv.dtype), cv)

    n_sel = s_len // SEL_LEN
    sel_start = jnp.arange(n_sel) * SEL_LEN
    cover = ((cmp_start[:, None] < sel_start[None, :] + SEL_LEN) &
             (cmp_start[:, None] + CMP_LEN > sel_start[None, :])).astype(jnp.float32)
    imp = jnp.einsum('bqghn,nj->bqgj', p_cmp, cover)
    cur = pos // SEL_LEN
    jj = jnp.arange(n_sel)
    forced = (jj[None, :] == 0) | (jj[None, :] == cur[:, None]) | (jj[None, :] == cur[:, None] - 1)
    causal_blk = sel_start[None, :] <= pos[:, None]
    score = jnp.where(forced[None, :, None, :], FORCE_SCORE,
                      jnp.where(causal_blk[None, :, None, :], imp, -1.0))
    n_top = min(SEL_TOPN, n_sel)
    _, sel_idx = lax.top_k(score, n_top)
    ksb = ks.reshape(b, n_sel, SEL_LEN, NSA_KV, HEAD_DIM).transpose(0, 3, 1, 2, 4)
    vsb = vs.reshape(b, n_sel, SEL_LEN, NSA_KV, HEAD_DIM).transpose(0, 3, 1, 2, 4)
    b_ix = jnp.arange(b)[:, None, None, None]
    g_ix = jnp.arange(NSA_KV)[None, None, :, None]

    def sel_block(i):
        t0 = i * SEL_Q_BLOCK
        qi = lax.dynamic_slice_in_dim(q, t0, SEL_Q_BLOCK, axis=1)
        ii = lax.dynamic_slice_in_dim(sel_idx, t0, SEL_Q_BLOCK, axis=1)
        kg = ksb[b_ix, g_ix, ii]
        vg = vsb[b_ix, g_ix, ii]
        s = jnp.einsum('bqghd,bqgnld->bqghnl', qi, kg).astype(jnp.float32) * scale
        tq = t0 + jnp.arange(SEL_Q_BLOCK)
        kpos = ii[..., None] * SEL_LEN + jnp.arange(SEL_LEN)
        m = (kpos <= tq[None, :, None, None, None])[:, :, :, None]
        p = masked_softmax(s.reshape(s.shape[:4] + (-1,)), m.reshape(m.shape[:4] + (-1,))).reshape(s.shape)
        return jnp.einsum('bqghnl,bqgnld->bqghd', p.astype(vg.dtype), vg)

    o_sel = sweep_blocks(sel_block, s_len // SEL_Q_BLOCK)

    span = WINDOW + Q_BLOCK
    kwp = jnp.pad(kw, ((0, 0), (WINDOW, 0), (0, 0), (0, 0)))
    vwp = jnp.pad(vw, ((0, 0), (WINDOW, 0), (0, 0), (0, 0)))

    def win_block(i):
        t0 = i * Q_BLOCK
        qi = lax.dynamic_slice_in_dim(q, t0, Q_BLOCK, axis=1)
        ki = lax.dynamic_slice_in_dim(kwp, t0, span, axis=1)
        vi = lax.dynamic_slice_in_dim(vwp, t0, span, axis=1)
        s = jnp.einsum('bqghd,bkgd->bqghk', qi, ki).astype(jnp.float32) * scale
        tq = t0 + jnp.arange(Q_BLOCK)
        tk = t0 - WINDOW + jnp.arange(span)
        m = (tk[None, :] <= tq[:, None]) & (tk[None, :] > tq[:, None] - WINDOW) & (tk[None, :] >= 0)
        p = masked_softmax(s, m[None, :, None, None, :])
        return jnp.einsum('bqghk,bkgd->bqghd', p.astype(vi.dtype), vi)

    o_win = sweep_blocks(win_block, s_len // Q_BLOCK)

    g = jax.nn.sigmoid(gates.astype(jnp.float32)).reshape(b, s_len, NSA_KV, NSA_GROUP, 3).astype(q.dtype)
    o = g[..., 0:1] * o_cmp + g[..., 1:2] * o_sel + g[..., 2:3] * o_win
    return o.reshape(b, s_len, NSA_HEADS * HEAD_DIM)


def fox_mixer(q, k, v, f_logit, b_forget):
    b, s_len, _ = q.shape
    scale = HEAD_DIM ** -0.5
    hs = (b, s_len, FOX_HEADS, HEAD_DIM)
    q, k, v = q.reshape(hs), k.reshape(hs), v.reshape(hs)
    cumf = jnp.cumsum(jax.nn.log_sigmoid(f_logit.astype(jnp.float32) + b_forget.astype(jnp.float32)),
                      axis=1).transpose(0, 2, 1)
    tk = jnp.arange(s_len)

    def blk(i):
        t0 = i * Q_BLOCK
        qi = lax.dynamic_slice_in_dim(q, t0, Q_BLOCK, axis=1)
        ci = lax.dynamic_slice_in_dim(cumf, t0, Q_BLOCK, axis=2)
        s = (jnp.einsum('bqhd,bkhd->bhqk', qi, k).astype(jnp.float32) * scale
             + ci[..., :, None] - cumf[:, :, None, :])
        tq = t0 + jnp.arange(Q_BLOCK)
        p = masked_softmax(s, tk[None, :] <= tq[:, None])
        return jnp.einsum('bhqk,bkhd->bqhd', p.astype(v.dtype), v)

    o = sweep_blocks(blk, s_len // Q_BLOCK)
    return o.reshape(b, s_len, FOX_HEADS * HEAD_DIM)


def stick_breaking_mixer(q, k, v):
    b, s_len, _ = q.shape
    scale = HEAD_DIM ** -0.5
    hs = (b, s_len, SB_HEADS, HEAD_DIM)
    q, k, v = q.reshape(hs), k.reshape(hs), v.reshape(hs)
    tk = jnp.arange(s_len)

    def blk(i):
        t0 = i * Q_BLOCK
        qi = lax.dynamic_slice_in_dim(q, t0, Q_BLOCK, axis=1)
        z = jnp.einsum('bqhd,bkhd->bhqk', qi, k).astype(jnp.float32) * scale
        tq = t0 + jnp.arange(Q_BLOCK)
        m = tk[None, :] < tq[:, None]
        l = jnp.where(m, jax.nn.log_sigmoid(-z), 0.0)
        rest = lax.cumsum(l, axis=3, reverse=True) - l
        a = jnp.where(m, jnp.exp(jax.nn.log_sigmoid(z) + rest), 0.0)
        return jnp.einsum('bhqk,bkhd->bqhd', a.astype(v.dtype), v)

    o = sweep_blocks(blk, s_len // Q_BLOCK)
    return o.reshape(b, s_len, SB_HEADS * HEAD_DIM)


def hybrid_mixer(h, w_in, b_forget, cmp_pos_k, cmp_w1_k, cmp_w2_k, cmp_pos_v, cmp_w1_v, cmp_w2_v,
                 out_norm_g, w_out):
    b, s_len, _ = h.shape
    u = h @ w_in
    pts = []
    acc = 0
    for sz in IN_SIZES[:-1]:
        acc += sz
        pts.append(acc)
    (qa, kca, vca, ksa, vsa, kwa, vwa, ga, qb, kb, vb, fb, qc, kcc, vcc) = jnp.split(u, pts, axis=-1)
    o_a = nsa_mixer(qa, kca, vca, ksa, vsa, kwa, vwa, ga,
                    cmp_pos_k, cmp_w1_k, cmp_w2_k, cmp_pos_v, cmp_w1_v, cmp_w2_v)
    o_b = fox_mixer(qb, kb, vb, fb, b_forget)
    o_c = stick_breaking_mixer(qc, kcc, vcc)
    o = jnp.concatenate([o_a, o_b, o_c], axis=-1).reshape(b, s_len, N_HEADS, HEAD_DIM)
    o = rmsnorm(o, out_norm_g.reshape(N_HEADS, HEAD_DIM))
    return o.reshape(b, s_len, MIX_WIDTH) @ w_out


def hier_moe(h, wg, bg, we, be, w1, w3, w2):
    b, s_len, d = h.shape
    t = b * s_len
    xt = h.reshape(t, d)
    gp = jax.nn.softmax((xt @ wg).astype(jnp.float32) + bg.astype(jnp.float32), axis=-1)
    pg, gi = lax.top_k(gp, 1)
    el = ((xt @ we).astype(jnp.float32) + be.astype(jnp.float32)).reshape(t, N_GROUPS, EXPERTS_PER_GROUP)
    el_sel = el[jnp.arange(t), gi[:, 0]]
    pe, ei = lax.top_k(jax.nn.softmax(el_sel, axis=-1), EXPERT_TOPK)
    pe = pe / jnp.sum(pe, axis=-1, keepdims=True)
    wts = (pg * pe).reshape(-1)
    eid = (gi * EXPERTS_PER_GROUP + ei).reshape(-1)
    tok = jnp.repeat(jnp.arange(t, dtype=jnp.int32), EXPERT_TOPK)
    n_assign = t * EXPERT_TOPK
    order = jnp.argsort(eid)
    e_sorted = eid[order]
    counts = jnp.zeros((N_EXPERTS,), jnp.int32).at[eid].add(1)
    starts = jnp.cumsum(counts) - counts
    padded = ((counts + MOE_CHUNK - 1) // MOE_CHUNK) * MOE_CHUNK
    pends = jnp.cumsum(padded)
    pstarts = pends - padded
    dest = pstarts[e_sorted] + (jnp.arange(n_assign) - starts[e_sorted])
    n_chunks = -(-(n_assign + N_EXPERTS * (MOE_CHUNK - 1)) // MOE_CHUNK)
    p_rows = n_chunks * MOE_CHUNK
    buf_tok = jnp.full((p_rows,), t, jnp.int32).at[dest].set(tok[order])
    buf_w = jnp.zeros((p_rows,), jnp.float32).at[dest].set(wts[order])
    chunk_e = jnp.clip(jnp.searchsorted(pends, jnp.arange(n_chunks, dtype=jnp.int32) * MOE_CHUNK,
                                        side='right'), 0, N_EXPERTS - 1)
    xpad = jnp.concatenate([xt, jnp.zeros((1, d), xt.dtype)], axis=0)
    xs = xpad[buf_tok].reshape(n_chunks, MOE_CHUNK, d)

    def expert_chunk(args):
        xc, e = args
        return (jax.nn.silu(xc @ w1[e]) * (xc @ w3[e])) @ w2[e]

    ys = lax.map(expert_chunk, (xs, chunk_e)).reshape(p_rows, d)
    out = jnp.zeros((t + 1, d), ys.dtype).at[buf_tok].add(ys * buf_w[:, None].astype(ys.dtype))
    return out[:t].reshape(b, s_len, d)


def setup_inputs(seed: int = 0) -> dict:
    key = jax.random.key(seed)
    ks = jax.random.split(key, 24)
    L, D = DEPTH, D_MODEL

    def nrm(k, shape, s):
        return s * jax.random.normal(k, shape, jnp.float32)

    return {
        'x': nrm(ks[0], (BATCH, SEQ, D), 1.0),
        'c': nrm(ks[1], (BATCH, D), 1.0),
        'norm1_g': 1.0 + nrm(ks[2], (L, D), 0.02),
        'norm2_g': 1.0 + nrm(ks[3], (L, D), 0.02),
        'ada_w': nrm(ks[4], (L, D, 6 * D), 0.5 * D ** -0.5),
        'ada_b': nrm(ks[5], (L, 6 * D), 0.02),
        'w_in': nrm(ks[6], (L, D, D_IN), D ** -0.5),
        'b_forget': 2.0 + nrm(ks[7], (L, FOX_HEADS), 0.1),
        'cmp_pos_k': nrm(ks[8], (L, CMP_LEN, HEAD_DIM), 0.1),
        'cmp_w1_k': nrm(ks[9], (L, CMP_LEN * HEAD_DIM, CMP_HIDDEN), (CMP_LEN * HEAD_DIM) ** -0.5),
        'cmp_w2_k': nrm(ks[10], (L, CMP_HIDDEN, HEAD_DIM), CMP_HIDDEN ** -0.5),
        'cmp_pos_v': nrm(ks[11], (L, CMP_LEN, HEAD_DIM), 0.1),
        'cmp_w1_v': nrm(ks[12], (L, CMP_LEN * HEAD_DIM, CMP_HIDDEN), (CMP_LEN * HEAD_DIM) ** -0.5),
        'cmp_w2_v': nrm(ks[13], (L, CMP_HIDDEN, HEAD_DIM), CMP_HIDDEN ** -0.5),
        'out_norm_g': 1.0 + nrm(ks[14], (L, MIX_WIDTH), 0.02),
        'w_out': nrm(ks[15], (L, MIX_WIDTH, D), MIX_WIDTH ** -0.5),
        'router_group_w': nrm(ks[16], (L, D, N_GROUPS), D ** -0.5),
        'router_group_b': nrm(ks[17], (L, N_GROUPS), 0.01),
        'router_expert_w': nrm(ks[18], (L, D, N_EXPERTS), D ** -0.5),
        'router_expert_b': nrm(ks[19], (L, N_EXPERTS), 0.01),
        'expert_w1': nrm(ks[20], (L, N_EXPERTS, D, D_EXPERT), D ** -0.5),
        'expert_w3': nrm(ks[21], (L, N_EXPERTS, D, D_EXPERT), D ** -0.5),
        'expert_w2': nrm(ks[22], (L, N_EXPERTS, D_EXPERT, D), D_EXPERT ** -0.5),
        'final_g': 1.0 + nrm(ks[23], (D,), 0.02),
    }


def reference(x, c, norm1_g, norm2_g, ada_w, ada_b, w_in, b_forget, cmp_pos_k, cmp_w1_k, cmp_w2_k,
              cmp_pos_v, cmp_w1_v, cmp_w2_v, out_norm_g, w_out, router_group_w, router_group_b,
              router_expert_w, router_expert_b, expert_w1, expert_w3, expert_w2, final_g):
    cond = jax.nn.silu(c)
    for l in range(DEPTH):
        mod = cond @ ada_w[l] + ada_b[l]
        sh1, sc1, g1, sh2, sc2, g2 = jnp.split(mod[:, None, :], 6, axis=-1)
        h = rmsnorm(x, norm1_g[l]) * (1.0 + sc1) + sh1
        x = x + g1 * hybrid_mixer(h, w_in[l], b_forget[l], cmp_pos_k[l], cmp_w1_k[l], cmp_w2_k[l],
                                  cmp_pos_v[l], cmp_w1_v[l], cmp_w2_v[l], out_norm_g[l], w_out[l])
        h = rmsnorm(x, norm2_g[l]) * (1.0 + sc2) + sh2
        x = x + g2 * hier_moe(h, router_group_w[l], router_group_b[l], router_expert_w[l],
                              router_expert_b[l], expert_w1[l], expert_w3[l], expert_w2[l])
    return rmsnorm(x, final_g)
```

```python
import functools
import math

import numpy as np
import jax
import jax.numpy as jnp
from jax import lax
from jax.experimental import pallas as pl
from jax.experimental.pallas import tpu as pltpu

F32 = jnp.float32
BF16 = jnp.bfloat16

HEAD_DIM = 64
LANES = 128
N_HEADS = 16
NSA_HEADS = 8
NSA_KV = 2
NSA_GROUP = 4
FOX_HEADS = 4
SB_HEADS = 4
ROPE_DIM = 16
ROPE_HALF = 8
ROPE_THETA = 500000.0
CMP_LEN = 32
CMP_STRIDE = 16
CMP_HIDDEN = 128
SEL_LEN = 64
SEL_TOPN = 16
WINDOW = 512
FORCE_SCORE = 1.0e4
N_GROUPS = 4
EXPERTS_PER_GROUP = 8
N_EXPERTS = 32
EPS = 1e-6
NEG = -1e30

COL_QA = 0
COL_KS = 1024
COL_KW = 1152
ROPE_COLS = 1280
COL_VS = 1280
COL_VW = 1408
COL_FOX = 1536
COL_SB = 2560
N_MAIN = 3584
N_SMALL = 512
PROJ_CHUNK = 1280

TM_PROJ = 512
TQ_NSA = 128
TK_ATT = 256
TQ_PAIR = 256
TM_OUT = 512
TM_EXP = 256
TM_CMB = 512
VMEM_LIMIT = 56 * 1024 * 1024


def _cp(n_axes, vmem=VMEM_LIMIT):
    return pltpu.CompilerParams(dimension_semantics=("arbitrary",) * n_axes, vmem_limit_bytes=vmem)


def _dot(a, b):
    return jnp.dot(a, b, preferred_element_type=F32)


def _dot_nt(a, b):
    return lax.dot_general(a, b, (((1,), (1,)), ((), ())), preferred_element_type=F32)


def _split_bf16(x, parts):
    out = []
    r = x
    for _ in range(parts):
        p = r.astype(BF16)
        out.append(p)
        r = r - p.astype(F32)
    return out


def _dot_split(x, m, parts):
    acc = None
    for p in _split_bf16(x, parts):
        d = _dot(p, m)
        acc = d if acc is None else acc + d
    return acc


def _rope(x, c, s1, s2):
    return x * c + pltpu.roll(x, ROPE_HALF, 1) * s1 + pltpu.roll(x, LANES - ROPE_HALF, 1) * s2


def _softplus(z):
    return jnp.maximum(z, 0.0) + jnp.log(1.0 + jnp.exp(-jnp.abs(z)))


def _mod_kernel(c_ref, w_ref, b_ref, o_ref):
    c = c_ref[...]
    cond = c * (1.0 / (1.0 + jnp.exp(-c)))
    o_ref[0] = _dot(cond, w_ref[0]) + b_ref[0]


def _modulation(c, ada_w, ada_b):
    depth, d, n = ada_w.shape
    b = c.shape[0]
    tn = 1024
    return pl.pallas_call(
        _mod_kernel,
        grid=(depth, n // tn),
        in_specs=[pl.BlockSpec((b, d), lambda l, j: (0, 0)),
                  pl.BlockSpec((1, d, tn), lambda l, j: (l, 0, j)),
                  pl.BlockSpec((1, 1, tn), lambda l, j: (l, 0, j))],
        out_specs=pl.BlockSpec((1, b, tn), lambda l, j: (l, 0, j)),
        out_shape=jax.ShapeDtypeStruct((depth, b, n), F32),
        compiler_params=_cp(2),
        name="modulation",
    )(c, ada_w, ada_b.reshape(depth, 1, n))


def _inproj_kernel(x_ref, g_ref, sc_ref, sh_ref, w_ref, wc_ref, ws_ref, rc_ref, r1_ref, r2_ref,
                   u_ref, kc_ref, vc_ref, sm_ref):
    x = x_ref[0]
    ms = jnp.mean(x * x, axis=-1, keepdims=True)
    h = (x * lax.rsqrt(ms + EPS) * g_ref[...]) * (1.0 + sc_ref[0]) + sh_ref[0]
    hb = h.astype(BF16)
    rc, r1, r2 = rc_ref[...], r1_ref[...], r2_ref[...]
    for j in range(N_MAIN // PROJ_CHUNK + (1 if N_MAIN % PROJ_CHUNK else 0)):
        lo = j * PROJ_CHUNK
        hi = min(lo + PROJ_CHUNK, N_MAIN)
        acc = _dot(hb, w_ref[:, lo:hi])
        if lo < ROPE_COLS:
            for k in range((hi - lo) // LANES):
                blk = acc[:, k * LANES:(k + 1) * LANES]
                u_ref[0, :, lo + k * LANES:lo + (k + 1) * LANES] = _rope(blk, rc, r1, r2).astype(BF16)
        else:
            u_ref[0, :, lo:hi] = acc.astype(BF16)
    cmp_in = _dot(hb, wc_ref[...])
    kc_ref[0] = cmp_in[:, :LANES].astype(BF16)
    vc_ref[0] = cmp_in[:, LANES:].astype(BF16)
    sm_ref[0] = _dot(hb, ws_ref[...])


def _in_projection(x, g, sc, sh, w_main, w_cmp, w_small, rope_c, rope_1, rope_2):
    b, s, d = x.shape
    tm = min(TM_PROJ, s)
    row = lambda i, j: (i, j, 0)
    const2 = lambda i, j: (0, 0)
    per_b = lambda i, j: (i, 0, 0)
    seq = lambda i, j: (j, 0)
    return pl.pallas_call(
        _inproj_kernel,
        grid=(b, s // tm),
        in_specs=[pl.BlockSpec((1, tm, d), row),
                  pl.BlockSpec((1, d), const2),
                  pl.BlockSpec((1, 1, d), per_b),
                  pl.BlockSpec((1, 1, d), per_b),
                  pl.BlockSpec((d, N_MAIN), const2),
                  pl.BlockSpec((d, 2 * LANES), const2),
                  pl.BlockSpec((d, N_SMALL), const2),
                  pl.BlockSpec((tm, LANES), seq),
                  pl.BlockSpec((tm, LANES), seq),
                  pl.BlockSpec((tm, LANES), seq)],
        out_specs=[pl.BlockSpec((1, tm, N_MAIN), row),
                   pl.BlockSpec((1, tm, LANES), row),
                   pl.BlockSpec((1, tm, LANES), row),
                   pl.BlockSpec((1, tm, N_SMALL), row)],
        out_shape=[jax.ShapeDtypeStruct((b, s, N_MAIN), BF16),
                   jax.ShapeDtypeStruct((b, s, LANES), BF16),
                   jax.ShapeDtypeStruct((b, s, LANES), BF16),
                   jax.ShapeDtypeStruct((b, s, N_SMALL), F32)],
        compiler_params=_cp(2),
        name="in_projection",
    )(x, g.reshape(1, d), sc.reshape(b, 1, d), sh.reshape(b, 1, d), w_main, w_cmp, w_small,
      rope_c, rope_1, rope_2)


def _cumf_kernel(f_ref, b_ref, cq_ref, ckt_ref):
    n_chunks = f_ref.shape[1] // TK_ATT
    r = lax.broadcasted_iota(jnp.int32, (TK_ATT, TK_ATT), 0)
    c = lax.broadcasted_iota(jnp.int32, (TK_ATT, TK_ATT), 1)
    tri = jnp.where(c <= r, 1.0, 0.0).astype(BF16)
    carry = jnp.zeros((1, LANES), F32)
    for j in range(n_chunks):
        f = f_ref[0, j * TK_ATT:(j + 1) * TK_ATT, :] + b_ref[0]
        ls = -_softplus(-f)
        acc = None
        for p in _split_bf16(ls, 3):
            dd = _dot(tri, p)
            acc = dd if acc is None else acc + dd
        cs = acc + carry
        cq_ref[0, j * TK_ATT:(j + 1) * TK_ATT, :] = cs
        ckt_ref[0, 0, j] = cs.T[:8, :]
        carry = cs[TK_ATT - 1:TK_ATT, :]


def _forget_cumsum(small, b_pairs):
    b, s, _ = small.shape
    return pl.pallas_call(
        _cumf_kernel,
        grid=(b, 2),
        in_specs=[pl.BlockSpec((1, s, LANES), lambda i, p: (i, 0, 2 + p)),
                  pl.BlockSpec((1, 1, LANES), lambda i, p: (p, 0, 0))],
        out_specs=[pl.BlockSpec((1, s, LANES), lambda i, p: (i, 0, p)),
                   pl.BlockSpec((1, 1, s // TK_ATT, 8, TK_ATT), lambda i, p: (i, p, 0, 0, 0))],
        out_shape=[jax.ShapeDtypeStruct((b, s, 2 * LANES), F32),
                   jax.ShapeDtypeStruct((b, 2, s // TK_ATT, 8, TK_ATT), F32)],
        compiler_params=_cp(2),
        name="forget_cumsum",
    )(small, b_pairs)


def _compress_kernel(ks_ref, vs_ref, wk_ref, wv_ref, w1k_ref, w1v_ref, pek_ref, pev_ref,
                     w2k_ref, w2v_ref, rc_ref, r1_ref, r2_ref, ck_ref, cv_ref):
    def one(seg_ref, w_ref, w1_ref, pe_ref, w2_ref):
        p = _dot(seg_ref[0], w_ref[...])
        half = 2 * CMP_HIDDEN
        bias = _dot(pe_ref[...].astype(BF16), w1_ref[...].astype(BF16))[0:1, :]
        bias2 = jnp.concatenate([bias, bias], axis=1)
        n = p.shape[0]
        hid = p[:, :half] + pltpu.roll(p[:, half:], n - 1, 0) + bias2
        act = hid * (1.0 / (1.0 + jnp.exp(-hid)))
        return _dot(act.astype(BF16), w2_ref[...])

    ck = one(ks_ref, wk_ref, w1k_ref, pek_ref, w2k_ref)
    ck_ref[0] = _rope(ck, rc_ref[...], r1_ref[...], r2_ref[...]).astype(BF16)
    cv_ref[0] = one(vs_ref, wv_ref, w1v_ref, pev_ref, w2v_ref).astype(BF16)


def _compress(kc, vc, wk, wv, w1k, w1v, pek, pev, w2k, w2v, rc, r1, r2):
    b, s, _ = kc.shape
    n = s // CMP_STRIDE
    width = CMP_STRIDE * LANES
    kseg = kc.reshape(b, n, width)
    vseg = vc.reshape(b, n, width)
    seg = pl.BlockSpec((1, n, width), lambda i: (i, 0, 0))
    full = lambda a: pl.BlockSpec(a.shape, lambda i: (0,) * a.ndim)
    return pl.pallas_call(
        _compress_kernel,
        grid=(b,),
        in_specs=[seg, seg, full(wk), full(wv), full(w1k), full(w1v), full(pek), full(pev),
                  full(w2k), full(w2v), full(rc), full(r1), full(r2)],
        out_specs=[pl.BlockSpec((1, n, LANES), lambda i: (i, 0, 0)),
                   pl.BlockSpec((1, n, LANES), lambda i: (i, 0, 0))],
        out_shape=[jax.ShapeDtypeStruct((b, n, LANES), BF16),
                   jax.ShapeDtypeStruct((b, n, LANES), BF16)],
        compiler_params=_cp(1),
        name="nsa_compress",
    )(kseg, vseg, wk, wv, w1k, w1v, pek, pev, w2k, w2v, rc, r1, r2)


def _flash_loop(qall, k_ref, v_ref, lo, hi, mask_fn, heads, tq):
    rows = heads * tq

    def body(kt, carry):
        m, l, acc = carry
        k0 = pl.multiple_of(kt * TK_ATT, TK_ATT)
        k = k_ref[0, pl.ds(k0, TK_ATT), :]
        v = v_ref[0, pl.ds(k0, TK_ATT), :]
        s = _dot_nt(qall, k).reshape(heads, tq, TK_ATT)
        s = s + mask_fn(kt, k0)
        mn = jnp.maximum(m, jnp.max(s, axis=-1, keepdims=True))
        alpha = jnp.exp(m - mn)
        p = jnp.exp(s - mn)
        l = alpha * l + jnp.sum(p, axis=-1, keepdims=True)
        pv = _dot(p.astype(BF16).reshape(rows, TK_ATT), v).reshape(heads, tq, LANES)
        return mn, l, alpha * acc + pv

    m0 = jnp.full((heads, tq, 1), NEG, F32)
    l0 = jnp.zeros((heads, tq, 1), F32)
    a0 = jnp.zeros((heads, tq, LANES), F32)
    m, l, acc = lax.fori_loop(lo, hi, body, (m0, l0, a0))
    return acc / l


def _nsa_kernel(q_ref, ck_ref, cv_ref, ks_ref, vs_ref, kw_ref, vw_ref, gate_ref, gn_ref,
                cover_ref, expand_ref, o_ref):
    tq = q_ref.shape[1]
    g = pl.program_id(1)
    t0 = pl.program_id(2) * tq
    q = q_ref[0]
    qall = jnp.concatenate([q[:, h * LANES:(h + 1) * LANES] for h in range(NSA_GROUP)], axis=0)
    tpos = t0 + lax.broadcasted_iota(jnp.int32, (tq, 1), 0)
    lane = lax.broadcasted_iota(jnp.int32, (tq, LANES), 1)

    s = _dot_nt(qall, ck_ref[0]).reshape(NSA_GROUP, tq, LANES)
    valid = (CMP_STRIDE * lane + (CMP_LEN - 1)) <= tpos
    sm = jnp.where(valid[None], s, NEG)
    mx = jnp.max(sm, axis=-1, keepdims=True)
    e = jnp.where(valid[None], jnp.exp(sm - mx), 0.0)
    den = jnp.sum(e, axis=-1, keepdims=True)
    p_cmp = e * jnp.where(den > 0.0, 1.0 / den, 0.0)
    o_cmp = _dot(p_cmp.astype(BF16).reshape(NSA_GROUP * tq, LANES), cv_ref[0])
    o_cmp = o_cmp.reshape(NSA_GROUP, tq, LANES)

    p_sum = p_cmp[0] + p_cmp[1] + p_cmp[2] + p_cmp[3]
    imp = _dot_split(p_sum, cover_ref[...], 3)
    cur = jnp.right_shift(tpos, int(math.log2(SEL_LEN)))
    forced = (lane == 0) | (lane == cur) | (lane == cur - 1)
    causal_blk = (lane * SEL_LEN) <= tpos
    n_sel = expand_ref.shape[0] * (TK_ATT // SEL_LEN)
    score = jnp.where(forced, FORCE_SCORE, jnp.where(causal_blk, imp, -1.0))
    score = jnp.where(lane < n_sel, score, -2.0)
    cnt = jnp.zeros((tq, LANES), F32)
    for j in range(n_sel):
        cj = score[:, j:j + 1]
        beats = (cj > score) | ((cj == score) & (lane > j))
        cnt = cnt + jnp.where(beats, 1.0, 0.0)
    sel = jnp.where((cnt < float(min(SEL_TOPN, n_sel))) & (lane < n_sel), 1.0, 0.0).astype(BF16)

    kcol = lax.broadcasted_iota(jnp.int32, (tq, TK_ATT), 1)

    def sel_mask(kt, k0):
        hit = _dot(sel, expand_ref[kt])
        ok = (hit > 0.5) & ((k0 + kcol) <= tpos)
        return jnp.where(ok, 0.0, NEG)[None]

    def win_mask(kt, k0):
        kp = k0 + kcol
        ok = (kp <= tpos) & (kp > tpos - WINDOW)
        return jnp.where(ok, 0.0, NEG)[None]

    hi = lax.div(t0 + tq - 1, TK_ATT) + 1
    o_sel = _flash_loop(qall, ks_ref, vs_ref, 0, hi, sel_mask, NSA_GROUP, tq)
    lo_w = lax.div(jnp.maximum(t0 - (WINDOW - 1), 0), TK_ATT)
    o_win = _flash_loop(qall, kw_ref, vw_ref, lo_w, hi, win_mask, NSA_GROUP, tq)

    gt = gate_ref[0]
    gt = 1.0 / (1.0 + jnp.exp(-gt))
    mine = (lane >= g * HEAD_DIM) & (lane < (g + 1) * HEAD_DIM)
    outs = []
    for h in range(NSA_GROUP):
        o = (gt[:, 3 * h:3 * h + 1] * o_cmp[h] + gt[:, 3 * h + 1:3 * h + 2] * o_sel[h]
             + gt[:, 3 * h + 2:3 * h + 3] * o_win[h])
        o = jnp.where(mine, o, 0.0)
        ms = jnp.sum(o * o, axis=-1, keepdims=True) * (1.0 / HEAD_DIM)
        o = o * lax.rsqrt(ms + EPS)
        outs.append(o + pltpu.roll(o, HEAD_DIM, 1))
    left = lane < HEAD_DIM
    o_ref[0, :, :LANES] = (jnp.where(left, outs[0], outs[1]) * gn_ref[:, :LANES]).astype(o_ref.dtype)
    o_ref[0, :, LANES:] = (jnp.where(left, outs[2], outs[3]) * gn_ref[:, LANES:]).astype(o_ref.dtype)


def _nsa_attention(u, ck, cv, small, gn, cover, expand):
    b, s, _ = u.shape
    tq = min(TQ_NSA, s)
    n_cmp = ck.shape[1]
    blk = LANES
    kv = lambda col: pl.BlockSpec((1, s, LANES), lambda i, g, j, col=col: (i, 0, col // blk))
    return pl.pallas_call(
        _nsa_kernel,
        grid=(b, NSA_KV, s // tq),
        in_specs=[pl.BlockSpec((1, tq, NSA_GROUP * LANES), lambda i, g, j: (i, j, g)),
                  pl.BlockSpec((1, n_cmp, LANES), lambda i, g, j: (i, 0, 0)),
                  pl.BlockSpec((1, n_cmp, LANES), lambda i, g, j: (i, 0, 0)),
                  kv(COL_KS), kv(COL_VS), kv(COL_KW), kv(COL_VW),
                  pl.BlockSpec((1, tq, LANES), lambda i, g, j: (i, j, g)),
                  pl.BlockSpec((1, 2 * LANES), lambda i, g, j: (0, g)),
                  pl.BlockSpec(cover.shape, lambda i, g, j: (0, 0)),
                  pl.BlockSpec(expand.shape, lambda i, g, j: (0, 0, 0))],
        out_specs=pl.BlockSpec((1, tq, 2 * LANES), lambda i, g, j: (i, j, g)),
        out_shape=jax.ShapeDtypeStruct((b, s, NSA_HEADS * HEAD_DIM), BF16),
        compiler_params=_cp(3),
        name="nsa_attention",
    )(u, ck, cv, u, u, u, u, small, gn, cover, expand)


def _pair_finish(acc, gn_ref, o_ref, tq):
    lane = lax.broadcasted_iota(jnp.int32, (tq, LANES), 1)
    left = lane < HEAD_DIM
    o = jnp.where(left, acc[0], acc[1])
    o2 = o * o
    ms_l = jnp.sum(jnp.where(left, o2, 0.0), axis=-1, keepdims=True) * (1.0 / HEAD_DIM)
    ms_r = jnp.sum(jnp.where(left, 0.0, o2), axis=-1, keepdims=True) * (1.0 / HEAD_DIM)
    inv = jnp.where(left, lax.rsqrt(ms_l + EPS), lax.rsqrt(ms_r + EPS))
    o_ref[0] = (o * inv * gn_ref[...]).astype(o_ref.dtype)


def _fox_kernel(q_ref, k_ref, v_ref, cq_ref, ckt_ref, gn_ref, o_ref):
    tq = q_ref.shape[1]
    t0 = pl.program_id(2) * tq
    q = q_ref[0]
    qall = jnp.concatenate([q[:, :LANES], q[:, LANES:]], axis=0)
    tpos = t0 + lax.broadcasted_iota(jnp.int32, (tq, 1), 0)
    kcol = lax.broadcasted_iota(jnp.int32, (tq, TK_ATT), 1)
    cq = cq_ref[0]
    cq0, cq1 = cq[:, 0:1], cq[:, 1:2]

    def bias(kt, k0):
        ck = ckt_ref[0, 0, kt]
        causal = jnp.where((k0 + kcol) <= tpos, 0.0, NEG)
        b0 = (cq0 - ck[0:1, :]) + causal
        b1 = (cq1 - ck[1:2, :]) + causal
        return jnp.concatenate([b0[None], b1[None]], axis=0)

    hi = lax.div(t0 + tq - 1, TK_ATT) + 1
    acc = _flash_loop(qall, k_ref, v_ref, 0, hi, bias, 2, tq)
    _pair_finish(acc, gn_ref, o_ref, tq)


def _fox_attention(u, cq, ckt, gn):
    b, s, _ = u.shape
    tq = min(TQ_PAIR, s)
    qb = COL_FOX // (2 * LANES)
    kb = (COL_FOX + FOX_HEADS * LANES) // LANES
    vb = kb + 2
    return pl.pallas_call(
        _fox_kernel,
        grid=(b, 2, s // tq),
        in_specs=[pl.BlockSpec((1, tq, 2 * LANES), lambda i, p, j: (i, j, qb + p)),
                  pl.BlockSpec((1, s, LANES), lambda i, p, j: (i, 0, kb + p)),
                  pl.BlockSpec((1, s, LANES), lambda i, p, j: (i, 0, vb + p)),
                  pl.BlockSpec((1, tq, LANES), lambda i, p, j: (i, j, p)),
                  pl.BlockSpec((1, 1, s // TK_ATT, 8, TK_ATT), lambda i, p, j: (i, p, 0, 0, 0)),
                  pl.BlockSpec((1, LANES), lambda i, p, j: (0, p))],
        out_specs=pl.BlockSpec((1, tq, LANES), lambda i, p, j: (i, j, p)),
        out_shape=jax.ShapeDtypeStruct((b, s, FOX_HEADS * HEAD_DIM), BF16),
        compiler_params=_cp(3),
        name="fox_attention",
    )(u, u, u, cq, ckt, gn)


def _sb_kernel(q_ref, k_ref, v_ref, gn_ref, o_ref):
    tq = q_ref.shape[1]
    t0 = pl.program_id(2) * tq
    q = q_ref[0]
    qall = jnp.concatenate([q[:, :LANES], q[:, LANES:]], axis=0)
    tpos = t0 + lax.broadcasted_iota(jnp.int32, (tq, 1), 0)
    kcol = lax.broadcasted_iota(jnp.int32, (tq, TK_ATT), 1)
    r = lax.broadcasted_iota(jnp.int32, (TK_ATT, TK_ATT), 0)
    c = lax.broadcasted_iota(jnp.int32, (TK_ATT, TK_ATT), 1)
    upper = jnp.where(r >= c, 1.0, 0.0).astype(BF16)
    n_tiles = lax.div(t0 + tq - 1, TK_ATT) + 1

    def body(i, carry):
        rest, acc = carry
        kt = n_tiles - 1 - i
        k0 = pl.multiple_of(kt * TK_ATT, TK_ATT)
        k = k_ref[0, pl.ds(k0, TK_ATT), :]
        v = v_ref[0, pl.ds(k0, TK_ATT), :]
        z = _dot_nt(qall, k).reshape(2, tq, TK_ATT)
        ok = ((k0 + kcol) < tpos)[None]
        l = jnp.where(ok, -_softplus(z), 0.0)
        cum = _dot_split(l.reshape(2 * tq, TK_ATT), upper, 2).reshape(2, tq, TK_ATT)
        a = jnp.where(ok, jnp.exp(z + cum + rest), 0.0)
        pv = _dot(a.astype(BF16).reshape(2 * tq, TK_ATT), v).reshape(2, tq, LANES)
        return rest + cum[:, :, 0:1], acc + pv

    rest0 = jnp.zeros((2, tq, 1), F32)
    acc0 = jnp.zeros((2, tq, LANES), F32)
    _, acc = lax.fori_loop(0, n_tiles, body, (rest0, acc0))
    _pair_finish(acc, gn_ref, o_ref, tq)


def _sb_attention(u, gn):
    b, s, _ = u.shape
    tq = min(TQ_PAIR, s)
    qb = COL_SB // (2 * LANES)
    kb = (COL_SB + SB_HEADS * LANES) // LANES
    vb = kb + 2
    return pl.pallas_call(
        _sb_kernel,
        grid=(b, 2, s // tq),
        in_specs=[pl.BlockSpec((1, tq, 2 * LANES), lambda i, p, j: (i, j, qb + p)),
                  pl.BlockSpec((1, s, LANES), lambda i, p, j: (i, 0, kb + p)),
                  pl.BlockSpec((1, s, LANES), lambda i, p, j: (i, 0, vb + p)),
                  pl.BlockSpec((1, LANES), lambda i, p, j: (0, p))],
        out_specs=pl.BlockSpec((1, tq, LANES), lambda i, p, j: (i, j, p)),
        out_shape=jax.ShapeDtypeStruct((b, s, SB_HEADS * HEAD_DIM), BF16),
        compiler_params=_cp(3),
        name="sb_attention",
    )(u, u, u, gn)


def _outproj_kernel(oa_ref, ob_ref, oc_ref, x_ref, w_ref, g1_ref, n2_ref, sc_ref, sh_ref,
                    wrh_ref, wrl_ref, br_ref, xo_ref, h_ref, rw_ref, ri_ref):
    na = oa_ref.shape[2]
    nb = ob_ref.shape[2]
    y = _dot(oa_ref[0], w_ref[0:na, :])
    y = y + _dot(ob_ref[0], w_ref[na:na + nb, :])
    y = y + _dot(oc_ref[0], w_ref[na + nb:, :])
    x = x_ref[0] + g1_ref[0] * y
    xo_ref[0] = x
    ms = jnp.mean(x * x, axis=-1, keepdims=True)
    h = (x * lax.rsqrt(ms + EPS) * n2_ref[...]) * (1.0 + sc_ref[0]) + sh_ref[0]
    hb = h.astype(BF16)
    h_ref[0] = hb
    hl = (h - hb.astype(F32)).astype(BF16)
    logit = _dot(hb, wrh_ref[...]) + _dot(hl, wrh_ref[...]) + _dot(hb, wrl_ref[...]) + br_ref[...]

    tm = logit.shape[0]
    lane = lax.broadcasted_iota(jnp.int32, (tm, LANES), 1).astype(F32)
    big = float(LANES)
    is_g = lane < N_GROUPS
    lg = jnp.where(is_g, logit, NEG)
    mg = jnp.max(lg, axis=-1, keepdims=True)
    zg = jnp.sum(jnp.where(is_g, jnp.exp(lg - mg), 0.0), axis=-1, keepdims=True)
    pg = 1.0 / zg
    gi = jnp.min(jnp.where(is_g & (lg == mg), lane, big), axis=-1, keepdims=True)
    e_lane = lane - N_GROUPS
    in_grp = (e_lane >= gi * EXPERTS_PER_GROUP) & (e_lane < (gi + 1) * EXPERTS_PER_GROUP)
    le = jnp.where(in_grp, logit, NEG)
    m1 = jnp.max(le, axis=-1, keepdims=True)
    i1 = jnp.min(jnp.where(in_grp & (le == m1), lane, big), axis=-1, keepdims=True)
    rest = in_grp & (lane != i1)
    le2 = jnp.where(rest, logit, NEG)
    m2 = jnp.max(le2, axis=-1, keepdims=True)
    i2 = jnp.min(jnp.where(rest & (le2 == m2), lane, big), axis=-1, keepdims=True)
    ze = jnp.sum(jnp.where(in_grp, jnp.exp(le - m1), 0.0), axis=-1, keepdims=True)
    p1 = 1.0 / ze
    p2 = jnp.exp(m2 - m1) / ze
    den = p1 + p2
    w1 = pg * (p1 / den)
    w2 = pg * (p2 / den)
    rw_ref[0] = jnp.where(lane == 0.0, w1, jnp.where(lane == 1.0, w2, 0.0))
    ri_ref[0] = jnp.where(lane == 0.0, i1 - N_GROUPS, jnp.where(lane == 1.0, i2 - N_GROUPS, 0.0)).astype(jnp.int32)


def _out_projection(oa, ob, oc, x, w_out, g1, n2, sc, sh, wr_hi, wr_lo, br):
    b, s, d = x.shape
    tm = min(TM_OUT, s)
    row = lambda i, j: (i, j, 0)
    const2 = lambda i, j: (0, 0)
    per_b = lambda i, j: (i, 0, 0)
    return pl.pallas_call(
        _outproj_kernel,
        grid=(b, s // tm),
        in_specs=[pl.BlockSpec((1, tm, oa.shape[2]), row),
                  pl.BlockSpec((1, tm, ob.shape[2]), row),
                  pl.BlockSpec((1, tm, oc.shape[2]), row),
                  pl.BlockSpec((1, tm, d), row),
                  pl.BlockSpec(w_out.shape, const2),
                  pl.BlockSpec((1, 1, d), per_b),
                  pl.BlockSpec((1, d), const2),
                  pl.BlockSpec((1, 1, d), per_b),
                  pl.BlockSpec((1, 1, d), per_b),
                  pl.BlockSpec((d, LANES), const2),
                  pl.BlockSpec((d, LANES), const2),
                  pl.BlockSpec((1, LANES), const2)],
        out_specs=[pl.BlockSpec((1, tm, d), row),
                   pl.BlockSpec((1, tm, d), row),
                   pl.BlockSpec((1, tm, LANES), row),
                   pl.BlockSpec((1, tm, LANES), row)],
        out_shape=[jax.ShapeDtypeStruct((b, s, d), F32),
                   jax.ShapeDtypeStruct((b, s, d), BF16),
                   jax.ShapeDtypeStruct((b, s, LANES), F32),
                   jax.ShapeDtypeStruct((b, s, LANES), jnp.int32)],
        compiler_params=_cp(2),
        name="out_projection",
    )(oa, ob, oc, x, w_out, g1.reshape(b, 1, d), n2.reshape(1, d), sc.reshape(b, 1, d),
      sh.reshape(b, 1, d), wr_hi, wr_lo, br)


def _expert_kernel(te_ref, nu_ref, x_ref, w1_ref, w3_ref, w2_ref, y_ref):
    i = pl.program_id(0)

    @pl.when(i < nu_ref[0])
    def _():
        x = x_ref[...]
        a = _dot(x, w1_ref[0].astype(BF16))
        g = _dot(x, w3_ref[0].astype(BF16))
        act = (a * (1.0 / (1.0 + jnp.exp(-a)))) * g
        y_ref[...] = _dot(act.astype(BF16), w2_ref[0].astype(BF16))

    @pl.when(i >= nu_ref[0])
    def _():
        y_ref[...] = jnp.zeros_like(y_ref)


def _expert_mlp(xs, tile_expert, n_used, w1, w3, w2):
    p_rows, d = xs.shape
    de = w1.shape[2]
    n_tiles = p_rows // TM_EXP
    grid_spec = pltpu.PrefetchScalarGridSpec(
        num_scalar_prefetch=2,
        grid=(n_tiles,),
        in_specs=[pl.BlockSpec((TM_EXP, d), lambda i, te, nu: (i, 0)),
                  pl.BlockSpec((1, d, de), lambda i, te, nu: (te[i], 0, 0)),
                  pl.BlockSpec((1, d, de), lambda i, te, nu: (te[i], 0, 0)),
                  pl.BlockSpec((1, de, d), lambda i, te, nu: (te[i], 0, 0))],
        out_specs=pl.BlockSpec((TM_EXP, d), lambda i, te, nu: (i, 0)),
    )
    return pl.pallas_call(
        _expert_kernel,
        grid_spec=grid_spec,
        out_shape=jax.ShapeDtypeStruct((p_rows, d), F32),
        compiler_params=_cp(1),
        name="expert_mlp",
    )(tile_expert, n_used, xs, w1, w3, w2)


def _combine_kernel(x_ref, y0_ref, y1_ref, rw_ref, g2_ref, fg_ref, o_ref, *, final):
    rw = rw_ref[0]
    moe = y0_ref[0] * rw[:, 0:1] + y1_ref[0] * rw[:, 1:2]
    x = x_ref[0] + g2_ref[0] * moe
    if final:
        ms = jnp.mean(x * x, axis=-1, keepdims=True)
        x = x * lax.rsqrt(ms + EPS) * fg_ref[...]
    o_ref[0] = x


def _combine(x, y0, y1, rw, g2, final_g, final):
    b, s, d = x.shape
    tm = min(TM_CMB, s)
    row = lambda i, j: (i, j, 0)
    return pl.pallas_call(
        functools.partial(_combine_kernel, final=final),
        grid=(b, s // tm),
        in_specs=[pl.BlockSpec((1, tm, d), row),
                  pl.BlockSpec((1, tm, d), row),
                  pl.BlockSpec((1, tm, d), row),
                  pl.BlockSpec((1, tm, LANES), row),
                  pl.BlockSpec((1, 1, d), lambda i, j: (i, 0, 0)),
                  pl.BlockSpec((1, d), lambda i, j: (0, 0))],
        out_specs=pl.BlockSpec((1, tm, d), row),
        out_shape=jax.ShapeDtypeStruct((b, s, d), F32),
        compiler_params=_cp(2),
        name="moe_combine_final" if final else "moe_combine",
    )(x, y0, y1, rw, g2.reshape(b, 1, d), final_g.reshape(1, d))


def _pad_heads(w, n_heads, offsets):
    d = w.shape[0]
    w = w.reshape(d, n_heads, HEAD_DIM)
    z = jnp.zeros((d, n_heads, HEAD_DIM), w.dtype)
    off = jnp.asarray(offsets, jnp.int32).reshape(1, n_heads, 1)
    blk = jnp.where(off == 0, jnp.concatenate([w, z], axis=-1), jnp.concatenate([z, w], axis=-1))
    return blk.reshape(d, n_heads * LANES)


def _layout_w_in(w_in):
    d = w_in.shape[0]
    kvw = NSA_KV * HEAD_DIM
    sizes = (NSA_HEADS * HEAD_DIM, kvw, kvw, kvw, kvw, kvw, kvw, NSA_HEADS * 3,
             FOX_HEADS * HEAD_DIM, FOX_HEADS * HEAD_DIM, FOX_HEADS * HEAD_DIM, FOX_HEADS,
             SB_HEADS * HEAD_DIM, SB_HEADS * HEAD_DIM, SB_HEADS * HEAD_DIM)
    pts = np.cumsum(sizes)[:-1].tolist()
    (qa, kca, vca, ksa, vsa, kwa, vwa, ga, qb, kb, vb, fb, qc, kc, vc) = jnp.split(w_in, pts, axis=1)
    scale = HEAD_DIM ** -0.5
    qa_p = _pad_heads(qa * scale, NSA_HEADS, [0] * NSA_GROUP + [HEAD_DIM] * NSA_GROUP)
    qb_p = _pad_heads(qb * scale, FOX_HEADS, [0, HEAD_DIM, 0, HEAD_DIM])
    qc_p = _pad_heads(qc * scale, SB_HEADS, [0, HEAD_DIM, 0, HEAD_DIM])
    main = jnp.concatenate([qa_p, ksa, kwa, vsa, vwa, qb_p, kb, vb, qc_p, kc, vc], axis=1).astype(BF16)
    cmp_w = jnp.concatenate([kca, vca], axis=1).astype(BF16)
    zpad = lambda n: jnp.zeros((d, n), w_in.dtype)
    per_grp = NSA_GROUP * 3
    small = jnp.concatenate([ga[:, :per_grp], zpad(LANES - per_grp), ga[:, per_grp:], zpad(LANES - per_grp),
                             fb[:, 0:2], zpad(LANES - 2), fb[:, 2:4], zpad(LANES - 2)], axis=1).astype(BF16)
    return main, cmp_w, small


def _layout_cmp(w1, w2):
    hid = w1.shape[1]
    w1r = w1.reshape(2, CMP_STRIDE, HEAD_DIM, hid)
    z = jnp.zeros((CMP_STRIDE, HEAD_DIM, hid), w1.dtype)
    cols = []
    for half in range(2):
        for g in range(NSA_KV):
            parts = [w1r[half] if gg == g else z for gg in range(NSA_KV)]
            cols.append(jnp.concatenate(parts, axis=1).reshape(CMP_STRIDE * LANES, hid))
    wcat = jnp.concatenate(cols, axis=1).astype(BF16)
    zz = jnp.zeros_like(w2)
    w2bd = jnp.concatenate([jnp.concatenate([w2, zz], axis=1),
                            jnp.concatenate([zz, w2], axis=1)], axis=0).astype(BF16)
    return wcat, w2bd


def _rope_tables(pos):
    inv = jnp.exp(jnp.arange(ROPE_HALF, dtype=F32) * (-2.0 * math.log(ROPE_THETA) / ROPE_DIM))
    ang = pos.astype(F32)[:, None] * inv[None, :]
    cos, sin = jnp.cos(ang), jnp.sin(ang)
    n = pos.shape[0]
    z8 = jnp.zeros((n, ROPE_HALF), F32)
    rest1 = jnp.ones((n, HEAD_DIM - ROPE_DIM), F32)
    rest0 = jnp.zeros((n, HEAD_DIM - ROPE_DIM), F32)
    c = jnp.concatenate([cos, cos, rest1], axis=1)
    s1 = jnp.concatenate([z8, sin, rest0], axis=1)
    s2 = jnp.concatenate([-sin, z8, rest0], axis=1)
    dup = lambda a: jnp.concatenate([a, a], axis=1)
    return dup(c), dup(s1), dup(s2)


def _static_tables(s):
    n_cmp_pad = s // CMP_STRIDE
    n = np.arange(n_cmp_pad)[:, None]
    j = np.arange(LANES)[None, :]
    n_sel = s // SEL_LEN
    cover = ((n * CMP_STRIDE < j * SEL_LEN + SEL_LEN) & (n * CMP_STRIDE + CMP_LEN > j * SEL_LEN)
             & (j < n_sel)).astype(np.float32)
    nt = s // TK_ATT
    key = np.arange(nt)[:, None, None] * TK_ATT + np.arange(TK_ATT)[None, None, :]
    expand = (key // SEL_LEN == np.arange(LANES)[None, :, None]).astype(np.float32)
    return jnp.asarray(cover, BF16), jnp.asarray(expand, BF16)


def _dispatch_plan(ri, t):
    eid = ri.reshape(t, LANES)[:, :2].reshape(-1)
    n_assign = eid.shape[0]
    onehot = (eid[:, None] == jnp.arange(N_EXPERTS, dtype=jnp.int32)[None, :]).astype(jnp.int32)
    csum = jnp.cumsum(onehot, axis=0)
    counts = csum[-1]
    rank = jnp.take_along_axis(csum, eid[:, None], axis=1)[:, 0] - 1
    padded = ((counts + TM_EXP - 1) // TM_EXP) * TM_EXP
    pends = jnp.cumsum(padded)
    pstarts = pends - padded
    dest = pstarts[eid] + rank
    n_tiles = -(-(n_assign + N_EXPERTS * (TM_EXP - 1)) // TM_EXP)
    tile_expert = jnp.clip(jnp.searchsorted(pends, jnp.arange(n_tiles, dtype=jnp.int32) * TM_EXP,
                                            side='right'), 0, N_EXPERTS - 1).astype(jnp.int32)
    n_used = (pends[-1] // TM_EXP).astype(jnp.int32).reshape(1)
    tok = jnp.arange(n_assign, dtype=jnp.int32) // 2
    buf_tok = jnp.zeros((n_tiles * TM_EXP,), jnp.int32).at[dest].set(tok)
    return dest, buf_tok, tile_expert, n_used


def kernel(x, c, norm1_g, norm2_g, ada_w, ada_b, w_in, b_forget, cmp_pos_k, cmp_w1_k, cmp_w2_k,
           cmp_pos_v, cmp_w1_v, cmp_w2_v, out_norm_g, w_out, router_group_w, router_group_b,
           router_expert_w, router_expert_b, expert_w1, expert_w3, expert_w2, final_g):
    b, s, d = x.shape
    depth = ada_w.shape[0]
    t = b * s
    mod = _modulation(c, ada_w, ada_b)
    rope_c, rope_1, rope_2 = _rope_tables(jnp.arange(s))
    n_cmp_pad = s // CMP_STRIDE
    crc, cr1, cr2 = _rope_tables(jnp.arange(n_cmp_pad) * CMP_STRIDE + (CMP_LEN - 1))
    cover, expand = _static_tables(s)

    for l in range(depth):
        sh1, sc1, g1, sh2, sc2, g2 = [mod[l][:, i * d:(i + 1) * d] for i in range(6)]
        w_main, w_cmp, w_small = _layout_w_in(w_in[l])
        u, kc, vc, small = _in_projection(x, norm1_g[l], sc1, sh1, w_main, w_cmp, w_small,
                                          rope_c, rope_1, rope_2)
        bf = b_forget[l]
        zf = jnp.zeros((LANES - 2,), F32)
        b_pairs = jnp.stack([jnp.concatenate([bf[0:2], zf]), jnp.concatenate([bf[2:4], zf])]).reshape(2, 1, LANES)
        cq, ckt = _forget_cumsum(small, b_pairs)
        wk, w2k = _layout_cmp(cmp_w1_k[l], cmp_w2_k[l])
        wv, w2v = _layout_cmp(cmp_w1_v[l], cmp_w2_v[l])
        ck, cv = _compress(kc, vc, wk, wv, cmp_w1_k[l], cmp_w1_v[l],
                           jnp.broadcast_to(cmp_pos_k[l].reshape(1, -1), (8, CMP_LEN * HEAD_DIM)),
                           jnp.broadcast_to(cmp_pos_v[l].reshape(1, -1), (8, CMP_LEN * HEAD_DIM)),
                           w2k, w2v, crc, cr1, cr2)
        gn = out_norm_g[l].reshape(1, -1)
        o_a = _nsa_attention(u, ck, cv, small, gn[:, :NSA_HEADS * HEAD_DIM], cover, expand)
        o_b = _fox_attention(u, cq, ckt, gn[:, NSA_HEADS * HEAD_DIM:(NSA_HEADS + FOX_HEADS) * HEAD_DIM])
        o_c = _sb_attention(u, gn[:, (NSA_HEADS + FOX_HEADS) * HEAD_DIM:])

        wr = jnp.concatenate([router_group_w[l], router_expert_w[l],
                              jnp.zeros((d, LANES - N_GROUPS - N_EXPERTS), F32)], axis=1)
        wr_hi = wr.astype(BF16)
        wr_lo = (wr - wr_hi.astype(F32)).astype(BF16)
        br = jnp.concatenate([router_group_b[l], router_expert_b[l],
                              jnp.zeros((LANES - N_GROUPS - N_EXPERTS,), F32)]).reshape(1, LANES)
        x, h2, rw, ri = _out_projection(o_a, o_b, o_c, x, w_out[l].astype(BF16), g1, norm2_g[l],
                                        sc2, sh2, wr_hi, wr_lo, br)

        dest, buf_tok, tile_expert, n_used = _dispatch_plan(ri, t)
        xs = h2.reshape(t, d)[buf_tok]
        ys = _expert_mlp(xs, tile_expert, n_used, expert_w1[l], expert_w3[l], expert_w2[l])
        y0 = ys[dest[0::2]].reshape(b, s, d)
        y1 = ys[dest[1::2]].reshape(b, s, d)
        x = _combine(x, y0, y1, rw, g2, final_g, final=(l == depth - 1))
    return x
```

```python
import functools
import math

import numpy as np
import jax
import jax.numpy as jnp
from jax import lax
from jax.experimental import pallas as pl
from jax.experimental.pallas import tpu as pltpu

F32 = jnp.float32
BF16 = jnp.bfloat16

HEAD_DIM = 64
LANES = 128
N_HEADS = 16
NSA_HEADS = 8
NSA_KV = 2
NSA_GROUP = 4
FOX_HEADS = 4
SB_HEADS = 4
ROPE_DIM = 16
ROPE_HALF = 8
ROPE_THETA = 500000.0
CMP_LEN = 32
CMP_STRIDE = 16
CMP_HIDDEN = 128
SEL_LEN = 64
SEL_TOPN = 16
WINDOW = 512
FORCE_SCORE = 1.0e4
N_GROUPS = 4
EXPERTS_PER_GROUP = 8
N_EXPERTS = 32
EPS = 1e-6
NEG = -1e30

COL_QA = 0
COL_KS = 1024
COL_KW = 1152
ROPE_COLS = 1280
COL_VS = 1280
COL_VW = 1408
COL_FOX = 1536
COL_SB = 2560
N_MAIN = 3584
N_SMALL = 512
PROJ_CHUNK = 1280

TM_PROJ = 512
TQ_NSA = 128
TK_ATT = 256
TQ_PAIR = 256
TK_FOX = 512
ROWS = 32
TM_OUT = 512
TM_EXP = 256
TM_CMB = 512
VMEM_LIMIT = 56 * 1024 * 1024


def _cp(n_axes, vmem=VMEM_LIMIT):
    return pltpu.CompilerParams(dimension_semantics=("arbitrary",) * n_axes, vmem_limit_bytes=vmem)


def _dot(a, b):
    return jnp.dot(a, b, preferred_element_type=F32)


def _dot_nt(a, b):
    return lax.dot_general(a, b, (((1,), (1,)), ((), ())), preferred_element_type=F32)


def _split_bf16(x, parts):
    out = []
    r = x
    for _ in range(parts):
        p = r.astype(BF16)
        out.append(p)
        r = r - p.astype(F32)
    return out


def _dot_split(x, m, parts):
    acc = None
    for p in _split_bf16(x, parts):
        d = _dot(p, m)
        acc = d if acc is None else acc + d
    return acc


def _rope(x, c, s1, s2):
    return x * c + pltpu.roll(x, ROPE_HALF, 1) * s1 + pltpu.roll(x, LANES - ROPE_HALF, 1) * s2


def _softplus(z):
    return jnp.maximum(z, 0.0) + jnp.log(1.0 + jnp.exp(-jnp.abs(z)))


def _mod_kernel(c_ref, w_ref, b_ref, o_ref):
    c = c_ref[...]
    cond = c * (1.0 / (1.0 + jnp.exp(-c)))
    o_ref[0] = _dot(cond, w_ref[0]) + b_ref[0]


def _modulation(c, ada_w, ada_b):
    depth, d, n = ada_w.shape
    b = c.shape[0]
    tn = 1024
    return pl.pallas_call(
        _mod_kernel,
        grid=(depth, n // tn),
        in_specs=[pl.BlockSpec((b, d), lambda l, j: (0, 0)),
                  pl.BlockSpec((1, d, tn), lambda l, j: (l, 0, j)),
                  pl.BlockSpec((1, 1, tn), lambda l, j: (l, 0, j))],
        out_specs=pl.BlockSpec((1, b, tn), lambda l, j: (l, 0, j)),
        out_shape=jax.ShapeDtypeStruct((depth, b, n), F32),
        compiler_params=_cp(2),
        name="modulation",
    )(c, ada_w, ada_b.reshape(depth, 1, n))


def _inproj_kernel(x_ref, g_ref, sc_ref, sh_ref, w_ref, wc_ref, ws_ref, rc_ref, r1_ref, r2_ref,
                   u_ref, kc_ref, vc_ref, sm_ref):
    x = x_ref[0]
    ms = jnp.mean(x * x, axis=-1, keepdims=True)
    h = (x * lax.rsqrt(ms + EPS) * g_ref[...]) * (1.0 + sc_ref[0]) + sh_ref[0]
    hb = h.astype(BF16)
    rc, r1, r2 = rc_ref[...], r1_ref[...], r2_ref[...]
    for j in range(N_MAIN // PROJ_CHUNK + (1 if N_MAIN % PROJ_CHUNK else 0)):
        lo = j * PROJ_CHUNK
        hi = min(lo + PROJ_CHUNK, N_MAIN)
        acc = _dot(hb, w_ref[:, lo:hi])
        if lo < ROPE_COLS:
            for k in range((hi - lo) // LANES):
                blk = acc[:, k * LANES:(k + 1) * LANES]
                u_ref[0, :, lo + k * LANES:lo + (k + 1) * LANES] = _rope(blk, rc, r1, r2).astype(BF16)
        else:
            u_ref[0, :, lo:hi] = acc.astype(BF16)
    cmp_in = _dot(hb, wc_ref[...])
    kc_ref[0] = cmp_in[:, :LANES].astype(BF16)
    vc_ref[0] = cmp_in[:, LANES:].astype(BF16)
    sm_ref[0] = _dot(hb, ws_ref[...])


def _in_projection(x, g, sc, sh, w_main, w_cmp, w_small, rope_c, rope_1, rope_2):
    b, s, d = x.shape
    tm = min(TM_PROJ, s)
    row = lambda i, j: (i, j, 0)
    const2 = lambda i, j: (0, 0)
    per_b = lambda i, j: (i, 0, 0)
    seq = lambda i, j: (j, 0)
    return pl.pallas_call(
        _inproj_kernel,
        grid=(b, s // tm),
        in_specs=[pl.BlockSpec((1, tm, d), row),
                  pl.BlockSpec((1, d), const2),
                  pl.BlockSpec((1, 1, d), per_b),
                  pl.BlockSpec((1, 1, d), per_b),
                  pl.BlockSpec((d, N_MAIN), const2),
                  pl.BlockSpec((d, 2 * LANES), const2),
                  pl.BlockSpec((d, N_SMALL), const2),
                  pl.BlockSpec((tm, LANES), seq),
                  pl.BlockSpec((tm, LANES), seq),
                  pl.BlockSpec((tm, LANES), seq)],
        out_specs=[pl.BlockSpec((1, tm, N_MAIN), row),
                   pl.BlockSpec((1, tm, LANES), row),
                   pl.BlockSpec((1, tm, LANES), row),
                   pl.BlockSpec((1, tm, N_SMALL), row)],
        out_shape=[jax.ShapeDtypeStruct((b, s, N_MAIN), BF16),
                   jax.ShapeDtypeStruct((b, s, LANES), BF16),
                   jax.ShapeDtypeStruct((b, s, LANES), BF16),
                   jax.ShapeDtypeStruct((b, s, N_SMALL), F32)],
        compiler_params=_cp(2),
        name="in_projection",
    )(x, g.reshape(1, d), sc.reshape(b, 1, d), sh.reshape(b, 1, d), w_main, w_cmp, w_small,
      rope_c, rope_1, rope_2)


def _cumf_kernel(f_ref, b_ref, cq_ref, ckt_ref):
    n_chunks = f_ref.shape[1] // TK_ATT
    r = lax.broadcasted_iota(jnp.int32, (TK_ATT, TK_ATT), 0)
    c = lax.broadcasted_iota(jnp.int32, (TK_ATT, TK_ATT), 1)
    tri = jnp.where(c <= r, 1.0, 0.0).astype(BF16)
    carry = jnp.zeros((1, LANES), F32)
    for j in range(n_chunks):
        f = f_ref[0, j * TK_ATT:(j + 1) * TK_ATT, :] + b_ref[0]
        ls = -_softplus(-f)
        acc = None
        for p in _split_bf16(ls, 3):
            dd = _dot(tri, p)
            acc = dd if acc is None else acc + dd
        cs = acc + carry
        cq_ref[0, j * TK_ATT:(j + 1) * TK_ATT, :] = cs
        ckt_ref[0, 0, j] = cs.T[:8, :]
        carry = cs[TK_ATT - 1:TK_ATT, :]


def _forget_cumsum(small, b_pairs):
    b, s, _ = small.shape
    return pl.pallas_call(
        _cumf_kernel,
        grid=(b, 2),
        in_specs=[pl.BlockSpec((1, s, LANES), lambda i, p: (i, 0, 2 + p)),
                  pl.BlockSpec((1, 1, LANES), lambda i, p: (p, 0, 0))],
        out_specs=[pl.BlockSpec((1, s, LANES), lambda i, p: (i, 0, p)),
                   pl.BlockSpec((1, 1, s // TK_ATT, 8, TK_ATT), lambda i, p: (i, p, 0, 0, 0))],
        out_shape=[jax.ShapeDtypeStruct((b, s, 2 * LANES), F32),
                   jax.ShapeDtypeStruct((b, 2, s // TK_ATT, 8, TK_ATT), F32)],
        compiler_params=_cp(2),
        name="forget_cumsum",
    )(small, b_pairs)


def _compress_kernel(ks_ref, vs_ref, wk_ref, wv_ref, w1k_ref, w1v_ref, pek_ref, pev_ref,
                     w2k_ref, w2v_ref, rc_ref, r1_ref, r2_ref, ck_ref, cv_ref):
    def one(seg_ref, w_ref, w1_ref, pe_ref, w2_ref):
        p = _dot(seg_ref[0], w_ref[...])
        half = 2 * CMP_HIDDEN
        bias = _dot(pe_ref[...].astype(BF16), w1_ref[...].astype(BF16))[0:1, :]
        bias2 = jnp.concatenate([bias, bias], axis=1)
        n = p.shape[0]
        hid = p[:, :half] + pltpu.roll(p[:, half:], n - 1, 0) + bias2
        act = hid * (1.0 / (1.0 + jnp.exp(-hid)))
        return _dot(act.astype(BF16), w2_ref[...])

    ck = one(ks_ref, wk_ref, w1k_ref, pek_ref, w2k_ref)
    ck_ref[0] = _rope(ck, rc_ref[...], r1_ref[...], r2_ref[...]).astype(BF16)
    cv_ref[0] = one(vs_ref, wv_ref, w1v_ref, pev_ref, w2v_ref).astype(BF16)


def _compress(kc, vc, wk, wv, w1k, w1v, pek, pev, w2k, w2v, rc, r1, r2):
    b, s, _ = kc.shape
    n = s // CMP_STRIDE
    width = CMP_STRIDE * LANES
    kseg = kc.reshape(b, n, width)
    vseg = vc.reshape(b, n, width)
    seg = pl.BlockSpec((1, n, width), lambda i: (i, 0, 0))
    full = lambda a: pl.BlockSpec(a.shape, lambda i: (0,) * a.ndim)
    return pl.pallas_call(
        _compress_kernel,
        grid=(b,),
        in_specs=[seg, seg, full(wk), full(wv), full(w1k), full(w1v), full(pek), full(pev),
                  full(w2k), full(w2v), full(rc), full(r1), full(r2)],
        out_specs=[pl.BlockSpec((1, n, LANES), lambda i: (i, 0, 0)),
                   pl.BlockSpec((1, n, LANES), lambda i: (i, 0, 0))],
        out_shape=[jax.ShapeDtypeStruct((b, n, LANES), BF16),
                   jax.ShapeDtypeStruct((b, n, LANES), BF16)],
        compiler_params=_cp(1),
        name="nsa_compress",
    )(kseg, vseg, wk, wv, w1k, w1v, pek, pev, w2k, w2v, rc, r1, r2)


def _nsa_kernel(q_ref, ck_ref, cv_ref, ks_ref, vs_ref, kw_ref, vw_ref, gate_ref, gn_ref,
                cover_ref, expand_ref, o_ref, m_ref, l_ref, acc_ref):
    tq = q_ref.shape[1]
    g = pl.program_id(1)
    t0 = pl.program_id(2) * tq
    q = q_ref[0]
    qall = jnp.concatenate([q[:, h * LANES:(h + 1) * LANES] for h in range(NSA_GROUP)], axis=0)
    tpos = t0 + lax.broadcasted_iota(jnp.int32, (tq, 1), 0)
    lane = lax.broadcasted_iota(jnp.int32, (tq, LANES), 1)

    s = _dot_nt(qall, ck_ref[0]).reshape(NSA_GROUP, tq, LANES)
    valid = (CMP_STRIDE * lane + (CMP_LEN - 1)) <= tpos
    sm = jnp.where(valid[None], s, NEG)
    mx = jnp.max(sm, axis=-1, keepdims=True)
    e = jnp.where(valid[None], jnp.exp(sm - mx), 0.0)
    den = jnp.sum(e, axis=-1, keepdims=True)
    p_cmp = e * jnp.where(den > 0.0, 1.0 / den, 0.0)
    o_cmp = _dot(p_cmp.astype(BF16).reshape(NSA_GROUP * tq, LANES), cv_ref[0])
    o_cmp = o_cmp.reshape(NSA_GROUP, tq, LANES)

    p_sum = p_cmp[0] + p_cmp[1] + p_cmp[2] + p_cmp[3]
    imp = _dot_split(p_sum, cover_ref[...], 3)
    cur = jnp.right_shift(tpos, int(math.log2(SEL_LEN)))
    forced = (lane == 0) | (lane == cur) | (lane == cur - 1)
    causal_blk = (lane * SEL_LEN) <= tpos
    n_sel = expand_ref.shape[0] * (TK_FOX // SEL_LEN)
    score = jnp.where(forced, FORCE_SCORE, jnp.where(causal_blk, imp, -1.0))
    score = jnp.where(lane < n_sel, score, -2.0)
    cnt = jnp.zeros((tq, LANES), F32)
    for j in range(n_sel):
        cj = score[:, j:j + 1]
        beats = (cj > score) | ((cj == score) & (lane > j))
        cnt = cnt + jnp.where(beats, 1.0, 0.0)
    sel = jnp.where((cnt < float(min(SEL_TOPN, n_sel))) & (lane < n_sel), 1.0, 0.0).astype(BF16)

    def biased(bias):
        def adjust(h, t_off, cols):
            return [cols[kk] + bias[t_off:t_off + ROWS, kk * LANES:(kk + 1) * LANES]
                    for kk in range(len(cols))]
        return adjust

    kcol_s = lax.broadcasted_iota(jnp.int32, (tq, TK_FOX), 1)

    def sel_tile(kt, carry):
        k0 = pl.multiple_of(kt * TK_FOX, TK_FOX)
        hit = _dot(sel, expand_ref[kt])
        ok = (hit > 0.5) & ((k0 + kcol_s) <= tpos)
        s_t = _dot_nt(qall, ks_ref[0, pl.ds(k0, TK_FOX), :])
        _softmax_tile(s_t, vs_ref[0, pl.ds(k0, TK_FOX), :], m_ref, l_ref, acc_ref, NSA_GROUP, tq,
                      biased(jnp.where(ok, 0.0, NEG)))
        return carry

    _softmax_init(m_ref, l_ref, acc_ref)
    lax.fori_loop(0, lax.div(t0 + tq - 1, TK_FOX) + 1, sel_tile, 0)
    o_sel = _softmax_result(l_ref, acc_ref).reshape(NSA_GROUP, tq, LANES)

    kcol_w = lax.broadcasted_iota(jnp.int32, (tq, TK_ATT), 1)

    def win_tile(kt, carry):
        k0 = pl.multiple_of(kt * TK_ATT, TK_ATT)
        kp = k0 + kcol_w
        ok = (kp <= tpos) & (kp > tpos - WINDOW)
        s_t = _dot_nt(qall, kw_ref[0, pl.ds(k0, TK_ATT), :])
        _softmax_tile(s_t, vw_ref[0, pl.ds(k0, TK_ATT), :], m_ref, l_ref, acc_ref, NSA_GROUP, tq,
                      biased(jnp.where(ok, 0.0, NEG)))
        return carry

    _softmax_init(m_ref, l_ref, acc_ref)
    lo_w = lax.div(jnp.maximum(t0 - (WINDOW - 1), 0), TK_ATT)
    lax.fori_loop(lo_w, lax.div(t0 + tq - 1, TK_ATT) + 1, win_tile, 0)
    o_win = _softmax_result(l_ref, acc_ref).reshape(NSA_GROUP, tq, LANES)

    gt = gate_ref[0]
    gt = 1.0 / (1.0 + jnp.exp(-gt))
    mine = (lane >= g * HEAD_DIM) & (lane < (g + 1) * HEAD_DIM)
    outs = []
    for h in range(NSA_GROUP):
        o = (gt[:, 3 * h:3 * h + 1] * o_cmp[h] + gt[:, 3 * h + 1:3 * h + 2] * o_sel[h]
             + gt[:, 3 * h + 2:3 * h + 3] * o_win[h])
        o = jnp.where(mine, o, 0.0)
        ms = jnp.sum(o * o, axis=-1, keepdims=True) * (1.0 / HEAD_DIM)
        o = o * lax.rsqrt(ms + EPS)
        outs.append(o + pltpu.roll(o, HEAD_DIM, 1))
    left = lane < HEAD_DIM
    o_ref[0, :, :LANES] = (jnp.where(left, outs[0], outs[1]) * gn_ref[:, :LANES]).astype(o_ref.dtype)
    o_ref[0, :, LANES:] = (jnp.where(left, outs[2], outs[3]) * gn_ref[:, LANES:]).astype(o_ref.dtype)


def _nsa_attention(u, ck, cv, small, gn, cover, expand):
    b, s, _ = u.shape
    tq = min(TQ_NSA, s)
    n_cmp = ck.shape[1]
    blk = LANES
    kv = lambda col: pl.BlockSpec((1, s, LANES), lambda i, g, j, col=col: (i, 0, col // blk))
    return pl.pallas_call(
        _nsa_kernel,
        grid=(b, NSA_KV, s // tq),
        in_specs=[pl.BlockSpec((1, tq, NSA_GROUP * LANES), lambda i, g, j: (i, j, g)),
                  pl.BlockSpec((1, n_cmp, LANES), lambda i, g, j: (i, 0, 0)),
                  pl.BlockSpec((1, n_cmp, LANES), lambda i, g, j: (i, 0, 0)),
                  kv(COL_KS), kv(COL_VS), kv(COL_KW), kv(COL_VW),
                  pl.BlockSpec((1, tq, LANES), lambda i, g, j: (i, j, g)),
                  pl.BlockSpec((1, 2 * LANES), lambda i, g, j: (0, g)),
                  pl.BlockSpec(cover.shape, lambda i, g, j: (0, 0)),
                  pl.BlockSpec(expand.shape, lambda i, g, j: (0, 0, 0))],
        out_specs=pl.BlockSpec((1, tq, 2 * LANES), lambda i, g, j: (i, j, g)),
        out_shape=jax.ShapeDtypeStruct((b, s, NSA_HEADS * HEAD_DIM), BF16),
        scratch_shapes=_softmax_scratch(NSA_GROUP * tq),
        compiler_params=_cp(3),
        name="nsa_attention",
    )(u, ck, cv, u, u, u, u, small, gn, cover, expand)


def _pair_finish(acc, gn_ref, o_ref, tq):
    lane = lax.broadcasted_iota(jnp.int32, (tq, LANES), 1)
    left = lane < HEAD_DIM
    o = jnp.where(left, acc[0], acc[1])
    o2 = o * o
    ms_l = jnp.sum(jnp.where(left, o2, 0.0), axis=-1, keepdims=True) * (1.0 / HEAD_DIM)
    ms_r = jnp.sum(jnp.where(left, 0.0, o2), axis=-1, keepdims=True) * (1.0 / HEAD_DIM)
    inv = jnp.where(left, lax.rsqrt(ms_l + EPS), lax.rsqrt(ms_r + EPS))
    o_ref[0] = (o * inv * gn_ref[...]).astype(o_ref.dtype)


def _softmax_tile(s, v, m_ref, l_ref, acc_ref, heads, tq, adjust):
    nk = s.shape[1] // LANES
    p_rows = []
    for h in range(heads):
        for c in range(tq // ROWS):
            t_off = c * ROWS
            r0 = h * tq + t_off
            cols = [s[r0:r0 + ROWS, k * LANES:(k + 1) * LANES] for k in range(nk)]
            cols = adjust(h, t_off, cols)
            mx = cols[0]
            for k in range(1, nk):
                mx = jnp.maximum(mx, cols[k])
            m_old = m_ref[r0:r0 + ROWS, :]
            m_new = jnp.maximum(m_old, jnp.max(mx, axis=-1, keepdims=True))
            alpha = jnp.exp(m_old - m_new)
            pks = [jnp.exp(cols[k] - m_new) for k in range(nk)]
            psum = pks[0]
            for k in range(1, nk):
                psum = psum + pks[k]
            p_rows.append(jnp.concatenate([pk.astype(BF16) for pk in pks], axis=1))
            l_ref[r0:r0 + ROWS, :] = alpha * l_ref[r0:r0 + ROWS, :] + psum
            acc_ref[r0:r0 + ROWS, :] = alpha * acc_ref[r0:r0 + ROWS, :]
            m_ref[r0:r0 + ROWS, :] = m_new
    p = jnp.concatenate(p_rows, axis=0)
    acc_ref[...] += _dot(p, v)


def _softmax_init(m_ref, l_ref, acc_ref):
    m_ref[...] = jnp.full(m_ref.shape, NEG, F32)
    l_ref[...] = jnp.zeros(l_ref.shape, F32)
    acc_ref[...] = jnp.zeros(acc_ref.shape, F32)


def _softmax_result(l_ref, acc_ref):
    return acc_ref[...] / jnp.sum(l_ref[...], axis=-1, keepdims=True)


def _softmax_scratch(rows):
    return [pltpu.VMEM((rows, LANES), F32), pltpu.VMEM((rows, LANES), F32), pltpu.VMEM((rows, LANES), F32)]


def _fox_kernel(q_ref, k_ref, v_ref, cq_ref, ckt_ref, gn_ref, o_ref,
                m_ref, l_ref, acc_ref, cqr_ref):
    tq = q_ref.shape[1]
    tk = TK_FOX
    t0 = pl.program_id(2) * tq
    q = q_ref[0]
    qall = jnp.concatenate([q[:, :LANES], q[:, LANES:]], axis=0)
    cq = cq_ref[0]
    cqr_ref[0:tq, :] = jnp.broadcast_to(cq[:, 0:1], (tq, LANES))
    cqr_ref[tq:2 * tq, :] = jnp.broadcast_to(cq[:, 1:2], (tq, LANES))
    _softmax_init(m_ref, l_ref, acc_ref)
    n_tiles = lax.div(t0 + tq - 1, tk) + 1
    diag = (lax.broadcasted_iota(jnp.int32, (ROWS, LANES), 1)
            - lax.broadcasted_iota(jnp.int32, (ROWS, LANES), 0))

    def tile(kt, masked):
        k0 = pl.multiple_of(kt * tk, tk)
        s = _dot_nt(qall, k_ref[0, pl.ds(k0, tk), :])
        cks = [ckt_ref[0, 0, kt * (tk // TK_ATT) + j] for j in range(tk // TK_ATT)]

        def adjust(h, t_off, cols):
            out = []
            cqr = cqr_ref[h * tq + t_off:h * tq + t_off + ROWS, :]
            for kk in range(tk // LANES):
                lo = (kk * LANES) % TK_ATT
                ck = cks[(kk * LANES) // TK_ATT][h:h + 1, lo:lo + LANES]
                val = (cols[kk] - ck) + cqr
                if masked:
                    val = jnp.where(diag <= (t0 + t_off) - (k0 + kk * LANES), val, NEG)
                out.append(val)
            return out

        _softmax_tile(s, v_ref[0, pl.ds(k0, tk), :], m_ref, l_ref, acc_ref, 2, tq, adjust)

    def full_tile(kt, carry):
        tile(kt, False)
        return carry

    lax.fori_loop(0, n_tiles - 1, full_tile, 0)
    tile(n_tiles - 1, True)
    acc = _softmax_result(l_ref, acc_ref).reshape(2, tq, LANES)
    _pair_finish(acc, gn_ref, o_ref, tq)


def _fox_attention(u, cq, ckt, gn):
    b, s, _ = u.shape
    tq = min(TQ_PAIR, s)
    qb = COL_FOX // (2 * LANES)
    kb = (COL_FOX + FOX_HEADS * LANES) // LANES
    vb = kb + 2
    return pl.pallas_call(
        _fox_kernel,
        grid=(b, 2, s // tq),
        in_specs=[pl.BlockSpec((1, tq, 2 * LANES), lambda i, p, j: (i, j, qb + p)),
                  pl.BlockSpec((1, s, LANES), lambda i, p, j: (i, 0, kb + p)),
                  pl.BlockSpec((1, s, LANES), lambda i, p, j: (i, 0, vb + p)),
                  pl.BlockSpec((1, tq, LANES), lambda i, p, j: (i, j, p)),
                  pl.BlockSpec((1, 1, s // TK_ATT, 8, TK_ATT), lambda i, p, j: (i, p, 0, 0, 0)),
                  pl.BlockSpec((1, LANES), lambda i, p, j: (0, p))],
        out_specs=pl.BlockSpec((1, tq, LANES), lambda i, p, j: (i, j, p)),
        out_shape=jax.ShapeDtypeStruct((b, s, FOX_HEADS * HEAD_DIM), BF16),
        scratch_shapes=_softmax_scratch(2 * tq) + [pltpu.VMEM((2 * tq, LANES), F32)],
        compiler_params=_cp(3),
        name="fox_attention",
    )(u, u, u, cq, ckt, gn)


def _sb_kernel(q_ref, k_ref, v_ref, gn_ref, o_ref, rest_ref, acc_ref):
    tq = q_ref.shape[1]
    tk = TK_ATT
    nk = tk // LANES
    t0 = pl.program_id(2) * tq
    q = q_ref[0]
    qall = jnp.concatenate([q[:, :LANES], q[:, LANES:]], axis=0)
    r = lax.broadcasted_iota(jnp.int32, (tk, tk), 0)
    c = lax.broadcasted_iota(jnp.int32, (tk, tk), 1)
    upper = jnp.where(r >= c, 1.0, 0.0).astype(BF16)
    n_tiles = lax.div(t0 + tq - 1, tk) + 1
    diag = (lax.broadcasted_iota(jnp.int32, (ROWS, LANES), 1)
            - lax.broadcasted_iota(jnp.int32, (ROWS, LANES), 0))
    rest_ref[...] = jnp.zeros(rest_ref.shape, F32)
    acc_ref[...] = jnp.zeros(acc_ref.shape, F32)
    chunks = [(h * tq + cc * ROWS, cc * ROWS) for h in range(2) for cc in range(tq // ROWS)]

    def tile(kt, masked):
        k0 = pl.multiple_of(kt * tk, tk)
        z = _dot_nt(qall, k_ref[0, pl.ds(k0, tk), :])

        def strictly_before(t_off, kk):
            return diag < (t0 + t_off) - (k0 + kk * LANES)

        his, los = [], []
        for r0, t_off in chunks:
            hi_c, lo_c = [], []
            for kk in range(nk):
                l = -_softplus(z[r0:r0 + ROWS, kk * LANES:(kk + 1) * LANES])
                if masked:
                    l = jnp.where(strictly_before(t_off, kk), l, 0.0)
                hi = l.astype(BF16)
                hi_c.append(hi)
                lo_c.append((l - hi.astype(F32)).astype(BF16))
            his.append(jnp.concatenate(hi_c, axis=1))
            los.append(jnp.concatenate(lo_c, axis=1))
        cum = _dot(jnp.concatenate(his, axis=0), upper) + _dot(jnp.concatenate(los, axis=0), upper)

        a_rows = []
        for r0, t_off in chunks:
            rest = rest_ref[r0:r0 + ROWS, :]
            a_c = []
            for kk in range(nk):
                sl = (slice(r0, r0 + ROWS), slice(kk * LANES, (kk + 1) * LANES))
                a = jnp.exp(z[sl] + cum[sl] + rest)
                if masked:
                    a = jnp.where(strictly_before(t_off, kk), a, 0.0)
                a_c.append(a.astype(BF16))
            a_rows.append(jnp.concatenate(a_c, axis=1))
            rest_ref[r0:r0 + ROWS, :] = rest + jnp.broadcast_to(cum[r0:r0 + ROWS, 0:1], (ROWS, LANES))
        acc_ref[...] += _dot(jnp.concatenate(a_rows, axis=0), v_ref[0, pl.ds(k0, tk), :])

    tile(n_tiles - 1, True)

    def full_tile(i, carry):
        tile(n_tiles - 2 - i, False)
        return carry

    lax.fori_loop(0, n_tiles - 1, full_tile, 0)
    _pair_finish(acc_ref[...].reshape(2, tq, LANES), gn_ref, o_ref, tq)


def _sb_attention(u, gn):
    b, s, _ = u.shape
    tq = min(TQ_PAIR, s)
    qb = COL_SB // (2 * LANES)
    kb = (COL_SB + SB_HEADS * LANES) // LANES
    vb = kb + 2
    return pl.pallas_call(
        _sb_kernel,
        grid=(b, 2, s // tq),
        in_specs=[pl.BlockSpec((1, tq, 2 * LANES), lambda i, p, j: (i, j, qb + p)),
                  pl.BlockSpec((1, s, LANES), lambda i, p, j: (i, 0, kb + p)),
                  pl.BlockSpec((1, s, LANES), lambda i, p, j: (i, 0, vb + p)),
                  pl.BlockSpec((1, LANES), lambda i, p, j: (0, p))],
        out_specs=pl.BlockSpec((1, tq, LANES), lambda i, p, j: (i, j, p)),
        out_shape=jax.ShapeDtypeStruct((b, s, SB_HEADS * HEAD_DIM), BF16),
        scratch_shapes=[pltpu.VMEM((2 * tq, LANES), F32), pltpu.VMEM((2 * tq, LANES), F32)],
        compiler_params=_cp(3),
        name="sb_attention",
    )(u, u, u, gn)


def _outproj_kernel(oa_ref, ob_ref, oc_ref, x_ref, w_ref, g1_ref, n2_ref, sc_ref, sh_ref,
                    wrh_ref, wrl_ref, br_ref, xo_ref, h_ref, rw_ref, ri_ref):
    na = oa_ref.shape[2]
    nb = ob_ref.shape[2]
    y = _dot(oa_ref[0], w_ref[0:na, :])
    y = y + _dot(ob_ref[0], w_ref[na:na + nb, :])
    y = y + _dot(oc_ref[0], w_ref[na + nb:, :])
    x = x_ref[0] + g1_ref[0] * y
    xo_ref[0] = x
    ms = jnp.mean(x * x, axis=-1, keepdims=True)
    h = (x * lax.rsqrt(ms + EPS) * n2_ref[...]) * (1.0 + sc_ref[0]) + sh_ref[0]
    hb = h.astype(BF16)
    h_ref[0] = hb
    hl = (h - hb.astype(F32)).astype(BF16)
    logit = _dot(hb, wrh_ref[...]) + _dot(hl, wrh_ref[...]) + _dot(hb, wrl_ref[...]) + br_ref[...]

    tm = logit.shape[0]
    lane = lax.broadcasted_iota(jnp.int32, (tm, LANES), 1).astype(F32)
    big = float(LANES)
    is_g = lane < N_GROUPS
    lg = jnp.where(is_g, logit, NEG)
    mg = jnp.max(lg, axis=-1, keepdims=True)
    zg = jnp.sum(jnp.where(is_g, jnp.exp(lg - mg), 0.0), axis=-1, keepdims=True)
    pg = 1.0 / zg
    gi = jnp.min(jnp.where(is_g & (lg == mg), lane, big), axis=-1, keepdims=True)
    e_lane = lane - N_GROUPS
    in_grp = (e_lane >= gi * EXPERTS_PER_GROUP) & (e_lane < (gi + 1) * EXPERTS_PER_GROUP)
    le = jnp.where(in_grp, logit, NEG)
    m1 = jnp.max(le, axis=-1, keepdims=True)
    i1 = jnp.min(jnp.where(in_grp & (le == m1), lane, big), axis=-1, keepdims=True)
    rest = in_grp & (lane != i1)
    le2 = jnp.where(rest, logit, NEG)
    m2 = jnp.max(le2, axis=-1, keepdims=True)
    i2 = jnp.min(jnp.where(rest & (le2 == m2), lane, big), axis=-1, keepdims=True)
    ze = jnp.sum(jnp.where(in_grp, jnp.exp(le - m1), 0.0), axis=-1, keepdims=True)
    p1 = 1.0 / ze
    p2 = jnp.exp(m2 - m1) / ze
    den = p1 + p2
    w1 = pg * (p1 / den)
    w2 = pg * (p2 / den)
    rw_ref[0] = jnp.where(lane == 0.0, w1, jnp.where(lane == 1.0, w2, 0.0))
    ri_ref[0] = jnp.where(lane == 0.0, i1 - N_GROUPS, jnp.where(lane == 1.0, i2 - N_GROUPS, 0.0)).astype(jnp.int32)


def _out_projection(oa, ob, oc, x, w_out, g1, n2, sc, sh, wr_hi, wr_lo, br):
    b, s, d = x.shape
    tm = min(TM_OUT, s)
    row = lambda i, j: (i, j, 0)
    const2 = lambda i, j: (0, 0)
    per_b = lambda i, j: (i, 0, 0)
    return pl.pallas_call(
        _outproj_kernel,
        grid=(b, s // tm),
        in_specs=[pl.BlockSpec((1, tm, oa.shape[2]), row),
                  pl.BlockSpec((1, tm, ob.shape[2]), row),
                  pl.BlockSpec((1, tm, oc.shape[2]), row),
                  pl.BlockSpec((1, tm, d), row),
                  pl.BlockSpec(w_out.shape, const2),
                  pl.BlockSpec((1, 1, d), per_b),
                  pl.BlockSpec((1, d), const2),
                  pl.BlockSpec((1, 1, d), per_b),
                  pl.BlockSpec((1, 1, d), per_b),
                  pl.BlockSpec((d, LANES), const2),
                  pl.BlockSpec((d, LANES), const2),
                  pl.BlockSpec((1, LANES), const2)],
        out_specs=[pl.BlockSpec((1, tm, d), row),
                   pl.BlockSpec((1, tm, d), row),
                   pl.BlockSpec((1, tm, LANES), row),
                   pl.BlockSpec((1, tm, LANES), row)],
        out_shape=[jax.ShapeDtypeStruct((b, s, d), F32),
                   jax.ShapeDtypeStruct((b, s, d), BF16),
                   jax.ShapeDtypeStruct((b, s, LANES), F32),
                   jax.ShapeDtypeStruct((b, s, LANES), jnp.int32)],
        compiler_params=_cp(2),
        name="out_projection",
    )(oa, ob, oc, x, w_out, g1.reshape(b, 1, d), n2.reshape(1, d), sc.reshape(b, 1, d),
      sh.reshape(b, 1, d), wr_hi, wr_lo, br)


def _expert_kernel(te_ref, nu_ref, x_ref, w1_ref, w3_ref, w2_ref, y_ref):
    i = pl.program_id(0)

    @pl.when(i < nu_ref[0])
    def _():
        x = x_ref[...]
        a = _dot(x, w1_ref[0].astype(BF16))
        g = _dot(x, w3_ref[0].astype(BF16))
        act = (a * (1.0 / (1.0 + jnp.exp(-a)))) * g
        y_ref[...] = _dot(act.astype(BF16), w2_ref[0].astype(BF16))

    @pl.when(i >= nu_ref[0])
    def _():
        y_ref[...] = jnp.zeros_like(y_ref)


def _expert_mlp(xs, tile_expert, n_used, w1, w3, w2):
    p_rows, d = xs.shape
    de = w1.shape[2]
    n_tiles = p_rows // TM_EXP
    grid_spec = pltpu.PrefetchScalarGridSpec(
        num_scalar_prefetch=2,
        grid=(n_tiles,),
        in_specs=[pl.BlockSpec((TM_EXP, d), lambda i, te, nu: (i, 0)),
                  pl.BlockSpec((1, d, de), lambda i, te, nu: (te[i], 0, 0)),
                  pl.BlockSpec((1, d, de), lambda i, te, nu: (te[i], 0, 0)),
                  pl.BlockSpec((1, de, d), lambda i, te, nu: (te[i], 0, 0))],
        out_specs=pl.BlockSpec((TM_EXP, d), lambda i, te, nu: (i, 0)),
    )
    return pl.pallas_call(
        _expert_kernel,
        grid_spec=grid_spec,
        out_shape=jax.ShapeDtypeStruct((p_rows, d), F32),
        compiler_params=_cp(1),
        name="expert_mlp",
    )(tile_expert, n_used, xs, w1, w3, w2)


def _combine_kernel(x_ref, y0_ref, y1_ref, rw_ref, g2_ref, fg_ref, o_ref, *, final):
    rw = rw_ref[0]
    moe = y0_ref[0] * rw[:, 0:1] + y1_ref[0] * rw[:, 1:2]
    x = x_ref[0] + g2_ref[0] * moe
    if final:
        ms = jnp.mean(x * x, axis=-1, keepdims=True)
        x = x * lax.rsqrt(ms + EPS) * fg_ref[...]
    o_ref[0] = x


def _combine(x, y0, y1, rw, g2, final_g, final):
    b, s, d = x.shape
    tm = min(TM_CMB, s)
    row = lambda i, j: (i, j, 0)
    return pl.pallas_call(
        functools.partial(_combine_kernel, final=final),
        grid=(b, s // tm),
        in_specs=[pl.BlockSpec((1, tm, d), row),
                  pl.BlockSpec((1, tm, d), row),
                  pl.BlockSpec((1, tm, d), row),
                  pl.BlockSpec((1, tm, LANES), row),
                  pl.BlockSpec((1, 1, d), lambda i, j: (i, 0, 0)),
                  pl.BlockSpec((1, d), lambda i, j: (0, 0))],
        out_specs=pl.BlockSpec((1, tm, d), row),
        out_shape=jax.ShapeDtypeStruct((b, s, d), F32),
        compiler_params=_cp(2),
        name="moe_combine_final" if final else "moe_combine",
    )(x, y0, y1, rw, g2.reshape(b, 1, d), final_g.reshape(1, d))


def _pad_heads(w, n_heads, offsets):
    d = w.shape[0]
    w = w.reshape(d, n_heads, HEAD_DIM)
    z = jnp.zeros((d, n_heads, HEAD_DIM), w.dtype)
    off = jnp.asarray(offsets, jnp.int32).reshape(1, n_heads, 1)
    blk = jnp.where(off == 0, jnp.concatenate([w, z], axis=-1), jnp.concatenate([z, w], axis=-1))
    return blk.reshape(d, n_heads * LANES)


def _layout_w_in(w_in):
    d = w_in.shape[0]
    kvw = NSA_KV * HEAD_DIM
    sizes = (NSA_HEADS * HEAD_DIM, kvw, kvw, kvw, kvw, kvw, kvw, NSA_HEADS * 3,
             FOX_HEADS * HEAD_DIM, FOX_HEADS * HEAD_DIM, FOX_HEADS * HEAD_DIM, FOX_HEADS,
             SB_HEADS * HEAD_DIM, SB_HEADS * HEAD_DIM, SB_HEADS * HEAD_DIM)
    pts = np.cumsum(sizes)[:-1].tolist()
    (qa, kca, vca, ksa, vsa, kwa, vwa, ga, qb, kb, vb, fb, qc, kc, vc) = jnp.split(w_in, pts, axis=1)
    scale = HEAD_DIM ** -0.5
    qa_p = _pad_heads(qa * scale, NSA_HEADS, [0] * NSA_GROUP + [HEAD_DIM] * NSA_GROUP)
    qb_p = _pad_heads(qb * scale, FOX_HEADS, [0, HEAD_DIM, 0, HEAD_DIM])
    qc_p = _pad_heads(qc * scale, SB_HEADS, [0, HEAD_DIM, 0, HEAD_DIM])
    main = jnp.concatenate([qa_p, ksa, kwa, vsa, vwa, qb_p, kb, vb, qc_p, kc, vc], axis=1).astype(BF16)
    cmp_w = jnp.concatenate([kca, vca], axis=1).astype(BF16)
    zpad = lambda n: jnp.zeros((d, n), w_in.dtype)
    per_grp = NSA_GROUP * 3
    small = jnp.concatenate([ga[:, :per_grp], zpad(LANES - per_grp), ga[:, per_grp:], zpad(LANES - per_grp),
                             fb[:, 0:2], zpad(LANES - 2), fb[:, 2:4], zpad(LANES - 2)], axis=1).astype(BF16)
    return main, cmp_w, small


def _layout_cmp(w1, w2):
    hid = w1.shape[1]
    w1r = w1.reshape(2, CMP_STRIDE, HEAD_DIM, hid)
    z = jnp.zeros((CMP_STRIDE, HEAD_DIM, hid), w1.dtype)
    cols = []
    for half in range(2):
        for g in range(NSA_KV):
            parts = [w1r[half] if gg == g else z for gg in range(NSA_KV)]
            cols.append(jnp.concatenate(parts, axis=1).reshape(CMP_STRIDE * LANES, hid))
    wcat = jnp.concatenate(cols, axis=1).astype(BF16)
    zz = jnp.zeros_like(w2)
    w2bd = jnp.concatenate([jnp.concatenate([w2, zz], axis=1),
                            jnp.concatenate([zz, w2], axis=1)], axis=0).astype(BF16)
    return wcat, w2bd


def _rope_tables(pos):
    inv = jnp.exp(jnp.arange(ROPE_HALF, dtype=F32) * (-2.0 * math.log(ROPE_THETA) / ROPE_DIM))
    ang = pos.astype(F32)[:, None] * inv[None, :]
    cos, sin = jnp.cos(ang), jnp.sin(ang)
    n = pos.shape[0]
    z8 = jnp.zeros((n, ROPE_HALF), F32)
    rest1 = jnp.ones((n, HEAD_DIM - ROPE_DIM), F32)
    rest0 = jnp.zeros((n, HEAD_DIM - ROPE_DIM), F32)
    c = jnp.concatenate([cos, cos, rest1], axis=1)
    s1 = jnp.concatenate([z8, sin, rest0], axis=1)
    s2 = jnp.concatenate([-sin, z8, rest0], axis=1)
    dup = lambda a: jnp.concatenate([a, a], axis=1)
    return dup(c), dup(s1), dup(s2)


def _static_tables(s):
    n_cmp_pad = s // CMP_STRIDE
    n = np.arange(n_cmp_pad)[:, None]
    j = np.arange(LANES)[None, :]
    n_sel = s // SEL_LEN
    cover = ((n * CMP_STRIDE < j * SEL_LEN + SEL_LEN) & (n * CMP_STRIDE + CMP_LEN > j * SEL_LEN)
             & (j < n_sel)).astype(np.float32)
    nt = s // TK_FOX
    key = np.arange(nt)[:, None, None] * TK_FOX + np.arange(TK_FOX)[None, None, :]
    expand = (key // SEL_LEN == np.arange(LANES)[None, :, None]).astype(np.float32)
    return jnp.asarray(cover, BF16), jnp.asarray(expand, BF16)


def _dispatch_plan(ri, t):
    eid = ri.reshape(t, LANES)[:, :2].reshape(-1)
    n_assign = eid.shape[0]
    onehot = (eid[:, None] == jnp.arange(N_EXPERTS, dtype=jnp.int32)[None, :]).astype(jnp.int32)
    csum = jnp.cumsum(onehot, axis=0)
    counts = csum[-1]
    rank = jnp.take_along_axis(csum, eid[:, None], axis=1)[:, 0] - 1
    padded = ((counts + TM_EXP - 1) // TM_EXP) * TM_EXP
    pends = jnp.cumsum(padded)
    pstarts = pends - padded
    dest = pstarts[eid] + rank
    n_tiles = -(-(n_assign + N_EXPERTS * (TM_EXP - 1)) // TM_EXP)
    tile_start = jnp.arange(n_tiles, dtype=jnp.int32) * TM_EXP
    tile_expert = jnp.minimum(jnp.sum((pends[None, :] <= tile_start[:, None]).astype(jnp.int32), axis=1),
                              N_EXPERTS - 1).astype(jnp.int32)
    n_used = (pends[-1] // TM_EXP).astype(jnp.int32).reshape(1)
    tok = jnp.arange(n_assign, dtype=jnp.int32) // 2
    buf_tok = jnp.zeros((n_tiles * TM_EXP,), jnp.int32).at[dest].set(tok)
    return dest, buf_tok, tile_expert, n_used


def kernel(x, c, norm1_g, norm2_g, ada_w, ada_b, w_in, b_forget, cmp_pos_k, cmp_w1_k, cmp_w2_k,
           cmp_pos_v, cmp_w1_v, cmp_w2_v, out_norm_g, w_out, router_group_w, router_group_b,
           router_expert_w, router_expert_b, expert_w1, expert_w3, expert_w2, final_g):
    b, s, d = x.shape
    depth = ada_w.shape[0]
    t = b * s
    mod = _modulation(c, ada_w, ada_b)
    rope_c, rope_1, rope_2 = _rope_tables(jnp.arange(s))
    n_cmp_pad = s // CMP_STRIDE
    crc, cr1, cr2 = _rope_tables(jnp.arange(n_cmp_pad) * CMP_STRIDE + (CMP_LEN - 1))
    cover, expand = _static_tables(s)

    for l in range(depth):
        sh1, sc1, g1, sh2, sc2, g2 = [mod[l][:, i * d:(i + 1) * d] for i in range(6)]
        w_main, w_cmp, w_small = _layout_w_in(w_in[l])
        u, kc, vc, small = _in_projection(x, norm1_g[l], sc1, sh1, w_main, w_cmp, w_small,
                                          rope_c, rope_1, rope_2)
        bf = b_forget[l]
        zf = jnp.zeros((LANES - 2,), F32)
        b_pairs = jnp.stack([jnp.concatenate([bf[0:2], zf]), jnp.concatenate([bf[2:4], zf])]).reshape(2, 1, LANES)
        cq, ckt = _forget_cumsum(small, b_pairs)
        wk, w2k = _layout_cmp(cmp_w1_k[l], cmp_w2_k[l])
        wv, w2v = _layout_cmp(cmp_w1_v[l], cmp_w2_v[l])
        ck, cv = _compress(kc, vc, wk, wv, cmp_w1_k[l], cmp_w1_v[l],
                           jnp.broadcast_to(cmp_pos_k[l].reshape(1, -1), (8, CMP_LEN * HEAD_DIM)),
                           jnp.broadcast_to(cmp_pos_v[l].reshape(1, -1), (8, CMP_LEN * HEAD_DIM)),
                           w2k, w2v, crc, cr1, cr2)
        gn = out_norm_g[l].reshape(1, -1)
        o_a = _nsa_attention(u, ck, cv, small, gn[:, :NSA_HEADS * HEAD_DIM], cover, expand)
        o_b = _fox_attention(u, cq, ckt, gn[:, NSA_HEADS * HEAD_DIM:(NSA_HEADS + FOX_HEADS) * HEAD_DIM])
        o_c = _sb_attention(u, gn[:, (NSA_HEADS + FOX_HEADS) * HEAD_DIM:])

        wr = jnp.concatenate([router_group_w[l], router_expert_w[l],
                              jnp.zeros((d, LANES - N_GROUPS - N_EXPERTS), F32)], axis=1)
        wr_hi = wr.astype(BF16)
        wr_lo = (wr - wr_hi.astype(F32)).astype(BF16)
        br = jnp.concatenate([router_group_b[l], router_expert_b[l],
                              jnp.zeros((LANES - N_GROUPS - N_EXPERTS,), F32)]).reshape(1, LANES)
        x, h2, rw, ri = _out_projection(o_a, o_b, o_c, x, w_out[l].astype(BF16), g1, norm2_g[l],
                                        sc2, sh2, wr_hi, wr_lo, br)

        dest, buf_tok, tile_expert, n_used = _dispatch_plan(ri, t)
        xs = h2.reshape(t, d)[buf_tok]
        ys = _expert_mlp(xs, tile_expert, n_used, expert_w1[l], expert_w3[l], expert_w2[l])
        y0 = ys[dest[0::2]].reshape(b, s, d)
        y1 = ys[dest[1::2]].reshape(b, s, d)
        x = _combine(x, y0, y1, rw, g2, final_g, final=(l == depth - 1))
    return x
```

```python
import functools
import math

import numpy as np
import jax
import jax.numpy as jnp
from jax import lax
from jax.experimental import pallas as pl
from jax.experimental.pallas import tpu as pltpu

F32 = jnp.float32
BF16 = jnp.bfloat16

HEAD_DIM = 64
LANES = 128
N_HEADS = 16
NSA_HEADS = 8
NSA_KV = 2
NSA_GROUP = 4
FOX_HEADS = 4
SB_HEADS = 4
ROPE_DIM = 16
ROPE_HALF = 8
ROPE_THETA = 500000.0
CMP_LEN = 32
CMP_STRIDE = 16
CMP_HIDDEN = 128
SEL_LEN = 64
SEL_TOPN = 16
WINDOW = 512
FORCE_SCORE = 1.0e4
N_GROUPS = 4
EXPERTS_PER_GROUP = 8
N_EXPERTS = 32
EPS = 1e-6
NEG = -1e30

COL_QA = 0
COL_KS = 1024
COL_KW = 1152
ROPE_COLS = 1280
COL_VS = 1280
COL_VW = 1408
COL_FOX = 1536
COL_SB = 2560
N_MAIN = 3584
N_SMALL = 512
PROJ_CHUNK = 1280

TM_PROJ = 512
TQ_NSA = 128
TK_ATT = 256
TQ_PAIR = 256
TK_FOX = 512
ROWS = 32
TM_OUT = 512
TM_EXP = 256
TM_CMB = 512
COPY_CHUNK = 512
VMEM_LIMIT = 56 * 1024 * 1024


def _cp(n_axes, vmem=VMEM_LIMIT):
    return pltpu.CompilerParams(dimension_semantics=("arbitrary",) * n_axes, vmem_limit_bytes=vmem)


def _dot(a, b):
    return jnp.dot(a, b, preferred_element_type=F32)


def _dot_nt(a, b):
    return lax.dot_general(a, b, (((1,), (1,)), ((), ())), preferred_element_type=F32)


def _split_bf16(x, parts):
    out = []
    r = x
    for _ in range(parts):
        p = r.astype(BF16)
        out.append(p)
        r = r - p.astype(F32)
    return out


def _dot_split(x, m, parts):
    acc = None
    for p in _split_bf16(x, parts):
        d = _dot(p, m)
        acc = d if acc is None else acc + d
    return acc


def _rope(x, c, s1, s2):
    return x * c + pltpu.roll(x, ROPE_HALF, 1) * s1 + pltpu.roll(x, LANES - ROPE_HALF, 1) * s2


def _softplus(z):
    return jnp.maximum(z, 0.0) + jnp.log(1.0 + jnp.exp(-jnp.abs(z)))


def _mod_kernel(c_ref, w_ref, b_ref, o_ref):
    c = c_ref[...]
    cond = c * (1.0 / (1.0 + jnp.exp(-c)))
    o_ref[0] = _dot(cond, w_ref[0]) + b_ref[0]


def _modulation(c, ada_w, ada_b):
    depth, d, n = ada_w.shape
    b = c.shape[0]
    tn = 1024
    return pl.pallas_call(
        _mod_kernel,
        grid=(depth, n // tn),
        in_specs=[pl.BlockSpec((b, d), lambda l, j: (0, 0)),
                  pl.BlockSpec((1, d, tn), lambda l, j: (l, 0, j)),
                  pl.BlockSpec((1, 1, tn), lambda l, j: (l, 0, j))],
        out_specs=pl.BlockSpec((1, b, tn), lambda l, j: (l, 0, j)),
        out_shape=jax.ShapeDtypeStruct((depth, b, n), F32),
        compiler_params=_cp(2),
        name="modulation",
    )(c, ada_w, ada_b.reshape(depth, 1, n))


def _inproj_kernel(x_ref, g_ref, sc_ref, sh_ref, w_ref, wc_ref, ws_ref, rc_ref, r1_ref, r2_ref,
                   u_ref, kc_ref, vc_ref, sm_ref):
    x = x_ref[0]
    ms = jnp.mean(x * x, axis=-1, keepdims=True)
    h = (x * lax.rsqrt(ms + EPS) * g_ref[...]) * (1.0 + sc_ref[0]) + sh_ref[0]
    hb = h.astype(BF16)
    rc, r1, r2 = rc_ref[...], r1_ref[...], r2_ref[...]
    for j in range(N_MAIN // PROJ_CHUNK + (1 if N_MAIN % PROJ_CHUNK else 0)):
        lo = j * PROJ_CHUNK
        hi = min(lo + PROJ_CHUNK, N_MAIN)
        acc = _dot(hb, w_ref[:, lo:hi])
        if lo < ROPE_COLS:
            for k in range((hi - lo) // LANES):
                blk = acc[:, k * LANES:(k + 1) * LANES]
                u_ref[0, :, lo + k * LANES:lo + (k + 1) * LANES] = _rope(blk, rc, r1, r2).astype(BF16)
        else:
            u_ref[0, :, lo:hi] = acc.astype(BF16)
    cmp_in = _dot(hb, wc_ref[...])
    kc_ref[0] = cmp_in[:, :LANES].astype(BF16)
    vc_ref[0] = cmp_in[:, LANES:].astype(BF16)
    sm_ref[0] = _dot(hb, ws_ref[...])


def _in_projection(x, g, sc, sh, w_main, w_cmp, w_small, rope_c, rope_1, rope_2):
    b, s, d = x.shape
    tm = min(TM_PROJ, s)
    row = lambda i, j: (i, j, 0)
    const2 = lambda i, j: (0, 0)
    per_b = lambda i, j: (i, 0, 0)
    seq = lambda i, j: (j, 0)
    return pl.pallas_call(
        _inproj_kernel,
        grid=(b, s // tm),
        in_specs=[pl.BlockSpec((1, tm, d), row),
                  pl.BlockSpec((1, d), const2),
                  pl.BlockSpec((1, 1, d), per_b),
                  pl.BlockSpec((1, 1, d), per_b),
                  pl.BlockSpec((d, N_MAIN), const2),
                  pl.BlockSpec((d, 2 * LANES), const2),
                  pl.BlockSpec((d, N_SMALL), const2),
                  pl.BlockSpec((tm, LANES), seq),
                  pl.BlockSpec((tm, LANES), seq),
                  pl.BlockSpec((tm, LANES), seq)],
        out_specs=[pl.BlockSpec((1, tm, N_MAIN), row),
                   pl.BlockSpec((1, tm, LANES), row),
                   pl.BlockSpec((1, tm, LANES), row),
                   pl.BlockSpec((1, tm, N_SMALL), row)],
        out_shape=[jax.ShapeDtypeStruct((b, s, N_MAIN), BF16),
                   jax.ShapeDtypeStruct((b, s, LANES), BF16),
                   jax.ShapeDtypeStruct((b, s, LANES), BF16),
                   jax.ShapeDtypeStruct((b, s, N_SMALL), F32)],
        compiler_params=_cp(2),
        name="in_projection",
    )(x, g.reshape(1, d), sc.reshape(b, 1, d), sh.reshape(b, 1, d), w_main, w_cmp, w_small,
      rope_c, rope_1, rope_2)


def _cumf_kernel(f_ref, b_ref, cq_ref, ckt_ref):
    n_chunks = f_ref.shape[1] // TK_ATT
    r = lax.broadcasted_iota(jnp.int32, (TK_ATT, TK_ATT), 0)
    c = lax.broadcasted_iota(jnp.int32, (TK_ATT, TK_ATT), 1)
    tri = jnp.where(c <= r, 1.0, 0.0).astype(BF16)
    carry = jnp.zeros((1, LANES), F32)
    for j in range(n_chunks):
        f = f_ref[0, j * TK_ATT:(j + 1) * TK_ATT, :] + b_ref[0]
        ls = -_softplus(-f)
        acc = None
        for p in _split_bf16(ls, 3):
            dd = _dot(tri, p)
            acc = dd if acc is None else acc + dd
        cs = acc + carry
        cq_ref[0, j * TK_ATT:(j + 1) * TK_ATT, :] = cs
        ckt_ref[0, 0, j] = cs.T[:8, :]
        carry = cs[TK_ATT - 1:TK_ATT, :]


def _forget_cumsum(small, b_pairs):
    b, s, _ = small.shape
    return pl.pallas_call(
        _cumf_kernel,
        grid=(b, 2),
        in_specs=[pl.BlockSpec((1, s, LANES), lambda i, p: (i, 0, 2 + p)),
                  pl.BlockSpec((1, 1, LANES), lambda i, p: (p, 0, 0))],
        out_specs=[pl.BlockSpec((1, s, LANES), lambda i, p: (i, 0, p)),
                   pl.BlockSpec((1, 1, s // TK_ATT, 8, TK_ATT), lambda i, p: (i, p, 0, 0, 0))],
        out_shape=[jax.ShapeDtypeStruct((b, s, 2 * LANES), F32),
                   jax.ShapeDtypeStruct((b, 2, s // TK_ATT, 8, TK_ATT), F32)],
        compiler_params=_cp(2),
        name="forget_cumsum",
    )(small, b_pairs)


def _compress_kernel(ks_ref, vs_ref, wk_ref, wv_ref, w1k_ref, w1v_ref, pek_ref, pev_ref,
                     w2k_ref, w2v_ref, rc_ref, r1_ref, r2_ref, ck_ref, cv_ref):
    def one(seg_ref, w_ref, w1_ref, pe_ref, w2_ref):
        p = _dot(seg_ref[0], w_ref[...])
        half = 2 * CMP_HIDDEN
        bias = _dot(pe_ref[...].astype(BF16), w1_ref[...].astype(BF16))[0:1, :]
        bias2 = jnp.concatenate([bias, bias], axis=1)
        n = p.shape[0]
        hid = p[:, :half] + pltpu.roll(p[:, half:], n - 1, 0) + bias2
        act = hid * (1.0 / (1.0 + jnp.exp(-hid)))
        return _dot(act.astype(BF16), w2_ref[...])

    ck = one(ks_ref, wk_ref, w1k_ref, pek_ref, w2k_ref)
    ck_ref[0] = _rope(ck, rc_ref[...], r1_ref[...], r2_ref[...]).astype(BF16)
    cv_ref[0] = one(vs_ref, wv_ref, w1v_ref, pev_ref, w2v_ref).astype(BF16)


def _compress(kc, vc, wk, wv, w1k, w1v, pek, pev, w2k, w2v, rc, r1, r2):
    b, s, _ = kc.shape
    n = s // CMP_STRIDE
    width = CMP_STRIDE * LANES
    kseg = kc.reshape(b, n, width)
    vseg = vc.reshape(b, n, width)
    seg = pl.BlockSpec((1, n, width), lambda i: (i, 0, 0))
    full = lambda a: pl.BlockSpec(a.shape, lambda i: (0,) * a.ndim)
    return pl.pallas_call(
        _compress_kernel,
        grid=(b,),
        in_specs=[seg, seg, full(wk), full(wv), full(w1k), full(w1v), full(pek), full(pev),
                  full(w2k), full(w2v), full(rc), full(r1), full(r2)],
        out_specs=[pl.BlockSpec((1, n, LANES), lambda i: (i, 0, 0)),
                   pl.BlockSpec((1, n, LANES), lambda i: (i, 0, 0))],
        out_shape=[jax.ShapeDtypeStruct((b, n, LANES), BF16),
                   jax.ShapeDtypeStruct((b, n, LANES), BF16)],
        compiler_params=_cp(1),
        name="nsa_compress",
    )(kseg, vseg, wk, wv, w1k, w1v, pek, pev, w2k, w2v, rc, r1, r2)


def _nsa_kernel(q_ref, ck_ref, cv_ref, ks_ref, vs_ref, kw_ref, vw_ref, gate_ref, gn_ref,
                cover_ref, expand_ref, o_ref, m_ref, l_ref, acc_ref):
    tq = q_ref.shape[1]
    g = pl.program_id(1)
    t0 = pl.program_id(2) * tq
    q = q_ref[0]
    qall = jnp.concatenate([q[:, h * LANES:(h + 1) * LANES] for h in range(NSA_GROUP)], axis=0)
    tpos = t0 + lax.broadcasted_iota(jnp.int32, (tq, 1), 0)
    lane = lax.broadcasted_iota(jnp.int32, (tq, LANES), 1)

    s = _dot_nt(qall, ck_ref[0]).reshape(NSA_GROUP, tq, LANES)
    valid = (CMP_STRIDE * lane + (CMP_LEN - 1)) <= tpos
    sm = jnp.where(valid[None], s, NEG)
    mx = jnp.max(sm, axis=-1, keepdims=True)
    e = jnp.where(valid[None], jnp.exp(sm - mx), 0.0)
    den = jnp.sum(e, axis=-1, keepdims=True)
    p_cmp = e * jnp.where(den > 0.0, 1.0 / den, 0.0)
    o_cmp = _dot(p_cmp.astype(BF16).reshape(NSA_GROUP * tq, LANES), cv_ref[0])
    o_cmp = o_cmp.reshape(NSA_GROUP, tq, LANES)

    p_sum = p_cmp[0] + p_cmp[1] + p_cmp[2] + p_cmp[3]
    imp = _dot_split(p_sum, cover_ref[...], 3)
    cur = jnp.right_shift(tpos, int(math.log2(SEL_LEN)))
    forced = (lane == 0) | (lane == cur) | (lane == cur - 1)
    causal_blk = (lane * SEL_LEN) <= tpos
    n_sel = expand_ref.shape[0] * (TK_FOX // SEL_LEN)
    score = jnp.where(forced, FORCE_SCORE, jnp.where(causal_blk, imp, -1.0))
    score = jnp.where(lane < n_sel, score, -2.0)
    cnt = jnp.zeros((tq, LANES), F32)
    for j in range(n_sel):
        cj = score[:, j:j + 1]
        beats = (cj > score) | ((cj == score) & (lane > j))
        cnt = cnt + jnp.where(beats, 1.0, 0.0)
    sel = jnp.where((cnt < float(min(SEL_TOPN, n_sel))) & (lane < n_sel), 1.0, 0.0).astype(BF16)

    def biased(bias):
        def adjust(h, t_off, cols):
            return [cols[kk] + bias[t_off:t_off + ROWS, kk * LANES:(kk + 1) * LANES]
                    for kk in range(len(cols))]
        return adjust

    kcol_s = lax.broadcasted_iota(jnp.int32, (tq, TK_FOX), 1)

    def sel_tile(kt, carry):
        k0 = pl.multiple_of(kt * TK_FOX, TK_FOX)
        hit = _dot(sel, expand_ref[kt])
        ok = (hit > 0.5) & ((k0 + kcol_s) <= tpos)
        s_t = _dot_nt(qall, ks_ref[0, pl.ds(k0, TK_FOX), :])
        _softmax_tile(s_t, vs_ref[0, pl.ds(k0, TK_FOX), :], m_ref, l_ref, acc_ref, NSA_GROUP, tq,
                      biased(jnp.where(ok, 0.0, NEG)))
        return carry

    _softmax_init(m_ref, l_ref, acc_ref)
    lax.fori_loop(0, lax.div(t0 + tq - 1, TK_FOX) + 1, sel_tile, 0)
    o_sel = _softmax_result(l_ref, acc_ref).reshape(NSA_GROUP, tq, LANES)

    kcol_w = lax.broadcasted_iota(jnp.int32, (tq, TK_ATT), 1)

    def win_tile(kt, carry):
        k0 = pl.multiple_of(kt * TK_ATT, TK_ATT)
        kp = k0 + kcol_w
        ok = (kp <= tpos) & (kp > tpos - WINDOW)
        s_t = _dot_nt(qall, kw_ref[0, pl.ds(k0, TK_ATT), :])
        _softmax_tile(s_t, vw_ref[0, pl.ds(k0, TK_ATT), :], m_ref, l_ref, acc_ref, NSA_GROUP, tq,
                      biased(jnp.where(ok, 0.0, NEG)))
        return carry

    _softmax_init(m_ref, l_ref, acc_ref)
    lo_w = lax.div(jnp.maximum(t0 - (WINDOW - 1), 0), TK_ATT)
    lax.fori_loop(lo_w, lax.div(t0 + tq - 1, TK_ATT) + 1, win_tile, 0)
    o_win = _softmax_result(l_ref, acc_ref).reshape(NSA_GROUP, tq, LANES)

    gt = gate_ref[0]
    gt = 1.0 / (1.0 + jnp.exp(-gt))
    mine = (lane >= g * HEAD_DIM) & (lane < (g + 1) * HEAD_DIM)
    outs = []
    for h in range(NSA_GROUP):
        o = (gt[:, 3 * h:3 * h + 1] * o_cmp[h] + gt[:, 3 * h + 1:3 * h + 2] * o_sel[h]
             + gt[:, 3 * h + 2:3 * h + 3] * o_win[h])
        o = jnp.where(mine, o, 0.0)
        ms = jnp.sum(o * o, axis=-1, keepdims=True) * (1.0 / HEAD_DIM)
        o = o * lax.rsqrt(ms + EPS)
        outs.append(o + pltpu.roll(o, HEAD_DIM, 1))
    left = lane < HEAD_DIM
    o_ref[0, :, :LANES] = (jnp.where(left, outs[0], outs[1]) * gn_ref[:, :LANES]).astype(o_ref.dtype)
    o_ref[0, :, LANES:] = (jnp.where(left, outs[2], outs[3]) * gn_ref[:, LANES:]).astype(o_ref.dtype)


def _nsa_attention(u, ck, cv, small, gn, cover, expand):
    b, s, _ = u.shape
    tq = min(TQ_NSA, s)
    n_cmp = ck.shape[1]
    blk = LANES
    kv = lambda col: pl.BlockSpec((1, s, LANES), lambda i, g, j, col=col: (i, 0, col // blk))
    return pl.pallas_call(
        _nsa_kernel,
        grid=(b, NSA_KV, s // tq),
        in_specs=[pl.BlockSpec((1, tq, NSA_GROUP * LANES), lambda i, g, j: (i, j, g)),
                  pl.BlockSpec((1, n_cmp, LANES), lambda i, g, j: (i, 0, 0)),
                  pl.BlockSpec((1, n_cmp, LANES), lambda i, g, j: (i, 0, 0)),
                  kv(COL_KS), kv(COL_VS), kv(COL_KW), kv(COL_VW),
                  pl.BlockSpec((1, tq, LANES), lambda i, g, j: (i, j, g)),
                  pl.BlockSpec((1, 2 * LANES), lambda i, g, j: (0, g)),
                  pl.BlockSpec(cover.shape, lambda i, g, j: (0, 0)),
                  pl.BlockSpec(expand.shape, lambda i, g, j: (0, 0, 0))],
        out_specs=pl.BlockSpec((1, tq, 2 * LANES), lambda i, g, j: (i, j, g)),
        out_shape=jax.ShapeDtypeStruct((b, s, NSA_HEADS * HEAD_DIM), BF16),
        scratch_shapes=_softmax_scratch(NSA_GROUP * tq),
        compiler_params=_cp(3),
        name="nsa_attention",
    )(u, ck, cv, u, u, u, u, small, gn, cover, expand)


def _pair_finish(acc, gn_ref, o_ref, tq):
    lane = lax.broadcasted_iota(jnp.int32, (tq, LANES), 1)
    left = lane < HEAD_DIM
    o = jnp.where(left, acc[0], acc[1])
    o2 = o * o
    ms_l = jnp.sum(jnp.where(left, o2, 0.0), axis=-1, keepdims=True) * (1.0 / HEAD_DIM)
    ms_r = jnp.sum(jnp.where(left, 0.0, o2), axis=-1, keepdims=True) * (1.0 / HEAD_DIM)
    inv = jnp.where(left, lax.rsqrt(ms_l + EPS), lax.rsqrt(ms_r + EPS))
    o_ref[0] = (o * inv * gn_ref[...]).astype(o_ref.dtype)


def _softmax_tile(s, v, m_ref, l_ref, acc_ref, heads, tq, adjust):
    nk = s.shape[1] // LANES
    p_rows = []
    for h in range(heads):
        for c in range(tq // ROWS):
            t_off = c * ROWS
            r0 = h * tq + t_off
            cols = [s[r0:r0 + ROWS, k * LANES:(k + 1) * LANES] for k in range(nk)]
            cols = adjust(h, t_off, cols)
            mx = cols[0]
            for k in range(1, nk):
                mx = jnp.maximum(mx, cols[k])
            m_old = m_ref[r0:r0 + ROWS, :]
            m_new = jnp.maximum(m_old, jnp.max(mx, axis=-1, keepdims=True))
            alpha = jnp.exp(m_old - m_new)
            pks = [jnp.exp(cols[k] - m_new) for k in range(nk)]
            psum = pks[0]
            for k in range(1, nk):
                psum = psum + pks[k]
            p_rows.append(jnp.concatenate([pk.astype(BF16) for pk in pks], axis=1))
            l_ref[r0:r0 + ROWS, :] = alpha * l_ref[r0:r0 + ROWS, :] + psum
            acc_ref[r0:r0 + ROWS, :] = alpha * acc_ref[r0:r0 + ROWS, :]
            m_ref[r0:r0 + ROWS, :] = m_new
    p = jnp.concatenate(p_rows, axis=0)
    acc_ref[...] += _dot(p, v)


def _softmax_init(m_ref, l_ref, acc_ref):
    m_ref[...] = jnp.full(m_ref.shape, NEG, F32)
    l_ref[...] = jnp.zeros(l_ref.shape, F32)
    acc_ref[...] = jnp.zeros(acc_ref.shape, F32)


def _softmax_result(l_ref, acc_ref):
    return acc_ref[...] / jnp.sum(l_ref[...], axis=-1, keepdims=True)


def _softmax_scratch(rows):
    return [pltpu.VMEM((rows, LANES), F32), pltpu.VMEM((rows, LANES), F32), pltpu.VMEM((rows, LANES), F32)]


def _fox_kernel(q_ref, k_ref, v_ref, cq_ref, ckt_ref, gn_ref, o_ref,
                m_ref, l_ref, acc_ref, cqr_ref):
    tq = q_ref.shape[1]
    tk = TK_FOX
    t0 = pl.program_id(2) * tq
    q = q_ref[0]
    qall = jnp.concatenate([q[:, :LANES], q[:, LANES:]], axis=0)
    cq = cq_ref[0]
    cqr_ref[0:tq, :] = jnp.broadcast_to(cq[:, 0:1], (tq, LANES))
    cqr_ref[tq:2 * tq, :] = jnp.broadcast_to(cq[:, 1:2], (tq, LANES))
    _softmax_init(m_ref, l_ref, acc_ref)
    n_tiles = lax.div(t0 + tq - 1, tk) + 1
    diag = (lax.broadcasted_iota(jnp.int32, (ROWS, LANES), 1)
            - lax.broadcasted_iota(jnp.int32, (ROWS, LANES), 0))

    def tile(kt, masked):
        k0 = pl.multiple_of(kt * tk, tk)
        s = _dot_nt(qall, k_ref[0, pl.ds(k0, tk), :])
        cks = [ckt_ref[0, 0, kt * (tk // TK_ATT) + j] for j in range(tk // TK_ATT)]

        def adjust(h, t_off, cols):
            out = []
            cqr = cqr_ref[h * tq + t_off:h * tq + t_off + ROWS, :]
            for kk in range(tk // LANES):
                lo = (kk * LANES) % TK_ATT
                ck = cks[(kk * LANES) // TK_ATT][h:h + 1, lo:lo + LANES]
                val = (cols[kk] - ck) + cqr
                if masked:
                    val = jnp.where(diag <= (t0 + t_off) - (k0 + kk * LANES), val, NEG)
                out.append(val)
            return out

        _softmax_tile(s, v_ref[0, pl.ds(k0, tk), :], m_ref, l_ref, acc_ref, 2, tq, adjust)

    def full_tile(kt, carry):
        tile(kt, False)
        return carry

    lax.fori_loop(0, n_tiles - 1, full_tile, 0)
    tile(n_tiles - 1, True)
    acc = _softmax_result(l_ref, acc_ref).reshape(2, tq, LANES)
    _pair_finish(acc, gn_ref, o_ref, tq)


def _fox_attention(u, cq, ckt, gn):
    b, s, _ = u.shape
    tq = min(TQ_PAIR, s)
    qb = COL_FOX // (2 * LANES)
    kb = (COL_FOX + FOX_HEADS * LANES) // LANES
    vb = kb + 2
    return pl.pallas_call(
        _fox_kernel,
        grid=(b, 2, s // tq),
        in_specs=[pl.BlockSpec((1, tq, 2 * LANES), lambda i, p, j: (i, j, qb + p)),
                  pl.BlockSpec((1, s, LANES), lambda i, p, j: (i, 0, kb + p)),
                  pl.BlockSpec((1, s, LANES), lambda i, p, j: (i, 0, vb + p)),
                  pl.BlockSpec((1, tq, LANES), lambda i, p, j: (i, j, p)),
                  pl.BlockSpec((1, 1, s // TK_ATT, 8, TK_ATT), lambda i, p, j: (i, p, 0, 0, 0)),
                  pl.BlockSpec((1, LANES), lambda i, p, j: (0, p))],
        out_specs=pl.BlockSpec((1, tq, LANES), lambda i, p, j: (i, j, p)),
        out_shape=jax.ShapeDtypeStruct((b, s, FOX_HEADS * HEAD_DIM), BF16),
        scratch_shapes=_softmax_scratch(2 * tq) + [pltpu.VMEM((2 * tq, LANES), F32)],
        compiler_params=_cp(3),
        name="fox_attention",
    )(u, u, u, cq, ckt, gn)


def _sb_kernel(q_ref, k_ref, v_ref, gn_ref, o_ref, rest_ref, acc_ref):
    tq = q_ref.shape[1]
    tk = TK_ATT
    nk = tk // LANES
    t0 = pl.program_id(2) * tq
    q = q_ref[0]
    qall = jnp.concatenate([q[:, :LANES], q[:, LANES:]], axis=0)
    r = lax.broadcasted_iota(jnp.int32, (tk, tk), 0)
    c = lax.broadcasted_iota(jnp.int32, (tk, tk), 1)
    upper = jnp.where(r >= c, 1.0, 0.0).astype(BF16)
    n_tiles = lax.div(t0 + tq - 1, tk) + 1
    diag = (lax.broadcasted_iota(jnp.int32, (ROWS, LANES), 1)
            - lax.broadcasted_iota(jnp.int32, (ROWS, LANES), 0))
    rest_ref[...] = jnp.zeros(rest_ref.shape, F32)
    acc_ref[...] = jnp.zeros(acc_ref.shape, F32)
    chunks = [(h * tq + cc * ROWS, cc * ROWS) for h in range(2) for cc in range(tq // ROWS)]

    def tile(kt, masked):
        k0 = pl.multiple_of(kt * tk, tk)
        z = _dot_nt(qall, k_ref[0, pl.ds(k0, tk), :])

        def strictly_before(t_off, kk):
            return diag < (t0 + t_off) - (k0 + kk * LANES)

        his, los = [], []
        for r0, t_off in chunks:
            hi_c, lo_c = [], []
            for kk in range(nk):
                l = -_softplus(z[r0:r0 + ROWS, kk * LANES:(kk + 1) * LANES])
                if masked:
                    l = jnp.where(strictly_before(t_off, kk), l, 0.0)
                hi = l.astype(BF16)
                hi_c.append(hi)
                lo_c.append((l - hi.astype(F32)).astype(BF16))
            his.append(jnp.concatenate(hi_c, axis=1))
            los.append(jnp.concatenate(lo_c, axis=1))
        cum = _dot(jnp.concatenate(his, axis=0), upper) + _dot(jnp.concatenate(los, axis=0), upper)

        a_rows = []
        for r0, t_off in chunks:
            rest = rest_ref[r0:r0 + ROWS, :]
            a_c = []
            for kk in range(nk):
                sl = (slice(r0, r0 + ROWS), slice(kk * LANES, (kk + 1) * LANES))
                a = jnp.exp(z[sl] + cum[sl] + rest)
                if masked:
                    a = jnp.where(strictly_before(t_off, kk), a, 0.0)
                a_c.append(a.astype(BF16))
            a_rows.append(jnp.concatenate(a_c, axis=1))
            rest_ref[r0:r0 + ROWS, :] = rest + jnp.broadcast_to(cum[r0:r0 + ROWS, 0:1], (ROWS, LANES))
        acc_ref[...] += _dot(jnp.concatenate(a_rows, axis=0), v_ref[0, pl.ds(k0, tk), :])

    tile(n_tiles - 1, True)

    def full_tile(i, carry):
        tile(n_tiles - 2 - i, False)
        return carry

    lax.fori_loop(0, n_tiles - 1, full_tile, 0)
    _pair_finish(acc_ref[...].reshape(2, tq, LANES), gn_ref, o_ref, tq)


def _sb_attention(u, gn):
    b, s, _ = u.shape
    tq = min(TQ_PAIR, s)
    qb = COL_SB // (2 * LANES)
    kb = (COL_SB + SB_HEADS * LANES) // LANES
    vb = kb + 2
    return pl.pallas_call(
        _sb_kernel,
        grid=(b, 2, s // tq),
        in_specs=[pl.BlockSpec((1, tq, 2 * LANES), lambda i, p, j: (i, j, qb + p)),
                  pl.BlockSpec((1, s, LANES), lambda i, p, j: (i, 0, kb + p)),
                  pl.BlockSpec((1, s, LANES), lambda i, p, j: (i, 0, vb + p)),
                  pl.BlockSpec((1, LANES), lambda i, p, j: (0, p))],
        out_specs=pl.BlockSpec((1, tq, LANES), lambda i, p, j: (i, j, p)),
        out_shape=jax.ShapeDtypeStruct((b, s, SB_HEADS * HEAD_DIM), BF16),
        scratch_shapes=[pltpu.VMEM((2 * tq, LANES), F32), pltpu.VMEM((2 * tq, LANES), F32)],
        compiler_params=_cp(3),
        name="sb_attention",
    )(u, u, u, gn)


def _outproj_kernel(oa_ref, ob_ref, oc_ref, x_ref, w_ref, g1_ref, n2_ref, sc_ref, sh_ref,
                    wrh_ref, wrl_ref, br_ref, xo_ref, h_ref, rw_ref, ri_ref):
    na = oa_ref.shape[2]
    nb = ob_ref.shape[2]
    y = _dot(oa_ref[0], w_ref[0:na, :])
    y = y + _dot(ob_ref[0], w_ref[na:na + nb, :])
    y = y + _dot(oc_ref[0], w_ref[na + nb:, :])
    x = x_ref[0] + g1_ref[0] * y
    xo_ref[0] = x
    ms = jnp.mean(x * x, axis=-1, keepdims=True)
    h = (x * lax.rsqrt(ms + EPS) * n2_ref[...]) * (1.0 + sc_ref[0]) + sh_ref[0]
    hb = h.astype(BF16)
    h_ref[0] = h
    hl = (h - hb.astype(F32)).astype(BF16)
    logit = _dot(hb, wrh_ref[...]) + _dot(hl, wrh_ref[...]) + _dot(hb, wrl_ref[...]) + br_ref[...]

    tm = logit.shape[0]
    lane = lax.broadcasted_iota(jnp.int32, (tm, LANES), 1).astype(F32)
    big = float(LANES)
    is_g = lane < N_GROUPS
    lg = jnp.where(is_g, logit, NEG)
    mg = jnp.max(lg, axis=-1, keepdims=True)
    zg = jnp.sum(jnp.where(is_g, jnp.exp(lg - mg), 0.0), axis=-1, keepdims=True)
    pg = 1.0 / zg
    gi = jnp.min(jnp.where(is_g & (lg == mg), lane, big), axis=-1, keepdims=True)
    e_lane = lane - N_GROUPS
    in_grp = (e_lane >= gi * EXPERTS_PER_GROUP) & (e_lane < (gi + 1) * EXPERTS_PER_GROUP)
    le = jnp.where(in_grp, logit, NEG)
    m1 = jnp.max(le, axis=-1, keepdims=True)
    i1 = jnp.min(jnp.where(in_grp & (le == m1), lane, big), axis=-1, keepdims=True)
    rest = in_grp & (lane != i1)
    le2 = jnp.where(rest, logit, NEG)
    m2 = jnp.max(le2, axis=-1, keepdims=True)
    i2 = jnp.min(jnp.where(rest & (le2 == m2), lane, big), axis=-1, keepdims=True)
    ze = jnp.sum(jnp.where(in_grp, jnp.exp(le - m1), 0.0), axis=-1, keepdims=True)
    p1 = 1.0 / ze
    p2 = jnp.exp(m2 - m1) / ze
    den = p1 + p2
    w1 = pg * (p1 / den)
    w2 = pg * (p2 / den)
    rw_ref[0] = jnp.where(lane == 0.0, w1, jnp.where(lane == 1.0, w2, 0.0))
    ri_ref[0] = jnp.where(lane == 0.0, i1 - N_GROUPS, jnp.where(lane == 1.0, i2 - N_GROUPS, 0.0)).astype(jnp.int32)


def _out_projection(oa, ob, oc, x, w_out, g1, n2, sc, sh, wr_hi, wr_lo, br):
    b, s, d = x.shape
    tm = min(TM_OUT, s)
    row = lambda i, j: (i, j, 0)
    const2 = lambda i, j: (0, 0)
    per_b = lambda i, j: (i, 0, 0)
    return pl.pallas_call(
        _outproj_kernel,
        grid=(b, s // tm),
        in_specs=[pl.BlockSpec((1, tm, oa.shape[2]), row),
                  pl.BlockSpec((1, tm, ob.shape[2]), row),
                  pl.BlockSpec((1, tm, oc.shape[2]), row),
                  pl.BlockSpec((1, tm, d), row),
                  pl.BlockSpec(w_out.shape, const2),
                  pl.BlockSpec((1, 1, d), per_b),
                  pl.BlockSpec((1, d), const2),
                  pl.BlockSpec((1, 1, d), per_b),
                  pl.BlockSpec((1, 1, d), per_b),
                  pl.BlockSpec((d, LANES), const2),
                  pl.BlockSpec((d, LANES), const2),
                  pl.BlockSpec((1, LANES), const2)],
        out_specs=[pl.BlockSpec((1, tm, d), row),
                   pl.BlockSpec((1, tm, d), row),
                   pl.BlockSpec((1, tm, LANES), row),
                   pl.BlockSpec((1, tm, LANES), row)],
        out_shape=[jax.ShapeDtypeStruct((b, s, d), F32),
                   jax.ShapeDtypeStruct((b, s, d), F32),
                   jax.ShapeDtypeStruct((b, s, LANES), F32),
                   jax.ShapeDtypeStruct((b, s, LANES), jnp.int32)],
        compiler_params=_cp(2),
        name="out_projection",
    )(oa, ob, oc, x, w_out, g1.reshape(b, 1, d), n2.reshape(1, d), sc.reshape(b, 1, d),
      sh.reshape(b, 1, d), wr_hi, wr_lo, br)


def _copy_wait(src_hbm, dst_hbm, sem):
    pltpu.make_async_copy(src_hbm.at[pl.ds(0, COPY_CHUNK), :], dst_hbm.at[pl.ds(0, COPY_CHUNK), :], sem).wait()


def _scatter_rows_kernel(idx_ref, src_hbm, init_hbm, dst_hbm, sem):
    del init_hbm
    base = pl.program_id(0) * COPY_CHUNK

    def issue(j, carry):
        tok = lax.shift_right_logical(base + j, 1)
        pltpu.make_async_copy(src_hbm.at[pl.ds(tok, 1), :], dst_hbm.at[pl.ds(idx_ref[0, 0, j], 1), :],
                              sem).start()
        return carry

    lax.fori_loop(0, COPY_CHUNK, issue, 0, unroll=8)
    _copy_wait(src_hbm, dst_hbm, sem)


def _scatter_rows(src, idx, n_dst):
    n, d = idx.shape[0], src.shape[1]
    init = jnp.zeros((n_dst, d), src.dtype)
    return pl.pallas_call(
        _scatter_rows_kernel,
        grid=(n // COPY_CHUNK,),
        in_specs=[pl.BlockSpec((1, 1, COPY_CHUNK), lambda i: (i, 0, 0), memory_space=pltpu.SMEM),
                  pl.BlockSpec(memory_space=pl.ANY),
                  pl.BlockSpec(memory_space=pl.ANY)],
        out_specs=pl.BlockSpec(memory_space=pl.ANY),
        out_shape=jax.ShapeDtypeStruct((n_dst, d), src.dtype),
        scratch_shapes=[pltpu.SemaphoreType.DMA],
        input_output_aliases={2: 0},
        compiler_params=_cp(1),
        name="moe_dispatch_rows",
    )(idx.reshape(n // COPY_CHUNK, 1, COPY_CHUNK), src, init)


def _gather_rows_kernel(idx_ref, src_hbm, dst_hbm, sem):
    base = pl.program_id(0) * COPY_CHUNK

    def issue(j, carry):
        pltpu.make_async_copy(src_hbm.at[pl.ds(idx_ref[0, 0, j], 1), :], dst_hbm.at[pl.ds(base + j, 1), :],
                              sem).start()
        return carry

    lax.fori_loop(0, COPY_CHUNK, issue, 0, unroll=8)
    _copy_wait(src_hbm, dst_hbm, sem)


def _gather_rows(src, idx):
    n, d = idx.shape[0], src.shape[1]
    return pl.pallas_call(
        _gather_rows_kernel,
        grid=(n // COPY_CHUNK,),
        in_specs=[pl.BlockSpec((1, 1, COPY_CHUNK), lambda i: (i, 0, 0), memory_space=pltpu.SMEM),
                  pl.BlockSpec(memory_space=pl.ANY)],
        out_specs=pl.BlockSpec(memory_space=pl.ANY),
        out_shape=jax.ShapeDtypeStruct((n, d), src.dtype),
        scratch_shapes=[pltpu.SemaphoreType.DMA],
        compiler_params=_cp(1),
        name="moe_gather_rows",
    )(idx.reshape(n // COPY_CHUNK, 1, COPY_CHUNK), src)


def _expert_kernel(te_ref, nu_ref, x_ref, w1_ref, w3_ref, w2_ref, y_ref):
    i = pl.program_id(0)

    @pl.when(i < nu_ref[0])
    def _():
        x = x_ref[...].astype(BF16)
        a = _dot(x, w1_ref[0].astype(BF16))
        g = _dot(x, w3_ref[0].astype(BF16))
        act = (a * (1.0 / (1.0 + jnp.exp(-a)))) * g
        y_ref[...] = _dot(act.astype(BF16), w2_ref[0].astype(BF16))

    @pl.when(i >= nu_ref[0])
    def _():
        y_ref[...] = jnp.zeros_like(y_ref)


def _expert_mlp(xs, tile_expert, n_used, w1, w3, w2):
    p_rows, d = xs.shape
    de = w1.shape[2]
    n_tiles = p_rows // TM_EXP
    grid_spec = pltpu.PrefetchScalarGridSpec(
        num_scalar_prefetch=2,
        grid=(n_tiles,),
        in_specs=[pl.BlockSpec((TM_EXP, d), lambda i, te, nu: (i, 0)),
                  pl.BlockSpec((1, d, de), lambda i, te, nu: (te[i], 0, 0)),
                  pl.BlockSpec((1, d, de), lambda i, te, nu: (te[i], 0, 0)),
                  pl.BlockSpec((1, de, d), lambda i, te, nu: (te[i], 0, 0))],
        out_specs=pl.BlockSpec((TM_EXP, d), lambda i, te, nu: (i, 0)),
    )
    return pl.pallas_call(
        _expert_kernel,
        grid_spec=grid_spec,
        out_shape=jax.ShapeDtypeStruct((p_rows, d), F32),
        compiler_params=_cp(1),
        name="expert_mlp",
    )(tile_expert, n_used, xs, w1, w3, w2)


def _combine_kernel(x_ref, y0_ref, y1_ref, rw_ref, g2_ref, fg_ref, o_ref, *, final):
    rw = rw_ref[0]
    moe = y0_ref[0] * rw[:, 0:1] + y1_ref[0] * rw[:, 1:2]
    x = x_ref[0] + g2_ref[0] * moe
    if final:
        ms = jnp.mean(x * x, axis=-1, keepdims=True)
        x = x * lax.rsqrt(ms + EPS) * fg_ref[...]
    o_ref[0] = x


def _combine(x, y0, y1, rw, g2, final_g, final):
    b, s, d = x.shape
    tm = min(TM_CMB, s)
    row = lambda i, j: (i, j, 0)
    return pl.pallas_call(
        functools.partial(_combine_kernel, final=final),
        grid=(b, s // tm),
        in_specs=[pl.BlockSpec((1, tm, d), row),
                  pl.BlockSpec((1, tm, d), row),
                  pl.BlockSpec((1, tm, d), row),
                  pl.BlockSpec((1, tm, LANES), row),
                  pl.BlockSpec((1, 1, d), lambda i, j: (i, 0, 0)),
                  pl.BlockSpec((1, d), lambda i, j: (0, 0))],
        out_specs=pl.BlockSpec((1, tm, d), row),
        out_shape=jax.ShapeDtypeStruct((b, s, d), F32),
        compiler_params=_cp(2),
        name="moe_combine_final" if final else "moe_combine",
    )(x, y0, y1, rw, g2.reshape(b, 1, d), final_g.reshape(1, d))


def _pad_heads(w, n_heads, offsets):
    d = w.shape[0]
    w = w.reshape(d, n_heads, HEAD_DIM)
    z = jnp.zeros((d, n_heads, HEAD_DIM), w.dtype)
    off = jnp.asarray(offsets, jnp.int32).reshape(1, n_heads, 1)
    blk = jnp.where(off == 0, jnp.concatenate([w, z], axis=-1), jnp.concatenate([z, w], axis=-1))
    return blk.reshape(d, n_heads * LANES)


def _layout_w_in(w_in):
    d = w_in.shape[0]
    kvw = NSA_KV * HEAD_DIM
    sizes = (NSA_HEADS * HEAD_DIM, kvw, kvw, kvw, kvw, kvw, kvw, NSA_HEADS * 3,
             FOX_HEADS * HEAD_DIM, FOX_HEADS * HEAD_DIM, FOX_HEADS * HEAD_DIM, FOX_HEADS,
             SB_HEADS * HEAD_DIM, SB_HEADS * HEAD_DIM, SB_HEADS * HEAD_DIM)
    pts = np.cumsum(sizes)[:-1].tolist()
    (qa, kca, vca, ksa, vsa, kwa, vwa, ga, qb, kb, vb, fb, qc, kc, vc) = jnp.split(w_in, pts, axis=1)
    scale = HEAD_DIM ** -0.5
    qa_p = _pad_heads(qa * scale, NSA_HEADS, [0] * NSA_GROUP + [HEAD_DIM] * NSA_GROUP)
    qb_p = _pad_heads(qb * scale, FOX_HEADS, [0, HEAD_DIM, 0, HEAD_DIM])
    qc_p = _pad_heads(qc * scale, SB_HEADS, [0, HEAD_DIM, 0, HEAD_DIM])
    main = jnp.concatenate([qa_p, ksa, kwa, vsa, vwa, qb_p, kb, vb, qc_p, kc, vc], axis=1).astype(BF16)
    cmp_w = jnp.concatenate([kca, vca], axis=1).astype(BF16)
    zpad = lambda n: jnp.zeros((d, n), w_in.dtype)
    per_grp = NSA_GROUP * 3
    small = jnp.concatenate([ga[:, :per_grp], zpad(LANES - per_grp), ga[:, per_grp:], zpad(LANES - per_grp),
                             fb[:, 0:2], zpad(LANES - 2), fb[:, 2:4], zpad(LANES - 2)], axis=1).astype(BF16)
    return main, cmp_w, small


def _layout_cmp(w1, w2):
    hid = w1.shape[1]
    w1r = w1.reshape(2, CMP_STRIDE, HEAD_DIM, hid)
    z = jnp.zeros((CMP_STRIDE, HEAD_DIM, hid), w1.dtype)
    cols = []
    for half in range(2):
        for g in range(NSA_KV):
            parts = [w1r[half] if gg == g else z for gg in range(NSA_KV)]
            cols.append(jnp.concatenate(parts, axis=1).reshape(CMP_STRIDE * LANES, hid))
    wcat = jnp.concatenate(cols, axis=1).astype(BF16)
    zz = jnp.zeros_like(w2)
    w2bd = jnp.concatenate([jnp.concatenate([w2, zz], axis=1),
                            jnp.concatenate([zz, w2], axis=1)], axis=0).astype(BF16)
    return wcat, w2bd


def _rope_tables(pos):
    inv = jnp.exp(jnp.arange(ROPE_HALF, dtype=F32) * (-2.0 * math.log(ROPE_THETA) / ROPE_DIM))
    ang = pos.astype(F32)[:, None] * inv[None, :]
    cos, sin = jnp.cos(ang), jnp.sin(ang)
    n = pos.shape[0]
    z8 = jnp.zeros((n, ROPE_HALF), F32)
    rest1 = jnp.ones((n, HEAD_DIM - ROPE_DIM), F32)
    rest0 = jnp.zeros((n, HEAD_DIM - ROPE_DIM), F32)
    c = jnp.concatenate([cos, cos, rest1], axis=1)
    s1 = jnp.concatenate([z8, sin, rest0], axis=1)
    s2 = jnp.concatenate([-sin, z8, rest0], axis=1)
    dup = lambda a: jnp.concatenate([a, a], axis=1)
    return dup(c), dup(s1), dup(s2)


def _static_tables(s):
    n_cmp_pad = s // CMP_STRIDE
    n = np.arange(n_cmp_pad)[:, None]
    j = np.arange(LANES)[None, :]
    n_sel = s // SEL_LEN
    cover = ((n * CMP_STRIDE < j * SEL_LEN + SEL_LEN) & (n * CMP_STRIDE + CMP_LEN > j * SEL_LEN)
             & (j < n_sel)).astype(np.float32)
    nt = s // TK_FOX
    key = np.arange(nt)[:, None, None] * TK_FOX + np.arange(TK_FOX)[None, None, :]
    expand = (key // SEL_LEN == np.arange(LANES)[None, :, None]).astype(np.float32)
    return jnp.asarray(cover, BF16), jnp.asarray(expand, BF16)


def _dispatch_plan(ri, t):
    eid = ri.reshape(t, LANES)[:, :2].reshape(-1)
    n_assign = eid.shape[0]
    onehot = (eid[:, None] == jnp.arange(N_EXPERTS, dtype=jnp.int32)[None, :]).astype(jnp.int32)
    csum = jnp.cumsum(onehot, axis=0)
    counts = csum[-1]
    rank = jnp.take_along_axis(csum, eid[:, None], axis=1)[:, 0] - 1
    padded = ((counts + TM_EXP - 1) // TM_EXP) * TM_EXP
    pends = jnp.cumsum(padded)
    pstarts = pends - padded
    dest = pstarts[eid] + rank
    n_tiles = -(-(n_assign + N_EXPERTS * (TM_EXP - 1)) // TM_EXP)
    tile_start = jnp.arange(n_tiles, dtype=jnp.int32) * TM_EXP
    tile_expert = jnp.minimum(jnp.sum((pends[None, :] <= tile_start[:, None]).astype(jnp.int32), axis=1),
                              N_EXPERTS - 1).astype(jnp.int32)
    n_used = (pends[-1] // TM_EXP).astype(jnp.int32).reshape(1)
    return dest.astype(jnp.int32), n_tiles * TM_EXP, tile_expert, n_used


def kernel(x, c, norm1_g, norm2_g, ada_w, ada_b, w_in, b_forget, cmp_pos_k, cmp_w1_k, cmp_w2_k,
           cmp_pos_v, cmp_w1_v, cmp_w2_v, out_norm_g, w_out, router_group_w, router_group_b,
           router_expert_w, router_expert_b, expert_w1, expert_w3, expert_w2, final_g):
    b, s, d = x.shape
    depth = ada_w.shape[0]
    t = b * s
    mod = _modulation(c, ada_w, ada_b)
    rope_c, rope_1, rope_2 = _rope_tables(jnp.arange(s))
    n_cmp_pad = s // CMP_STRIDE
    crc, cr1, cr2 = _rope_tables(jnp.arange(n_cmp_pad) * CMP_STRIDE + (CMP_LEN - 1))
    cover, expand = _static_tables(s)

    for l in range(depth):
        sh1, sc1, g1, sh2, sc2, g2 = [mod[l][:, i * d:(i + 1) * d] for i in range(6)]
        w_main, w_cmp, w_small = _layout_w_in(w_in[l])
        u, kc, vc, small = _in_projection(x, norm1_g[l], sc1, sh1, w_main, w_cmp, w_small,
                                          rope_c, rope_1, rope_2)
        bf = b_forget[l]
        zf = jnp.zeros((LANES - 2,), F32)
        b_pairs = jnp.stack([jnp.concatenate([bf[0:2], zf]), jnp.concatenate([bf[2:4], zf])]).reshape(2, 1, LANES)
        cq, ckt = _forget_cumsum(small, b_pairs)
        wk, w2k = _layout_cmp(cmp_w1_k[l], cmp_w2_k[l])
        wv, w2v = _layout_cmp(cmp_w1_v[l], cmp_w2_v[l])
        ck, cv = _compress(kc, vc, wk, wv, cmp_w1_k[l], cmp_w1_v[l],
                           jnp.broadcast_to(cmp_pos_k[l].reshape(1, -1), (8, CMP_LEN * HEAD_DIM)),
                           jnp.broadcast_to(cmp_pos_v[l].reshape(1, -1), (8, CMP_LEN * HEAD_DIM)),
                           w2k, w2v, crc, cr1, cr2)
        gn = out_norm_g[l].reshape(1, -1)
        o_a = _nsa_attention(u, ck, cv, small, gn[:, :NSA_HEADS * HEAD_DIM], cover, expand)
        o_b = _fox_attention(u, cq, ckt, gn[:, NSA_HEADS * HEAD_DIM:(NSA_HEADS + FOX_HEADS) * HEAD_DIM])
        o_c = _sb_attention(u, gn[:, (NSA_HEADS + FOX_HEADS) * HEAD_DIM:])

        wr = jnp.concatenate([router_group_w[l], router_expert_w[l],
                              jnp.zeros((d, LANES - N_GROUPS - N_EXPERTS), F32)], axis=1)
        wr_hi = wr.astype(BF16)
        wr_lo = (wr - wr_hi.astype(F32)).astype(BF16)
        br = jnp.concatenate([router_group_b[l], router_expert_b[l],
                              jnp.zeros((LANES - N_GROUPS - N_EXPERTS,), F32)]).reshape(1, LANES)
        x, h2, rw, ri = _out_projection(o_a, o_b, o_c, x, w_out[l].astype(BF16), g1, norm2_g[l],
                                        sc2, sh2, wr_hi, wr_lo, br)

        dest, p_rows, tile_expert, n_used = _dispatch_plan(ri, t)
        xs = _scatter_rows(h2.reshape(t, d), dest, p_rows)
        ys = _expert_mlp(xs, tile_expert, n_used, expert_w1[l], expert_w3[l], expert_w2[l])
        yg = _gather_rows(ys, dest.reshape(t, 2).T.reshape(-1))
        y0 = yg[:t].reshape(b, s, d)
        y1 = yg[t:].reshape(b, s, d)
        x = _combine(x, y0, y1, rw, g2, final_g, final=(l == depth - 1))
    return x
```

```python
import functools
import math

import numpy as np
import jax
import jax.numpy as jnp
from jax import lax
from jax.experimental import pallas as pl
from jax.experimental.pallas import tpu as pltpu

F32 = jnp.float32
BF16 = jnp.bfloat16

HEAD_DIM = 64
LANES = 128
N_HEADS = 16
NSA_HEADS = 8
NSA_KV = 2
NSA_GROUP = 4
FOX_HEADS = 4
SB_HEADS = 4
ROPE_DIM = 16
ROPE_HALF = 8
ROPE_THETA = 500000.0
CMP_LEN = 32
CMP_STRIDE = 16
CMP_HIDDEN = 128
SEL_LEN = 64
SEL_TOPN = 16
WINDOW = 512
FORCE_SCORE = 1.0e4
N_GROUPS = 4
EXPERTS_PER_GROUP = 8
N_EXPERTS = 32
EPS = 1e-6
NEG = -1e30

COL_QA = 0
COL_KS = 1024
COL_KW = 1152
ROPE_COLS = 1280
COL_VS = 1280
COL_VW = 1408
COL_FOX = 1536
COL_SB = 2560
N_MAIN = 3584
N_SMALL = 512
PROJ_CHUNK = 1280

TM_PROJ = 512
TQ_NSA = 128
TK_ATT = 256
TQ_PAIR = 256
TK_FOX = 512
ROWS = 32
TM_OUT = 512
TM_EXP = 256
TM_CMB = 512
COPY_CHUNK = 512
ROW_TILE = 8
VMEM_LIMIT = 56 * 1024 * 1024


def _cp(n_axes, vmem=VMEM_LIMIT):
    return pltpu.CompilerParams(dimension_semantics=("arbitrary",) * n_axes, vmem_limit_bytes=vmem)


def _dot(a, b):
    return jnp.dot(a, b, preferred_element_type=F32)


def _dot_nt(a, b):
    return lax.dot_general(a, b, (((1,), (1,)), ((), ())), preferred_element_type=F32)


def _split_bf16(x, parts):
    out = []
    r = x
    for _ in range(parts):
        p = r.astype(BF16)
        out.append(p)
        r = r - p.astype(F32)
    return out


def _dot_split(x, m, parts):
    acc = None
    for p in _split_bf16(x, parts):
        d = _dot(p, m)
        acc = d if acc is None else acc + d
    return acc


def _rope(x, c, s1, s2):
    return x * c + pltpu.roll(x, ROPE_HALF, 1) * s1 + pltpu.roll(x, LANES - ROPE_HALF, 1) * s2


def _softplus(z):
    return jnp.maximum(z, 0.0) + jnp.log(1.0 + jnp.exp(-jnp.abs(z)))


def _mod_kernel(c_ref, w_ref, b_ref, o_ref):
    c = c_ref[...]
    cond = c * (1.0 / (1.0 + jnp.exp(-c)))
    o_ref[0] = _dot(cond, w_ref[0]) + b_ref[0]


def _modulation(c, ada_w, ada_b):
    depth, d, n = ada_w.shape
    b = c.shape[0]
    tn = 1024
    return pl.pallas_call(
        _mod_kernel,
        grid=(depth, n // tn),
        in_specs=[pl.BlockSpec((b, d), lambda l, j: (0, 0)),
                  pl.BlockSpec((1, d, tn), lambda l, j: (l, 0, j)),
                  pl.BlockSpec((1, 1, tn), lambda l, j: (l, 0, j))],
        out_specs=pl.BlockSpec((1, b, tn), lambda l, j: (l, 0, j)),
        out_shape=jax.ShapeDtypeStruct((depth, b, n), F32),
        compiler_params=_cp(2),
        name="modulation",
    )(c, ada_w, ada_b.reshape(depth, 1, n))


def _inproj_kernel(x_ref, g_ref, sc_ref, sh_ref, w_ref, wc_ref, ws_ref, rc_ref, r1_ref, r2_ref,
                   u_ref, kc_ref, vc_ref, sm_ref):
    x = x_ref[0]
    ms = jnp.mean(x * x, axis=-1, keepdims=True)
    h = (x * lax.rsqrt(ms + EPS) * g_ref[...]) * (1.0 + sc_ref[0]) + sh_ref[0]
    hb = h.astype(BF16)
    rc, r1, r2 = rc_ref[...], r1_ref[...], r2_ref[...]
    for j in range(N_MAIN // PROJ_CHUNK + (1 if N_MAIN % PROJ_CHUNK else 0)):
        lo = j * PROJ_CHUNK
        hi = min(lo + PROJ_CHUNK, N_MAIN)
        acc = _dot(hb, w_ref[:, lo:hi])
        if lo < ROPE_COLS:
            for k in range((hi - lo) // LANES):
                blk = acc[:, k * LANES:(k + 1) * LANES]
                u_ref[0, :, lo + k * LANES:lo + (k + 1) * LANES] = _rope(blk, rc, r1, r2).astype(BF16)
        else:
            u_ref[0, :, lo:hi] = acc.astype(BF16)
    cmp_in = _dot(hb, wc_ref[...])
    kc_ref[0] = cmp_in[:, :LANES].astype(BF16)
    vc_ref[0] = cmp_in[:, LANES:].astype(BF16)
    sm_ref[0] = _dot(hb, ws_ref[...])


def _in_projection(x, g, sc, sh, w_main, w_cmp, w_small, rope_c, rope_1, rope_2):
    b, s, d = x.shape
    tm = min(TM_PROJ, s)
    row = lambda i, j: (i, j, 0)
    const2 = lambda i, j: (0, 0)
    per_b = lambda i, j: (i, 0, 0)
    seq = lambda i, j: (j, 0)
    return pl.pallas_call(
        _inproj_kernel,
        grid=(b, s // tm),
        in_specs=[pl.BlockSpec((1, tm, d), row),
                  pl.BlockSpec((1, d), const2),
                  pl.BlockSpec((1, 1, d), per_b),
                  pl.BlockSpec((1, 1, d), per_b),
                  pl.BlockSpec((d, N_MAIN), const2),
                  pl.BlockSpec((d, 2 * LANES), const2),
                  pl.BlockSpec((d, N_SMALL), const2),
                  pl.BlockSpec((tm, LANES), seq),
                  pl.BlockSpec((tm, LANES), seq),
                  pl.BlockSpec((tm, LANES), seq)],
        out_specs=[pl.BlockSpec((1, tm, N_MAIN), row),
                   pl.BlockSpec((1, tm, LANES), row),
                   pl.BlockSpec((1, tm, LANES), row),
                   pl.BlockSpec((1, tm, N_SMALL), row)],
        out_shape=[jax.ShapeDtypeStruct((b, s, N_MAIN), BF16),
                   jax.ShapeDtypeStruct((b, s, LANES), BF16),
                   jax.ShapeDtypeStruct((b, s, LANES), BF16),
                   jax.ShapeDtypeStruct((b, s, N_SMALL), F32)],
        compiler_params=_cp(2),
        name="in_projection",
    )(x, g.reshape(1, d), sc.reshape(b, 1, d), sh.reshape(b, 1, d), w_main, w_cmp, w_small,
      rope_c, rope_1, rope_2)


def _cumf_kernel(f_ref, b_ref, cq_ref, ckt_ref):
    n_chunks = f_ref.shape[1] // TK_ATT
    r = lax.broadcasted_iota(jnp.int32, (TK_ATT, TK_ATT), 0)
    c = lax.broadcasted_iota(jnp.int32, (TK_ATT, TK_ATT), 1)
    tri = jnp.where(c <= r, 1.0, 0.0).astype(BF16)
    carry = jnp.zeros((1, LANES), F32)
    for j in range(n_chunks):
        f = f_ref[0, j * TK_ATT:(j + 1) * TK_ATT, :] + b_ref[0]
        ls = -_softplus(-f)
        acc = None
        for p in _split_bf16(ls, 3):
            dd = _dot(tri, p)
            acc = dd if acc is None else acc + dd
        cs = acc + carry
        cq_ref[0, j * TK_ATT:(j + 1) * TK_ATT, :] = cs
        ckt_ref[0, 0, j] = cs.T[:8, :]
        carry = cs[TK_ATT - 1:TK_ATT, :]


def _forget_cumsum(small, b_pairs):
    b, s, _ = small.shape
    return pl.pallas_call(
        _cumf_kernel,
        grid=(b, 2),
        in_specs=[pl.BlockSpec((1, s, LANES), lambda i, p: (i, 0, 2 + p)),
                  pl.BlockSpec((1, 1, LANES), lambda i, p: (p, 0, 0))],
        out_specs=[pl.BlockSpec((1, s, LANES), lambda i, p: (i, 0, p)),
                   pl.BlockSpec((1, 1, s // TK_ATT, 8, TK_ATT), lambda i, p: (i, p, 0, 0, 0))],
        out_shape=[jax.ShapeDtypeStruct((b, s, 2 * LANES), F32),
                   jax.ShapeDtypeStruct((b, 2, s // TK_ATT, 8, TK_ATT), F32)],
        compiler_params=_cp(2),
        name="forget_cumsum",
    )(small, b_pairs)


def _compress_kernel(ks_ref, vs_ref, wk_ref, wv_ref, w1k_ref, w1v_ref, pek_ref, pev_ref,
                     w2k_ref, w2v_ref, rc_ref, r1_ref, r2_ref, ck_ref, cv_ref):
    def one(seg_ref, w_ref, w1_ref, pe_ref, w2_ref):
        p = _dot(seg_ref[0], w_ref[...])
        half = 2 * CMP_HIDDEN
        bias = _dot(pe_ref[...].astype(BF16), w1_ref[...].astype(BF16))[0:1, :]
        bias2 = jnp.concatenate([bias, bias], axis=1)
        n = p.shape[0]
        hid = p[:, :half] + pltpu.roll(p[:, half:], n - 1, 0) + bias2
        act = hid * (1.0 / (1.0 + jnp.exp(-hid)))
        return _dot(act.astype(BF16), w2_ref[...])

    ck = one(ks_ref, wk_ref, w1k_ref, pek_ref, w2k_ref)
    ck_ref[0] = _rope(ck, rc_ref[...], r1_ref[...], r2_ref[...]).astype(BF16)
    cv_ref[0] = one(vs_ref, wv_ref, w1v_ref, pev_ref, w2v_ref).astype(BF16)


def _compress(kc, vc, wk, wv, w1k, w1v, pek, pev, w2k, w2v, rc, r1, r2):
    b, s, _ = kc.shape
    n = s // CMP_STRIDE
    width = CMP_STRIDE * LANES
    kseg = kc.reshape(b, n, width)
    vseg = vc.reshape(b, n, width)
    seg = pl.BlockSpec((1, n, width), lambda i: (i, 0, 0))
    full = lambda a: pl.BlockSpec(a.shape, lambda i: (0,) * a.ndim)
    return pl.pallas_call(
        _compress_kernel,
        grid=(b,),
        in_specs=[seg, seg, full(wk), full(wv), full(w1k), full(w1v), full(pek), full(pev),
                  full(w2k), full(w2v), full(rc), full(r1), full(r2)],
        out_specs=[pl.BlockSpec((1, n, LANES), lambda i: (i, 0, 0)),
                   pl.BlockSpec((1, n, LANES), lambda i: (i, 0, 0))],
        out_shape=[jax.ShapeDtypeStruct((b, n, LANES), BF16),
                   jax.ShapeDtypeStruct((b, n, LANES), BF16)],
        compiler_params=_cp(1),
        name="nsa_compress",
    )(kseg, vseg, wk, wv, w1k, w1v, pek, pev, w2k, w2v, rc, r1, r2)


def _nsa_kernel(q_ref, ck_ref, cv_ref, ks_ref, vs_ref, kw_ref, vw_ref, gate_ref, gn_ref,
                cover_ref, expand_ref, o_ref, m_ref, l_ref, acc_ref):
    tq = q_ref.shape[1]
    g = pl.program_id(1)
    t0 = pl.program_id(2) * tq
    q = q_ref[0]
    qall = jnp.concatenate([q[:, h * LANES:(h + 1) * LANES] for h in range(NSA_GROUP)], axis=0)
    tpos = t0 + lax.broadcasted_iota(jnp.int32, (tq, 1), 0)
    lane = lax.broadcasted_iota(jnp.int32, (tq, LANES), 1)

    s = _dot_nt(qall, ck_ref[0]).reshape(NSA_GROUP, tq, LANES)
    valid = (CMP_STRIDE * lane + (CMP_LEN - 1)) <= tpos
    sm = jnp.where(valid[None], s, NEG)
    mx = jnp.max(sm, axis=-1, keepdims=True)
    e = jnp.where(valid[None], jnp.exp(sm - mx), 0.0)
    den = jnp.sum(e, axis=-1, keepdims=True)
    p_cmp = e * jnp.where(den > 0.0, 1.0 / den, 0.0)
    o_cmp = _dot(p_cmp.astype(BF16).reshape(NSA_GROUP * tq, LANES), cv_ref[0])
    o_cmp = o_cmp.reshape(NSA_GROUP, tq, LANES)

    p_sum = p_cmp[0] + p_cmp[1] + p_cmp[2] + p_cmp[3]
    imp = _dot_split(p_sum, cover_ref[...], 3)
    cur = jnp.right_shift(tpos, int(math.log2(SEL_LEN)))
    forced = (lane == 0) | (lane == cur) | (lane == cur - 1)
    causal_blk = (lane * SEL_LEN) <= tpos
    n_sel = expand_ref.shape[0] * (TK_FOX // SEL_LEN)
    score = jnp.where(forced, FORCE_SCORE, jnp.where(causal_blk, imp, -1.0))
    score = jnp.where(lane < n_sel, score, -2.0)
    cnt = jnp.zeros((tq, LANES), F32)
    for j in range(n_sel):
        cj = score[:, j:j + 1]
        beats = (cj > score) | ((cj == score) & (lane > j))
        cnt = cnt + jnp.where(beats, 1.0, 0.0)
    sel = jnp.where((cnt < float(min(SEL_TOPN, n_sel))) & (lane < n_sel), 1.0, 0.0).astype(BF16)

    def biased(bias):
        def adjust(h, t_off, cols):
            return [cols[kk] + bias[t_off:t_off + ROWS, kk * LANES:(kk + 1) * LANES]
                    for kk in range(len(cols))]
        return adjust

    kcol_s = lax.broadcasted_iota(jnp.int32, (tq, TK_FOX), 1)

    def sel_tile(kt, carry):
        k0 = pl.multiple_of(kt * TK_FOX, TK_FOX)
        hit = _dot(sel, expand_ref[kt])
        ok = (hit > 0.5) & ((k0 + kcol_s) <= tpos)
        s_t = _dot_nt(qall, ks_ref[0, pl.ds(k0, TK_FOX), :])
        _softmax_tile(s_t, vs_ref[0, pl.ds(k0, TK_FOX), :], m_ref, l_ref, acc_ref, NSA_GROUP, tq,
                      biased(jnp.where(ok, 0.0, NEG)))
        return carry

    _softmax_init(m_ref, l_ref, acc_ref)
    lax.fori_loop(0, lax.div(t0 + tq - 1, TK_FOX) + 1, sel_tile, 0)
    o_sel = _softmax_result(l_ref, acc_ref).reshape(NSA_GROUP, tq, LANES)

    kcol_w = lax.broadcasted_iota(jnp.int32, (tq, TK_ATT), 1)

    def win_tile(kt, carry):
        k0 = pl.multiple_of(kt * TK_ATT, TK_ATT)
        kp = k0 + kcol_w
        ok = (kp <= tpos) & (kp > tpos - WINDOW)
        s_t = _dot_nt(qall, kw_ref[0, pl.ds(k0, TK_ATT), :])
        _softmax_tile(s_t, vw_ref[0, pl.ds(k0, TK_ATT), :], m_ref, l_ref, acc_ref, NSA_GROUP, tq,
                      biased(jnp.where(ok, 0.0, NEG)))
        return carry

    _softmax_init(m_ref, l_ref, acc_ref)
    lo_w = lax.div(jnp.maximum(t0 - (WINDOW - 1), 0), TK_ATT)
    lax.fori_loop(lo_w, lax.div(t0 + tq - 1, TK_ATT) + 1, win_tile, 0)
    o_win = _softmax_result(l_ref, acc_ref).reshape(NSA_GROUP, tq, LANES)

    gt = gate_ref[0]
    gt = 1.0 / (1.0 + jnp.exp(-gt))
    mine = (lane >= g * HEAD_DIM) & (lane < (g + 1) * HEAD_DIM)
    outs = []
    for h in range(NSA_GROUP):
        o = (gt[:, 3 * h:3 * h + 1] * o_cmp[h] + gt[:, 3 * h + 1:3 * h + 2] * o_sel[h]
             + gt[:, 3 * h + 2:3 * h + 3] * o_win[h])
        o = jnp.where(mine, o, 0.0)
        ms = jnp.sum(o * o, axis=-1, keepdims=True) * (1.0 / HEAD_DIM)
        o = o * lax.rsqrt(ms + EPS)
        outs.append(o + pltpu.roll(o, HEAD_DIM, 1))
    left = lane < HEAD_DIM
    o_ref[0, :, :LANES] = (jnp.where(left, outs[0], outs[1]) * gn_ref[:, :LANES]).astype(o_ref.dtype)
    o_ref[0, :, LANES:] = (jnp.where(left, outs[2], outs[3]) * gn_ref[:, LANES:]).astype(o_ref.dtype)


def _nsa_attention(u, ck, cv, small, gn, cover, expand):
    b, s, _ = u.shape
    tq = min(TQ_NSA, s)
    n_cmp = ck.shape[1]
    blk = LANES
    kv = lambda col: pl.BlockSpec((1, s, LANES), lambda i, g, j, col=col: (i, 0, col // blk))
    return pl.pallas_call(
        _nsa_kernel,
        grid=(b, NSA_KV, s // tq),
        in_specs=[pl.BlockSpec((1, tq, NSA_GROUP * LANES), lambda i, g, j: (i, j, g)),
                  pl.BlockSpec((1, n_cmp, LANES), lambda i, g, j: (i, 0, 0)),
                  pl.BlockSpec((1, n_cmp, LANES), lambda i, g, j: (i, 0, 0)),
                  kv(COL_KS), kv(COL_VS), kv(COL_KW), kv(COL_VW),
                  pl.BlockSpec((1, tq, LANES), lambda i, g, j: (i, j, g)),
                  pl.BlockSpec((1, 2 * LANES), lambda i, g, j: (0, g)),
                  pl.BlockSpec(cover.shape, lambda i, g, j: (0, 0)),
                  pl.BlockSpec(expand.shape, lambda i, g, j: (0, 0, 0))],
        out_specs=pl.BlockSpec((1, tq, 2 * LANES), lambda i, g, j: (i, j, g)),
        out_shape=jax.ShapeDtypeStruct((b, s, NSA_HEADS * HEAD_DIM), BF16),
        scratch_shapes=_softmax_scratch(NSA_GROUP * tq),
        compiler_params=_cp(3),
        name="nsa_attention",
    )(u, ck, cv, u, u, u, u, small, gn, cover, expand)


def _pair_finish(acc, gn_ref, o_ref, tq):
    lane = lax.broadcasted_iota(jnp.int32, (tq, LANES), 1)
    left = lane < HEAD_DIM
    o = jnp.where(left, acc[0], acc[1])
    o2 = o * o
    ms_l = jnp.sum(jnp.where(left, o2, 0.0), axis=-1, keepdims=True) * (1.0 / HEAD_DIM)
    ms_r = jnp.sum(jnp.where(left, 0.0, o2), axis=-1, keepdims=True) * (1.0 / HEAD_DIM)
    inv = jnp.where(left, lax.rsqrt(ms_l + EPS), lax.rsqrt(ms_r + EPS))
    o_ref[0] = (o * inv * gn_ref[...]).astype(o_ref.dtype)


def _softmax_tile(s, v, m_ref, l_ref, acc_ref, heads, tq, adjust):
    nk = s.shape[1] // LANES
    p_rows = []
    for h in range(heads):
        for c in range(tq // ROWS):
            t_off = c * ROWS
            r0 = h * tq + t_off
            cols = [s[r0:r0 + ROWS, k * LANES:(k + 1) * LANES] for k in range(nk)]
            cols = adjust(h, t_off, cols)
            mx = cols[0]
            for k in range(1, nk):
                mx = jnp.maximum(mx, cols[k])
            m_old = m_ref[r0:r0 + ROWS, :]
            m_new = jnp.maximum(m_old, jnp.max(mx, axis=-1, keepdims=True))
            alpha = jnp.exp(m_old - m_new)
            pks = [jnp.exp(cols[k] - m_new) for k in range(nk)]
            psum = pks[0]
            for k in range(1, nk):
                psum = psum + pks[k]
            p_rows.append(jnp.concatenate([pk.astype(BF16) for pk in pks], axis=1))
            l_ref[r0:r0 + ROWS, :] = alpha * l_ref[r0:r0 + ROWS, :] + psum
            acc_ref[r0:r0 + ROWS, :] = alpha * acc_ref[r0:r0 + ROWS, :]
            m_ref[r0:r0 + ROWS, :] = m_new
    p = jnp.concatenate(p_rows, axis=0)
    acc_ref[...] += _dot(p, v)


def _softmax_init(m_ref, l_ref, acc_ref):
    m_ref[...] = jnp.full(m_ref.shape, NEG, F32)
    l_ref[...] = jnp.zeros(l_ref.shape, F32)
    acc_ref[...] = jnp.zeros(acc_ref.shape, F32)


def _softmax_result(l_ref, acc_ref):
    return acc_ref[...] / jnp.sum(l_ref[...], axis=-1, keepdims=True)


def _softmax_scratch(rows):
    return [pltpu.VMEM((rows, LANES), F32), pltpu.VMEM((rows, LANES), F32), pltpu.VMEM((rows, LANES), F32)]


def _fox_kernel(q_ref, k_ref, v_ref, cq_ref, ckt_ref, gn_ref, o_ref,
                m_ref, l_ref, acc_ref, cqr_ref):
    tq = q_ref.shape[1]
    tk = TK_FOX
    t0 = pl.program_id(2) * tq
    q = q_ref[0]
    qall = jnp.concatenate([q[:, :LANES], q[:, LANES:]], axis=0)
    cq = cq_ref[0]
    cqr_ref[0:tq, :] = jnp.broadcast_to(cq[:, 0:1], (tq, LANES))
    cqr_ref[tq:2 * tq, :] = jnp.broadcast_to(cq[:, 1:2], (tq, LANES))
    _softmax_init(m_ref, l_ref, acc_ref)
    n_tiles = lax.div(t0 + tq - 1, tk) + 1
    diag = (lax.broadcasted_iota(jnp.int32, (ROWS, LANES), 1)
            - lax.broadcasted_iota(jnp.int32, (ROWS, LANES), 0))

    def tile(kt, masked):
        k0 = pl.multiple_of(kt * tk, tk)
        s = _dot_nt(qall, k_ref[0, pl.ds(k0, tk), :])
        cks = [ckt_ref[0, 0, kt * (tk // TK_ATT) + j] for j in range(tk // TK_ATT)]

        def adjust(h, t_off, cols):
            out = []
            cqr = cqr_ref[h * tq + t_off:h * tq + t_off + ROWS, :]
            for kk in range(tk // LANES):
                lo = (kk * LANES) % TK_ATT
                ck = cks[(kk * LANES) // TK_ATT][h:h + 1, lo:lo + LANES]
                val = (cols[kk] - ck) + cqr
                if masked:
                    val = jnp.where(diag <= (t0 + t_off) - (k0 + kk * LANES), val, NEG)
                out.append(val)
            return out

        _softmax_tile(s, v_ref[0, pl.ds(k0, tk), :], m_ref, l_ref, acc_ref, 2, tq, adjust)

    def full_tile(kt, carry):
        tile(kt, False)
        return carry

    lax.fori_loop(0, n_tiles - 1, full_tile, 0)
    tile(n_tiles - 1, True)
    acc = _softmax_result(l_ref, acc_ref).reshape(2, tq, LANES)
    _pair_finish(acc, gn_ref, o_ref, tq)


def _fox_attention(u, cq, ckt, gn):
    b, s, _ = u.shape
    tq = min(TQ_PAIR, s)
    qb = COL_FOX // (2 * LANES)
    kb = (COL_FOX + FOX_HEADS * LANES) // LANES
    vb = kb + 2
    return pl.pallas_call(
        _fox_kernel,
        grid=(b, 2, s // tq),
        in_specs=[pl.BlockSpec((1, tq, 2 * LANES), lambda i, p, j: (i, j, qb + p)),
                  pl.BlockSpec((1, s, LANES), lambda i, p, j: (i, 0, kb + p)),
                  pl.BlockSpec((1, s, LANES), lambda i, p, j: (i, 0, vb + p)),
                  pl.BlockSpec((1, tq, LANES), lambda i, p, j: (i, j, p)),
                  pl.BlockSpec((1, 1, s // TK_ATT, 8, TK_ATT), lambda i, p, j: (i, p, 0, 0, 0)),
                  pl.BlockSpec((1, LANES), lambda i, p, j: (0, p))],
        out_specs=pl.BlockSpec((1, tq, LANES), lambda i, p, j: (i, j, p)),
        out_shape=jax.ShapeDtypeStruct((b, s, FOX_HEADS * HEAD_DIM), BF16),
        scratch_shapes=_softmax_scratch(2 * tq) + [pltpu.VMEM((2 * tq, LANES), F32)],
        compiler_params=_cp(3),
        name="fox_attention",
    )(u, u, u, cq, ckt, gn)


def _sb_kernel(q_ref, k_ref, v_ref, gn_ref, o_ref, rest_ref, acc_ref):
    tq = q_ref.shape[1]
    tk = TK_ATT
    nk = tk // LANES
    t0 = pl.program_id(2) * tq
    q = q_ref[0]
    qall = jnp.concatenate([q[:, :LANES], q[:, LANES:]], axis=0)
    r = lax.broadcasted_iota(jnp.int32, (tk, tk), 0)
    c = lax.broadcasted_iota(jnp.int32, (tk, tk), 1)
    upper = jnp.where(r >= c, 1.0, 0.0).astype(BF16)
    n_tiles = lax.div(t0 + tq - 1, tk) + 1
    diag = (lax.broadcasted_iota(jnp.int32, (ROWS, LANES), 1)
            - lax.broadcasted_iota(jnp.int32, (ROWS, LANES), 0))
    rest_ref[...] = jnp.zeros(rest_ref.shape, F32)
    acc_ref[...] = jnp.zeros(acc_ref.shape, F32)
    chunks = [(h * tq + cc * ROWS, cc * ROWS) for h in range(2) for cc in range(tq // ROWS)]

    def tile(kt, masked):
        k0 = pl.multiple_of(kt * tk, tk)
        z = _dot_nt(qall, k_ref[0, pl.ds(k0, tk), :])

        def strictly_before(t_off, kk):
            return diag < (t0 + t_off) - (k0 + kk * LANES)

        his, los = [], []
        for r0, t_off in chunks:
            hi_c, lo_c = [], []
            for kk in range(nk):
                l = -_softplus(z[r0:r0 + ROWS, kk * LANES:(kk + 1) * LANES])
                if masked:
                    l = jnp.where(strictly_before(t_off, kk), l, 0.0)
                hi = l.astype(BF16)
                hi_c.append(hi)
                lo_c.append((l - hi.astype(F32)).astype(BF16))
            his.append(jnp.concatenate(hi_c, axis=1))
            los.append(jnp.concatenate(lo_c, axis=1))
        cum = _dot(jnp.concatenate(his, axis=0), upper) + _dot(jnp.concatenate(los, axis=0), upper)

        a_rows = []
        for r0, t_off in chunks:
            rest = rest_ref[r0:r0 + ROWS, :]
            a_c = []
            for kk in range(nk):
                sl = (slice(r0, r0 + ROWS), slice(kk * LANES, (kk + 1) * LANES))
                a = jnp.exp(z[sl] + cum[sl] + rest)
                if masked:
                    a = jnp.where(strictly_before(t_off, kk), a, 0.0)
                a_c.append(a.astype(BF16))
            a_rows.append(jnp.concatenate(a_c, axis=1))
            rest_ref[r0:r0 + ROWS, :] = rest + jnp.broadcast_to(cum[r0:r0 + ROWS, 0:1], (ROWS, LANES))
        acc_ref[...] += _dot(jnp.concatenate(a_rows, axis=0), v_ref[0, pl.ds(k0, tk), :])

    tile(n_tiles - 1, True)

    def full_tile(i, carry):
        tile(n_tiles - 2 - i, False)
        return carry

    lax.fori_loop(0, n_tiles - 1, full_tile, 0)
    _pair_finish(acc_ref[...].reshape(2, tq, LANES), gn_ref, o_ref, tq)


def _sb_attention(u, gn):
    b, s, _ = u.shape
    tq = min(TQ_PAIR, s)
    qb = COL_SB // (2 * LANES)
    kb = (COL_SB + SB_HEADS * LANES) // LANES
    vb = kb + 2
    return pl.pallas_call(
        _sb_kernel,
        grid=(b, 2, s // tq),
        in_specs=[pl.BlockSpec((1, tq, 2 * LANES), lambda i, p, j: (i, j, qb + p)),
                  pl.BlockSpec((1, s, LANES), lambda i, p, j: (i, 0, kb + p)),
                  pl.BlockSpec((1, s, LANES), lambda i, p, j: (i, 0, vb + p)),
                  pl.BlockSpec((1, LANES), lambda i, p, j: (0, p))],
        out_specs=pl.BlockSpec((1, tq, LANES), lambda i, p, j: (i, j, p)),
        out_shape=jax.ShapeDtypeStruct((b, s, SB_HEADS * HEAD_DIM), BF16),
        scratch_shapes=[pltpu.VMEM((2 * tq, LANES), F32), pltpu.VMEM((2 * tq, LANES), F32)],
        compiler_params=_cp(3),
        name="sb_attention",
    )(u, u, u, gn)


def _outproj_kernel(oa_ref, ob_ref, oc_ref, x_ref, w_ref, g1_ref, n2_ref, sc_ref, sh_ref,
                    wrh_ref, wrl_ref, br_ref, xo_ref, h_ref, rw_ref, ri_ref):
    na = oa_ref.shape[2]
    nb = ob_ref.shape[2]
    y = _dot(oa_ref[0], w_ref[0:na, :])
    y = y + _dot(ob_ref[0], w_ref[na:na + nb, :])
    y = y + _dot(oc_ref[0], w_ref[na + nb:, :])
    x = x_ref[0] + g1_ref[0] * y
    xo_ref[0] = x
    ms = jnp.mean(x * x, axis=-1, keepdims=True)
    h = (x * lax.rsqrt(ms + EPS) * n2_ref[...]) * (1.0 + sc_ref[0]) + sh_ref[0]
    hb = h.astype(BF16)
    _store_row_tiles(h_ref, h)
    hl = (h - hb.astype(F32)).astype(BF16)
    logit = _dot(hb, wrh_ref[...]) + _dot(hl, wrh_ref[...]) + _dot(hb, wrl_ref[...]) + br_ref[...]

    tm = logit.shape[0]
    lane = lax.broadcasted_iota(jnp.int32, (tm, LANES), 1).astype(F32)
    big = float(LANES)
    is_g = lane < N_GROUPS
    lg = jnp.where(is_g, logit, NEG)
    mg = jnp.max(lg, axis=-1, keepdims=True)
    zg = jnp.sum(jnp.where(is_g, jnp.exp(lg - mg), 0.0), axis=-1, keepdims=True)
    pg = 1.0 / zg
    gi = jnp.min(jnp.where(is_g & (lg == mg), lane, big), axis=-1, keepdims=True)
    e_lane = lane - N_GROUPS
    in_grp = (e_lane >= gi * EXPERTS_PER_GROUP) & (e_lane < (gi + 1) * EXPERTS_PER_GROUP)
    le = jnp.where(in_grp, logit, NEG)
    m1 = jnp.max(le, axis=-1, keepdims=True)
    i1 = jnp.min(jnp.where(in_grp & (le == m1), lane, big), axis=-1, keepdims=True)
    rest = in_grp & (lane != i1)
    le2 = jnp.where(rest, logit, NEG)
    m2 = jnp.max(le2, axis=-1, keepdims=True)
    i2 = jnp.min(jnp.where(rest & (le2 == m2), lane, big), axis=-1, keepdims=True)
    ze = jnp.sum(jnp.where(in_grp, jnp.exp(le - m1), 0.0), axis=-1, keepdims=True)
    p1 = 1.0 / ze
    p2 = jnp.exp(m2 - m1) / ze
    den = p1 + p2
    w1 = pg * (p1 / den)
    w2 = pg * (p2 / den)
    rw_ref[0] = jnp.where(lane == 0.0, w1, jnp.where(lane == 1.0, w2, 0.0))
    ri_ref[0] = jnp.where(lane == 0.0, i1 - N_GROUPS, jnp.where(lane == 1.0, i2 - N_GROUPS, 0.0)).astype(jnp.int32)


def _out_projection(oa, ob, oc, x, w_out, g1, n2, sc, sh, wr_hi, wr_lo, br):
    b, s, d = x.shape
    tm = min(TM_OUT, s)
    row = lambda i, j: (i, j, 0)
    const2 = lambda i, j: (0, 0)
    per_b = lambda i, j: (i, 0, 0)
    return pl.pallas_call(
        _outproj_kernel,
        grid=(b, s // tm),
        in_specs=[pl.BlockSpec((1, tm, oa.shape[2]), row),
                  pl.BlockSpec((1, tm, ob.shape[2]), row),
                  pl.BlockSpec((1, tm, oc.shape[2]), row),
                  pl.BlockSpec((1, tm, d), row),
                  pl.BlockSpec(w_out.shape, const2),
                  pl.BlockSpec((1, 1, d), per_b),
                  pl.BlockSpec((1, d), const2),
                  pl.BlockSpec((1, 1, d), per_b),
                  pl.BlockSpec((1, 1, d), per_b),
                  pl.BlockSpec((d, LANES), const2),
                  pl.BlockSpec((d, LANES), const2),
                  pl.BlockSpec((1, LANES), const2)],
        out_specs=[pl.BlockSpec((1, tm, d), row),
                   pl.BlockSpec((tm * ROW_TILE, LANES), lambda i, j: (i * (s // tm) + j, 0)),
                   pl.BlockSpec((1, tm, LANES), row),
                   pl.BlockSpec((1, tm, LANES), row)],
        out_shape=[jax.ShapeDtypeStruct((b, s, d), F32),
                   jax.ShapeDtypeStruct((b * s * ROW_TILE, LANES), F32),
                   jax.ShapeDtypeStruct((b, s, LANES), F32),
                   jax.ShapeDtypeStruct((b, s, LANES), jnp.int32)],
        compiler_params=_cp(2),
        name="out_projection",
    )(oa, ob, oc, x, w_out, g1.reshape(b, 1, d), n2.reshape(1, d), sc.reshape(b, 1, d),
      sh.reshape(b, 1, d), wr_hi, wr_lo, br)


def _store_row_tiles(ref, val):
    tm = val.shape[0]
    for c in range(ROW_TILE):
        ref[pl.ds(c, tm, stride=ROW_TILE), :] = val[:, c * LANES:(c + 1) * LANES]


def _load_row_tiles(ref, tm):
    return [ref[pl.ds(c, tm, stride=ROW_TILE), :] for c in range(ROW_TILE)]


def _tile_rows(ref, n):
    return ref.at[pl.ds(pl.multiple_of(n * ROW_TILE, ROW_TILE), ROW_TILE), :]


def _dispatch_kernel(idx_ref, src_hbm, init_hbm, dst_hbm, sem):
    del init_hbm
    base = pl.program_id(0) * COPY_CHUNK

    def issue(j, carry):
        tok = lax.shift_right_logical(base + j, 1)
        pltpu.make_async_copy(_tile_rows(src_hbm, tok), _tile_rows(dst_hbm, idx_ref[0, 0, j]), sem).start()
        return carry

    lax.fori_loop(0, COPY_CHUNK, issue, 0, unroll=8)
    n = COPY_CHUNK * ROW_TILE
    pltpu.make_async_copy(src_hbm.at[pl.ds(0, n), :], dst_hbm.at[pl.ds(0, n), :], sem).wait()


def _dispatch(src, idx, n_dst):
    n = idx.shape[0]
    init = jnp.zeros((n_dst * ROW_TILE, LANES), src.dtype)
    return pl.pallas_call(
        _dispatch_kernel,
        grid=(n // COPY_CHUNK,),
        in_specs=[pl.BlockSpec((1, 1, COPY_CHUNK), lambda i: (i, 0, 0), memory_space=pltpu.SMEM),
                  pl.BlockSpec(memory_space=pl.ANY),
                  pl.BlockSpec(memory_space=pl.ANY)],
        out_specs=pl.BlockSpec(memory_space=pl.ANY),
        out_shape=jax.ShapeDtypeStruct((n_dst * ROW_TILE, LANES), src.dtype),
        scratch_shapes=[pltpu.SemaphoreType.DMA],
        input_output_aliases={2: 0},
        compiler_params=_cp(1),
        name="moe_dispatch",
    )(idx.reshape(n // COPY_CHUNK, 1, COPY_CHUNK), src, init)


def _expert_kernel(te_ref, nu_ref, x_ref, w1_ref, w3_ref, w2_ref, y_ref):
    i = pl.program_id(0)

    @pl.when(i < nu_ref[0])
    def _():
        x = jnp.concatenate(_load_row_tiles(x_ref, TM_EXP), axis=1).astype(BF16)
        a = _dot(x, w1_ref[0].astype(BF16))
        g = _dot(x, w3_ref[0].astype(BF16))
        act = (a * (1.0 / (1.0 + jnp.exp(-a)))) * g
        _store_row_tiles(y_ref, _dot(act.astype(BF16), w2_ref[0].astype(BF16)))

    @pl.when(i >= nu_ref[0])
    def _():
        y_ref[...] = jnp.zeros_like(y_ref)


def _expert_mlp(xs, tile_expert, n_used, w1, w3, w2):
    d, de = w1.shape[1], w1.shape[2]
    n_tiles = xs.shape[0] // (TM_EXP * ROW_TILE)
    grid_spec = pltpu.PrefetchScalarGridSpec(
        num_scalar_prefetch=2,
        grid=(n_tiles,),
        in_specs=[pl.BlockSpec((TM_EXP * ROW_TILE, LANES), lambda i, te, nu: (i, 0)),
                  pl.BlockSpec((1, d, de), lambda i, te, nu: (te[i], 0, 0)),
                  pl.BlockSpec((1, d, de), lambda i, te, nu: (te[i], 0, 0)),
                  pl.BlockSpec((1, de, d), lambda i, te, nu: (te[i], 0, 0))],
        out_specs=pl.BlockSpec((TM_EXP * ROW_TILE, LANES), lambda i, te, nu: (i, 0)),
    )
    return pl.pallas_call(
        _expert_kernel,
        grid_spec=grid_spec,
        out_shape=jax.ShapeDtypeStruct(xs.shape, F32),
        compiler_params=_cp(1),
        name="expert_mlp",
    )(tile_expert, n_used, xs, w1, w3, w2)


def _combine_kernel(d0_ref, d1_ref, x_ref, ys_hbm, rw_ref, g2_ref, fg_ref, o_ref, y0_ref, y1_ref, sem,
                    *, final):
    tm = x_ref.shape[1]

    def issue(r, carry):
        pltpu.make_async_copy(_tile_rows(ys_hbm, d0_ref[0, 0, r]), _tile_rows(y0_ref, r), sem.at[0]).start()
        pltpu.make_async_copy(_tile_rows(ys_hbm, d1_ref[0, 0, r]), _tile_rows(y1_ref, r), sem.at[1]).start()
        return carry

    lax.fori_loop(0, tm, issue, 0, unroll=8)
    pltpu.make_async_copy(ys_hbm.at[pl.ds(0, tm * ROW_TILE), :], y0_ref, sem.at[0]).wait()
    pltpu.make_async_copy(ys_hbm.at[pl.ds(0, tm * ROW_TILE), :], y1_ref, sem.at[1]).wait()

    rw = rw_ref[0]
    w0 = jnp.broadcast_to(rw[:, 0:1], (tm, LANES))
    w1 = jnp.broadcast_to(rw[:, 1:2], (tm, LANES))
    y0 = _load_row_tiles(y0_ref, tm)
    y1 = _load_row_tiles(y1_ref, tm)
    cols = []
    for c in range(ROW_TILE):
        sl = slice(c * LANES, (c + 1) * LANES)
        cols.append(x_ref[0, :, sl] + g2_ref[0, :, sl] * (y0[c] * w0 + y1[c] * w1))
    if final:
        ssq = cols[0] * cols[0]
        for c in range(1, ROW_TILE):
            ssq = ssq + cols[c] * cols[c]
        inv = lax.rsqrt(jnp.sum(ssq, axis=-1, keepdims=True) * (1.0 / (ROW_TILE * LANES)) + EPS)
        cols = [cols[c] * inv * fg_ref[:, c * LANES:(c + 1) * LANES] for c in range(ROW_TILE)]
    for c in range(ROW_TILE):
        o_ref[0, :, c * LANES:(c + 1) * LANES] = cols[c]


def _combine(x, ys, dest0, dest1, rw, g2, final_g, final):
    b, s, d = x.shape
    tm = min(TM_CMB, s)
    row = lambda i, j: (i, j, 0)
    idx_spec = pl.BlockSpec((1, 1, tm), lambda i, j: (i * (s // tm) + j, 0, 0), memory_space=pltpu.SMEM)
    return pl.pallas_call(
        functools.partial(_combine_kernel, final=final),
        grid=(b, s // tm),
        in_specs=[idx_spec, idx_spec,
                  pl.BlockSpec((1, tm, d), row),
                  pl.BlockSpec(memory_space=pl.ANY),
                  pl.BlockSpec((1, tm, LANES), row),
                  pl.BlockSpec((1, 1, d), lambda i, j: (i, 0, 0)),
                  pl.BlockSpec((1, d), lambda i, j: (0, 0))],
        out_specs=pl.BlockSpec((1, tm, d), row),
        out_shape=jax.ShapeDtypeStruct((b, s, d), F32),
        scratch_shapes=[pltpu.VMEM((tm * ROW_TILE, LANES), F32), pltpu.VMEM((tm * ROW_TILE, LANES), F32),
                        pltpu.SemaphoreType.DMA((2,))],
        compiler_params=_cp(2),
        name="moe_combine_final" if final else "moe_combine",
    )(dest0.reshape(-1, 1, tm), dest1.reshape(-1, 1, tm), x, ys, rw, g2.reshape(b, 1, d),
      final_g.reshape(1, d))


def _pad_heads(w, n_heads, offsets):
    d = w.shape[0]
    w = w.reshape(d, n_heads, HEAD_DIM)
    z = jnp.zeros((d, n_heads, HEAD_DIM), w.dtype)
    off = jnp.asarray(offsets, jnp.int32).reshape(1, n_heads, 1)
    blk = jnp.where(off == 0, jnp.concatenate([w, z], axis=-1), jnp.concatenate([z, w], axis=-1))
    return blk.reshape(d, n_heads * LANES)


def _layout_w_in(w_in):
    d = w_in.shape[0]
    kvw = NSA_KV * HEAD_DIM
    sizes = (NSA_HEADS * HEAD_DIM, kvw, kvw, kvw, kvw, kvw, kvw, NSA_HEADS * 3,
             FOX_HEADS * HEAD_DIM, FOX_HEADS * HEAD_DIM, FOX_HEADS * HEAD_DIM, FOX_HEADS,
             SB_HEADS * HEAD_DIM, SB_HEADS * HEAD_DIM, SB_HEADS * HEAD_DIM)
    pts = np.cumsum(sizes)[:-1].tolist()
    (qa, kca, vca, ksa, vsa, kwa, vwa, ga, qb, kb, vb, fb, qc, kc, vc) = jnp.split(w_in, pts, axis=1)
    scale = HEAD_DIM ** -0.5
    qa_p = _pad_heads(qa * scale, NSA_HEADS, [0] * NSA_GROUP + [HEAD_DIM] * NSA_GROUP)
    qb_p = _pad_heads(qb * scale, FOX_HEADS, [0, HEAD_DIM, 0, HEAD_DIM])
    qc_p = _pad_heads(qc * scale, SB_HEADS, [0, HEAD_DIM, 0, HEAD_DIM])
    main = jnp.concatenate([qa_p, ksa, kwa, vsa, vwa, qb_p, kb, vb, qc_p, kc, vc], axis=1).astype(BF16)
    cmp_w = jnp.concatenate([kca, vca], axis=1).astype(BF16)
    zpad = lambda n: jnp.zeros((d, n), w_in.dtype)
    per_grp = NSA_GROUP * 3
    small = jnp.concatenate([ga[:, :per_grp], zpad(LANES - per_grp), ga[:, per_grp:], zpad(LANES - per_grp),
                             fb[:, 0:2], zpad(LANES - 2), fb[:, 2:4], zpad(LANES - 2)], axis=1).astype(BF16)
    return main, cmp_w, small


def _layout_cmp(w1, w2):
    hid = w1.shape[1]
    w1r = w1.reshape(2, CMP_STRIDE, HEAD_DIM, hid)
    z = jnp.zeros((CMP_STRIDE, HEAD_DIM, hid), w1.dtype)
    cols = []
    for half in range(2):
        for g in range(NSA_KV):
            parts = [w1r[half] if gg == g else z for gg in range(NSA_KV)]
            cols.append(jnp.concatenate(parts, axis=1).reshape(CMP_STRIDE * LANES, hid))
    wcat = jnp.concatenate(cols, axis=1).astype(BF16)
    zz = jnp.zeros_like(w2)
    w2bd = jnp.concatenate([jnp.concatenate([w2, zz], axis=1),
                            jnp.concatenate([zz, w2], axis=1)], axis=0).astype(BF16)
    return wcat, w2bd


def _rope_tables(pos):
    inv = jnp.exp(jnp.arange(ROPE_HALF, dtype=F32) * (-2.0 * math.log(ROPE_THETA) / ROPE_DIM))
    ang = pos.astype(F32)[:, None] * inv[None, :]
    cos, sin = jnp.cos(ang), jnp.sin(ang)
    n = pos.shape[0]
    z8 = jnp.zeros((n, ROPE_HALF), F32)
    rest1 = jnp.ones((n, HEAD_DIM - ROPE_DIM), F32)
    rest0 = jnp.zeros((n, HEAD_DIM - ROPE_DIM), F32)
    c = jnp.concatenate([cos, cos, rest1], axis=1)
    s1 = jnp.concatenate([z8, sin, rest0], axis=1)
    s2 = jnp.concatenate([-sin, z8, rest0], axis=1)
    dup = lambda a: jnp.concatenate([a, a], axis=1)
    return dup(c), dup(s1), dup(s2)


def _static_tables(s):
    n_cmp_pad = s // CMP_STRIDE
    n = np.arange(n_cmp_pad)[:, None]
    j = np.arange(LANES)[None, :]
    n_sel = s // SEL_LEN
    cover = ((n * CMP_STRIDE < j * SEL_LEN + SEL_LEN) & (n * CMP_STRIDE + CMP_LEN > j * SEL_LEN)
             & (j < n_sel)).astype(np.float32)
    nt = s // TK_FOX
    key = np.arange(nt)[:, None, None] * TK_FOX + np.arange(TK_FOX)[None, None, :]
    expand = (key // SEL_LEN == np.arange(LANES)[None, :, None]).astype(np.float32)
    return jnp.asarray(cover, BF16), jnp.asarray(expand, BF16)


def _dispatch_plan(ri, t):
    eid = ri.reshape(t, LANES)[:, :2].reshape(-1)
    n_assign = eid.shape[0]
    onehot = (eid[:, None] == jnp.arange(N_EXPERTS, dtype=jnp.int32)[None, :]).astype(jnp.int32)
    csum = jnp.cumsum(onehot, axis=0)
    counts = csum[-1]
    rank = jnp.take_along_axis(csum, eid[:, None], axis=1)[:, 0] - 1
    padded = ((counts + TM_EXP - 1) // TM_EXP) * TM_EXP
    pends = jnp.cumsum(padded)
    pstarts = pends - padded
    dest = pstarts[eid] + rank
    n_tiles = -(-(n_assign + N_EXPERTS * (TM_EXP - 1)) // TM_EXP)
    tile_start = jnp.arange(n_tiles, dtype=jnp.int32) * TM_EXP
    tile_expert = jnp.minimum(jnp.sum((pends[None, :] <= tile_start[:, None]).astype(jnp.int32), axis=1),
                              N_EXPERTS - 1).astype(jnp.int32)
    n_used = (pends[-1] // TM_EXP).astype(jnp.int32).reshape(1)
    return dest.astype(jnp.int32), n_tiles * TM_EXP, tile_expert, n_used


def kernel(x, c, norm1_g, norm2_g, ada_w, ada_b, w_in, b_forget, cmp_pos_k, cmp_w1_k, cmp_w2_k,
           cmp_pos_v, cmp_w1_v, cmp_w2_v, out_norm_g, w_out, router_group_w, router_group_b,
           router_expert_w, router_expert_b, expert_w1, expert_w3, expert_w2, final_g):
    b, s, d = x.shape
    depth = ada_w.shape[0]
    t = b * s
    mod = _modulation(c, ada_w, ada_b)
    rope_c, rope_1, rope_2 = _rope_tables(jnp.arange(s))
    n_cmp_pad = s // CMP_STRIDE
    crc, cr1, cr2 = _rope_tables(jnp.arange(n_cmp_pad) * CMP_STRIDE + (CMP_LEN - 1))
    cover, expand = _static_tables(s)

    for l in range(depth):
        sh1, sc1, g1, sh2, sc2, g2 = [mod[l][:, i * d:(i + 1) * d] for i in range(6)]
        w_main, w_cmp, w_small = _layout_w_in(w_in[l])
        u, kc, vc, small = _in_projection(x, norm1_g[l], sc1, sh1, w_main, w_cmp, w_small,
                                          rope_c, rope_1, rope_2)
        bf = b_forget[l]
        zf = jnp.zeros((LANES - 2,), F32)
        b_pairs = jnp.stack([jnp.concatenate([bf[0:2], zf]), jnp.concatenate([bf[2:4], zf])]).reshape(2, 1, LANES)
        cq, ckt = _forget_cumsum(small, b_pairs)
        wk, w2k = _layout_cmp(cmp_w1_k[l], cmp_w2_k[l])
        wv, w2v = _layout_cmp(cmp_w1_v[l], cmp_w2_v[l])
        ck, cv = _compress(kc, vc, wk, wv, cmp_w1_k[l], cmp_w1_v[l],
                           jnp.broadcast_to(cmp_pos_k[l].reshape(1, -1), (8, CMP_LEN * HEAD_DIM)),
                           jnp.broadcast_to(cmp_pos_v[l].reshape(1, -1), (8, CMP_LEN * HEAD_DIM)),
                           w2k, w2v, crc, cr1, cr2)
        gn = out_norm_g[l].reshape(1, -1)
        o_a = _nsa_attention(u, ck, cv, small, gn[:, :NSA_HEADS * HEAD_DIM], cover, expand)
        o_b = _fox_attention(u, cq, ckt, gn[:, NSA_HEADS * HEAD_DIM:(NSA_HEADS + FOX_HEADS) * HEAD_DIM])
        o_c = _sb_attention(u, gn[:, (NSA_HEADS + FOX_HEADS) * HEAD_DIM:])

        wr = jnp.concatenate([router_group_w[l], router_expert_w[l],
                              jnp.zeros((d, LANES - N_GROUPS - N_EXPERTS), F32)], axis=1)
        wr_hi = wr.astype(BF16)
        wr_lo = (wr - wr_hi.astype(F32)).astype(BF16)
        br = jnp.concatenate([router_group_b[l], router_expert_b[l],
                              jnp.zeros((LANES - N_GROUPS - N_EXPERTS,), F32)]).reshape(1, LANES)
        x, h2, rw, ri = _out_projection(o_a, o_b, o_c, x, w_out[l].astype(BF16), g1, norm2_g[l],
                                        sc2, sh2, wr_hi, wr_lo, br)

        dest, p_rows, tile_expert, n_used = _dispatch_plan(ri, t)
        xs = _dispatch(h2, dest, p_rows)
        ys = _expert_mlp(xs, tile_expert, n_used, expert_w1[l], expert_w3[l], expert_w2[l])
        x = _combine(x, ys, dest[0::2], dest[1::2], rw, g2, final_g, final=(l == depth - 1))
    return x
```

```python
import functools
import math

import numpy as np
import jax
import jax.numpy as jnp
from jax import lax
from jax.experimental import pallas as pl
from jax.experimental.pallas import tpu as pltpu

F32 = jnp.float32
BF16 = jnp.bfloat16

HEAD_DIM = 64
LANES = 128
N_HEADS = 16
NSA_HEADS = 8
NSA_KV = 2
NSA_GROUP = 4
FOX_HEADS = 4
SB_HEADS = 4
ROPE_DIM = 16
ROPE_HALF = 8
ROPE_THETA = 500000.0
CMP_LEN = 32
CMP_STRIDE = 16
CMP_HIDDEN = 128
SEL_LEN = 64
SEL_TOPN = 16
WINDOW = 512
FORCE_SCORE = 1.0e4
N_GROUPS = 4
EXPERTS_PER_GROUP = 8
N_EXPERTS = 32
EPS = 1e-6
NEG = -1e30

COL_QA = 0
COL_KS = 1024
COL_KW = 1152
ROPE_COLS = 1280
COL_VS = 1280
COL_VW = 1408
COL_FOX = 1536
COL_SB = 2560
N_MAIN = 3584
N_SMALL = 512
PROJ_CHUNK = 1280

TM_PROJ = 512
TQ_NSA = 128
TK_ATT = 256
TQ_PAIR = 256
TK_FOX = 512
ROWS = 32
TM_OUT = 512
TM_EXP = 256
TM_CMB = 512
COPY_CHUNK = 512
ROW_TILE = 8
VMEM_LIMIT = 56 * 1024 * 1024


def _cp(n_axes, vmem=VMEM_LIMIT):
    return pltpu.CompilerParams(dimension_semantics=("arbitrary",) * n_axes, vmem_limit_bytes=vmem)


def _dot(a, b):
    return jnp.dot(a, b, preferred_element_type=F32)


def _dot_nt(a, b):
    return lax.dot_general(a, b, (((1,), (1,)), ((), ())), preferred_element_type=F32)


def _split_bf16(x, parts):
    out = []
    r = x
    for _ in range(parts):
        p = r.astype(BF16)
        out.append(p)
        r = r - p.astype(F32)
    return out


def _dot_split(x, m, parts):
    acc = None
    for p in _split_bf16(x, parts):
        d = _dot(p, m)
        acc = d if acc is None else acc + d
    return acc


def _rope(x, c, s1, s2):
    return x * c + pltpu.roll(x, ROPE_HALF, 1) * s1 + pltpu.roll(x, LANES - ROPE_HALF, 1) * s2


def _softplus(z):
    return jnp.maximum(z, 0.0) + jnp.log(1.0 + jnp.exp(-jnp.abs(z)))


def _mod_kernel(c_ref, w_ref, b_ref, o_ref):
    c = c_ref[...]
    cond = c * (1.0 / (1.0 + jnp.exp(-c)))
    o_ref[0] = _dot(cond, w_ref[0]) + b_ref[0]


def _modulation(c, ada_w, ada_b):
    depth, d, n = ada_w.shape
    b = c.shape[0]
    tn = 1024
    return pl.pallas_call(
        _mod_kernel,
        grid=(depth, n // tn),
        in_specs=[pl.BlockSpec((b, d), lambda l, j: (0, 0)),
                  pl.BlockSpec((1, d, tn), lambda l, j: (l, 0, j)),
                  pl.BlockSpec((1, 1, tn), lambda l, j: (l, 0, j))],
        out_specs=pl.BlockSpec((1, b, tn), lambda l, j: (l, 0, j)),
        out_shape=jax.ShapeDtypeStruct((depth, b, n), F32),
        compiler_params=_cp(2),
        name="modulation",
    )(c, ada_w, ada_b.reshape(depth, 1, n))


def _inproj_kernel(x_ref, g_ref, sc_ref, sh_ref, w_ref, wc_ref, ws_ref, rc_ref, r1_ref, r2_ref,
                   u_ref, kc_ref, vc_ref, sm_ref):
    x = x_ref[0]
    ms = jnp.mean(x * x, axis=-1, keepdims=True)
    h = (x * lax.rsqrt(ms + EPS) * g_ref[...]) * (1.0 + sc_ref[0]) + sh_ref[0]
    hb = h.astype(BF16)
    rc, r1, r2 = rc_ref[...], r1_ref[...], r2_ref[...]
    for j in range(N_MAIN // PROJ_CHUNK + (1 if N_MAIN % PROJ_CHUNK else 0)):
        lo = j * PROJ_CHUNK
        hi = min(lo + PROJ_CHUNK, N_MAIN)
        acc = _dot(hb, w_ref[:, lo:hi])
        if lo < ROPE_COLS:
            for k in range((hi - lo) // LANES):
                blk = acc[:, k * LANES:(k + 1) * LANES]
                u_ref[0, :, lo + k * LANES:lo + (k + 1) * LANES] = _rope(blk, rc, r1, r2).astype(BF16)
        else:
            u_ref[0, :, lo:hi] = acc.astype(BF16)
    cmp_in = _dot(hb, wc_ref[...])
    kc_ref[0] = cmp_in[:, :LANES].astype(BF16)
    vc_ref[0] = cmp_in[:, LANES:].astype(BF16)
    sm_ref[0] = _dot(hb, ws_ref[...])


def _in_projection(x, g, sc, sh, w_main, w_cmp, w_small, rope_c, rope_1, rope_2):
    b, s, d = x.shape
    tm = min(TM_PROJ, s)
    row = lambda i, j: (i, j, 0)
    const2 = lambda i, j: (0, 0)
    per_b = lambda i, j: (i, 0, 0)
    seq = lambda i, j: (j, 0)
    return pl.pallas_call(
        _inproj_kernel,
        grid=(b, s // tm),
        in_specs=[pl.BlockSpec((1, tm, d), row),
                  pl.BlockSpec((1, d), const2),
                  pl.BlockSpec((1, 1, d), per_b),
                  pl.BlockSpec((1, 1, d), per_b),
                  pl.BlockSpec((d, N_MAIN), const2),
                  pl.BlockSpec((d, 2 * LANES), const2),
                  pl.BlockSpec((d, N_SMALL), const2),
                  pl.BlockSpec((tm, LANES), seq),
                  pl.BlockSpec((tm, LANES), seq),
                  pl.BlockSpec((tm, LANES), seq)],
        out_specs=[pl.BlockSpec((1, tm, N_MAIN), row),
                   pl.BlockSpec((1, tm, LANES), row),
                   pl.BlockSpec((1, tm, LANES), row),
                   pl.BlockSpec((1, tm, N_SMALL), row)],
        out_shape=[jax.ShapeDtypeStruct((b, s, N_MAIN), BF16),
                   jax.ShapeDtypeStruct((b, s, LANES), BF16),
                   jax.ShapeDtypeStruct((b, s, LANES), BF16),
                   jax.ShapeDtypeStruct((b, s, N_SMALL), F32)],
        compiler_params=_cp(2),
        name="in_projection",
    )(x, g.reshape(1, d), sc.reshape(b, 1, d), sh.reshape(b, 1, d), w_main, w_cmp, w_small,
      rope_c, rope_1, rope_2)


def _cumf_kernel(f_ref, b_ref, cq_ref, ckt_ref):
    n_chunks = f_ref.shape[1] // TK_ATT
    r = lax.broadcasted_iota(jnp.int32, (TK_ATT, TK_ATT), 0)
    c = lax.broadcasted_iota(jnp.int32, (TK_ATT, TK_ATT), 1)
    tri = jnp.where(c <= r, 1.0, 0.0).astype(BF16)
    carry = jnp.zeros((1, LANES), F32)
    for j in range(n_chunks):
        f = f_ref[0, j * TK_ATT:(j + 1) * TK_ATT, :] + b_ref[0]
        ls = -_softplus(-f)
        acc = None
        for p in _split_bf16(ls, 3):
            dd = _dot(tri, p)
            acc = dd if acc is None else acc + dd
        cs = acc + carry
        cq_ref[0, j * TK_ATT:(j + 1) * TK_ATT, :] = cs
        ckt_ref[0, 0, j] = cs.T[:8, :]
        carry = cs[TK_ATT - 1:TK_ATT, :]


def _forget_cumsum(small, b_pairs):
    b, s, _ = small.shape
    return pl.pallas_call(
        _cumf_kernel,
        grid=(b, 2),
        in_specs=[pl.BlockSpec((1, s, LANES), lambda i, p: (i, 0, 2 + p)),
                  pl.BlockSpec((1, 1, LANES), lambda i, p: (p, 0, 0))],
        out_specs=[pl.BlockSpec((1, s, LANES), lambda i, p: (i, 0, p)),
                   pl.BlockSpec((1, 1, s // TK_ATT, 8, TK_ATT), lambda i, p: (i, p, 0, 0, 0))],
        out_shape=[jax.ShapeDtypeStruct((b, s, 2 * LANES), F32),
                   jax.ShapeDtypeStruct((b, 2, s // TK_ATT, 8, TK_ATT), F32)],
        compiler_params=_cp(2),
        name="forget_cumsum",
    )(small, b_pairs)


def _compress_kernel(ks_ref, vs_ref, wk_ref, wv_ref, w1k_ref, w1v_ref, pek_ref, pev_ref,
                     w2k_ref, w2v_ref, rc_ref, r1_ref, r2_ref, ck_ref, cv_ref):
    def one(seg_ref, w_ref, w1_ref, pe_ref, w2_ref):
        p = _dot(seg_ref[0], w_ref[...])
        half = 2 * CMP_HIDDEN
        bias = _dot(pe_ref[...].astype(BF16), w1_ref[...].astype(BF16))[0:1, :]
        bias2 = jnp.concatenate([bias, bias], axis=1)
        n = p.shape[0]
        hid = p[:, :half] + pltpu.roll(p[:, half:], n - 1, 0) + bias2
        act = hid * (1.0 / (1.0 + jnp.exp(-hid)))
        return _dot(act.astype(BF16), w2_ref[...])

    ck = one(ks_ref, wk_ref, w1k_ref, pek_ref, w2k_ref)
    ck_ref[0] = _rope(ck, rc_ref[...], r1_ref[...], r2_ref[...]).astype(BF16)
    cv_ref[0] = one(vs_ref, wv_ref, w1v_ref, pev_ref, w2v_ref).astype(BF16)


def _compress(kc, vc, wk, wv, w1k, w1v, pek, pev, w2k, w2v, rc, r1, r2):
    b, s, _ = kc.shape
    n = s // CMP_STRIDE
    width = CMP_STRIDE * LANES
    kseg = kc.reshape(b, n, width)
    vseg = vc.reshape(b, n, width)
    seg = pl.BlockSpec((1, n, width), lambda i: (i, 0, 0))
    full = lambda a: pl.BlockSpec(a.shape, lambda i: (0,) * a.ndim)
    return pl.pallas_call(
        _compress_kernel,
        grid=(b,),
        in_specs=[seg, seg, full(wk), full(wv), full(w1k), full(w1v), full(pek), full(pev),
                  full(w2k), full(w2v), full(rc), full(r1), full(r2)],
        out_specs=[pl.BlockSpec((1, n, LANES), lambda i: (i, 0, 0)),
                   pl.BlockSpec((1, n, LANES), lambda i: (i, 0, 0))],
        out_shape=[jax.ShapeDtypeStruct((b, n, LANES), BF16),
                   jax.ShapeDtypeStruct((b, n, LANES), BF16)],
        compiler_params=_cp(1),
        name="nsa_compress",
    )(kseg, vseg, wk, wv, w1k, w1v, pek, pev, w2k, w2v, rc, r1, r2)


def _nsa_kernel(q_ref, ck_ref, cv_ref, ks_ref, vs_ref, kw_ref, vw_ref, gate_ref, gn_ref,
                cover_ref, expand_ref, o_ref, m_ref, l_ref, acc_ref):
    tq = q_ref.shape[1]
    g = pl.program_id(1)
    t0 = pl.program_id(2) * tq
    q = q_ref[0]
    qall = jnp.concatenate([q[:, h * LANES:(h + 1) * LANES] for h in range(NSA_GROUP)], axis=0)
    tpos = t0 + lax.broadcasted_iota(jnp.int32, (tq, 1), 0)
    lane = lax.broadcasted_iota(jnp.int32, (tq, LANES), 1)

    s = _dot_nt(qall, ck_ref[0]).reshape(NSA_GROUP, tq, LANES)
    valid = (CMP_STRIDE * lane + (CMP_LEN - 1)) <= tpos
    sm = jnp.where(valid[None], s, NEG)
    mx = jnp.max(sm, axis=-1, keepdims=True)
    e = jnp.where(valid[None], jnp.exp(sm - mx), 0.0)
    den = jnp.sum(e, axis=-1, keepdims=True)
    p_cmp = e * jnp.where(den > 0.0, 1.0 / den, 0.0)
    o_cmp = _dot(p_cmp.astype(BF16).reshape(NSA_GROUP * tq, LANES), cv_ref[0])
    o_cmp = o_cmp.reshape(NSA_GROUP, tq, LANES)

    p_sum = p_cmp[0] + p_cmp[1] + p_cmp[2] + p_cmp[3]
    imp = _dot_split(p_sum, cover_ref[...], 3)
    cur = jnp.right_shift(tpos, int(math.log2(SEL_LEN)))
    forced = (lane == 0) | (lane == cur) | (lane == cur - 1)
    causal_blk = (lane * SEL_LEN) <= tpos
    n_sel = expand_ref.shape[0] * (TK_FOX // SEL_LEN)
    score = jnp.where(forced, FORCE_SCORE, jnp.where(causal_blk, imp, -1.0))
    score = jnp.where(lane < n_sel, score, -2.0)
    cnt = jnp.zeros((tq, LANES), F32)
    for j in range(n_sel):
        cj = score[:, j:j + 1]
        beats = (cj > score) | ((cj == score) & (lane > j))
        cnt = cnt + jnp.where(beats, 1.0, 0.0)
    sel = jnp.where((cnt < float(min(SEL_TOPN, n_sel))) & (lane < n_sel), 1.0, 0.0).astype(BF16)

    def biased(bias):
        def adjust(h, t_off, cols):
            return [cols[kk] + bias[t_off:t_off + ROWS, kk * LANES:(kk + 1) * LANES]
                    for kk in range(len(cols))]
        return adjust

    kcol_s = lax.broadcasted_iota(jnp.int32, (tq, TK_FOX), 1)

    def sel_tile(kt, carry):
        k0 = pl.multiple_of(kt * TK_FOX, TK_FOX)
        hit = _dot(sel, expand_ref[kt])
        ok = (hit > 0.5) & ((k0 + kcol_s) <= tpos)
        s_t = _dot_nt(qall, ks_ref[0, pl.ds(k0, TK_FOX), :])
        _softmax_tile(s_t, vs_ref[0, pl.ds(k0, TK_FOX), :], m_ref, l_ref, acc_ref, NSA_GROUP, tq,
                      biased(jnp.where(ok, 0.0, NEG)))
        return carry

    _softmax_init(m_ref, l_ref, acc_ref)
    lax.fori_loop(0, lax.div(t0 + tq - 1, TK_FOX) + 1, sel_tile, 0)
    o_sel = _softmax_result(l_ref, acc_ref).reshape(NSA_GROUP, tq, LANES)

    kcol_w = lax.broadcasted_iota(jnp.int32, (tq, TK_ATT), 1)

    def win_tile(kt, carry):
        k0 = pl.multiple_of(kt * TK_ATT, TK_ATT)
        kp = k0 + kcol_w
        ok = (kp <= tpos) & (kp > tpos - WINDOW)
        s_t = _dot_nt(qall, kw_ref[0, pl.ds(k0, TK_ATT), :])
        _softmax_tile(s_t, vw_ref[0, pl.ds(k0, TK_ATT), :], m_ref, l_ref, acc_ref, NSA_GROUP, tq,
                      biased(jnp.where(ok, 0.0, NEG)))
        return carry

    _softmax_init(m_ref, l_ref, acc_ref)
    lo_w = lax.div(jnp.maximum(t0 - (WINDOW - 1), 0), TK_ATT)
    lax.fori_loop(lo_w, lax.div(t0 + tq - 1, TK_ATT) + 1, win_tile, 0)
    o_win = _softmax_result(l_ref, acc_ref).reshape(NSA_GROUP, tq, LANES)

    gt = gate_ref[0]
    gt = 1.0 / (1.0 + jnp.exp(-gt))
    mine = (lane >= g * HEAD_DIM) & (lane < (g + 1) * HEAD_DIM)
    outs = []
    for h in range(NSA_GROUP):
        o = (gt[:, 3 * h:3 * h + 1] * o_cmp[h] + gt[:, 3 * h + 1:3 * h + 2] * o_sel[h]
             + gt[:, 3 * h + 2:3 * h + 3] * o_win[h])
        o = jnp.where(mine, o, 0.0)
        ms = jnp.sum(o * o, axis=-1, keepdims=True) * (1.0 / HEAD_DIM)
        o = o * lax.rsqrt(ms + EPS)
        outs.append(o + pltpu.roll(o, HEAD_DIM, 1))
    left = lane < HEAD_DIM
    o_ref[0, :, :LANES] = (jnp.where(left, outs[0], outs[1]) * gn_ref[:, :LANES]).astype(o_ref.dtype)
    o_ref[0, :, LANES:] = (jnp.where(left, outs[2], outs[3]) * gn_ref[:, LANES:]).astype(o_ref.dtype)


def _nsa_attention(u, ck, cv, small, gn, cover, expand):
    b, s, _ = u.shape
    tq = min(TQ_NSA, s)
    n_cmp = ck.shape[1]
    blk = LANES
    kv = lambda col: pl.BlockSpec((1, s, LANES), lambda i, g, j, col=col: (i, 0, col // blk))
    return pl.pallas_call(
        _nsa_kernel,
        grid=(b, NSA_KV, s // tq),
        in_specs=[pl.BlockSpec((1, tq, NSA_GROUP * LANES), lambda i, g, j: (i, j, g)),
                  pl.BlockSpec((1, n_cmp, LANES), lambda i, g, j: (i, 0, 0)),
                  pl.BlockSpec((1, n_cmp, LANES), lambda i, g, j: (i, 0, 0)),
                  kv(COL_KS), kv(COL_VS), kv(COL_KW), kv(COL_VW),
                  pl.BlockSpec((1, tq, LANES), lambda i, g, j: (i, j, g)),
                  pl.BlockSpec((1, 2 * LANES), lambda i, g, j: (0, g)),
                  pl.BlockSpec(cover.shape, lambda i, g, j: (0, 0)),
                  pl.BlockSpec(expand.shape, lambda i, g, j: (0, 0, 0))],
        out_specs=pl.BlockSpec((1, tq, 2 * LANES), lambda i, g, j: (i, j, g)),
        out_shape=jax.ShapeDtypeStruct((b, s, NSA_HEADS * HEAD_DIM), BF16),
        scratch_shapes=_softmax_scratch(NSA_GROUP * tq),
        compiler_params=_cp(3),
        name="nsa_attention",
    )(u, ck, cv, u, u, u, u, small, gn, cover, expand)


def _pair_finish(acc, gn_ref, o_ref, tq):
    lane = lax.broadcasted_iota(jnp.int32, (tq, LANES), 1)
    left = lane < HEAD_DIM
    o = jnp.where(left, acc[0], acc[1])
    o2 = o * o
    ms_l = jnp.sum(jnp.where(left, o2, 0.0), axis=-1, keepdims=True) * (1.0 / HEAD_DIM)
    ms_r = jnp.sum(jnp.where(left, 0.0, o2), axis=-1, keepdims=True) * (1.0 / HEAD_DIM)
    inv = jnp.where(left, lax.rsqrt(ms_l + EPS), lax.rsqrt(ms_r + EPS))
    o_ref[0] = (o * inv * gn_ref[...]).astype(o_ref.dtype)


def _softmax_tile(s, v, m_ref, l_ref, acc_ref, heads, tq, adjust):
    nk = s.shape[1] // LANES
    p_rows = []
    for h in range(heads):
        for c in range(tq // ROWS):
            t_off = c * ROWS
            r0 = h * tq + t_off
            cols = [s[r0:r0 + ROWS, k * LANES:(k + 1) * LANES] for k in range(nk)]
            cols = adjust(h, t_off, cols)
            mx = cols[0]
            for k in range(1, nk):
                mx = jnp.maximum(mx, cols[k])
            m_old = m_ref[r0:r0 + ROWS, :]
            m_new = jnp.maximum(m_old, jnp.max(mx, axis=-1, keepdims=True))
            alpha = jnp.exp(m_old - m_new)
            pks = [jnp.exp(cols[k] - m_new) for k in range(nk)]
            psum = pks[0]
            for k in range(1, nk):
                psum = psum + pks[k]
            p_rows.append(jnp.concatenate([pk.astype(BF16) for pk in pks], axis=1))
            l_ref[r0:r0 + ROWS, :] = alpha * l_ref[r0:r0 + ROWS, :] + psum
            acc_ref[r0:r0 + ROWS, :] = alpha * acc_ref[r0:r0 + ROWS, :]
            m_ref[r0:r0 + ROWS, :] = m_new
    p = jnp.concatenate(p_rows, axis=0)
    acc_ref[...] += _dot(p, v)


def _softmax_init(m_ref, l_ref, acc_ref):
    m_ref[...] = jnp.full(m_ref.shape, NEG, F32)
    l_ref[...] = jnp.zeros(l_ref.shape, F32)
    acc_ref[...] = jnp.zeros(acc_ref.shape, F32)


def _softmax_result(l_ref, acc_ref):
    return acc_ref[...] / jnp.sum(l_ref[...], axis=-1, keepdims=True)


def _softmax_scratch(rows):
    return [pltpu.VMEM((rows, LANES), F32), pltpu.VMEM((rows, LANES), F32), pltpu.VMEM((rows, LANES), F32)]


def _fox_kernel(q_ref, k_ref, v_ref, cq_ref, ckt_ref, gn_ref, o_ref,
                m_ref, l_ref, acc_ref, cqr_ref):
    tq = q_ref.shape[1]
    tk = TK_FOX
    t0 = pl.program_id(2) * tq
    q = q_ref[0]
    qall = jnp.concatenate([q[:, :LANES], q[:, LANES:]], axis=0)
    cq = cq_ref[0]
    cqr_ref[0:tq, :] = jnp.broadcast_to(cq[:, 0:1], (tq, LANES))
    cqr_ref[tq:2 * tq, :] = jnp.broadcast_to(cq[:, 1:2], (tq, LANES))
    _softmax_init(m_ref, l_ref, acc_ref)
    n_tiles = lax.div(t0 + tq - 1, tk) + 1
    diag = (lax.broadcasted_iota(jnp.int32, (ROWS, LANES), 1)
            - lax.broadcasted_iota(jnp.int32, (ROWS, LANES), 0))

    def tile(kt, masked):
        k0 = pl.multiple_of(kt * tk, tk)
        s = _dot_nt(qall, k_ref[0, pl.ds(k0, tk), :])
        cks = [ckt_ref[0, 0, kt * (tk // TK_ATT) + j] for j in range(tk // TK_ATT)]

        def adjust(h, t_off, cols):
            out = []
            cqr = cqr_ref[h * tq + t_off:h * tq + t_off + ROWS, :]
            for kk in range(tk // LANES):
                lo = (kk * LANES) % TK_ATT
                ck = cks[(kk * LANES) // TK_ATT][h:h + 1, lo:lo + LANES]
                val = (cols[kk] - ck) + cqr
                if masked:
                    val = jnp.where(diag <= (t0 + t_off) - (k0 + kk * LANES), val, NEG)
                out.append(val)
            return out

        _softmax_tile(s, v_ref[0, pl.ds(k0, tk), :], m_ref, l_ref, acc_ref, 2, tq, adjust)

    def full_tile(kt, carry):
        tile(kt, False)
        return carry

    lax.fori_loop(0, n_tiles - 1, full_tile, 0)
    tile(n_tiles - 1, True)
    acc = _softmax_result(l_ref, acc_ref).reshape(2, tq, LANES)
    _pair_finish(acc, gn_ref, o_ref, tq)


def _fox_attention(u, cq, ckt, gn):
    b, s, _ = u.shape
    tq = min(TQ_PAIR, s)
    qb = COL_FOX // (2 * LANES)
    kb = (COL_FOX + FOX_HEADS * LANES) // LANES
    vb = kb + 2
    return pl.pallas_call(
        _fox_kernel,
        grid=(b, 2, s // tq),
        in_specs=[pl.BlockSpec((1, tq, 2 * LANES), lambda i, p, j: (i, j, qb + p)),
                  pl.BlockSpec((1, s, LANES), lambda i, p, j: (i, 0, kb + p)),
                  pl.BlockSpec((1, s, LANES), lambda i, p, j: (i, 0, vb + p)),
                  pl.BlockSpec((1, tq, LANES), lambda i, p, j: (i, j, p)),
                  pl.BlockSpec((1, 1, s // TK_ATT, 8, TK_ATT), lambda i, p, j: (i, p, 0, 0, 0)),
                  pl.BlockSpec((1, LANES), lambda i, p, j: (0, p))],
        out_specs=pl.BlockSpec((1, tq, LANES), lambda i, p, j: (i, j, p)),
        out_shape=jax.ShapeDtypeStruct((b, s, FOX_HEADS * HEAD_DIM), BF16),
        scratch_shapes=_softmax_scratch(2 * tq) + [pltpu.VMEM((2 * tq, LANES), F32)],
        compiler_params=_cp(3),
        name="fox_attention",
    )(u, u, u, cq, ckt, gn)


def _sb_kernel(q_ref, k_ref, v_ref, gn_ref, o_ref, rest_ref, acc_ref):
    tq = q_ref.shape[1]
    tk = TK_ATT
    nk = tk // LANES
    t0 = pl.program_id(2) * tq
    q = q_ref[0]
    qall = jnp.concatenate([q[:, :LANES], q[:, LANES:]], axis=0)
    r = lax.broadcasted_iota(jnp.int32, (tk, tk), 0)
    c = lax.broadcasted_iota(jnp.int32, (tk, tk), 1)
    upper = jnp.where(r >= c, 1.0, 0.0).astype(BF16)
    n_tiles = lax.div(t0 + tq - 1, tk) + 1
    diag = (lax.broadcasted_iota(jnp.int32, (ROWS, LANES), 1)
            - lax.broadcasted_iota(jnp.int32, (ROWS, LANES), 0))
    rest_ref[...] = jnp.zeros(rest_ref.shape, F32)
    acc_ref[...] = jnp.zeros(acc_ref.shape, F32)
    chunks = [(h * tq + cc * ROWS, cc * ROWS) for h in range(2) for cc in range(tq // ROWS)]

    def tile(kt, masked):
        k0 = pl.multiple_of(kt * tk, tk)
        z = _dot_nt(qall, k_ref[0, pl.ds(k0, tk), :])

        def strictly_before(t_off, kk):
            return diag < (t0 + t_off) - (k0 + kk * LANES)

        his, los = [], []
        for r0, t_off in chunks:
            hi_c, lo_c = [], []
            for kk in range(nk):
                l = -_softplus(z[r0:r0 + ROWS, kk * LANES:(kk + 1) * LANES])
                if masked:
                    l = jnp.where(strictly_before(t_off, kk), l, 0.0)
                hi = l.astype(BF16)
                hi_c.append(hi)
                lo_c.append((l - hi.astype(F32)).astype(BF16))
            his.append(jnp.concatenate(hi_c, axis=1))
            los.append(jnp.concatenate(lo_c, axis=1))
        cum = _dot(jnp.concatenate(his, axis=0), upper) + _dot(jnp.concatenate(los, axis=0), upper)

        a_rows = []
        for r0, t_off in chunks:
            rest = rest_ref[r0:r0 + ROWS, :]
            a_c = []
            for kk in range(nk):
                sl = (slice(r0, r0 + ROWS), slice(kk * LANES, (kk + 1) * LANES))
                a = jnp.exp(z[sl] + cum[sl] + rest)
                if masked:
                    a = jnp.where(strictly_before(t_off, kk), a, 0.0)
                a_c.append(a.astype(BF16))
            a_rows.append(jnp.concatenate(a_c, axis=1))
            rest_ref[r0:r0 + ROWS, :] = rest + jnp.broadcast_to(cum[r0:r0 + ROWS, 0:1], (ROWS, LANES))
        acc_ref[...] += _dot(jnp.concatenate(a_rows, axis=0), v_ref[0, pl.ds(k0, tk), :])

    tile(n_tiles - 1, True)

    def full_tile(i, carry):
        tile(n_tiles - 2 - i, False)
        return carry

    lax.fori_loop(0, n_tiles - 1, full_tile, 0)
    _pair_finish(acc_ref[...].reshape(2, tq, LANES), gn_ref, o_ref, tq)


def _sb_attention(u, gn):
    b, s, _ = u.shape
    tq = min(TQ_PAIR, s)
    qb = COL_SB // (2 * LANES)
    kb = (COL_SB + SB_HEADS * LANES) // LANES
    vb = kb + 2
    return pl.pallas_call(
        _sb_kernel,
        grid=(b, 2, s // tq),
        in_specs=[pl.BlockSpec((1, tq, 2 * LANES), lambda i, p, j: (i, j, qb + p)),
                  pl.BlockSpec((1, s, LANES), lambda i, p, j: (i, 0, kb + p)),
                  pl.BlockSpec((1, s, LANES), lambda i, p, j: (i, 0, vb + p)),
                  pl.BlockSpec((1, LANES), lambda i, p, j: (0, p))],
        out_specs=pl.BlockSpec((1, tq, LANES), lambda i, p, j: (i, j, p)),
        out_shape=jax.ShapeDtypeStruct((b, s, SB_HEADS * HEAD_DIM), BF16),
        scratch_shapes=[pltpu.VMEM((2 * tq, LANES), F32), pltpu.VMEM((2 * tq, LANES), F32)],
        compiler_params=_cp(3),
        name="sb_attention",
    )(u, u, u, gn)


def _outproj_kernel(oa_ref, ob_ref, oc_ref, x_ref, w_ref, g1_ref, n2_ref, sc_ref, sh_ref,
                    wrh_ref, wrl_ref, br_ref, xo_ref, h_ref, rw_ref, ri_ref):
    na = oa_ref.shape[2]
    nb = ob_ref.shape[2]
    y = _dot(oa_ref[0], w_ref[0:na, :])
    y = y + _dot(ob_ref[0], w_ref[na:na + nb, :])
    y = y + _dot(oc_ref[0], w_ref[na + nb:, :])
    x = x_ref[0] + g1_ref[0] * y
    xo_ref[0] = x
    ms = jnp.mean(x * x, axis=-1, keepdims=True)
    h = (x * lax.rsqrt(ms + EPS) * n2_ref[...]) * (1.0 + sc_ref[0]) + sh_ref[0]
    hb = h.astype(BF16)
    _store_row_tiles(h_ref, h)
    hl = (h - hb.astype(F32)).astype(BF16)
    logit = _dot(hb, wrh_ref[...]) + _dot(hl, wrh_ref[...]) + _dot(hb, wrl_ref[...]) + br_ref[...]

    tm = logit.shape[0]
    lane = lax.broadcasted_iota(jnp.int32, (tm, LANES), 1).astype(F32)
    big = float(LANES)
    is_g = lane < N_GROUPS
    lg = jnp.where(is_g, logit, NEG)
    mg = jnp.max(lg, axis=-1, keepdims=True)
    zg = jnp.sum(jnp.where(is_g, jnp.exp(lg - mg), 0.0), axis=-1, keepdims=True)
    pg = 1.0 / zg
    gi = jnp.min(jnp.where(is_g & (lg == mg), lane, big), axis=-1, keepdims=True)
    e_lane = lane - N_GROUPS
    in_grp = (e_lane >= gi * EXPERTS_PER_GROUP) & (e_lane < (gi + 1) * EXPERTS_PER_GROUP)
    le = jnp.where(in_grp, logit, NEG)
    m1 = jnp.max(le, axis=-1, keepdims=True)
    i1 = jnp.min(jnp.where(in_grp & (le == m1), lane, big), axis=-1, keepdims=True)
    rest = in_grp & (lane != i1)
    le2 = jnp.where(rest, logit, NEG)
    m2 = jnp.max(le2, axis=-1, keepdims=True)
    i2 = jnp.min(jnp.where(rest & (le2 == m2), lane, big), axis=-1, keepdims=True)
    ze = jnp.sum(jnp.where(in_grp, jnp.exp(le - m1), 0.0), axis=-1, keepdims=True)
    p1 = 1.0 / ze
    p2 = jnp.exp(m2 - m1) / ze
    den = p1 + p2
    w1 = pg * (p1 / den)
    w2 = pg * (p2 / den)
    rw_ref[0] = jnp.where(lane == 0.0, w1, jnp.where(lane == 1.0, w2, 0.0))
    ri_ref[0] = jnp.where(lane == 0.0, i1 - N_GROUPS, jnp.where(lane == 1.0, i2 - N_GROUPS, 0.0)).astype(jnp.int32)


def _out_projection(oa, ob, oc, x, w_out, g1, n2, sc, sh, wr_hi, wr_lo, br):
    b, s, d = x.shape
    tm = min(TM_OUT, s)
    row = lambda i, j: (i, j, 0)
    const2 = lambda i, j: (0, 0)
    per_b = lambda i, j: (i, 0, 0)
    return pl.pallas_call(
        _outproj_kernel,
        grid=(b, s // tm),
        in_specs=[pl.BlockSpec((1, tm, oa.shape[2]), row),
                  pl.BlockSpec((1, tm, ob.shape[2]), row),
                  pl.BlockSpec((1, tm, oc.shape[2]), row),
                  pl.BlockSpec((1, tm, d), row),
                  pl.BlockSpec(w_out.shape, const2),
                  pl.BlockSpec((1, 1, d), per_b),
                  pl.BlockSpec((1, d), const2),
                  pl.BlockSpec((1, 1, d), per_b),
                  pl.BlockSpec((1, 1, d), per_b),
                  pl.BlockSpec((d, LANES), const2),
                  pl.BlockSpec((d, LANES), const2),
                  pl.BlockSpec((1, LANES), const2)],
        out_specs=[pl.BlockSpec((1, tm, d), row),
                   pl.BlockSpec((tm * ROW_TILE, LANES), lambda i, j: (i * (s // tm) + j, 0)),
                   pl.BlockSpec((1, tm, LANES), row),
                   pl.BlockSpec((1, tm, LANES), row)],
        out_shape=[jax.ShapeDtypeStruct((b, s, d), F32),
                   jax.ShapeDtypeStruct((b * s * ROW_TILE, LANES), F32),
                   jax.ShapeDtypeStruct((b, s, LANES), F32),
                   jax.ShapeDtypeStruct((b, s, LANES), jnp.int32)],
        compiler_params=_cp(2),
        name="out_projection",
    )(oa, ob, oc, x, w_out, g1.reshape(b, 1, d), n2.reshape(1, d), sc.reshape(b, 1, d),
      sh.reshape(b, 1, d), wr_hi, wr_lo, br)


def _store_row_tiles(ref, val):
    tm = val.shape[0]
    for c in range(ROW_TILE):
        ref[pl.ds(c, tm, stride=ROW_TILE), :] = val[:, c * LANES:(c + 1) * LANES]


def _load_row_tiles(ref, tm):
    return [ref[pl.ds(c, tm, stride=ROW_TILE), :] for c in range(ROW_TILE)]


def _tile_rows(ref, n):
    return ref.at[pl.ds(pl.multiple_of(n * ROW_TILE, ROW_TILE), ROW_TILE), :]


def _dispatch_kernel(idx_ref, src_ref, init_hbm, dst_hbm, sem):
    del init_hbm

    def issue(j, carry):
        tok = lax.shift_right_logical(j, 1)
        pltpu.make_async_copy(_tile_rows(src_ref, tok), _tile_rows(dst_hbm, idx_ref[0, 0, j]), sem).start()
        return carry

    lax.fori_loop(0, COPY_CHUNK, issue, 0, unroll=8)
    for _ in range(2):
        pltpu.make_async_copy(src_ref, dst_hbm.at[pl.ds(0, src_ref.shape[0]), :], sem).wait()


def _dispatch(src, idx, n_dst):
    n = idx.shape[0]
    init = jnp.zeros((n_dst * ROW_TILE, LANES), src.dtype)
    return pl.pallas_call(
        _dispatch_kernel,
        grid=(n // COPY_CHUNK,),
        in_specs=[pl.BlockSpec((1, 1, COPY_CHUNK), lambda i: (i, 0, 0), memory_space=pltpu.SMEM),
                  pl.BlockSpec((COPY_CHUNK // 2 * ROW_TILE, LANES), lambda i: (i, 0)),
                  pl.BlockSpec(memory_space=pl.ANY)],
        out_specs=pl.BlockSpec(memory_space=pl.ANY),
        out_shape=jax.ShapeDtypeStruct((n_dst * ROW_TILE, LANES), src.dtype),
        scratch_shapes=[pltpu.SemaphoreType.DMA],
        input_output_aliases={2: 0},
        compiler_params=_cp(1),
        name="moe_dispatch",
    )(idx.reshape(n // COPY_CHUNK, 1, COPY_CHUNK), src, init)


def _expert_kernel(te_ref, nu_ref, x_ref, w1_ref, w3_ref, w2_ref, y_ref):
    i = pl.program_id(0)

    @pl.when(i < nu_ref[0])
    def _():
        x = jnp.concatenate(_load_row_tiles(x_ref, TM_EXP), axis=1).astype(BF16)
        a = _dot(x, w1_ref[0].astype(BF16))
        g = _dot(x, w3_ref[0].astype(BF16))
        act = (a * (1.0 / (1.0 + jnp.exp(-a)))) * g
        _store_row_tiles(y_ref, _dot(act.astype(BF16), w2_ref[0].astype(BF16)))

    @pl.when(i >= nu_ref[0])
    def _():
        y_ref[...] = jnp.zeros_like(y_ref)


def _expert_mlp(xs, tile_expert, n_used, w1, w3, w2, layer):
    d, de = w1.shape[2], w1.shape[3]
    n_tiles = xs.shape[0] // (TM_EXP * ROW_TILE)
    grid_spec = pltpu.PrefetchScalarGridSpec(
        num_scalar_prefetch=2,
        grid=(n_tiles,),
        in_specs=[pl.BlockSpec((TM_EXP * ROW_TILE, LANES), lambda i, te, nu: (i, 0)),
                  pl.BlockSpec((None, 1, d, de), lambda i, te, nu: (layer, te[i], 0, 0)),
                  pl.BlockSpec((None, 1, d, de), lambda i, te, nu: (layer, te[i], 0, 0)),
                  pl.BlockSpec((None, 1, de, d), lambda i, te, nu: (layer, te[i], 0, 0))],
        out_specs=pl.BlockSpec((TM_EXP * ROW_TILE, LANES), lambda i, te, nu: (i, 0)),
    )
    return pl.pallas_call(
        _expert_kernel,
        grid_spec=grid_spec,
        out_shape=jax.ShapeDtypeStruct(xs.shape, F32),
        compiler_params=_cp(1),
        name="expert_mlp",
    )(tile_expert, n_used, xs, w1, w3, w2)


def _combine_kernel(d0_ref, d1_ref, x_ref, ys_hbm, rw_ref, g2_ref, fg_ref, o_ref, y0_ref, y1_ref, sem,
                    *, final):
    tm = x_ref.shape[1]

    def issue(r, carry):
        pltpu.make_async_copy(_tile_rows(ys_hbm, d0_ref[0, 0, r]), _tile_rows(y0_ref, r), sem.at[0]).start()
        pltpu.make_async_copy(_tile_rows(ys_hbm, d1_ref[0, 0, r]), _tile_rows(y1_ref, r), sem.at[1]).start()
        return carry

    lax.fori_loop(0, tm, issue, 0, unroll=8)
    pltpu.make_async_copy(ys_hbm.at[pl.ds(0, tm * ROW_TILE), :], y0_ref, sem.at[0]).wait()
    pltpu.make_async_copy(ys_hbm.at[pl.ds(0, tm * ROW_TILE), :], y1_ref, sem.at[1]).wait()

    rw = rw_ref[0]
    w0 = jnp.broadcast_to(rw[:, 0:1], (tm, LANES))
    w1 = jnp.broadcast_to(rw[:, 1:2], (tm, LANES))
    y0 = _load_row_tiles(y0_ref, tm)
    y1 = _load_row_tiles(y1_ref, tm)
    cols = []
    for c in range(ROW_TILE):
        sl = slice(c * LANES, (c + 1) * LANES)
        cols.append(x_ref[0, :, sl] + g2_ref[0, :, sl] * (y0[c] * w0 + y1[c] * w1))
    if final:
        ssq = cols[0] * cols[0]
        for c in range(1, ROW_TILE):
            ssq = ssq + cols[c] * cols[c]
        inv = lax.rsqrt(jnp.sum(ssq, axis=-1, keepdims=True) * (1.0 / (ROW_TILE * LANES)) + EPS)
        cols = [cols[c] * inv * fg_ref[:, c * LANES:(c + 1) * LANES] for c in range(ROW_TILE)]
    for c in range(ROW_TILE):
        o_ref[0, :, c * LANES:(c + 1) * LANES] = cols[c]


def _combine(x, ys, dest0, dest1, rw, g2, final_g, final):
    b, s, d = x.shape
    tm = min(TM_CMB, s)
    row = lambda i, j: (i, j, 0)
    idx_spec = pl.BlockSpec((1, 1, tm), lambda i, j: (i * (s // tm) + j, 0, 0), memory_space=pltpu.SMEM)
    return pl.pallas_call(
        functools.partial(_combine_kernel, final=final),
        grid=(b, s // tm),
        in_specs=[idx_spec, idx_spec,
                  pl.BlockSpec((1, tm, d), row),
                  pl.BlockSpec(memory_space=pl.ANY),
                  pl.BlockSpec((1, tm, LANES), row),
                  pl.BlockSpec((1, 1, d), lambda i, j: (i, 0, 0)),
                  pl.BlockSpec((1, d), lambda i, j: (0, 0))],
        out_specs=pl.BlockSpec((1, tm, d), row),
        out_shape=jax.ShapeDtypeStruct((b, s, d), F32),
        scratch_shapes=[pltpu.VMEM((tm * ROW_TILE, LANES), F32), pltpu.VMEM((tm * ROW_TILE, LANES), F32),
                        pltpu.SemaphoreType.DMA((2,))],
        compiler_params=_cp(2),
        name="moe_combine_final" if final else "moe_combine",
    )(dest0.reshape(-1, 1, tm), dest1.reshape(-1, 1, tm), x, ys, rw, g2.reshape(b, 1, d),
      final_g.reshape(1, d))


def _pad_heads(w, n_heads, offsets):
    d = w.shape[0]
    w = w.reshape(d, n_heads, HEAD_DIM)
    z = jnp.zeros((d, n_heads, HEAD_DIM), w.dtype)
    off = jnp.asarray(offsets, jnp.int32).reshape(1, n_heads, 1)
    blk = jnp.where(off == 0, jnp.concatenate([w, z], axis=-1), jnp.concatenate([z, w], axis=-1))
    return blk.reshape(d, n_heads * LANES)


def _layout_w_in(w_in):
    d = w_in.shape[0]
    kvw = NSA_KV * HEAD_DIM
    sizes = (NSA_HEADS * HEAD_DIM, kvw, kvw, kvw, kvw, kvw, kvw, NSA_HEADS * 3,
             FOX_HEADS * HEAD_DIM, FOX_HEADS * HEAD_DIM, FOX_HEADS * HEAD_DIM, FOX_HEADS,
             SB_HEADS * HEAD_DIM, SB_HEADS * HEAD_DIM, SB_HEADS * HEAD_DIM)
    pts = np.cumsum(sizes)[:-1].tolist()
    (qa, kca, vca, ksa, vsa, kwa, vwa, ga, qb, kb, vb, fb, qc, kc, vc) = jnp.split(w_in, pts, axis=1)
    scale = HEAD_DIM ** -0.5
    qa_p = _pad_heads(qa * scale, NSA_HEADS, [0] * NSA_GROUP + [HEAD_DIM] * NSA_GROUP)
    qb_p = _pad_heads(qb * scale, FOX_HEADS, [0, HEAD_DIM, 0, HEAD_DIM])
    qc_p = _pad_heads(qc * scale, SB_HEADS, [0, HEAD_DIM, 0, HEAD_DIM])
    main = jnp.concatenate([qa_p, ksa, kwa, vsa, vwa, qb_p, kb, vb, qc_p, kc, vc], axis=1).astype(BF16)
    cmp_w = jnp.concatenate([kca, vca], axis=1).astype(BF16)
    zpad = lambda n: jnp.zeros((d, n), w_in.dtype)
    per_grp = NSA_GROUP * 3
    small = jnp.concatenate([ga[:, :per_grp], zpad(LANES - per_grp), ga[:, per_grp:], zpad(LANES - per_grp),
                             fb[:, 0:2], zpad(LANES - 2), fb[:, 2:4], zpad(LANES - 2)], axis=1).astype(BF16)
    return main, cmp_w, small


def _layout_cmp(w1, w2):
    hid = w1.shape[1]
    w1r = w1.reshape(2, CMP_STRIDE, HEAD_DIM, hid)
    z = jnp.zeros((CMP_STRIDE, HEAD_DIM, hid), w1.dtype)
    cols = []
    for half in range(2):
        for g in range(NSA_KV):
            parts = [w1r[half] if gg == g else z for gg in range(NSA_KV)]
            cols.append(jnp.concatenate(parts, axis=1).reshape(CMP_STRIDE * LANES, hid))
    wcat = jnp.concatenate(cols, axis=1).astype(BF16)
    zz = jnp.zeros_like(w2)
    w2bd = jnp.concatenate([jnp.concatenate([w2, zz], axis=1),
                            jnp.concatenate([zz, w2], axis=1)], axis=0).astype(BF16)
    return wcat, w2bd


def _rope_tables(pos):
    inv = jnp.exp(jnp.arange(ROPE_HALF, dtype=F32) * (-2.0 * math.log(ROPE_THETA) / ROPE_DIM))
    ang = pos.astype(F32)[:, None] * inv[None, :]
    cos, sin = jnp.cos(ang), jnp.sin(ang)
    n = pos.shape[0]
    z8 = jnp.zeros((n, ROPE_HALF), F32)
    rest1 = jnp.ones((n, HEAD_DIM - ROPE_DIM), F32)
    rest0 = jnp.zeros((n, HEAD_DIM - ROPE_DIM), F32)
    c = jnp.concatenate([cos, cos, rest1], axis=1)
    s1 = jnp.concatenate([z8, sin, rest0], axis=1)
    s2 = jnp.concatenate([-sin, z8, rest0], axis=1)
    dup = lambda a: jnp.concatenate([a, a], axis=1)
    return dup(c), dup(s1), dup(s2)


def _static_tables(s):
    n_cmp_pad = s // CMP_STRIDE
    n = np.arange(n_cmp_pad)[:, None]
    j = np.arange(LANES)[None, :]
    n_sel = s // SEL_LEN
    cover = ((n * CMP_STRIDE < j * SEL_LEN + SEL_LEN) & (n * CMP_STRIDE + CMP_LEN > j * SEL_LEN)
             & (j < n_sel)).astype(np.float32)
    nt = s // TK_FOX
    key = np.arange(nt)[:, None, None] * TK_FOX + np.arange(TK_FOX)[None, None, :]
    expand = (key // SEL_LEN == np.arange(LANES)[None, :, None]).astype(np.float32)
    return jnp.asarray(cover, BF16), jnp.asarray(expand, BF16)


def _dispatch_plan(ri, t):
    eid = ri.reshape(t, LANES)[:, :2].reshape(-1)
    n_assign = eid.shape[0]
    onehot = (eid[:, None] == jnp.arange(N_EXPERTS, dtype=jnp.int32)[None, :]).astype(jnp.int32)
    csum = jnp.cumsum(onehot, axis=0)
    counts = csum[-1]
    rank = jnp.take_along_axis(csum, eid[:, None], axis=1)[:, 0] - 1
    padded = ((counts + TM_EXP - 1) // TM_EXP) * TM_EXP
    pends = jnp.cumsum(padded)
    pstarts = pends - padded
    dest = pstarts[eid] + rank
    n_tiles = -(-(n_assign + N_EXPERTS * (TM_EXP - 1)) // TM_EXP)
    tile_start = jnp.arange(n_tiles, dtype=jnp.int32) * TM_EXP
    tile_expert = jnp.minimum(jnp.sum((pends[None, :] <= tile_start[:, None]).astype(jnp.int32), axis=1),
                              N_EXPERTS - 1).astype(jnp.int32)
    n_used = (pends[-1] // TM_EXP).astype(jnp.int32).reshape(1)
    return dest.astype(jnp.int32), n_tiles * TM_EXP, tile_expert, n_used


def kernel(x, c, norm1_g, norm2_g, ada_w, ada_b, w_in, b_forget, cmp_pos_k, cmp_w1_k, cmp_w2_k,
           cmp_pos_v, cmp_w1_v, cmp_w2_v, out_norm_g, w_out, router_group_w, router_group_b,
           router_expert_w, router_expert_b, expert_w1, expert_w3, expert_w2, final_g):
    b, s, d = x.shape
    depth = ada_w.shape[0]
    t = b * s
    mod = _modulation(c, ada_w, ada_b)
    rope_c, rope_1, rope_2 = _rope_tables(jnp.arange(s))
    n_cmp_pad = s // CMP_STRIDE
    crc, cr1, cr2 = _rope_tables(jnp.arange(n_cmp_pad) * CMP_STRIDE + (CMP_LEN - 1))
    cover, expand = _static_tables(s)

    for l in range(depth):
        sh1, sc1, g1, sh2, sc2, g2 = [mod[l][:, i * d:(i + 1) * d] for i in range(6)]
        w_main, w_cmp, w_small = _layout_w_in(w_in[l])
        u, kc, vc, small = _in_projection(x, norm1_g[l], sc1, sh1, w_main, w_cmp, w_small,
                                          rope_c, rope_1, rope_2)
        bf = b_forget[l]
        zf = jnp.zeros((LANES - 2,), F32)
        b_pairs = jnp.stack([jnp.concatenate([bf[0:2], zf]), jnp.concatenate([bf[2:4], zf])]).reshape(2, 1, LANES)
        cq, ckt = _forget_cumsum(small, b_pairs)
        wk, w2k = _layout_cmp(cmp_w1_k[l], cmp_w2_k[l])
        wv, w2v = _layout_cmp(cmp_w1_v[l], cmp_w2_v[l])
        ck, cv = _compress(kc, vc, wk, wv, cmp_w1_k[l], cmp_w1_v[l],
                           jnp.broadcast_to(cmp_pos_k[l].reshape(1, -1), (8, CMP_LEN * HEAD_DIM)),
                           jnp.broadcast_to(cmp_pos_v[l].reshape(1, -1), (8, CMP_LEN * HEAD_DIM)),
                           w2k, w2v, crc, cr1, cr2)
        gn = out_norm_g[l].reshape(1, -1)
        o_a = _nsa_attention(u, ck, cv, small, gn[:, :NSA_HEADS * HEAD_DIM], cover, expand)
        o_b = _fox_attention(u, cq, ckt, gn[:, NSA_HEADS * HEAD_DIM:(NSA_HEADS + FOX_HEADS) * HEAD_DIM])
        o_c = _sb_attention(u, gn[:, (NSA_HEADS + FOX_HEADS) * HEAD_DIM:])

        wr = jnp.concatenate([router_group_w[l], router_expert_w[l],
                              jnp.zeros((d, LANES - N_GROUPS - N_EXPERTS), F32)], axis=1)
        wr_hi = wr.astype(BF16)
        wr_lo = (wr - wr_hi.astype(F32)).astype(BF16)
        br = jnp.concatenate([router_group_b[l], router_expert_b[l],
                              jnp.zeros((LANES - N_GROUPS - N_EXPERTS,), F32)]).reshape(1, LANES)
        x, h2, rw, ri = _out_projection(o_a, o_b, o_c, x, w_out[l].astype(BF16), g1, norm2_g[l],
                                        sc2, sh2, wr_hi, wr_lo, br)

        dest, p_rows, tile_expert, n_used = _dispatch_plan(ri, t)
        xs = _dispatch(h2, dest, p_rows)
        ys = _expert_mlp(xs, tile_expert, n_used, expert_w1, expert_w3, expert_w2, l)
        x = _combine(x, ys, dest[0::2], dest[1::2], rw, g2, final_g, final=(l == depth - 1))
    return x
```

```python
import functools
import math

import numpy as np
import jax
import jax.numpy as jnp
from jax import lax
from jax.experimental import pallas as pl
from jax.experimental.pallas import tpu as pltpu

F32 = jnp.float32
BF16 = jnp.bfloat16

HEAD_DIM = 64
LANES = 128
N_HEADS = 16
NSA_HEADS = 8
NSA_KV = 2
NSA_GROUP = 4
FOX_HEADS = 4
SB_HEADS = 4
ROPE_DIM = 16
ROPE_HALF = 8
ROPE_THETA = 500000.0
CMP_LEN = 32
CMP_STRIDE = 16
CMP_HIDDEN = 128
SEL_LEN = 64
SEL_TOPN = 16
WINDOW = 512
FORCE_SCORE = 1.0e4
N_GROUPS = 4
EXPERTS_PER_GROUP = 8
N_EXPERTS = 32
EPS = 1e-6
LOG2E = math.log2(math.e)
NEG = -1e30

COL_QA = 0
COL_KS = 1024
COL_KW = 1152
ROPE_COLS = 1280
COL_VS = 1280
COL_VW = 1408
COL_FOX = 1536
COL_SB = 2560
N_MAIN = 3584
N_SMALL = 512
PROJ_CHUNK = 1280

TM_PROJ = 512
TQ_NSA = 128
TK_ATT = 256
TQ_PAIR = 256
TK_FOX = 512
ROWS = 32
TM_OUT = 512
TM_EXP = 256
TM_CMB = 512
COPY_CHUNK = 512
ROW_TILE = 8
VMEM_LIMIT = 56 * 1024 * 1024


def _cp(n_axes, vmem=VMEM_LIMIT):
    return pltpu.CompilerParams(dimension_semantics=("arbitrary",) * n_axes, vmem_limit_bytes=vmem)


def _dot(a, b):
    return jnp.dot(a, b, preferred_element_type=F32)


def _dot_nt(a, b):
    return lax.dot_general(a, b, (((1,), (1,)), ((), ())), preferred_element_type=F32)


def _split_bf16(x, parts):
    out = []
    r = x
    for _ in range(parts):
        p = r.astype(BF16)
        out.append(p)
        r = r - p.astype(F32)
    return out


def _dot_split(x, m, parts):
    acc = None
    for p in _split_bf16(x, parts):
        d = _dot(p, m)
        acc = d if acc is None else acc + d
    return acc


def _rope(x, c, s1, s2):
    return x * c + pltpu.roll(x, ROPE_HALF, 1) * s1 + pltpu.roll(x, LANES - ROPE_HALF, 1) * s2


def _softplus(z):
    return jnp.maximum(z, 0.0) + jnp.log(1.0 + jnp.exp(-jnp.abs(z)))


def _mod_kernel(c_ref, w_ref, b_ref, o_ref):
    c = c_ref[...]
    cond = c * (1.0 / (1.0 + jnp.exp(-c)))
    o_ref[0] = _dot(cond, w_ref[0]) + b_ref[0]


def _modulation(c, ada_w, ada_b):
    depth, d, n = ada_w.shape
    b = c.shape[0]
    tn = 1024
    return pl.pallas_call(
        _mod_kernel,
        grid=(depth, n // tn),
        in_specs=[pl.BlockSpec((b, d), lambda l, j: (0, 0)),
                  pl.BlockSpec((1, d, tn), lambda l, j: (l, 0, j)),
                  pl.BlockSpec((1, 1, tn), lambda l, j: (l, 0, j))],
        out_specs=pl.BlockSpec((1, b, tn), lambda l, j: (l, 0, j)),
        out_shape=jax.ShapeDtypeStruct((depth, b, n), F32),
        compiler_params=_cp(2),
        name="modulation",
    )(c, ada_w, ada_b.reshape(depth, 1, n))


def _inproj_kernel(x_ref, g_ref, sc_ref, sh_ref, w_ref, wc_ref, ws_ref, rc_ref, r1_ref, r2_ref,
                   u_ref, kc_ref, vc_ref, sm_ref):
    x = x_ref[0]
    ms = jnp.mean(x * x, axis=-1, keepdims=True)
    h = (x * lax.rsqrt(ms + EPS) * g_ref[...]) * (1.0 + sc_ref[0]) + sh_ref[0]
    hb = h.astype(BF16)
    rc, r1, r2 = rc_ref[...], r1_ref[...], r2_ref[...]
    for j in range(N_MAIN // PROJ_CHUNK + (1 if N_MAIN % PROJ_CHUNK else 0)):
        lo = j * PROJ_CHUNK
        hi = min(lo + PROJ_CHUNK, N_MAIN)
        acc = _dot(hb, w_ref[:, lo:hi])
        if lo < ROPE_COLS:
            for k in range((hi - lo) // LANES):
                blk = acc[:, k * LANES:(k + 1) * LANES]
                u_ref[0, :, lo + k * LANES:lo + (k + 1) * LANES] = _rope(blk, rc, r1, r2).astype(BF16)
        else:
            u_ref[0, :, lo:hi] = acc.astype(BF16)
    cmp_in = _dot(hb, wc_ref[...])
    kc_ref[0] = cmp_in[:, :LANES].astype(BF16)
    vc_ref[0] = cmp_in[:, LANES:].astype(BF16)
    sm_ref[0] = _dot(hb, ws_ref[...])


def _in_projection(x, g, sc, sh, w_main, w_cmp, w_small, rope_c, rope_1, rope_2):
    b, s, d = x.shape
    tm = min(TM_PROJ, s)
    row = lambda i, j: (i, j, 0)
    const2 = lambda i, j: (0, 0)
    per_b = lambda i, j: (i, 0, 0)
    seq = lambda i, j: (j, 0)
    return pl.pallas_call(
        _inproj_kernel,
        grid=(b, s // tm),
        in_specs=[pl.BlockSpec((1, tm, d), row),
                  pl.BlockSpec((1, d), const2),
                  pl.BlockSpec((1, 1, d), per_b),
                  pl.BlockSpec((1, 1, d), per_b),
                  pl.BlockSpec((d, N_MAIN), const2),
                  pl.BlockSpec((d, 2 * LANES), const2),
                  pl.BlockSpec((d, N_SMALL), const2),
                  pl.BlockSpec((tm, LANES), seq),
                  pl.BlockSpec((tm, LANES), seq),
                  pl.BlockSpec((tm, LANES), seq)],
        out_specs=[pl.BlockSpec((1, tm, N_MAIN), row),
                   pl.BlockSpec((1, tm, LANES), row),
                   pl.BlockSpec((1, tm, LANES), row),
                   pl.BlockSpec((1, tm, N_SMALL), row)],
        out_shape=[jax.ShapeDtypeStruct((b, s, N_MAIN), BF16),
                   jax.ShapeDtypeStruct((b, s, LANES), BF16),
                   jax.ShapeDtypeStruct((b, s, LANES), BF16),
                   jax.ShapeDtypeStruct((b, s, N_SMALL), F32)],
        compiler_params=_cp(2),
        name="in_projection",
    )(x, g.reshape(1, d), sc.reshape(b, 1, d), sh.reshape(b, 1, d), w_main, w_cmp, w_small,
      rope_c, rope_1, rope_2)


def _cumf_kernel(f_ref, b_ref, cq_ref, ckt_ref):
    n_chunks = f_ref.shape[1] // TK_ATT
    r = lax.broadcasted_iota(jnp.int32, (TK_ATT, TK_ATT), 0)
    c = lax.broadcasted_iota(jnp.int32, (TK_ATT, TK_ATT), 1)
    tri = jnp.where(c <= r, 1.0, 0.0).astype(BF16)
    carry = jnp.zeros((1, LANES), F32)
    for j in range(n_chunks):
        f = f_ref[0, j * TK_ATT:(j + 1) * TK_ATT, :] + b_ref[0]
        ls = -_softplus(-f)
        acc = None
        for p in _split_bf16(ls, 3):
            dd = _dot(tri, p)
            acc = dd if acc is None else acc + dd
        cs = acc + carry
        cs2 = cs * LOG2E
        cq_ref[0, j * TK_ATT:(j + 1) * TK_ATT, :] = cs2
        ckt_ref[0, 0, j] = cs2.T[:8, :]
        carry = cs[TK_ATT - 1:TK_ATT, :]


def _forget_cumsum(small, b_pairs):
    b, s, _ = small.shape
    return pl.pallas_call(
        _cumf_kernel,
        grid=(b, 2),
        in_specs=[pl.BlockSpec((1, s, LANES), lambda i, p: (i, 0, 2 + p)),
                  pl.BlockSpec((1, 1, LANES), lambda i, p: (p, 0, 0))],
        out_specs=[pl.BlockSpec((1, s, LANES), lambda i, p: (i, 0, p)),
                   pl.BlockSpec((1, 1, s // TK_ATT, 8, TK_ATT), lambda i, p: (i, p, 0, 0, 0))],
        out_shape=[jax.ShapeDtypeStruct((b, s, 2 * LANES), F32),
                   jax.ShapeDtypeStruct((b, 2, s // TK_ATT, 8, TK_ATT), F32)],
        compiler_params=_cp(2),
        name="forget_cumsum",
    )(small, b_pairs)


def _compress_kernel(ks_ref, vs_ref, wk_ref, wv_ref, w1k_ref, w1v_ref, pek_ref, pev_ref,
                     w2k_ref, w2v_ref, rc_ref, r1_ref, r2_ref, ck_ref, cv_ref):
    def one(seg_ref, w_ref, w1_ref, pe_ref, w2_ref):
        p = _dot(seg_ref[0], w_ref[...])
        half = 2 * CMP_HIDDEN
        bias = _dot(pe_ref[...].astype(BF16), w1_ref[...].astype(BF16))[0:1, :]
        bias2 = jnp.concatenate([bias, bias], axis=1)
        n = p.shape[0]
        hid = p[:, :half] + pltpu.roll(p[:, half:], n - 1, 0) + bias2
        act = hid * (1.0 / (1.0 + jnp.exp(-hid)))
        return _dot(act.astype(BF16), w2_ref[...])

    ck = one(ks_ref, wk_ref, w1k_ref, pek_ref, w2k_ref)
    ck_ref[0] = _rope(ck, rc_ref[...], r1_ref[...], r2_ref[...]).astype(BF16)
    cv_ref[0] = one(vs_ref, wv_ref, w1v_ref, pev_ref, w2v_ref).astype(BF16)


def _compress(kc, vc, wk, wv, w1k, w1v, pek, pev, w2k, w2v, rc, r1, r2):
    b, s, _ = kc.shape
    n = s // CMP_STRIDE
    width = CMP_STRIDE * LANES
    kseg = kc.reshape(b, n, width)
    vseg = vc.reshape(b, n, width)
    seg = pl.BlockSpec((1, n, width), lambda i: (i, 0, 0))
    full = lambda a: pl.BlockSpec(a.shape, lambda i: (0,) * a.ndim)
    return pl.pallas_call(
        _compress_kernel,
        grid=(b,),
        in_specs=[seg, seg, full(wk), full(wv), full(w1k), full(w1v), full(pek), full(pev),
                  full(w2k), full(w2v), full(rc), full(r1), full(r2)],
        out_specs=[pl.BlockSpec((1, n, LANES), lambda i: (i, 0, 0)),
                   pl.BlockSpec((1, n, LANES), lambda i: (i, 0, 0))],
        out_shape=[jax.ShapeDtypeStruct((b, n, LANES), BF16),
                   jax.ShapeDtypeStruct((b, n, LANES), BF16)],
        compiler_params=_cp(1),
        name="nsa_compress",
    )(kseg, vseg, wk, wv, w1k, w1v, pek, pev, w2k, w2v, rc, r1, r2)


def _nsa_kernel(q_ref, ck_ref, cv_ref, ks_ref, vs_ref, kw_ref, vw_ref, gate_ref, gn_ref,
                cover_ref, expand_ref, o_ref, m_ref, l_ref, acc_ref):
    tq = q_ref.shape[1]
    g = pl.program_id(1)
    t0 = pl.program_id(2) * tq
    q = q_ref[0]
    qall = jnp.concatenate([q[:, h * LANES:(h + 1) * LANES] for h in range(NSA_GROUP)], axis=0)
    tpos = t0 + lax.broadcasted_iota(jnp.int32, (tq, 1), 0)
    lane = lax.broadcasted_iota(jnp.int32, (tq, LANES), 1)

    s = _dot_nt(qall, ck_ref[0])
    cmp_end = (CMP_STRIDE * lax.broadcasted_iota(jnp.int32, (ROWS, LANES), 1) + (CMP_LEN - 1)
               - lax.broadcasted_iota(jnp.int32, (ROWS, LANES), 0))
    n_chunks = tq // ROWS
    p_rows, p_sum = [], [None] * n_chunks
    for h in range(NSA_GROUP):
        for c in range(n_chunks):
            r0 = h * tq + c * ROWS
            valid = cmp_end <= t0 + c * ROWS
            sm = jnp.where(valid, s[r0:r0 + ROWS, :], NEG)
            e = jnp.where(valid, jnp.exp2(sm - jnp.max(sm, axis=-1, keepdims=True)), 0.0)
            den = jnp.sum(e, axis=-1, keepdims=True)
            p = e * jnp.where(den > 0.0, 1.0 / den, 0.0)
            p_rows.append(p.astype(BF16))
            p_sum[c] = p if p_sum[c] is None else p_sum[c] + p
    o_cmp = _dot(jnp.concatenate(p_rows, axis=0), cv_ref[0]).reshape(NSA_GROUP, tq, LANES)

    n_sel = expand_ref.shape[0] * (TK_FOX // SEL_LEN)
    imp_t = _dot_split(jnp.concatenate(p_sum, axis=0), cover_ref[...], 3).T[:n_sel, :]
    blk = lax.broadcasted_iota(jnp.int32, (n_sel, tq), 0)
    tcol = t0 + lax.broadcasted_iota(jnp.int32, (n_sel, tq), 1)
    cur = jnp.right_shift(tcol, int(math.log2(SEL_LEN)))
    forced = (blk == 0) | (blk == cur) | (blk == cur - 1)
    score = jnp.where(forced, FORCE_SCORE, jnp.where(blk * SEL_LEN <= tcol, imp_t, -1.0))
    cnt = jnp.zeros((n_sel, tq), F32)
    for j in range(n_sel):
        sj = score[j:j + 1, :]
        beats = (sj > score) | ((sj == score) & (blk > j))
        cnt = cnt + jnp.where(beats, 1.0, 0.0)
    sel_t = jnp.where(cnt < float(min(SEL_TOPN, n_sel)), 1.0, 0.0)
    sel = jnp.concatenate([sel_t, jnp.zeros((LANES - n_sel, tq), F32)], axis=0).T.astype(BF16)

    def biased(bias):
        def adjust(h, t_off, cols):
            return [cols[kk] + bias[t_off:t_off + ROWS, kk * LANES:(kk + 1) * LANES]
                    for kk in range(len(cols))]
        return adjust

    kcol_s = lax.broadcasted_iota(jnp.int32, (tq, TK_FOX), 1)

    def sel_tile(kt, carry):
        k0 = pl.multiple_of(kt * TK_FOX, TK_FOX)
        hit = _dot(sel, expand_ref[kt])
        ok = (hit > 0.5) & ((k0 + kcol_s) <= tpos)
        s_t = _dot_nt(qall, ks_ref[0, pl.ds(k0, TK_FOX), :])
        _softmax_tile(s_t, vs_ref[0, pl.ds(k0, TK_FOX), :], m_ref, l_ref, acc_ref, NSA_GROUP, tq,
                      biased(jnp.where(ok, 0.0, NEG)))
        return carry

    _softmax_init(m_ref, l_ref, acc_ref)
    lax.fori_loop(0, lax.div(t0 + tq - 1, TK_FOX) + 1, sel_tile, 0)
    o_sel = _softmax_result(l_ref, acc_ref).reshape(NSA_GROUP, tq, LANES)

    span = WINDOW + tq
    w0 = pl.multiple_of(jnp.maximum(t0 - WINDOW, 0), LANES)
    kp = w0 + lax.broadcasted_iota(jnp.int32, (tq, span), 1)
    bias_w = jnp.where((kp <= tpos) & (kp > tpos - WINDOW), 0.0, NEG)
    s_w = _dot_nt(qall, kw_ref[0, pl.ds(w0, span), :])
    nkw = span // LANES
    p_rows, inv_rows = [], []
    for h in range(NSA_GROUP):
        for c in range(tq // ROWS):
            r0 = h * tq + c * ROWS
            cols = [s_w[r0:r0 + ROWS, kk * LANES:(kk + 1) * LANES]
                    + bias_w[c * ROWS:(c + 1) * ROWS, kk * LANES:(kk + 1) * LANES] for kk in range(nkw)]
            mx = cols[0]
            for kk in range(1, nkw):
                mx = jnp.maximum(mx, cols[kk])
            mx = jnp.max(mx, axis=-1, keepdims=True)
            pks = [jnp.exp2(cols[kk] - mx) for kk in range(nkw)]
            psum = pks[0]
            for kk in range(1, nkw):
                psum = psum + pks[kk]
            inv_rows.append(jnp.broadcast_to(1.0 / jnp.sum(psum, axis=-1, keepdims=True), (ROWS, LANES)))
            p_rows.append(jnp.concatenate([pk.astype(BF16) for pk in pks], axis=1))
    o_win = _dot(jnp.concatenate(p_rows, axis=0), vw_ref[0, pl.ds(w0, span), :])
    o_win = (o_win * jnp.concatenate(inv_rows, axis=0)).reshape(NSA_GROUP, tq, LANES)

    gt = gate_ref[0]
    gt = 1.0 / (1.0 + jnp.exp(-gt))
    mine = (lane >= g * HEAD_DIM) & (lane < (g + 1) * HEAD_DIM)
    outs = []
    for h in range(NSA_GROUP):
        o = (gt[:, 3 * h:3 * h + 1] * o_cmp[h] + gt[:, 3 * h + 1:3 * h + 2] * o_sel[h]
             + gt[:, 3 * h + 2:3 * h + 3] * o_win[h])
        o = jnp.where(mine, o, 0.0)
        ms = jnp.sum(o * o, axis=-1, keepdims=True) * (1.0 / HEAD_DIM)
        o = o * lax.rsqrt(ms + EPS)
        outs.append(o + pltpu.roll(o, HEAD_DIM, 1))
    left = lane < HEAD_DIM
    o_ref[0, :, :LANES] = (jnp.where(left, outs[0], outs[1]) * gn_ref[:, :LANES]).astype(o_ref.dtype)
    o_ref[0, :, LANES:] = (jnp.where(left, outs[2], outs[3]) * gn_ref[:, LANES:]).astype(o_ref.dtype)


def _nsa_attention(u, ck, cv, small, gn, cover, expand):
    b, s, _ = u.shape
    tq = min(TQ_NSA, s)
    assert s >= WINDOW + tq and WINDOW % tq == 0
    n_cmp = ck.shape[1]
    blk = LANES
    kv = lambda col: pl.BlockSpec((1, s, LANES), lambda i, g, j, col=col: (i, 0, col // blk))
    return pl.pallas_call(
        _nsa_kernel,
        grid=(b, NSA_KV, s // tq),
        in_specs=[pl.BlockSpec((1, tq, NSA_GROUP * LANES), lambda i, g, j: (i, j, g)),
                  pl.BlockSpec((1, n_cmp, LANES), lambda i, g, j: (i, 0, 0)),
                  pl.BlockSpec((1, n_cmp, LANES), lambda i, g, j: (i, 0, 0)),
                  kv(COL_KS), kv(COL_VS), kv(COL_KW), kv(COL_VW),
                  pl.BlockSpec((1, tq, LANES), lambda i, g, j: (i, j, g)),
                  pl.BlockSpec((1, 2 * LANES), lambda i, g, j: (0, g)),
                  pl.BlockSpec(cover.shape, lambda i, g, j: (0, 0)),
                  pl.BlockSpec(expand.shape, lambda i, g, j: (0, 0, 0))],
        out_specs=pl.BlockSpec((1, tq, 2 * LANES), lambda i, g, j: (i, j, g)),
        out_shape=jax.ShapeDtypeStruct((b, s, NSA_HEADS * HEAD_DIM), BF16),
        scratch_shapes=_softmax_scratch(NSA_GROUP * tq),
        compiler_params=_cp(3),
        name="nsa_attention",
    )(u, ck, cv, u, u, u, u, small, gn, cover, expand)


def _pair_finish(acc, gn_ref, o_ref, tq):
    lane = lax.broadcasted_iota(jnp.int32, (tq, LANES), 1)
    left = lane < HEAD_DIM
    o = jnp.where(left, acc[0], acc[1])
    o2 = o * o
    ms_l = jnp.sum(jnp.where(left, o2, 0.0), axis=-1, keepdims=True) * (1.0 / HEAD_DIM)
    ms_r = jnp.sum(jnp.where(left, 0.0, o2), axis=-1, keepdims=True) * (1.0 / HEAD_DIM)
    inv = jnp.where(left, lax.rsqrt(ms_l + EPS), lax.rsqrt(ms_r + EPS))
    o_ref[0] = (o * inv * gn_ref[...]).astype(o_ref.dtype)


def _softmax_tile(s, v, m_ref, l_ref, acc_ref, heads, tq, adjust):
    nk = s.shape[1] // LANES
    p_rows = []
    for h in range(heads):
        for c in range(tq // ROWS):
            t_off = c * ROWS
            r0 = h * tq + t_off
            cols = [s[r0:r0 + ROWS, k * LANES:(k + 1) * LANES] for k in range(nk)]
            cols = adjust(h, t_off, cols)
            mx = cols[0]
            for k in range(1, nk):
                mx = jnp.maximum(mx, cols[k])
            m_old = m_ref[r0:r0 + ROWS, :]
            m_new = jnp.maximum(m_old, jnp.max(mx, axis=-1, keepdims=True))
            alpha = jnp.exp2(m_old - m_new)
            pks = [jnp.exp2(cols[k] - m_new) for k in range(nk)]
            psum = pks[0]
            for k in range(1, nk):
                psum = psum + pks[k]
            p_rows.append(jnp.concatenate([pk.astype(BF16) for pk in pks], axis=1))
            l_ref[r0:r0 + ROWS, :] = alpha * l_ref[r0:r0 + ROWS, :] + psum
            acc_ref[r0:r0 + ROWS, :] = alpha * acc_ref[r0:r0 + ROWS, :]
            m_ref[r0:r0 + ROWS, :] = m_new
    acc_ref[...] += _dot(jnp.concatenate(p_rows, axis=0), v)


def _tile_loop(score_fn, process_fn, lo, hi):
    def body(kt, carry):
        process_fn(kt, score_fn(kt))
        return carry

    lax.fori_loop(lo, hi - 1, body, 0)
    return score_fn(hi - 1)


def _softmax_init(m_ref, l_ref, acc_ref):
    m_ref[...] = jnp.full(m_ref.shape, NEG, F32)
    l_ref[...] = jnp.zeros(l_ref.shape, F32)
    acc_ref[...] = jnp.zeros(acc_ref.shape, F32)


def _softmax_result(l_ref, acc_ref):
    return acc_ref[...] / jnp.sum(l_ref[...], axis=-1, keepdims=True)


def _softmax_scratch(rows):
    return [pltpu.VMEM((rows, LANES), F32), pltpu.VMEM((rows, LANES), F32), pltpu.VMEM((rows, LANES), F32)]


def _fox_kernel(q_ref, k_ref, v_ref, cq_ref, ckt_ref, gn_ref, o_ref,
                m_ref, l_ref, acc_ref, cqr_ref):
    tq = q_ref.shape[1]
    tk = TK_FOX
    t0 = pl.program_id(2) * tq
    q = q_ref[0]
    qall = jnp.concatenate([q[:, :LANES], q[:, LANES:]], axis=0)
    cq = cq_ref[0]
    cqr_ref[0:tq, :] = jnp.broadcast_to(cq[:, 0:1], (tq, LANES))
    cqr_ref[tq:2 * tq, :] = jnp.broadcast_to(cq[:, 1:2], (tq, LANES))
    _softmax_init(m_ref, l_ref, acc_ref)
    n_tiles = lax.div(t0 + tq - 1, tk) + 1
    diag = (lax.broadcasted_iota(jnp.int32, (ROWS, LANES), 1)
            - lax.broadcasted_iota(jnp.int32, (ROWS, LANES), 0))

    def scores(kt):
        return _dot_nt(qall, k_ref[0, pl.ds(pl.multiple_of(kt * tk, tk), tk), :])

    def tile(kt, s, masked):
        k0 = pl.multiple_of(kt * tk, tk)
        cks = [ckt_ref[0, 0, kt * (tk // TK_ATT) + j] for j in range(tk // TK_ATT)]

        def adjust(h, t_off, cols):
            out = []
            cqr = cqr_ref[h * tq + t_off:h * tq + t_off + ROWS, :]
            for kk in range(tk // LANES):
                lo = (kk * LANES) % TK_ATT
                ck = cks[(kk * LANES) // TK_ATT][h:h + 1, lo:lo + LANES]
                val = (cols[kk] - ck) + cqr
                if masked:
                    val = jnp.where(diag <= (t0 + t_off) - (k0 + kk * LANES), val, NEG)
                out.append(val)
            return out

        _softmax_tile(s, v_ref[0, pl.ds(k0, tk), :], m_ref, l_ref, acc_ref, 2, tq, adjust)

    s_last = _tile_loop(scores, lambda kt, s: tile(kt, s, False), 0, n_tiles)
    tile(n_tiles - 1, s_last, True)
    acc = _softmax_result(l_ref, acc_ref).reshape(2, tq, LANES)
    _pair_finish(acc, gn_ref, o_ref, tq)


def _fox_attention(u, cq, ckt, gn):
    b, s, _ = u.shape
    tq = min(TQ_PAIR, s)
    qb = COL_FOX // (2 * LANES)
    kb = (COL_FOX + FOX_HEADS * LANES) // LANES
    vb = kb + 2
    return pl.pallas_call(
        _fox_kernel,
        grid=(b, 2, s // tq),
        in_specs=[pl.BlockSpec((1, tq, 2 * LANES), lambda i, p, j: (i, j, qb + p)),
                  pl.BlockSpec((1, s, LANES), lambda i, p, j: (i, 0, kb + p)),
                  pl.BlockSpec((1, s, LANES), lambda i, p, j: (i, 0, vb + p)),
                  pl.BlockSpec((1, tq, LANES), lambda i, p, j: (i, j, p)),
                  pl.BlockSpec((1, 1, s // TK_ATT, 8, TK_ATT), lambda i, p, j: (i, p, 0, 0, 0)),
                  pl.BlockSpec((1, LANES), lambda i, p, j: (0, p))],
        out_specs=pl.BlockSpec((1, tq, LANES), lambda i, p, j: (i, j, p)),
        out_shape=jax.ShapeDtypeStruct((b, s, FOX_HEADS * HEAD_DIM), BF16),
        scratch_shapes=_softmax_scratch(2 * tq) + [pltpu.VMEM((2 * tq, LANES), F32)],
        compiler_params=_cp(3),
        name="fox_attention",
    )(u, u, u, cq, ckt, gn)


def _sb_kernel(q_ref, k_ref, v_ref, gn_ref, o_ref, rest_ref, acc_ref):
    tq = q_ref.shape[1]
    tk = TK_ATT
    nk = tk // LANES
    t0 = pl.program_id(2) * tq
    r = lax.broadcasted_iota(jnp.int32, (tk, tk), 0)
    c = lax.broadcasted_iota(jnp.int32, (tk, tk), 1)
    upper = jnp.where(r >= c, 1.0, 0.0).astype(BF16)
    n_tiles = lax.div(t0 + tq - 1, tk) + 1
    diag = (lax.broadcasted_iota(jnp.int32, (ROWS, LANES), 1)
            - lax.broadcasted_iota(jnp.int32, (ROWS, LANES), 0))
    rest_ref[...] = jnp.zeros(rest_ref.shape, F32)
    acc_ref[...] = jnp.zeros(acc_ref.shape, F32)
    hi_mask = jnp.uint32(0xFFFF0000)
    q = q_ref[0]
    qall = jnp.concatenate([q[:, :LANES], q[:, LANES:]], axis=0)
    chunks = [(h * tq + cc * ROWS, cc * ROWS) for h in range(2) for cc in range(tq // ROWS)]

    def tile(kt, masked):
        k0 = pl.multiple_of(kt * tk, tk)
        k = k_ref[0, pl.ds(k0, tk), :]
        v = v_ref[0, pl.ds(k0, tk), :]

        def strictly_before(t_off, kk):
            return diag < (t0 + t_off) - (k0 + kk * LANES)

        z = _dot_nt(qall, k)
        his, los = [], []
        for r0, t_off in chunks:
            hi_c, lo_c = [], []
            for kk in range(nk):
                zc = z[r0:r0 + ROWS, kk * LANES:(kk + 1) * LANES]
                l = -(jnp.maximum(zc, 0.0) + jnp.log2(1.0 + jnp.exp2(-jnp.abs(zc))))
                if masked:
                    l = jnp.where(strictly_before(t_off, kk), l, 0.0)
                hi = pltpu.bitcast(pltpu.bitcast(l, jnp.uint32) & hi_mask, F32)
                hi_c.append(hi.astype(BF16))
                lo_c.append((l - hi).astype(BF16))
            his.append(jnp.concatenate(hi_c, axis=1))
            los.append(jnp.concatenate(lo_c, axis=1))
        cum = _dot(jnp.concatenate(his, axis=0), upper) + _dot(jnp.concatenate(los, axis=0), upper)

        a_rows = []
        for r0, t_off in chunks:
            rest = rest_ref[r0:r0 + ROWS, :]
            a_c = []
            for kk in range(nk):
                sl = (slice(r0, r0 + ROWS), slice(kk * LANES, (kk + 1) * LANES))
                a = jnp.exp2(z[sl] + cum[sl] + rest)
                if masked:
                    a = jnp.where(strictly_before(t_off, kk), a, 0.0)
                a_c.append(a.astype(BF16))
            a_rows.append(jnp.concatenate(a_c, axis=1))
            rest_ref[r0:r0 + ROWS, :] = rest + jnp.broadcast_to(cum[r0:r0 + ROWS, 0:1], (ROWS, LANES))
        acc_ref[...] += _dot(jnp.concatenate(a_rows, axis=0), v)

    tile(n_tiles - 1, True)

    def full_tile(i, carry):
        tile(n_tiles - 2 - i, False)
        return carry

    lax.fori_loop(0, n_tiles - 1, full_tile, 0)
    _pair_finish(acc_ref[...].reshape(2, tq, LANES), gn_ref, o_ref, tq)


def _sb_attention(u, gn):
    b, s, _ = u.shape
    tq = min(TQ_PAIR, s)
    qb = COL_SB // (2 * LANES)
    kb = (COL_SB + SB_HEADS * LANES) // LANES
    vb = kb + 2
    return pl.pallas_call(
        _sb_kernel,
        grid=(b, 2, s // tq),
        in_specs=[pl.BlockSpec((1, tq, 2 * LANES), lambda i, p, j: (i, j, qb + p)),
                  pl.BlockSpec((1, s, LANES), lambda i, p, j: (i, 0, kb + p)),
                  pl.BlockSpec((1, s, LANES), lambda i, p, j: (i, 0, vb + p)),
                  pl.BlockSpec((1, LANES), lambda i, p, j: (0, p))],
        out_specs=pl.BlockSpec((1, tq, LANES), lambda i, p, j: (i, j, p)),
        out_shape=jax.ShapeDtypeStruct((b, s, SB_HEADS * HEAD_DIM), BF16),
        scratch_shapes=[pltpu.VMEM((2 * tq, LANES), F32), pltpu.VMEM((2 * tq, LANES), F32)],
        compiler_params=_cp(3),
        name="sb_attention",
    )(u, u, u, gn)


def _outproj_kernel(oa_ref, ob_ref, oc_ref, x_ref, w_ref, g1_ref, n2_ref, sc_ref, sh_ref,
                    wrh_ref, wrl_ref, br_ref, xo_ref, h_ref, rw_ref, ri_ref):
    na = oa_ref.shape[2]
    nb = ob_ref.shape[2]
    y = _dot(oa_ref[0], w_ref[0:na, :])
    y = y + _dot(ob_ref[0], w_ref[na:na + nb, :])
    y = y + _dot(oc_ref[0], w_ref[na + nb:, :])
    x = x_ref[0] + g1_ref[0] * y
    xo_ref[0] = x
    ms = jnp.mean(x * x, axis=-1, keepdims=True)
    h = (x * lax.rsqrt(ms + EPS) * n2_ref[...]) * (1.0 + sc_ref[0]) + sh_ref[0]
    hb = h.astype(BF16)
    _store_row_tiles(h_ref, h)
    hl = (h - hb.astype(F32)).astype(BF16)
    logit = _dot(hb, wrh_ref[...]) + _dot(hl, wrh_ref[...]) + _dot(hb, wrl_ref[...]) + br_ref[...]

    tm = logit.shape[0]
    lane = lax.broadcasted_iota(jnp.int32, (tm, LANES), 1).astype(F32)
    big = float(LANES)
    is_g = lane < N_GROUPS
    lg = jnp.where(is_g, logit, NEG)
    mg = jnp.max(lg, axis=-1, keepdims=True)
    zg = jnp.sum(jnp.where(is_g, jnp.exp(lg - mg), 0.0), axis=-1, keepdims=True)
    pg = 1.0 / zg
    gi = jnp.min(jnp.where(is_g & (lg == mg), lane, big), axis=-1, keepdims=True)
    e_lane = lane - N_GROUPS
    in_grp = (e_lane >= gi * EXPERTS_PER_GROUP) & (e_lane < (gi + 1) * EXPERTS_PER_GROUP)
    le = jnp.where(in_grp, logit, NEG)
    m1 = jnp.max(le, axis=-1, keepdims=True)
    i1 = jnp.min(jnp.where(in_grp & (le == m1), lane, big), axis=-1, keepdims=True)
    rest = in_grp & (lane != i1)
    le2 = jnp.where(rest, logit, NEG)
    m2 = jnp.max(le2, axis=-1, keepdims=True)
    i2 = jnp.min(jnp.where(rest & (le2 == m2), lane, big), axis=-1, keepdims=True)
    ze = jnp.sum(jnp.where(in_grp, jnp.exp(le - m1), 0.0), axis=-1, keepdims=True)
    p1 = 1.0 / ze
    p2 = jnp.exp(m2 - m1) / ze
    den = p1 + p2
    w1 = pg * (p1 / den)
    w2 = pg * (p2 / den)
    rw_ref[0] = jnp.where(lane == 0.0, w1, jnp.where(lane == 1.0, w2, 0.0))
    ri_ref[0] = jnp.where(lane == 0.0, i1 - N_GROUPS, jnp.where(lane == 1.0, i2 - N_GROUPS, 0.0)).astype(jnp.int32)


def _out_projection(oa, ob, oc, x, w_out, g1, n2, sc, sh, wr_hi, wr_lo, br):
    b, s, d = x.shape
    tm = min(TM_OUT, s)
    row = lambda i, j: (i, j, 0)
    const2 = lambda i, j: (0, 0)
    per_b = lambda i, j: (i, 0, 0)
    return pl.pallas_call(
        _outproj_kernel,
        grid=(b, s // tm),
        in_specs=[pl.BlockSpec((1, tm, oa.shape[2]), row),
                  pl.BlockSpec((1, tm, ob.shape[2]), row),
                  pl.BlockSpec((1, tm, oc.shape[2]), row),
                  pl.BlockSpec((1, tm, d), row),
                  pl.BlockSpec(w_out.shape, const2),
                  pl.BlockSpec((1, 1, d), per_b),
                  pl.BlockSpec((1, d), const2),
                  pl.BlockSpec((1, 1, d), per_b),
                  pl.BlockSpec((1, 1, d), per_b),
                  pl.BlockSpec((d, LANES), const2),
                  pl.BlockSpec((d, LANES), const2),
                  pl.BlockSpec((1, LANES), const2)],
        out_specs=[pl.BlockSpec((1, tm, d), row),
                   pl.BlockSpec((tm * ROW_TILE, LANES), lambda i, j: (i * (s // tm) + j, 0)),
                   pl.BlockSpec((1, tm, LANES), row),
                   pl.BlockSpec((1, tm, LANES), row)],
        out_shape=[jax.ShapeDtypeStruct((b, s, d), F32),
                   jax.ShapeDtypeStruct((b * s * ROW_TILE, LANES), F32),
                   jax.ShapeDtypeStruct((b, s, LANES), F32),
                   jax.ShapeDtypeStruct((b, s, LANES), jnp.int32)],
        compiler_params=_cp(2),
        name="out_projection",
    )(oa, ob, oc, x, w_out, g1.reshape(b, 1, d), n2.reshape(1, d), sc.reshape(b, 1, d),
      sh.reshape(b, 1, d), wr_hi, wr_lo, br)


def _store_row_tiles(ref, val):
    tm = val.shape[0]
    for c in range(ROW_TILE):
        ref[pl.ds(c, tm, stride=ROW_TILE), :] = val[:, c * LANES:(c + 1) * LANES]


def _load_row_tiles(ref, tm):
    return [ref[pl.ds(c, tm, stride=ROW_TILE), :] for c in range(ROW_TILE)]


def _tile_rows(ref, n):
    return ref.at[pl.ds(pl.multiple_of(n * ROW_TILE, ROW_TILE), ROW_TILE), :]


def _dispatch_kernel(idx_ref, src_ref, init_hbm, dst_hbm, sem):
    del init_hbm

    def issue(j, carry):
        tok = lax.shift_right_logical(j, 1)
        pltpu.make_async_copy(_tile_rows(src_ref, tok), _tile_rows(dst_hbm, idx_ref[0, 0, j]), sem).start()
        return carry

    lax.fori_loop(0, COPY_CHUNK, issue, 0, unroll=8)
    for _ in range(2):
        pltpu.make_async_copy(src_ref, dst_hbm.at[pl.ds(0, src_ref.shape[0]), :], sem).wait()


def _dispatch(src, idx, n_dst):
    n = idx.shape[0]
    init = jnp.zeros((n_dst * ROW_TILE, LANES), src.dtype)
    return pl.pallas_call(
        _dispatch_kernel,
        grid=(n // COPY_CHUNK,),
        in_specs=[pl.BlockSpec((1, 1, COPY_CHUNK), lambda i: (i, 0, 0), memory_space=pltpu.SMEM),
                  pl.BlockSpec((COPY_CHUNK // 2 * ROW_TILE, LANES), lambda i: (i, 0)),
                  pl.BlockSpec(memory_space=pl.ANY)],
        out_specs=pl.BlockSpec(memory_space=pl.ANY),
        out_shape=jax.ShapeDtypeStruct((n_dst * ROW_TILE, LANES), src.dtype),
        scratch_shapes=[pltpu.SemaphoreType.DMA],
        input_output_aliases={2: 0},
        compiler_params=_cp(1),
        name="moe_dispatch",
    )(idx.reshape(n // COPY_CHUNK, 1, COPY_CHUNK), src, init)


def _expert_kernel(te_ref, nu_ref, x_ref, w1_ref, w3_ref, w2_ref, y_ref, w1b_ref, w3b_ref, w2b_ref):
    i = pl.program_id(0)

    @pl.when((i == 0) | (te_ref[i] != te_ref[jnp.maximum(i - 1, 0)]))
    def _():
        w1b_ref[...] = w1_ref[0].astype(BF16)
        w3b_ref[...] = w3_ref[0].astype(BF16)
        w2b_ref[...] = w2_ref[0].astype(BF16)

    @pl.when(i < nu_ref[0])
    def _():
        x = jnp.concatenate(_load_row_tiles(x_ref, TM_EXP), axis=1).astype(BF16)
        a = _dot(x, w1b_ref[...])
        g = _dot(x, w3b_ref[...])
        act = (a * (1.0 / (1.0 + jnp.exp(-a)))) * g
        _store_row_tiles(y_ref, _dot(act.astype(BF16), w2b_ref[...]))

    @pl.when(i >= nu_ref[0])
    def _():
        y_ref[...] = jnp.zeros_like(y_ref)


def _expert_mlp(xs, tile_expert, n_used, w1, w3, w2, layer):
    d, de = w1.shape[2], w1.shape[3]
    n_tiles = xs.shape[0] // (TM_EXP * ROW_TILE)
    grid_spec = pltpu.PrefetchScalarGridSpec(
        num_scalar_prefetch=2,
        grid=(n_tiles,),
        in_specs=[pl.BlockSpec((TM_EXP * ROW_TILE, LANES), lambda i, te, nu: (i, 0)),
                  pl.BlockSpec((None, 1, d, de), lambda i, te, nu: (layer, te[i], 0, 0)),
                  pl.BlockSpec((None, 1, d, de), lambda i, te, nu: (layer, te[i], 0, 0)),
                  pl.BlockSpec((None, 1, de, d), lambda i, te, nu: (layer, te[i], 0, 0))],
        out_specs=pl.BlockSpec((TM_EXP * ROW_TILE, LANES), lambda i, te, nu: (i, 0)),
        scratch_shapes=[pltpu.VMEM((d, de), BF16), pltpu.VMEM((d, de), BF16), pltpu.VMEM((de, d), BF16)],
    )
    return pl.pallas_call(
        _expert_kernel,
        grid_spec=grid_spec,
        out_shape=jax.ShapeDtypeStruct(xs.shape, F32),
        compiler_params=_cp(1),
        name="expert_mlp",
    )(tile_expert, n_used, xs, w1, w3, w2)


def _combine_kernel(d0_ref, d1_ref, x_ref, ys_hbm, rw_ref, g2_ref, fg_ref, o_ref, y0_ref, y1_ref, sem,
                    *, final):
    tm = x_ref.shape[1]

    def issue(r, carry):
        pltpu.make_async_copy(_tile_rows(ys_hbm, d0_ref[0, 0, r]), _tile_rows(y0_ref, r), sem.at[0]).start()
        pltpu.make_async_copy(_tile_rows(ys_hbm, d1_ref[0, 0, r]), _tile_rows(y1_ref, r), sem.at[1]).start()
        return carry

    lax.fori_loop(0, tm, issue, 0, unroll=8)
    pltpu.make_async_copy(ys_hbm.at[pl.ds(0, tm * ROW_TILE), :], y0_ref, sem.at[0]).wait()
    pltpu.make_async_copy(ys_hbm.at[pl.ds(0, tm * ROW_TILE), :], y1_ref, sem.at[1]).wait()

    rw = rw_ref[0]
    w0 = jnp.broadcast_to(rw[:, 0:1], (tm, LANES))
    w1 = jnp.broadcast_to(rw[:, 1:2], (tm, LANES))
    y0 = _load_row_tiles(y0_ref, tm)
    y1 = _load_row_tiles(y1_ref, tm)
    cols = []
    for c in range(ROW_TILE):
        sl = slice(c * LANES, (c + 1) * LANES)
        cols.append(x_ref[0, :, sl] + g2_ref[0, :, sl] * (y0[c] * w0 + y1[c] * w1))
    if final:
        ssq = cols[0] * cols[0]
        for c in range(1, ROW_TILE):
            ssq = ssq + cols[c] * cols[c]
        inv = lax.rsqrt(jnp.sum(ssq, axis=-1, keepdims=True) * (1.0 / (ROW_TILE * LANES)) + EPS)
        cols = [cols[c] * inv * fg_ref[:, c * LANES:(c + 1) * LANES] for c in range(ROW_TILE)]
    for c in range(ROW_TILE):
        o_ref[0, :, c * LANES:(c + 1) * LANES] = cols[c]


def _combine(x, ys, dest0, dest1, rw, g2, final_g, final):
    b, s, d = x.shape
    tm = min(TM_CMB, s)
    row = lambda i, j: (i, j, 0)
    idx_spec = pl.BlockSpec((1, 1, tm), lambda i, j: (i * (s // tm) + j, 0, 0), memory_space=pltpu.SMEM)
    return pl.pallas_call(
        functools.partial(_combine_kernel, final=final),
        grid=(b, s // tm),
        in_specs=[idx_spec, idx_spec,
                  pl.BlockSpec((1, tm, d), row),
                  pl.BlockSpec(memory_space=pl.ANY),
                  pl.BlockSpec((1, tm, LANES), row),
                  pl.BlockSpec((1, 1, d), lambda i, j: (i, 0, 0)),
                  pl.BlockSpec((1, d), lambda i, j: (0, 0))],
        out_specs=pl.BlockSpec((1, tm, d), row),
        out_shape=jax.ShapeDtypeStruct((b, s, d), F32),
        scratch_shapes=[pltpu.VMEM((tm * ROW_TILE, LANES), F32), pltpu.VMEM((tm * ROW_TILE, LANES), F32),
                        pltpu.SemaphoreType.DMA((2,))],
        compiler_params=_cp(2),
        name="moe_combine_final" if final else "moe_combine",
    )(dest0.reshape(-1, 1, tm), dest1.reshape(-1, 1, tm), x, ys, rw, g2.reshape(b, 1, d),
      final_g.reshape(1, d))


def _pad_heads(w, n_heads, offsets):
    d = w.shape[0]
    w = w.reshape(d, n_heads, HEAD_DIM)
    z = jnp.zeros((d, n_heads, HEAD_DIM), w.dtype)
    off = jnp.asarray(offsets, jnp.int32).reshape(1, n_heads, 1)
    blk = jnp.where(off == 0, jnp.concatenate([w, z], axis=-1), jnp.concatenate([z, w], axis=-1))
    return blk.reshape(d, n_heads * LANES)


def _layout_w_in(w_in):
    d = w_in.shape[0]
    kvw = NSA_KV * HEAD_DIM
    sizes = (NSA_HEADS * HEAD_DIM, kvw, kvw, kvw, kvw, kvw, kvw, NSA_HEADS * 3,
             FOX_HEADS * HEAD_DIM, FOX_HEADS * HEAD_DIM, FOX_HEADS * HEAD_DIM, FOX_HEADS,
             SB_HEADS * HEAD_DIM, SB_HEADS * HEAD_DIM, SB_HEADS * HEAD_DIM)
    pts = np.cumsum(sizes)[:-1].tolist()
    (qa, kca, vca, ksa, vsa, kwa, vwa, ga, qb, kb, vb, fb, qc, kc, vc) = jnp.split(w_in, pts, axis=1)
    scale = HEAD_DIM ** -0.5 * LOG2E
    qa_p = _pad_heads(qa * scale, NSA_HEADS, [0] * NSA_GROUP + [HEAD_DIM] * NSA_GROUP)
    qb_p = _pad_heads(qb * scale, FOX_HEADS, [0, HEAD_DIM, 0, HEAD_DIM])
    qc_p = _pad_heads(qc * scale, SB_HEADS, [0, HEAD_DIM, 0, HEAD_DIM])
    main = jnp.concatenate([qa_p, ksa, kwa, vsa, vwa, qb_p, kb, vb, qc_p, kc, vc], axis=1).astype(BF16)
    cmp_w = jnp.concatenate([kca, vca], axis=1).astype(BF16)
    zpad = lambda n: jnp.zeros((d, n), w_in.dtype)
    per_grp = NSA_GROUP * 3
    small = jnp.concatenate([ga[:, :per_grp], zpad(LANES - per_grp), ga[:, per_grp:], zpad(LANES - per_grp),
                             fb[:, 0:2], zpad(LANES - 2), fb[:, 2:4], zpad(LANES - 2)], axis=1).astype(BF16)
    return main, cmp_w, small


def _layout_cmp(w1, w2):
    hid = w1.shape[1]
    w1r = w1.reshape(2, CMP_STRIDE, HEAD_DIM, hid)
    z = jnp.zeros((CMP_STRIDE, HEAD_DIM, hid), w1.dtype)
    cols = []
    for half in range(2):
        for g in range(NSA_KV):
            parts = [w1r[half] if gg == g else z for gg in range(NSA_KV)]
            cols.append(jnp.concatenate(parts, axis=1).reshape(CMP_STRIDE * LANES, hid))
    wcat = jnp.concatenate(cols, axis=1).astype(BF16)
    zz = jnp.zeros_like(w2)
    w2bd = jnp.concatenate([jnp.concatenate([w2, zz], axis=1),
                            jnp.concatenate([zz, w2], axis=1)], axis=0).astype(BF16)
    return wcat, w2bd


def _rope_tables(pos):
    inv = jnp.exp(jnp.arange(ROPE_HALF, dtype=F32) * (-2.0 * math.log(ROPE_THETA) / ROPE_DIM))
    ang = pos.astype(F32)[:, None] * inv[None, :]
    cos, sin = jnp.cos(ang), jnp.sin(ang)
    n = pos.shape[0]
    z8 = jnp.zeros((n, ROPE_HALF), F32)
    rest1 = jnp.ones((n, HEAD_DIM - ROPE_DIM), F32)
    rest0 = jnp.zeros((n, HEAD_DIM - ROPE_DIM), F32)
    c = jnp.concatenate([cos, cos, rest1], axis=1)
    s1 = jnp.concatenate([z8, sin, rest0], axis=1)
    s2 = jnp.concatenate([-sin, z8, rest0], axis=1)
    dup = lambda a: jnp.concatenate([a, a], axis=1)
    return dup(c), dup(s1), dup(s2)


def _static_tables(s):
    n_cmp_pad = s // CMP_STRIDE
    n = np.arange(n_cmp_pad)[:, None]
    j = np.arange(LANES)[None, :]
    n_sel = s // SEL_LEN
    cover = ((n * CMP_STRIDE < j * SEL_LEN + SEL_LEN) & (n * CMP_STRIDE + CMP_LEN > j * SEL_LEN)
             & (j < n_sel)).astype(np.float32)
    nt = s // TK_FOX
    key = np.arange(nt)[:, None, None] * TK_FOX + np.arange(TK_FOX)[None, None, :]
    expand = (key // SEL_LEN == np.arange(LANES)[None, :, None]).astype(np.float32)
    return jnp.asarray(cover, BF16), jnp.asarray(expand, BF16)


def _dispatch_plan(ri, t):
    eid = ri.reshape(t, LANES)[:, :2].reshape(-1)
    n_assign = eid.shape[0]
    onehot = (eid[:, None] == jnp.arange(N_EXPERTS, dtype=jnp.int32)[None, :]).astype(jnp.int32)
    csum = jnp.cumsum(onehot, axis=0)
    counts = csum[-1]
    rank = jnp.take_along_axis(csum, eid[:, None], axis=1)[:, 0] - 1
    padded = ((counts + TM_EXP - 1) // TM_EXP) * TM_EXP
    pends = jnp.cumsum(padded)
    pstarts = pends - padded
    dest = pstarts[eid] + rank
    n_tiles = -(-(n_assign + N_EXPERTS * (TM_EXP - 1)) // TM_EXP)
    tile_start = jnp.arange(n_tiles, dtype=jnp.int32) * TM_EXP
    tile_expert = jnp.minimum(jnp.sum((pends[None, :] <= tile_start[:, None]).astype(jnp.int32), axis=1),
                              N_EXPERTS - 1).astype(jnp.int32)
    n_used = (pends[-1] // TM_EXP).astype(jnp.int32).reshape(1)
    return dest.astype(jnp.int32), n_tiles * TM_EXP, tile_expert, n_used


def kernel(x, c, norm1_g, norm2_g, ada_w, ada_b, w_in, b_forget, cmp_pos_k, cmp_w1_k, cmp_w2_k,
           cmp_pos_v, cmp_w1_v, cmp_w2_v, out_norm_g, w_out, router_group_w, router_group_b,
           router_expert_w, router_expert_b, expert_w1, expert_w3, expert_w2, final_g):
    b, s, d = x.shape
    depth = ada_w.shape[0]
    t = b * s
    mod = _modulation(c, ada_w, ada_b)
    rope_c, rope_1, rope_2 = _rope_tables(jnp.arange(s))
    n_cmp_pad = s // CMP_STRIDE
    crc, cr1, cr2 = _rope_tables(jnp.arange(n_cmp_pad) * CMP_STRIDE + (CMP_LEN - 1))
    cover, expand = _static_tables(s)

    for l in range(depth):
        sh1, sc1, g1, sh2, sc2, g2 = [mod[l][:, i * d:(i + 1) * d] for i in range(6)]
        w_main, w_cmp, w_small = _layout_w_in(w_in[l])
        u, kc, vc, small = _in_projection(x, norm1_g[l], sc1, sh1, w_main, w_cmp, w_small,
                                          rope_c, rope_1, rope_2)
        bf = b_forget[l]
        zf = jnp.zeros((LANES - 2,), F32)
        b_pairs = jnp.stack([jnp.concatenate([bf[0:2], zf]), jnp.concatenate([bf[2:4], zf])]).reshape(2, 1, LANES)
        cq, ckt = _forget_cumsum(small, b_pairs)
        wk, w2k = _layout_cmp(cmp_w1_k[l], cmp_w2_k[l])
        wv, w2v = _layout_cmp(cmp_w1_v[l], cmp_w2_v[l])
        ck, cv = _compress(kc, vc, wk, wv, cmp_w1_k[l], cmp_w1_v[l],
                           jnp.broadcast_to(cmp_pos_k[l].reshape(1, -1), (8, CMP_LEN * HEAD_DIM)),
                           jnp.broadcast_to(cmp_pos_v[l].reshape(1, -1), (8, CMP_LEN * HEAD_DIM)),
                           w2k, w2v, crc, cr1, cr2)
        gn = out_norm_g[l].reshape(1, -1)
        o_a = _nsa_attention(u, ck, cv, small, gn[:, :NSA_HEADS * HEAD_DIM], cover, expand)
        o_b = _fox_attention(u, cq, ckt, gn[:, NSA_HEADS * HEAD_DIM:(NSA_HEADS + FOX_HEADS) * HEAD_DIM])
        o_c = _sb_attention(u, gn[:, (NSA_HEADS + FOX_HEADS) * HEAD_DIM:])

        wr = jnp.concatenate([router_group_w[l], router_expert_w[l],
                              jnp.zeros((d, LANES - N_GROUPS - N_EXPERTS), F32)], axis=1)
        wr_hi = wr.astype(BF16)
        wr_lo = (wr - wr_hi.astype(F32)).astype(BF16)
        br = jnp.concatenate([router_group_b[l], router_expert_b[l],
                              jnp.zeros((LANES - N_GROUPS - N_EXPERTS,), F32)]).reshape(1, LANES)
        x, h2, rw, ri = _out_projection(o_a, o_b, o_c, x, w_out[l].astype(BF16), g1, norm2_g[l],
                                        sc2, sh2, wr_hi, wr_lo, br)

        dest, p_rows, tile_expert, n_used = _dispatch_plan(ri, t)
        xs = _dispatch(h2, dest, p_rows)
        ys = _expert_mlp(xs, tile_expert, n_used, expert_w1, expert_w3, expert_w2, l)
        x = _combine(x, ys, dest[0::2], dest[1::2], rw, g2, final_g, final=(l == depth - 1))
    return x
```

```python
import functools
import math

import numpy as np
import jax
import jax.numpy as jnp
from jax import lax
from jax.experimental import pallas as pl
from jax.experimental.pallas import tpu as pltpu

F32 = jnp.float32
BF16 = jnp.bfloat16

HEAD_DIM = 64
LANES = 128
N_HEADS = 16
NSA_HEADS = 8
NSA_KV = 2
NSA_GROUP = 4
FOX_HEADS = 4
SB_HEADS = 4
ROPE_DIM = 16
ROPE_HALF = 8
ROPE_THETA = 500000.0
CMP_LEN = 32
CMP_STRIDE = 16
CMP_HIDDEN = 128
SEL_LEN = 64
SEL_TOPN = 16
WINDOW = 512
FORCE_SCORE = 1.0e4
N_GROUPS = 4
EXPERTS_PER_GROUP = 8
N_EXPERTS = 32
EPS = 1e-6
LOG2E = math.log2(math.e)
NEG = -1e30

COL_QA = 0
COL_KS = 1024
COL_KW = 1152
ROPE_COLS = 1280
COL_VS = 1280
COL_VW = 1408
COL_FOX = 1536
COL_SB = 2560
N_MAIN = 3584
N_SMALL = 512
PROJ_CHUNK = 1280

TM_PROJ = 512
TQ_NSA = 128
TK_ATT = 256
TQ_PAIR = 256
TK_FOX = 512
ROWS = 32
SB_SUBS = 2
TM_OUT = 512
TM_EXP = 512
TM_CMB = 512
COPY_CHUNK = 512
ROW_TILE = 8
VMEM_LIMIT = 56 * 1024 * 1024


def _cp(n_axes, vmem=VMEM_LIMIT):
    return pltpu.CompilerParams(dimension_semantics=("arbitrary",) * n_axes, vmem_limit_bytes=vmem)


def _dot(a, b):
    return jnp.dot(a, b, preferred_element_type=F32)


def _dot_nt(a, b):
    return lax.dot_general(a, b, (((1,), (1,)), ((), ())), preferred_element_type=F32)


def _split_bf16(x, parts):
    out = []
    r = x
    for _ in range(parts):
        p = r.astype(BF16)
        out.append(p)
        r = r - p.astype(F32)
    return out


def _dot_split(x, m, parts):
    acc = None
    for p in _split_bf16(x, parts):
        d = _dot(p, m)
        acc = d if acc is None else acc + d
    return acc


def _rope(x, c, s1, s2):
    return x * c + pltpu.roll(x, ROPE_HALF, 1) * s1 + pltpu.roll(x, LANES - ROPE_HALF, 1) * s2


def _softplus(z):
    return jnp.maximum(z, 0.0) + jnp.log(1.0 + jnp.exp(-jnp.abs(z)))


def _mod_kernel(c_ref, w_ref, b_ref, o_ref):
    c = c_ref[...]
    cond = c * (1.0 / (1.0 + jnp.exp(-c)))
    o_ref[0] = _dot(cond, w_ref[0]) + b_ref[0]


def _modulation(c, ada_w, ada_b):
    depth, d, n = ada_w.shape
    b = c.shape[0]
    tn = 1024
    return pl.pallas_call(
        _mod_kernel,
        grid=(depth, n // tn),
        in_specs=[pl.BlockSpec((b, d), lambda l, j: (0, 0)),
                  pl.BlockSpec((1, d, tn), lambda l, j: (l, 0, j)),
                  pl.BlockSpec((1, 1, tn), lambda l, j: (l, 0, j))],
        out_specs=pl.BlockSpec((1, b, tn), lambda l, j: (l, 0, j)),
        out_shape=jax.ShapeDtypeStruct((depth, b, n), F32),
        compiler_params=_cp(2),
        name="modulation",
    )(c, ada_w, ada_b.reshape(depth, 1, n))


def _inproj_kernel(x_ref, g_ref, sc_ref, sh_ref, w_ref, wc_ref, ws_ref, rc_ref, r1_ref, r2_ref,
                   u_ref, kc_ref, vc_ref, sm_ref):
    x = x_ref[0]
    ms = jnp.mean(x * x, axis=-1, keepdims=True)
    h = (x * lax.rsqrt(ms + EPS) * g_ref[...]) * (1.0 + sc_ref[0]) + sh_ref[0]
    hb = h.astype(BF16)
    rc, r1, r2 = rc_ref[...], r1_ref[...], r2_ref[...]
    for j in range(N_MAIN // PROJ_CHUNK + (1 if N_MAIN % PROJ_CHUNK else 0)):
        lo = j * PROJ_CHUNK
        hi = min(lo + PROJ_CHUNK, N_MAIN)
        acc = _dot(hb, w_ref[:, lo:hi])
        if lo < ROPE_COLS:
            for k in range((hi - lo) // LANES):
                blk = acc[:, k * LANES:(k + 1) * LANES]
                u_ref[0, :, lo + k * LANES:lo + (k + 1) * LANES] = _rope(blk, rc, r1, r2).astype(BF16)
        else:
            u_ref[0, :, lo:hi] = acc.astype(BF16)
    cmp_in = _dot(hb, wc_ref[...])
    kc_ref[0] = cmp_in[:, :LANES].astype(BF16)
    vc_ref[0] = cmp_in[:, LANES:].astype(BF16)
    sm_ref[0] = _dot(hb, ws_ref[...])


def _in_projection(x, g, sc, sh, w_main, w_cmp, w_small, rope_c, rope_1, rope_2):
    b, s, d = x.shape
    tm = min(TM_PROJ, s)
    row = lambda i, j: (i, j, 0)
    const2 = lambda i, j: (0, 0)
    per_b = lambda i, j: (i, 0, 0)
    seq = lambda i, j: (j, 0)
    return pl.pallas_call(
        _inproj_kernel,
        grid=(b, s // tm),
        in_specs=[pl.BlockSpec((1, tm, d), row),
                  pl.BlockSpec((1, d), const2),
                  pl.BlockSpec((1, 1, d), per_b),
                  pl.BlockSpec((1, 1, d), per_b),
                  pl.BlockSpec((d, N_MAIN), const2),
                  pl.BlockSpec((d, 2 * LANES), const2),
                  pl.BlockSpec((d, N_SMALL), const2),
                  pl.BlockSpec((tm, LANES), seq),
                  pl.BlockSpec((tm, LANES), seq),
                  pl.BlockSpec((tm, LANES), seq)],
        out_specs=[pl.BlockSpec((1, tm, N_MAIN), row),
                   pl.BlockSpec((1, tm, LANES), row),
                   pl.BlockSpec((1, tm, LANES), row),
                   pl.BlockSpec((1, tm, N_SMALL), row)],
        out_shape=[jax.ShapeDtypeStruct((b, s, N_MAIN), BF16),
                   jax.ShapeDtypeStruct((b, s, LANES), BF16),
                   jax.ShapeDtypeStruct((b, s, LANES), BF16),
                   jax.ShapeDtypeStruct((b, s, N_SMALL), F32)],
        compiler_params=_cp(2),
        name="in_projection",
    )(x, g.reshape(1, d), sc.reshape(b, 1, d), sh.reshape(b, 1, d), w_main, w_cmp, w_small,
      rope_c, rope_1, rope_2)


def _cumf_kernel(f_ref, b_ref, cq_ref, ckt_ref):
    n_chunks = f_ref.shape[1] // TK_ATT
    r = lax.broadcasted_iota(jnp.int32, (TK_ATT, TK_ATT), 0)
    c = lax.broadcasted_iota(jnp.int32, (TK_ATT, TK_ATT), 1)
    tri = jnp.where(c <= r, 1.0, 0.0).astype(BF16)
    carry = jnp.zeros((1, LANES), F32)
    for j in range(n_chunks):
        f = f_ref[0, j * TK_ATT:(j + 1) * TK_ATT, :] + b_ref[0]
        ls = -_softplus(-f)
        acc = None
        for p in _split_bf16(ls, 3):
            dd = _dot(tri, p)
            acc = dd if acc is None else acc + dd
        cs = acc + carry
        cs2 = cs * LOG2E
        cq_ref[0, j * TK_ATT:(j + 1) * TK_ATT, :] = cs2
        ckt_ref[0, 0, j] = cs2.T[:8, :]
        carry = cs[TK_ATT - 1:TK_ATT, :]


def _forget_cumsum(small, b_pairs):
    b, s, _ = small.shape
    return pl.pallas_call(
        _cumf_kernel,
        grid=(b, 2),
        in_specs=[pl.BlockSpec((1, s, LANES), lambda i, p: (i, 0, 2 + p)),
                  pl.BlockSpec((1, 1, LANES), lambda i, p: (p, 0, 0))],
        out_specs=[pl.BlockSpec((1, s, LANES), lambda i, p: (i, 0, p)),
                   pl.BlockSpec((1, 1, s // TK_ATT, 8, TK_ATT), lambda i, p: (i, p, 0, 0, 0))],
        out_shape=[jax.ShapeDtypeStruct((b, s, 2 * LANES), F32),
                   jax.ShapeDtypeStruct((b, 2, s // TK_ATT, 8, TK_ATT), F32)],
        compiler_params=_cp(2),
        name="forget_cumsum",
    )(small, b_pairs)


def _compress_kernel(ks_ref, vs_ref, wk_ref, wv_ref, w1k_ref, w1v_ref, pek_ref, pev_ref,
                     w2k_ref, w2v_ref, rc_ref, r1_ref, r2_ref, ck_ref, cv_ref):
    def one(seg_ref, w_ref, w1_ref, pe_ref, w2_ref):
        p = _dot(seg_ref[0], w_ref[...])
        half = 2 * CMP_HIDDEN
        bias = _dot(pe_ref[...].astype(BF16), w1_ref[...].astype(BF16))[0:1, :]
        bias2 = jnp.concatenate([bias, bias], axis=1)
        n = p.shape[0]
        hid = p[:, :half] + pltpu.roll(p[:, half:], n - 1, 0) + bias2
        act = hid * (1.0 / (1.0 + jnp.exp(-hid)))
        return _dot(act.astype(BF16), w2_ref[...])

    ck = one(ks_ref, wk_ref, w1k_ref, pek_ref, w2k_ref)
    ck_ref[0] = _rope(ck, rc_ref[...], r1_ref[...], r2_ref[...]).astype(BF16)
    cv_ref[0] = one(vs_ref, wv_ref, w1v_ref, pev_ref, w2v_ref).astype(BF16)


def _compress(kc, vc, wk, wv, w1k, w1v, pek, pev, w2k, w2v, rc, r1, r2):
    b, s, _ = kc.shape
    n = s // CMP_STRIDE
    width = CMP_STRIDE * LANES
    kseg = kc.reshape(b, n, width)
    vseg = vc.reshape(b, n, width)
    seg = pl.BlockSpec((1, n, width), lambda i: (i, 0, 0))
    full = lambda a: pl.BlockSpec(a.shape, lambda i: (0,) * a.ndim)
    return pl.pallas_call(
        _compress_kernel,
        grid=(b,),
        in_specs=[seg, seg, full(wk), full(wv), full(w1k), full(w1v), full(pek), full(pev),
                  full(w2k), full(w2v), full(rc), full(r1), full(r2)],
        out_specs=[pl.BlockSpec((1, n, LANES), lambda i: (i, 0, 0)),
                   pl.BlockSpec((1, n, LANES), lambda i: (i, 0, 0))],
        out_shape=[jax.ShapeDtypeStruct((b, n, LANES), BF16),
                   jax.ShapeDtypeStruct((b, n, LANES), BF16)],
        compiler_params=_cp(1),
        name="nsa_compress",
    )(kseg, vseg, wk, wv, w1k, w1v, pek, pev, w2k, w2v, rc, r1, r2)


def _nsa_kernel(q_ref, ck_ref, cv_ref, ks_ref, vs_ref, kw_ref, vw_ref, gate_ref, gn_ref,
                cover_ref, expand_ref, o_ref, m_ref, l_ref, acc_ref):
    tq = q_ref.shape[1]
    g = pl.program_id(1)
    t0 = pl.program_id(2) * tq
    q = q_ref[0]
    qall = jnp.concatenate([q[:, h * LANES:(h + 1) * LANES] for h in range(NSA_GROUP)], axis=0)
    tpos = t0 + lax.broadcasted_iota(jnp.int32, (tq, 1), 0)
    lane = lax.broadcasted_iota(jnp.int32, (tq, LANES), 1)

    s = _dot_nt(qall, ck_ref[0])
    cmp_end = (CMP_STRIDE * lax.broadcasted_iota(jnp.int32, (ROWS, LANES), 1) + (CMP_LEN - 1)
               - lax.broadcasted_iota(jnp.int32, (ROWS, LANES), 0))
    n_chunks = tq // ROWS
    p_rows, p_sum = [], [None] * n_chunks
    for h in range(NSA_GROUP):
        for c in range(n_chunks):
            r0 = h * tq + c * ROWS
            valid = cmp_end <= t0 + c * ROWS
            sm = jnp.where(valid, s[r0:r0 + ROWS, :], NEG)
            e = jnp.where(valid, jnp.exp2(sm - jnp.max(sm, axis=-1, keepdims=True)), 0.0)
            den = jnp.sum(e, axis=-1, keepdims=True)
            p = e * jnp.where(den > 0.0, 1.0 / den, 0.0)
            p_rows.append(p.astype(BF16))
            p_sum[c] = p if p_sum[c] is None else p_sum[c] + p
    o_cmp = _dot(jnp.concatenate(p_rows, axis=0), cv_ref[0]).reshape(NSA_GROUP, tq, LANES)

    n_sel = expand_ref.shape[0] * (TK_FOX // SEL_LEN)
    imp_t = _dot_split(jnp.concatenate(p_sum, axis=0), cover_ref[...], 3).T[:n_sel, :]
    blk = lax.broadcasted_iota(jnp.int32, (n_sel, tq), 0)
    tcol = t0 + lax.broadcasted_iota(jnp.int32, (n_sel, tq), 1)
    cur = jnp.right_shift(tcol, int(math.log2(SEL_LEN)))
    forced = (blk == 0) | (blk == cur) | (blk == cur - 1)
    score = jnp.where(forced, FORCE_SCORE, jnp.where(blk * SEL_LEN <= tcol, imp_t, -1.0))
    cnt = jnp.zeros((n_sel, tq), F32)
    for j in range(n_sel):
        sj = score[j:j + 1, :]
        beats = (sj > score) | ((sj == score) & (blk > j))
        cnt = cnt + jnp.where(beats, 1.0, 0.0)
    sel_t = jnp.where(cnt < float(min(SEL_TOPN, n_sel)), 1.0, 0.0)
    sel = jnp.concatenate([sel_t, jnp.zeros((LANES - n_sel, tq), F32)], axis=0).T.astype(BF16)

    def biased(bias):
        def adjust(h, t_off, cols):
            return [cols[kk] + bias[t_off:t_off + ROWS, kk * LANES:(kk + 1) * LANES]
                    for kk in range(len(cols))]
        return adjust

    kcol_s = lax.broadcasted_iota(jnp.int32, (tq, TK_FOX), 1)

    def sel_tile(kt, carry):
        k0 = pl.multiple_of(kt * TK_FOX, TK_FOX)
        hit = _dot(sel, expand_ref[kt])
        ok = (hit > 0.5) & ((k0 + kcol_s) <= tpos)
        s_t = _dot_nt(qall, ks_ref[0, pl.ds(k0, TK_FOX), :])
        _softmax_tile(s_t, vs_ref[0, pl.ds(k0, TK_FOX), :], m_ref, l_ref, acc_ref, NSA_GROUP, tq,
                      biased(jnp.where(ok, 0.0, NEG)))
        return carry

    _softmax_init(m_ref, l_ref, acc_ref)
    lax.fori_loop(0, lax.div(t0 + tq - 1, TK_FOX) + 1, sel_tile, 0)
    o_sel = _softmax_result(l_ref, acc_ref).reshape(NSA_GROUP, tq, LANES)

    span = WINDOW + tq
    w0 = pl.multiple_of(jnp.maximum(t0 - WINDOW, 0), LANES)
    kp = w0 + lax.broadcasted_iota(jnp.int32, (tq, span), 1)
    bias_w = jnp.where((kp <= tpos) & (kp > tpos - WINDOW), 0.0, NEG)
    s_w = _dot_nt(qall, kw_ref[0, pl.ds(w0, span), :])
    nkw = span // LANES
    p_rows, inv_rows = [], []
    for h in range(NSA_GROUP):
        for c in range(tq // ROWS):
            r0 = h * tq + c * ROWS
            cols = [s_w[r0:r0 + ROWS, kk * LANES:(kk + 1) * LANES]
                    + bias_w[c * ROWS:(c + 1) * ROWS, kk * LANES:(kk + 1) * LANES] for kk in range(nkw)]
            mx = cols[0]
            for kk in range(1, nkw):
                mx = jnp.maximum(mx, cols[kk])
            mx = jnp.max(mx, axis=-1, keepdims=True)
            pks = [jnp.exp2(cols[kk] - mx) for kk in range(nkw)]
            psum = pks[0]
            for kk in range(1, nkw):
                psum = psum + pks[kk]
            inv_rows.append(jnp.broadcast_to(1.0 / jnp.sum(psum, axis=-1, keepdims=True), (ROWS, LANES)))
            p_rows.append(jnp.concatenate([pk.astype(BF16) for pk in pks], axis=1))
    o_win = _dot(jnp.concatenate(p_rows, axis=0), vw_ref[0, pl.ds(w0, span), :])
    o_win = (o_win * jnp.concatenate(inv_rows, axis=0)).reshape(NSA_GROUP, tq, LANES)

    gt = gate_ref[0]
    gt = 1.0 / (1.0 + jnp.exp(-gt))
    mine = (lane >= g * HEAD_DIM) & (lane < (g + 1) * HEAD_DIM)
    outs = []
    for h in range(NSA_GROUP):
        o = (gt[:, 3 * h:3 * h + 1] * o_cmp[h] + gt[:, 3 * h + 1:3 * h + 2] * o_sel[h]
             + gt[:, 3 * h + 2:3 * h + 3] * o_win[h])
        o = jnp.where(mine, o, 0.0)
        ms = jnp.sum(o * o, axis=-1, keepdims=True) * (1.0 / HEAD_DIM)
        o = o * lax.rsqrt(ms + EPS)
        outs.append(o + pltpu.roll(o, HEAD_DIM, 1))
    left = lane < HEAD_DIM
    o_ref[0, :, :LANES] = (jnp.where(left, outs[0], outs[1]) * gn_ref[:, :LANES]).astype(o_ref.dtype)
    o_ref[0, :, LANES:] = (jnp.where(left, outs[2], outs[3]) * gn_ref[:, LANES:]).astype(o_ref.dtype)


def _nsa_attention(u, ck, cv, small, gn, cover, expand):
    b, s, _ = u.shape
    tq = min(TQ_NSA, s)
    assert s >= WINDOW + tq and WINDOW % tq == 0
    n_cmp = ck.shape[1]
    blk = LANES
    kv = lambda col: pl.BlockSpec((1, s, LANES), lambda i, g, j, col=col: (i, 0, col // blk))
    return pl.pallas_call(
        _nsa_kernel,
        grid=(b, NSA_KV, s // tq),
        in_specs=[pl.BlockSpec((1, tq, NSA_GROUP * LANES), lambda i, g, j: (i, j, g)),
                  pl.BlockSpec((1, n_cmp, LANES), lambda i, g, j: (i, 0, 0)),
                  pl.BlockSpec((1, n_cmp, LANES), lambda i, g, j: (i, 0, 0)),
                  kv(COL_KS), kv(COL_VS), kv(COL_KW), kv(COL_VW),
                  pl.BlockSpec((1, tq, LANES), lambda i, g, j: (i, j, g)),
                  pl.BlockSpec((1, 2 * LANES), lambda i, g, j: (0, g)),
                  pl.BlockSpec(cover.shape, lambda i, g, j: (0, 0)),
                  pl.BlockSpec(expand.shape, lambda i, g, j: (0, 0, 0))],
        out_specs=pl.BlockSpec((1, tq, 2 * LANES), lambda i, g, j: (i, j, g)),
        out_shape=jax.ShapeDtypeStruct((b, s, NSA_HEADS * HEAD_DIM), BF16),
        scratch_shapes=_softmax_scratch(NSA_GROUP * tq),
        compiler_params=_cp(3),
        name="nsa_attention",
    )(u, ck, cv, u, u, u, u, small, gn, cover, expand)


def _pair_finish(acc, gn_ref, o_ref, tq):
    lane = lax.broadcasted_iota(jnp.int32, (tq, LANES), 1)
    left = lane < HEAD_DIM
    o = jnp.where(left, acc[0], acc[1])
    o2 = o * o
    ms_l = jnp.sum(jnp.where(left, o2, 0.0), axis=-1, keepdims=True) * (1.0 / HEAD_DIM)
    ms_r = jnp.sum(jnp.where(left, 0.0, o2), axis=-1, keepdims=True) * (1.0 / HEAD_DIM)
    inv = jnp.where(left, lax.rsqrt(ms_l + EPS), lax.rsqrt(ms_r + EPS))
    o_ref[0] = (o * inv * gn_ref[...]).astype(o_ref.dtype)


def _softmax_tile(s, v, m_ref, l_ref, acc_ref, heads, tq, adjust):
    nk = s.shape[1] // LANES
    p_rows = []
    for h in range(heads):
        for c in range(tq // ROWS):
            t_off = c * ROWS
            r0 = h * tq + t_off
            cols = [s[r0:r0 + ROWS, k * LANES:(k + 1) * LANES] for k in range(nk)]
            cols = adjust(h, t_off, cols)
            mx = cols[0]
            for k in range(1, nk):
                mx = jnp.maximum(mx, cols[k])
            m_old = m_ref[r0:r0 + ROWS, :]
            m_new = jnp.maximum(m_old, jnp.max(mx, axis=-1, keepdims=True))
            alpha = jnp.exp2(m_old - m_new)
            pks = [jnp.exp2(cols[k] - m_new) for k in range(nk)]
            psum = pks[0]
            for k in range(1, nk):
                psum = psum + pks[k]
            p_rows.append(jnp.concatenate([pk.astype(BF16) for pk in pks], axis=1))
            l_ref[r0:r0 + ROWS, :] = alpha * l_ref[r0:r0 + ROWS, :] + psum
            acc_ref[r0:r0 + ROWS, :] = alpha * acc_ref[r0:r0 + ROWS, :]
            m_ref[r0:r0 + ROWS, :] = m_new
    acc_ref[...] += _dot(jnp.concatenate(p_rows, axis=0), v)


def _tile_loop(score_fn, process_fn, lo, hi):
    def body(kt, carry):
        process_fn(kt, score_fn(kt))
        return carry

    lax.fori_loop(lo, hi - 1, body, 0)
    return score_fn(hi - 1)


def _softmax_init(m_ref, l_ref, acc_ref):
    m_ref[...] = jnp.full(m_ref.shape, NEG, F32)
    l_ref[...] = jnp.zeros(l_ref.shape, F32)
    acc_ref[...] = jnp.zeros(acc_ref.shape, F32)


def _softmax_result(l_ref, acc_ref):
    return acc_ref[...] / jnp.sum(l_ref[...], axis=-1, keepdims=True)


def _softmax_scratch(rows):
    return [pltpu.VMEM((rows, LANES), F32), pltpu.VMEM((rows, LANES), F32), pltpu.VMEM((rows, LANES), F32)]


def _fox_kernel(q_ref, k_ref, v_ref, cq_ref, ckt_ref, gn_ref, o_ref,
                m_ref, l_ref, acc_ref, cqr_ref):
    tq = q_ref.shape[1]
    tk = TK_FOX
    t0 = pl.program_id(2) * tq
    q = q_ref[0]
    qall = jnp.concatenate([q[:, :LANES], q[:, LANES:]], axis=0)
    cq = cq_ref[0]
    cqr_ref[0:tq, :] = jnp.broadcast_to(cq[:, 0:1], (tq, LANES))
    cqr_ref[tq:2 * tq, :] = jnp.broadcast_to(cq[:, 1:2], (tq, LANES))
    _softmax_init(m_ref, l_ref, acc_ref)
    n_tiles = lax.div(t0 + tq - 1, tk) + 1
    diag = (lax.broadcasted_iota(jnp.int32, (ROWS, LANES), 1)
            - lax.broadcasted_iota(jnp.int32, (ROWS, LANES), 0))

    def scores(kt):
        return _dot_nt(qall, k_ref[0, pl.ds(pl.multiple_of(kt * tk, tk), tk), :])

    def tile(kt, s, masked):
        k0 = pl.multiple_of(kt * tk, tk)
        cks = [ckt_ref[0, 0, kt * (tk // TK_ATT) + j] for j in range(tk // TK_ATT)]

        def adjust(h, t_off, cols):
            out = []
            cqr = cqr_ref[h * tq + t_off:h * tq + t_off + ROWS, :]
            for kk in range(tk // LANES):
                lo = (kk * LANES) % TK_ATT
                ck = cks[(kk * LANES) // TK_ATT][h:h + 1, lo:lo + LANES]
                val = (cols[kk] - ck) + cqr
                if masked:
                    val = jnp.where(diag <= (t0 + t_off) - (k0 + kk * LANES), val, NEG)
                out.append(val)
            return out

        _softmax_tile(s, v_ref[0, pl.ds(k0, tk), :], m_ref, l_ref, acc_ref, 2, tq, adjust)

    s_last = _tile_loop(scores, lambda kt, s: tile(kt, s, False), 0, n_tiles)
    tile(n_tiles - 1, s_last, True)
    acc = _softmax_result(l_ref, acc_ref).reshape(2, tq, LANES)
    _pair_finish(acc, gn_ref, o_ref, tq)


def _fox_attention(u, cq, ckt, gn):
    b, s, _ = u.shape
    tq = min(TQ_PAIR, s)
    qb = COL_FOX // (2 * LANES)
    kb = (COL_FOX + FOX_HEADS * LANES) // LANES
    vb = kb + 2
    return pl.pallas_call(
        _fox_kernel,
        grid=(b, 2, s // tq),
        in_specs=[pl.BlockSpec((1, tq, 2 * LANES), lambda i, p, j: (i, j, qb + p)),
                  pl.BlockSpec((1, s, LANES), lambda i, p, j: (i, 0, kb + p)),
                  pl.BlockSpec((1, s, LANES), lambda i, p, j: (i, 0, vb + p)),
                  pl.BlockSpec((1, tq, LANES), lambda i, p, j: (i, j, p)),
                  pl.BlockSpec((1, 1, s // TK_ATT, 8, TK_ATT), lambda i, p, j: (i, p, 0, 0, 0)),
                  pl.BlockSpec((1, LANES), lambda i, p, j: (0, p))],
        out_specs=pl.BlockSpec((1, tq, LANES), lambda i, p, j: (i, j, p)),
        out_shape=jax.ShapeDtypeStruct((b, s, FOX_HEADS * HEAD_DIM), BF16),
        scratch_shapes=_softmax_scratch(2 * tq) + [pltpu.VMEM((2 * tq, LANES), F32)],
        compiler_params=_cp(3),
        name="fox_attention",
    )(u, u, u, cq, ckt, gn)


def _sb_kernel(q_ref, k_ref, v_ref, gn_ref, o_ref, rest_ref, acc_ref):
    tq = q_ref.shape[1]
    sub = TK_ATT
    tk = SB_SUBS * sub
    nks = sub // LANES
    rows = 2 * tq
    t0 = pl.program_id(2) * tq
    r = lax.broadcasted_iota(jnp.int32, (sub, sub), 0)
    c = lax.broadcasted_iota(jnp.int32, (sub, sub), 1)
    upper = jnp.where(r >= c, 1.0, 0.0).astype(BF16)
    n_tiles = lax.div(t0 + tq - 1, tk) + 1
    diag = (lax.broadcasted_iota(jnp.int32, (ROWS, LANES), 1)
            - lax.broadcasted_iota(jnp.int32, (ROWS, LANES), 0))
    rest_ref[...] = jnp.zeros(rest_ref.shape, F32)
    acc_ref[...] = jnp.zeros(acc_ref.shape, F32)
    hi_mask = jnp.uint32(0xFFFF0000)
    q = q_ref[0]
    qall = jnp.concatenate([q[:, :LANES], q[:, LANES:]], axis=0)
    chunks = [(h * tq + cc * ROWS, cc * ROWS) for h in range(2) for cc in range(tq // ROWS)]

    def tile(kt, masked):
        k0 = pl.multiple_of(kt * tk, tk)
        k = k_ref[0, pl.ds(k0, tk), :]
        v = v_ref[0, pl.ds(k0, tk), :]

        def strictly_before(t_off, col):
            return diag < (t0 + t_off) - (k0 + col)

        z = _dot_nt(qall, k)
        his = [[] for _ in range(SB_SUBS)]
        los = [[] for _ in range(SB_SUBS)]
        for r0, t_off in chunks:
            for sb in range(SB_SUBS):
                hi_c, lo_c = [], []
                for kk in range(nks):
                    col = sb * sub + kk * LANES
                    zc = z[r0:r0 + ROWS, col:col + LANES]
                    l = -(jnp.maximum(zc, 0.0) + jnp.log2(1.0 + jnp.exp2(-jnp.abs(zc))))
                    if masked:
                        l = jnp.where(strictly_before(t_off, col), l, 0.0)
                    hi = pltpu.bitcast(pltpu.bitcast(l, jnp.uint32) & hi_mask, F32)
                    hi_c.append(hi.astype(BF16))
                    lo_c.append((l - hi).astype(BF16))
                his[sb].append(jnp.concatenate(hi_c, axis=1))
                los[sb].append(jnp.concatenate(lo_c, axis=1))
        hi_all = jnp.concatenate([x for sb in range(SB_SUBS) for x in his[sb]], axis=0)
        lo_all = jnp.concatenate([x for sb in range(SB_SUBS) for x in los[sb]], axis=0)
        cum = _dot(hi_all, upper) + _dot(lo_all, upper)

        a_rows = []
        for r0, t_off in chunks:
            base = rest_ref[r0:r0 + ROWS, :]
            a_c = [None] * (SB_SUBS * nks)
            for sb in reversed(range(SB_SUBS)):
                cs = cum[sb * rows + r0:sb * rows + r0 + ROWS, :]
                for kk in range(nks):
                    col = sb * sub + kk * LANES
                    a = jnp.exp2(z[r0:r0 + ROWS, col:col + LANES] + cs[:, kk * LANES:(kk + 1) * LANES] + base)
                    if masked:
                        a = jnp.where(strictly_before(t_off, col), a, 0.0)
                    a_c[sb * nks + kk] = a.astype(BF16)
                base = base + jnp.broadcast_to(cs[:, 0:1], (ROWS, LANES))
            a_rows.append(jnp.concatenate(a_c, axis=1))
            rest_ref[r0:r0 + ROWS, :] = base
        acc_ref[...] += _dot(jnp.concatenate(a_rows, axis=0), v)

    tile(n_tiles - 1, True)

    def full_tile(i, carry):
        tile(n_tiles - 2 - i, False)
        return carry

    lax.fori_loop(0, n_tiles - 1, full_tile, 0)
    _pair_finish(acc_ref[...].reshape(2, tq, LANES), gn_ref, o_ref, tq)


def _sb_attention(u, gn):
    b, s, _ = u.shape
    tq = min(TQ_PAIR, s)
    qb = COL_SB // (2 * LANES)
    kb = (COL_SB + SB_HEADS * LANES) // LANES
    vb = kb + 2
    return pl.pallas_call(
        _sb_kernel,
        grid=(b, 2, s // tq),
        in_specs=[pl.BlockSpec((1, tq, 2 * LANES), lambda i, p, j: (i, j, qb + p)),
                  pl.BlockSpec((1, s, LANES), lambda i, p, j: (i, 0, kb + p)),
                  pl.BlockSpec((1, s, LANES), lambda i, p, j: (i, 0, vb + p)),
                  pl.BlockSpec((1, LANES), lambda i, p, j: (0, p))],
        out_specs=pl.BlockSpec((1, tq, LANES), lambda i, p, j: (i, j, p)),
        out_shape=jax.ShapeDtypeStruct((b, s, SB_HEADS * HEAD_DIM), BF16),
        scratch_shapes=[pltpu.VMEM((2 * tq, LANES), F32), pltpu.VMEM((2 * tq, LANES), F32)],
        compiler_params=_cp(3),
        name="sb_attention",
    )(u, u, u, gn)


def _outproj_kernel(oa_ref, ob_ref, oc_ref, x_ref, w_ref, g1_ref, n2_ref, sc_ref, sh_ref,
                    wrh_ref, wrl_ref, br_ref, xo_ref, h_ref, rw_ref, ri_ref):
    na = oa_ref.shape[2]
    nb = ob_ref.shape[2]
    y = _dot(oa_ref[0], w_ref[0:na, :])
    y = y + _dot(ob_ref[0], w_ref[na:na + nb, :])
    y = y + _dot(oc_ref[0], w_ref[na + nb:, :])
    x = x_ref[0] + g1_ref[0] * y
    xo_ref[0] = x
    ms = jnp.mean(x * x, axis=-1, keepdims=True)
    h = (x * lax.rsqrt(ms + EPS) * n2_ref[...]) * (1.0 + sc_ref[0]) + sh_ref[0]
    hb = h.astype(BF16)
    _store_row_tiles(h_ref, h)
    hl = (h - hb.astype(F32)).astype(BF16)
    logit = _dot(hb, wrh_ref[...]) + _dot(hl, wrh_ref[...]) + _dot(hb, wrl_ref[...]) + br_ref[...]

    tm = logit.shape[0]
    lane = lax.broadcasted_iota(jnp.int32, (tm, LANES), 1).astype(F32)
    big = float(LANES)
    is_g = lane < N_GROUPS
    lg = jnp.where(is_g, logit, NEG)
    mg = jnp.max(lg, axis=-1, keepdims=True)
    zg = jnp.sum(jnp.where(is_g, jnp.exp(lg - mg), 0.0), axis=-1, keepdims=True)
    pg = 1.0 / zg
    gi = jnp.min(jnp.where(is_g & (lg == mg), lane, big), axis=-1, keepdims=True)
    e_lane = lane - N_GROUPS
    in_grp = (e_lane >= gi * EXPERTS_PER_GROUP) & (e_lane < (gi + 1) * EXPERTS_PER_GROUP)
    le = jnp.where(in_grp, logit, NEG)
    m1 = jnp.max(le, axis=-1, keepdims=True)
    i1 = jnp.min(jnp.where(in_grp & (le == m1), lane, big), axis=-1, keepdims=True)
    rest = in_grp & (lane != i1)
    le2 = jnp.where(rest, logit, NEG)
    m2 = jnp.max(le2, axis=-1, keepdims=True)
    i2 = jnp.min(jnp.where(rest & (le2 == m2), lane, big), axis=-1, keepdims=True)
    ze = jnp.sum(jnp.where(in_grp, jnp.exp(le - m1), 0.0), axis=-1, keepdims=True)
    p1 = 1.0 / ze
    p2 = jnp.exp(m2 - m1) / ze
    den = p1 + p2
    w1 = pg * (p1 / den)
    w2 = pg * (p2 / den)
    rw_ref[0] = jnp.where(lane == 0.0, w1, jnp.where(lane == 1.0, w2, 0.0))
    ri_ref[0] = jnp.where(lane == 0.0, i1 - N_GROUPS, jnp.where(lane == 1.0, i2 - N_GROUPS, 0.0)).astype(jnp.int32)


def _out_projection(oa, ob, oc, x, w_out, g1, n2, sc, sh, wr_hi, wr_lo, br):
    b, s, d = x.shape
    tm = min(TM_OUT, s)
    row = lambda i, j: (i, j, 0)
    const2 = lambda i, j: (0, 0)
    per_b = lambda i, j: (i, 0, 0)
    return pl.pallas_call(
        _outproj_kernel,
        grid=(b, s // tm),
        in_specs=[pl.BlockSpec((1, tm, oa.shape[2]), row),
                  pl.BlockSpec((1, tm, ob.shape[2]), row),
                  pl.BlockSpec((1, tm, oc.shape[2]), row),
                  pl.BlockSpec((1, tm, d), row),
                  pl.BlockSpec(w_out.shape, const2),
                  pl.BlockSpec((1, 1, d), per_b),
                  pl.BlockSpec((1, d), const2),
                  pl.BlockSpec((1, 1, d), per_b),
                  pl.BlockSpec((1, 1, d), per_b),
                  pl.BlockSpec((d, LANES), const2),
                  pl.BlockSpec((d, LANES), const2),
                  pl.BlockSpec((1, LANES), const2)],
        out_specs=[pl.BlockSpec((1, tm, d), row),
                   pl.BlockSpec((tm * ROW_TILE, LANES), lambda i, j: (i * (s // tm) + j, 0)),
                   pl.BlockSpec((1, tm, LANES), row),
                   pl.BlockSpec((1, tm, LANES), row)],
        out_shape=[jax.ShapeDtypeStruct((b, s, d), F32),
                   jax.ShapeDtypeStruct((b * s * ROW_TILE, LANES), F32),
                   jax.ShapeDtypeStruct((b, s, LANES), F32),
                   jax.ShapeDtypeStruct((b, s, LANES), jnp.int32)],
        compiler_params=_cp(2),
        name="out_projection",
    )(oa, ob, oc, x, w_out, g1.reshape(b, 1, d), n2.reshape(1, d), sc.reshape(b, 1, d),
      sh.reshape(b, 1, d), wr_hi, wr_lo, br)


def _store_row_tiles(ref, val):
    tm = val.shape[0]
    for c in range(ROW_TILE):
        ref[pl.ds(c, tm, stride=ROW_TILE), :] = val[:, c * LANES:(c + 1) * LANES]


def _load_row_tiles(ref, tm):
    return [ref[pl.ds(c, tm, stride=ROW_TILE), :] for c in range(ROW_TILE)]


def _tile_rows(ref, n):
    return ref.at[pl.ds(pl.multiple_of(n * ROW_TILE, ROW_TILE), ROW_TILE), :]


def _dispatch_kernel(idx_ref, src_ref, init_hbm, dst_hbm, sem):
    del init_hbm

    def issue(j, carry):
        tok = lax.shift_right_logical(j, 1)
        pltpu.make_async_copy(_tile_rows(src_ref, tok), _tile_rows(dst_hbm, idx_ref[0, 0, j]), sem).start()
        return carry

    lax.fori_loop(0, COPY_CHUNK, issue, 0, unroll=8)
    for _ in range(2):
        pltpu.make_async_copy(src_ref, dst_hbm.at[pl.ds(0, src_ref.shape[0]), :], sem).wait()


def _dispatch(src, idx, n_dst, init):
    n = idx.shape[0]
    if init is None:
        init = jnp.zeros((n_dst * ROW_TILE, LANES), src.dtype)
    return pl.pallas_call(
        _dispatch_kernel,
        grid=(n // COPY_CHUNK,),
        in_specs=[pl.BlockSpec((1, 1, COPY_CHUNK), lambda i: (i, 0, 0), memory_space=pltpu.SMEM),
                  pl.BlockSpec((COPY_CHUNK // 2 * ROW_TILE, LANES), lambda i: (i, 0)),
                  pl.BlockSpec(memory_space=pl.ANY)],
        out_specs=pl.BlockSpec(memory_space=pl.ANY),
        out_shape=jax.ShapeDtypeStruct((n_dst * ROW_TILE, LANES), src.dtype),
        scratch_shapes=[pltpu.SemaphoreType.DMA],
        input_output_aliases={2: 0},
        compiler_params=_cp(1),
        name="moe_dispatch",
    )(idx.reshape(n // COPY_CHUNK, 1, COPY_CHUNK), src, init)


def _expert_kernel(te_ref, nu_ref, x_ref, w1_ref, w3_ref, w2_ref, y_ref, w1b_ref, w3b_ref, w2b_ref):
    i = pl.program_id(0)

    @pl.when((i == 0) | (te_ref[i] != te_ref[jnp.maximum(i - 1, 0)]))
    def _():
        w1b_ref[...] = w1_ref[0].astype(BF16)
        w3b_ref[...] = w3_ref[0].astype(BF16)
        w2b_ref[...] = w2_ref[0].astype(BF16)

    @pl.when(i < nu_ref[0])
    def _():
        x = jnp.concatenate(_load_row_tiles(x_ref, TM_EXP), axis=1).astype(BF16)
        a = _dot(x, w1b_ref[...])
        g = _dot(x, w3b_ref[...])
        act = (a * (1.0 / (1.0 + jnp.exp(-a)))) * g
        _store_row_tiles(y_ref, _dot(act.astype(BF16), w2b_ref[...]))

    @pl.when(i >= nu_ref[0])
    def _():
        y_ref[...] = jnp.zeros_like(y_ref)


def _expert_mlp(xs, tile_expert, n_used, w1, w3, w2, layer):
    d, de = w1.shape[2], w1.shape[3]
    n_tiles = xs.shape[0] // (TM_EXP * ROW_TILE)
    grid_spec = pltpu.PrefetchScalarGridSpec(
        num_scalar_prefetch=2,
        grid=(n_tiles,),
        in_specs=[pl.BlockSpec((TM_EXP * ROW_TILE, LANES), lambda i, te, nu: (i, 0)),
                  pl.BlockSpec((None, 1, d, de), lambda i, te, nu: (layer, te[i], 0, 0)),
                  pl.BlockSpec((None, 1, d, de), lambda i, te, nu: (layer, te[i], 0, 0)),
                  pl.BlockSpec((None, 1, de, d), lambda i, te, nu: (layer, te[i], 0, 0))],
        out_specs=pl.BlockSpec((TM_EXP * ROW_TILE, LANES), lambda i, te, nu: (i, 0)),
        scratch_shapes=[pltpu.VMEM((d, de), BF16), pltpu.VMEM((d, de), BF16), pltpu.VMEM((de, d), BF16)],
    )
    return pl.pallas_call(
        _expert_kernel,
        grid_spec=grid_spec,
        out_shape=jax.ShapeDtypeStruct(xs.shape, F32),
        compiler_params=_cp(1),
        name="expert_mlp",
    )(tile_expert, n_used, xs, w1, w3, w2)


def _combine_kernel(d0_ref, d1_ref, x_ref, ys_hbm, rw_ref, g2_ref, fg_ref, o_ref, y0_ref, y1_ref, sem,
                    *, final):
    tm = x_ref.shape[1]

    def issue(r, carry):
        pltpu.make_async_copy(_tile_rows(ys_hbm, d0_ref[0, 0, r]), _tile_rows(y0_ref, r), sem.at[0]).start()
        pltpu.make_async_copy(_tile_rows(ys_hbm, d1_ref[0, 0, r]), _tile_rows(y1_ref, r), sem.at[1]).start()
        return carry

    lax.fori_loop(0, tm, issue, 0, unroll=8)
    pltpu.make_async_copy(ys_hbm.at[pl.ds(0, tm * ROW_TILE), :], y0_ref, sem.at[0]).wait()
    pltpu.make_async_copy(ys_hbm.at[pl.ds(0, tm * ROW_TILE), :], y1_ref, sem.at[1]).wait()

    rw = rw_ref[0]
    w0 = jnp.broadcast_to(rw[:, 0:1], (tm, LANES))
    w1 = jnp.broadcast_to(rw[:, 1:2], (tm, LANES))
    y0 = _load_row_tiles(y0_ref, tm)
    y1 = _load_row_tiles(y1_ref, tm)
    cols = []
    for c in range(ROW_TILE):
        sl = slice(c * LANES, (c + 1) * LANES)
        cols.append(x_ref[0, :, sl] + g2_ref[0, :, sl] * (y0[c] * w0 + y1[c] * w1))
    if final:
        ssq = cols[0] * cols[0]
        for c in range(1, ROW_TILE):
            ssq = ssq + cols[c] * cols[c]
        inv = lax.rsqrt(jnp.sum(ssq, axis=-1, keepdims=True) * (1.0 / (ROW_TILE * LANES)) + EPS)
        cols = [cols[c] * inv * fg_ref[:, c * LANES:(c + 1) * LANES] for c in range(ROW_TILE)]
    for c in range(ROW_TILE):
        o_ref[0, :, c * LANES:(c + 1) * LANES] = cols[c]


def _combine(x, ys, dest0, dest1, rw, g2, final_g, final):
    b, s, d = x.shape
    tm = min(TM_CMB, s)
    row = lambda i, j: (i, j, 0)
    idx_spec = pl.BlockSpec((1, 1, tm), lambda i, j: (i * (s // tm) + j, 0, 0), memory_space=pltpu.SMEM)
    return pl.pallas_call(
        functools.partial(_combine_kernel, final=final),
        grid=(b, s // tm),
        in_specs=[idx_spec, idx_spec,
                  pl.BlockSpec((1, tm, d), row),
                  pl.BlockSpec(memory_space=pl.ANY),
                  pl.BlockSpec((1, tm, LANES), row),
                  pl.BlockSpec((1, 1, d), lambda i, j: (i, 0, 0)),
                  pl.BlockSpec((1, d), lambda i, j: (0, 0))],
        out_specs=pl.BlockSpec((1, tm, d), row),
        out_shape=jax.ShapeDtypeStruct((b, s, d), F32),
        scratch_shapes=[pltpu.VMEM((tm * ROW_TILE, LANES), F32), pltpu.VMEM((tm * ROW_TILE, LANES), F32),
                        pltpu.SemaphoreType.DMA((2,))],
        compiler_params=_cp(2),
        name="moe_combine_final" if final else "moe_combine",
    )(dest0.reshape(-1, 1, tm), dest1.reshape(-1, 1, tm), x, ys, rw, g2.reshape(b, 1, d),
      final_g.reshape(1, d))


def _pad_heads(w, n_heads, offsets):
    d = w.shape[0]
    w = w.reshape(d, n_heads, HEAD_DIM)
    z = jnp.zeros((d, n_heads, HEAD_DIM), w.dtype)
    off = jnp.asarray(offsets, jnp.int32).reshape(1, n_heads, 1)
    blk = jnp.where(off == 0, jnp.concatenate([w, z], axis=-1), jnp.concatenate([z, w], axis=-1))
    return blk.reshape(d, n_heads * LANES)


def _layout_w_in(w_in):
    d = w_in.shape[0]
    kvw = NSA_KV * HEAD_DIM
    sizes = (NSA_HEADS * HEAD_DIM, kvw, kvw, kvw, kvw, kvw, kvw, NSA_HEADS * 3,
             FOX_HEADS * HEAD_DIM, FOX_HEADS * HEAD_DIM, FOX_HEADS * HEAD_DIM, FOX_HEADS,
             SB_HEADS * HEAD_DIM, SB_HEADS * HEAD_DIM, SB_HEADS * HEAD_DIM)
    pts = np.cumsum(sizes)[:-1].tolist()
    (qa, kca, vca, ksa, vsa, kwa, vwa, ga, qb, kb, vb, fb, qc, kc, vc) = jnp.split(w_in, pts, axis=1)
    scale = HEAD_DIM ** -0.5 * LOG2E
    qa_p = _pad_heads(qa * scale, NSA_HEADS, [0] * NSA_GROUP + [HEAD_DIM] * NSA_GROUP)
    qb_p = _pad_heads(qb * scale, FOX_HEADS, [0, HEAD_DIM, 0, HEAD_DIM])
    qc_p = _pad_heads(qc * scale, SB_HEADS, [0, HEAD_DIM, 0, HEAD_DIM])
    main = jnp.concatenate([qa_p, ksa, kwa, vsa, vwa, qb_p, kb, vb, qc_p, kc, vc], axis=1).astype(BF16)
    cmp_w = jnp.concatenate([kca, vca], axis=1).astype(BF16)
    zpad = lambda n: jnp.zeros((d, n), w_in.dtype)
    per_grp = NSA_GROUP * 3
    small = jnp.concatenate([ga[:, :per_grp], zpad(LANES - per_grp), ga[:, per_grp:], zpad(LANES - per_grp),
                             fb[:, 0:2], zpad(LANES - 2), fb[:, 2:4], zpad(LANES - 2)], axis=1).astype(BF16)
    return main, cmp_w, small


def _layout_cmp(w1, w2):
    hid = w1.shape[1]
    w1r = w1.reshape(2, CMP_STRIDE, HEAD_DIM, hid)
    z = jnp.zeros((CMP_STRIDE, HEAD_DIM, hid), w1.dtype)
    cols = []
    for half in range(2):
        for g in range(NSA_KV):
            parts = [w1r[half] if gg == g else z for gg in range(NSA_KV)]
            cols.append(jnp.concatenate(parts, axis=1).reshape(CMP_STRIDE * LANES, hid))
    wcat = jnp.concatenate(cols, axis=1).astype(BF16)
    zz = jnp.zeros_like(w2)
    w2bd = jnp.concatenate([jnp.concatenate([w2, zz], axis=1),
                            jnp.concatenate([zz, w2], axis=1)], axis=0).astype(BF16)
    return wcat, w2bd


def _rope_tables(pos):
    inv = jnp.exp(jnp.arange(ROPE_HALF, dtype=F32) * (-2.0 * math.log(ROPE_THETA) / ROPE_DIM))
    ang = pos.astype(F32)[:, None] * inv[None, :]
    cos, sin = jnp.cos(ang), jnp.sin(ang)
    n = pos.shape[0]
    z8 = jnp.zeros((n, ROPE_HALF), F32)
    rest1 = jnp.ones((n, HEAD_DIM - ROPE_DIM), F32)
    rest0 = jnp.zeros((n, HEAD_DIM - ROPE_DIM), F32)
    c = jnp.concatenate([cos, cos, rest1], axis=1)
    s1 = jnp.concatenate([z8, sin, rest0], axis=1)
    s2 = jnp.concatenate([-sin, z8, rest0], axis=1)
    dup = lambda a: jnp.concatenate([a, a], axis=1)
    return dup(c), dup(s1), dup(s2)


def _static_tables(s):
    n_cmp_pad = s // CMP_STRIDE
    n = np.arange(n_cmp_pad)[:, None]
    j = np.arange(LANES)[None, :]
    n_sel = s // SEL_LEN
    cover = ((n * CMP_STRIDE < j * SEL_LEN + SEL_LEN) & (n * CMP_STRIDE + CMP_LEN > j * SEL_LEN)
             & (j < n_sel)).astype(np.float32)
    nt = s // TK_FOX
    key = np.arange(nt)[:, None, None] * TK_FOX + np.arange(TK_FOX)[None, None, :]
    expand = (key // SEL_LEN == np.arange(LANES)[None, :, None]).astype(np.float32)
    return jnp.asarray(cover, BF16), jnp.asarray(expand, BF16)


def _dispatch_plan(ri, t):
    eid = ri.reshape(t, LANES)[:, :2].reshape(-1)
    n_assign = eid.shape[0]
    onehot = (eid[:, None] == jnp.arange(N_EXPERTS, dtype=jnp.int32)[None, :]).astype(jnp.int32)
    csum = jnp.cumsum(onehot, axis=0)
    counts = csum[-1]
    rank = jnp.take_along_axis(csum, eid[:, None], axis=1)[:, 0] - 1
    padded = ((counts + TM_EXP - 1) // TM_EXP) * TM_EXP
    pends = jnp.cumsum(padded)
    pstarts = pends - padded
    dest = pstarts[eid] + rank
    n_tiles = -(-(n_assign + N_EXPERTS * (TM_EXP - 1)) // TM_EXP)
    tile_start = jnp.arange(n_tiles, dtype=jnp.int32) * TM_EXP
    tile_expert = jnp.minimum(jnp.sum((pends[None, :] <= tile_start[:, None]).astype(jnp.int32), axis=1),
                              N_EXPERTS - 1).astype(jnp.int32)
    n_used = (pends[-1] // TM_EXP).astype(jnp.int32).reshape(1)
    return dest.astype(jnp.int32), n_tiles * TM_EXP, tile_expert, n_used


def kernel(x, c, norm1_g, norm2_g, ada_w, ada_b, w_in, b_forget, cmp_pos_k, cmp_w1_k, cmp_w2_k,
           cmp_pos_v, cmp_w1_v, cmp_w2_v, out_norm_g, w_out, router_group_w, router_group_b,
           router_expert_w, router_expert_b, expert_w1, expert_w3, expert_w2, final_g):
    b, s, d = x.shape
    depth = ada_w.shape[0]
    t = b * s
    mod = _modulation(c, ada_w, ada_b)
    rope_c, rope_1, rope_2 = _rope_tables(jnp.arange(s))
    n_cmp_pad = s // CMP_STRIDE
    crc, cr1, cr2 = _rope_tables(jnp.arange(n_cmp_pad) * CMP_STRIDE + (CMP_LEN - 1))
    cover, expand = _static_tables(s)
    xs = None

    for l in range(depth):
        sh1, sc1, g1, sh2, sc2, g2 = [mod[l][:, i * d:(i + 1) * d] for i in range(6)]
        w_main, w_cmp, w_small = _layout_w_in(w_in[l])
        u, kc, vc, small = _in_projection(x, norm1_g[l], sc1, sh1, w_main, w_cmp, w_small,
                                          rope_c, rope_1, rope_2)
        bf = b_forget[l]
        zf = jnp.zeros((LANES - 2,), F32)
        b_pairs = jnp.stack([jnp.concatenate([bf[0:2], zf]), jnp.concatenate([bf[2:4], zf])]).reshape(2, 1, LANES)
        cq, ckt = _forget_cumsum(small, b_pairs)
        wk, w2k = _layout_cmp(cmp_w1_k[l], cmp_w2_k[l])
        wv, w2v = _layout_cmp(cmp_w1_v[l], cmp_w2_v[l])
        ck, cv = _compress(kc, vc, wk, wv, cmp_w1_k[l], cmp_w1_v[l],
                           jnp.broadcast_to(cmp_pos_k[l].reshape(1, -1), (8, CMP_LEN * HEAD_DIM)),
                           jnp.broadcast_to(cmp_pos_v[l].reshape(1, -1), (8, CMP_LEN * HEAD_DIM)),
                           w2k, w2v, crc, cr1, cr2)
        gn = out_norm_g[l].reshape(1, -1)
        o_a = _nsa_attention(u, ck, cv, small, gn[:, :NSA_HEADS * HEAD_DIM], cover, expand)
        o_b = _fox_attention(u, cq, ckt, gn[:, NSA_HEADS * HEAD_DIM:(NSA_HEADS + FOX_HEADS) * HEAD_DIM])
        o_c = _sb_attention(u, gn[:, (NSA_HEADS + FOX_HEADS) * HEAD_DIM:])

        wr = jnp.concatenate([router_group_w[l], router_expert_w[l],
                              jnp.zeros((d, LANES - N_GROUPS - N_EXPERTS), F32)], axis=1)
        wr_hi = wr.astype(BF16)
        wr_lo = (wr - wr_hi.astype(F32)).astype(BF16)
        br = jnp.concatenate([router_group_b[l], router_expert_b[l],
                              jnp.zeros((LANES - N_GROUPS - N_EXPERTS,), F32)]).reshape(1, LANES)
        x, h2, rw, ri = _out_projection(o_a, o_b, o_c, x, w_out[l].astype(BF16), g1, norm2_g[l],
                                        sc2, sh2, wr_hi, wr_lo, br)

        dest, p_rows, tile_expert, n_used = _dispatch_plan(ri, t)
        xs = _dispatch(h2, dest, p_rows, xs)
        ys = _expert_mlp(xs, tile_expert, n_used, expert_w1, expert_w3, expert_w2, l)
        x = _combine(x, ys, dest[0::2], dest[1::2], rw, g2, final_g, final=(l == depth - 1))
    return x
```

```python
import functools
import math

import numpy as np
import jax
import jax.numpy as jnp
from jax import lax
from jax.experimental import pallas as pl
from jax.experimental.pallas import tpu as pltpu

F32 = jnp.float32
BF16 = jnp.bfloat16

HEAD_DIM = 64
LANES = 128
N_HEADS = 16
NSA_HEADS = 8
NSA_KV = 2
NSA_GROUP = 4
FOX_HEADS = 4
SB_HEADS = 4
ROPE_DIM = 16
ROPE_HALF = 8
ROPE_THETA = 500000.0
CMP_LEN = 32
CMP_STRIDE = 16
CMP_HIDDEN = 128
SEL_LEN = 64
SEL_TOPN = 16
WINDOW = 512
FORCE_SCORE = 1.0e4
N_GROUPS = 4
EXPERTS_PER_GROUP = 8
N_EXPERTS = 32
EPS = 1e-6
LOG2E = math.log2(math.e)
NEG = -1e30

COL_QA = 0
COL_KS = 1024
COL_KW = 1152
ROPE_COLS = 1280
COL_VS = 1280
COL_VW = 1408
COL_FOX = 1536
COL_SB = 2560
N_MAIN = 3584
N_SMALL = 512
PROJ_CHUNK = 1280

TM_PROJ = 512
TQ_NSA = 128
TK_ATT = 256
TQ_PAIR = 256
TK_FOX = 512
ROWS = 32
SB_SUBS = 2
TM_OUT = 512
TM_EXP = 512
TM_CMB = 512
COPY_CHUNK = 512
ROW_TILE = 8
VMEM_LIMIT = 56 * 1024 * 1024


def _cp(n_axes, vmem=VMEM_LIMIT):
    return pltpu.CompilerParams(dimension_semantics=("arbitrary",) * n_axes, vmem_limit_bytes=vmem)


def _dot(a, b):
    return jnp.dot(a, b, preferred_element_type=F32)


def _dot_nt(a, b):
    return lax.dot_general(a, b, (((1,), (1,)), ((), ())), preferred_element_type=F32)


def _split_bf16(x, parts):
    out = []
    r = x
    for _ in range(parts):
        p = r.astype(BF16)
        out.append(p)
        r = r - p.astype(F32)
    return out


def _dot_split(x, m, parts):
    acc = None
    for p in _split_bf16(x, parts):
        d = _dot(p, m)
        acc = d if acc is None else acc + d
    return acc


def _rope(x, c, s1, s2):
    return x * c + pltpu.roll(x, ROPE_HALF, 1) * s1 + pltpu.roll(x, LANES - ROPE_HALF, 1) * s2


def _softplus(z):
    return jnp.maximum(z, 0.0) + jnp.log(1.0 + jnp.exp(-jnp.abs(z)))


def _mod_kernel(c_ref, w_ref, b_ref, o_ref):
    c = c_ref[...]
    cond = c * (1.0 / (1.0 + jnp.exp(-c)))
    o_ref[0] = _dot(cond, w_ref[0]) + b_ref[0]


def _modulation(c, ada_w, ada_b):
    depth, d, n = ada_w.shape
    b = c.shape[0]
    tn = 1024
    return pl.pallas_call(
        _mod_kernel,
        grid=(depth, n // tn),
        in_specs=[pl.BlockSpec((b, d), lambda l, j: (0, 0)),
                  pl.BlockSpec((1, d, tn), lambda l, j: (l, 0, j)),
                  pl.BlockSpec((1, 1, tn), lambda l, j: (l, 0, j))],
        out_specs=pl.BlockSpec((1, b, tn), lambda l, j: (l, 0, j)),
        out_shape=jax.ShapeDtypeStruct((depth, b, n), F32),
        compiler_params=_cp(2),
        name="modulation",
    )(c, ada_w, ada_b.reshape(depth, 1, n))


def _inproj_kernel(x_ref, g_ref, sc_ref, sh_ref, w_ref, wc_ref, ws_ref, rc_ref, r1_ref, r2_ref,
                   u_ref, kc_ref, vc_ref, sm_ref):
    x = x_ref[0]
    ms = jnp.mean(x * x, axis=-1, keepdims=True)
    h = (x * lax.rsqrt(ms + EPS) * g_ref[...]) * (1.0 + sc_ref[0]) + sh_ref[0]
    hb = h.astype(BF16)
    rc, r1, r2 = rc_ref[...], r1_ref[...], r2_ref[...]
    for j in range(N_MAIN // PROJ_CHUNK + (1 if N_MAIN % PROJ_CHUNK else 0)):
        lo = j * PROJ_CHUNK
        hi = min(lo + PROJ_CHUNK, N_MAIN)
        acc = _dot(hb, w_ref[:, lo:hi])
        if lo < ROPE_COLS:
            for k in range((hi - lo) // LANES):
                blk = acc[:, k * LANES:(k + 1) * LANES]
                u_ref[0, :, lo + k * LANES:lo + (k + 1) * LANES] = _rope(blk, rc, r1, r2).astype(BF16)
        else:
            u_ref[0, :, lo:hi] = acc.astype(BF16)
    cmp_in = _dot(hb, wc_ref[...])
    kc_ref[0] = cmp_in[:, :LANES].astype(BF16)
    vc_ref[0] = cmp_in[:, LANES:].astype(BF16)
    sm_ref[0] = _dot(hb, ws_ref[...])


def _in_projection(x, g, sc, sh, w_main, w_cmp, w_small, rope_c, rope_1, rope_2):
    b, s, d = x.shape
    tm = min(TM_PROJ, s)
    row = lambda i, j: (i, j, 0)
    const2 = lambda i, j: (0, 0)
    per_b = lambda i, j: (i, 0, 0)
    seq = lambda i, j: (j, 0)
    return pl.pallas_call(
        _inproj_kernel,
        grid=(b, s // tm),
        in_specs=[pl.BlockSpec((1, tm, d), row),
                  pl.BlockSpec((1, d), const2),
                  pl.BlockSpec((1, 1, d), per_b),
                  pl.BlockSpec((1, 1, d), per_b),
                  pl.BlockSpec((d, N_MAIN), const2),
                  pl.BlockSpec((d, 2 * LANES), const2),
                  pl.BlockSpec((d, N_SMALL), const2),
                  pl.BlockSpec((tm, LANES), seq),
                  pl.BlockSpec((tm, LANES), seq),
                  pl.BlockSpec((tm, LANES), seq)],
        out_specs=[pl.BlockSpec((1, tm, N_MAIN), row),
                   pl.BlockSpec((1, tm, LANES), row),
                   pl.BlockSpec((1, tm, LANES), row),
                   pl.BlockSpec((1, tm, N_SMALL), row)],
        out_shape=[jax.ShapeDtypeStruct((b, s, N_MAIN), BF16),
                   jax.ShapeDtypeStruct((b, s, LANES), BF16),
                   jax.ShapeDtypeStruct((b, s, LANES), BF16),
                   jax.ShapeDtypeStruct((b, s, N_SMALL), F32)],
        compiler_params=_cp(2),
        name="in_projection",
    )(x, g.reshape(1, d), sc.reshape(b, 1, d), sh.reshape(b, 1, d), w_main, w_cmp, w_small,
      rope_c, rope_1, rope_2)


def _cumf_kernel(f_ref, b_ref, cq_ref, ckt_ref):
    n_chunks = f_ref.shape[1] // TK_ATT
    r = lax.broadcasted_iota(jnp.int32, (TK_ATT, TK_ATT), 0)
    c = lax.broadcasted_iota(jnp.int32, (TK_ATT, TK_ATT), 1)
    tri = jnp.where(c <= r, 1.0, 0.0).astype(BF16)
    carry = jnp.zeros((1, LANES), F32)
    for j in range(n_chunks):
        f = f_ref[0, j * TK_ATT:(j + 1) * TK_ATT, :] + b_ref[0]
        ls = -_softplus(-f)
        acc = None
        for p in _split_bf16(ls, 3):
            dd = _dot(tri, p)
            acc = dd if acc is None else acc + dd
        cs = acc + carry
        cs2 = cs * LOG2E
        cq_ref[0, j * TK_ATT:(j + 1) * TK_ATT, :] = cs2
        ckt_ref[0, 0, j] = cs2.T[:8, :]
        carry = cs[TK_ATT - 1:TK_ATT, :]


def _forget_cumsum(small, b_pairs):
    b, s, _ = small.shape
    return pl.pallas_call(
        _cumf_kernel,
        grid=(b, 2),
        in_specs=[pl.BlockSpec((1, s, LANES), lambda i, p: (i, 0, 2 + p)),
                  pl.BlockSpec((1, 1, LANES), lambda i, p: (p, 0, 0))],
        out_specs=[pl.BlockSpec((1, s, LANES), lambda i, p: (i, 0, p)),
                   pl.BlockSpec((1, 1, s // TK_ATT, 8, TK_ATT), lambda i, p: (i, p, 0, 0, 0))],
        out_shape=[jax.ShapeDtypeStruct((b, s, 2 * LANES), F32),
                   jax.ShapeDtypeStruct((b, 2, s // TK_ATT, 8, TK_ATT), F32)],
        compiler_params=_cp(2),
        name="forget_cumsum",
    )(small, b_pairs)


def _compress_kernel(ks_ref, vs_ref, wk_ref, wv_ref, w1k_ref, w1v_ref, pek_ref, pev_ref,
                     w2k_ref, w2v_ref, rc_ref, r1_ref, r2_ref, ck_ref, cv_ref):
    def one(seg_ref, w_ref, w1_ref, pe_ref, w2_ref):
        p = _dot(seg_ref[0], w_ref[...])
        half = 2 * CMP_HIDDEN
        bias = _dot(pe_ref[...].astype(BF16), w1_ref[...].astype(BF16))[0:1, :]
        bias2 = jnp.concatenate([bias, bias], axis=1)
        n = p.shape[0]
        hid = p[:, :half] + pltpu.roll(p[:, half:], n - 1, 0) + bias2
        act = hid * (1.0 / (1.0 + jnp.exp(-hid)))
        return _dot(act.astype(BF16), w2_ref[...])

    ck = one(ks_ref, wk_ref, w1k_ref, pek_ref, w2k_ref)
    ck_ref[0] = _rope(ck, rc_ref[...], r1_ref[...], r2_ref[...]).astype(BF16)
    cv_ref[0] = one(vs_ref, wv_ref, w1v_ref, pev_ref, w2v_ref).astype(BF16)


def _compress(kc, vc, wk, wv, w1k, w1v, pek, pev, w2k, w2v, rc, r1, r2):
    b, s, _ = kc.shape
    n = s // CMP_STRIDE
    width = CMP_STRIDE * LANES
    kseg = kc.reshape(b, n, width)
    vseg = vc.reshape(b, n, width)
    seg = pl.BlockSpec((1, n, width), lambda i: (i, 0, 0))
    full = lambda a: pl.BlockSpec(a.shape, lambda i: (0,) * a.ndim)
    return pl.pallas_call(
        _compress_kernel,
        grid=(b,),
        in_specs=[seg, seg, full(wk), full(wv), full(w1k), full(w1v), full(pek), full(pev),
                  full(w2k), full(w2v), full(rc), full(r1), full(r2)],
        out_specs=[pl.BlockSpec((1, n, LANES), lambda i: (i, 0, 0)),
                   pl.BlockSpec((1, n, LANES), lambda i: (i, 0, 0))],
        out_shape=[jax.ShapeDtypeStruct((b, n, LANES), BF16),
                   jax.ShapeDtypeStruct((b, n, LANES), BF16)],
        compiler_params=_cp(1),
        name="nsa_compress",
    )(kseg, vseg, wk, wv, w1k, w1v, pek, pev, w2k, w2v, rc, r1, r2)


def _nsa_kernel(q_ref, ck_ref, cv_ref, ks_ref, vs_ref, kw_ref, vw_ref, gate_ref, gn_ref,
                cover_ref, expand_ref, o_ref, m_ref, l_ref, acc_ref):
    tq = q_ref.shape[1]
    g = pl.program_id(1)
    t0 = pl.program_id(2) * tq
    q = q_ref[0]
    qall = jnp.concatenate([q[:, h * LANES:(h + 1) * LANES] for h in range(NSA_GROUP)], axis=0)
    tpos = t0 + lax.broadcasted_iota(jnp.int32, (tq, 1), 0)
    lane = lax.broadcasted_iota(jnp.int32, (tq, LANES), 1)

    s = _dot_nt(qall, ck_ref[0])
    cmp_end = (CMP_STRIDE * lax.broadcasted_iota(jnp.int32, (ROWS, LANES), 1) + (CMP_LEN - 1)
               - lax.broadcasted_iota(jnp.int32, (ROWS, LANES), 0))
    n_chunks = tq // ROWS
    p_rows, p_sum = [], [None] * n_chunks
    for h in range(NSA_GROUP):
        for c in range(n_chunks):
            r0 = h * tq + c * ROWS
            valid = cmp_end <= t0 + c * ROWS
            sm = jnp.where(valid, s[r0:r0 + ROWS, :], NEG)
            e = jnp.where(valid, jnp.exp2(sm - jnp.max(sm, axis=-1, keepdims=True)), 0.0)
            den = jnp.sum(e, axis=-1, keepdims=True)
            p = e * jnp.where(den > 0.0, 1.0 / den, 0.0)
            p_rows.append(p.astype(BF16))
            p_sum[c] = p if p_sum[c] is None else p_sum[c] + p
    o_cmp = _dot(jnp.concatenate(p_rows, axis=0), cv_ref[0]).reshape(NSA_GROUP, tq, LANES)

    n_sel = expand_ref.shape[0] * (TK_FOX // SEL_LEN)
    imp_t = _dot_split(jnp.concatenate(p_sum, axis=0), cover_ref[...], 3).T[:n_sel, :]
    blk = lax.broadcasted_iota(jnp.int32, (n_sel, tq), 0)
    tcol = t0 + lax.broadcasted_iota(jnp.int32, (n_sel, tq), 1)
    cur = jnp.right_shift(tcol, int(math.log2(SEL_LEN)))
    forced = (blk == 0) | (blk == cur) | (blk == cur - 1)
    score = jnp.where(forced, FORCE_SCORE, jnp.where(blk * SEL_LEN <= tcol, imp_t, -1.0))
    cnt = jnp.zeros((n_sel, tq), F32)
    for j in range(n_sel):
        sj = score[j:j + 1, :]
        beats = (sj > score) | ((sj == score) & (blk > j))
        cnt = cnt + jnp.where(beats, 1.0, 0.0)
    sel_t = jnp.where(cnt < float(min(SEL_TOPN, n_sel)), 1.0, 0.0)
    sel = jnp.concatenate([sel_t, jnp.zeros((LANES - n_sel, tq), F32)], axis=0).T.astype(BF16)

    def biased(bias):
        def adjust(h, t_off, cols):
            return [cols[kk] + bias[t_off:t_off + ROWS, kk * LANES:(kk + 1) * LANES]
                    for kk in range(len(cols))]
        return adjust

    kcol_s = lax.broadcasted_iota(jnp.int32, (tq, TK_FOX), 1)

    def sel_tile(kt, carry):
        k0 = pl.multiple_of(kt * TK_FOX, TK_FOX)
        hit = _dot(sel, expand_ref[kt])
        ok = (hit > 0.5) & ((k0 + kcol_s) <= tpos)
        s_t = _dot_nt(qall, ks_ref[0, pl.ds(k0, TK_FOX), :])
        _softmax_tile(s_t, vs_ref[0, pl.ds(k0, TK_FOX), :], m_ref, l_ref, acc_ref, NSA_GROUP, tq,
                      biased(jnp.where(ok, 0.0, NEG)))
        return carry

    _softmax_init(m_ref, l_ref, acc_ref)
    lax.fori_loop(0, lax.div(t0 + tq - 1, TK_FOX) + 1, sel_tile, 0)
    o_sel = _softmax_result(l_ref, acc_ref).reshape(NSA_GROUP, tq, LANES)

    span = WINDOW + tq
    w0 = pl.multiple_of(jnp.maximum(t0 - WINDOW, 0), LANES)
    kp = w0 + lax.broadcasted_iota(jnp.int32, (tq, span), 1)
    bias_w = jnp.where((kp <= tpos) & (kp > tpos - WINDOW), 0.0, NEG)
    s_w = _dot_nt(qall, kw_ref[0, pl.ds(w0, span), :])
    nkw = span // LANES
    p_rows, inv_rows = [], []
    for h in range(NSA_GROUP):
        for c in range(tq // ROWS):
            r0 = h * tq + c * ROWS
            cols = [s_w[r0:r0 + ROWS, kk * LANES:(kk + 1) * LANES]
                    + bias_w[c * ROWS:(c + 1) * ROWS, kk * LANES:(kk + 1) * LANES] for kk in range(nkw)]
            mx = cols[0]
            for kk in range(1, nkw):
                mx = jnp.maximum(mx, cols[kk])
            mx = jnp.max(mx, axis=-1, keepdims=True)
            pks = [jnp.exp2(cols[kk] - mx) for kk in range(nkw)]
            psum = pks[0]
            for kk in range(1, nkw):
                psum = psum + pks[kk]
            inv_rows.append(jnp.broadcast_to(1.0 / jnp.sum(psum, axis=-1, keepdims=True), (ROWS, LANES)))
            p_rows.append(jnp.concatenate([pk.astype(BF16) for pk in pks], axis=1))
    o_win = _dot(jnp.concatenate(p_rows, axis=0), vw_ref[0, pl.ds(w0, span), :])
    o_win = (o_win * jnp.concatenate(inv_rows, axis=0)).reshape(NSA_GROUP, tq, LANES)

    gt = gate_ref[0]
    gt = 1.0 / (1.0 + jnp.exp(-gt))
    mine = (lane >= g * HEAD_DIM) & (lane < (g + 1) * HEAD_DIM)
    outs = []
    for h in range(NSA_GROUP):
        o = (gt[:, 3 * h:3 * h + 1] * o_cmp[h] + gt[:, 3 * h + 1:3 * h + 2] * o_sel[h]
             + gt[:, 3 * h + 2:3 * h + 3] * o_win[h])
        o = jnp.where(mine, o, 0.0)
        ms = jnp.sum(o * o, axis=-1, keepdims=True) * (1.0 / HEAD_DIM)
        o = o * lax.rsqrt(ms + EPS)
        outs.append(o + pltpu.roll(o, HEAD_DIM, 1))
    left = lane < HEAD_DIM
    o_ref[0, :, :LANES] = (jnp.where(left, outs[0], outs[1]) * gn_ref[:, :LANES]).astype(o_ref.dtype)
    o_ref[0, :, LANES:] = (jnp.where(left, outs[2], outs[3]) * gn_ref[:, LANES:]).astype(o_ref.dtype)


def _nsa_attention(u, ck, cv, small, gn, cover, expand):
    b, s, _ = u.shape
    tq = min(TQ_NSA, s)
    assert s >= WINDOW + tq and WINDOW % tq == 0
    n_cmp = ck.shape[1]
    blk = LANES
    kv = lambda col: pl.BlockSpec((1, s, LANES), lambda i, g, j, col=col: (i, 0, col // blk))
    return pl.pallas_call(
        _nsa_kernel,
        grid=(b, NSA_KV, s // tq),
        in_specs=[pl.BlockSpec((1, tq, NSA_GROUP * LANES), lambda i, g, j: (i, j, g)),
                  pl.BlockSpec((1, n_cmp, LANES), lambda i, g, j: (i, 0, 0)),
                  pl.BlockSpec((1, n_cmp, LANES), lambda i, g, j: (i, 0, 0)),
                  kv(COL_KS), kv(COL_VS), kv(COL_KW), kv(COL_VW),
                  pl.BlockSpec((1, tq, LANES), lambda i, g, j: (i, j, g)),
                  pl.BlockSpec((1, 2 * LANES), lambda i, g, j: (0, g)),
                  pl.BlockSpec(cover.shape, lambda i, g, j: (0, 0)),
                  pl.BlockSpec(expand.shape, lambda i, g, j: (0, 0, 0))],
        out_specs=pl.BlockSpec((1, tq, 2 * LANES), lambda i, g, j: (i, j, g)),
        out_shape=jax.ShapeDtypeStruct((b, s, NSA_HEADS * HEAD_DIM), BF16),
        scratch_shapes=_softmax_scratch(NSA_GROUP * tq),
        compiler_params=_cp(3),
        name="nsa_attention",
    )(u, ck, cv, u, u, u, u, small, gn, cover, expand)


def _pair_finish(acc, gn_ref, o_ref, tq):
    lane = lax.broadcasted_iota(jnp.int32, (tq, LANES), 1)
    left = lane < HEAD_DIM
    o = jnp.where(left, acc[0], acc[1])
    o2 = o * o
    ms_l = jnp.sum(jnp.where(left, o2, 0.0), axis=-1, keepdims=True) * (1.0 / HEAD_DIM)
    ms_r = jnp.sum(jnp.where(left, 0.0, o2), axis=-1, keepdims=True) * (1.0 / HEAD_DIM)
    inv = jnp.where(left, lax.rsqrt(ms_l + EPS), lax.rsqrt(ms_r + EPS))
    o_ref[0] = (o * inv * gn_ref[...]).astype(o_ref.dtype)


def _softmax_tile(s, v, m_ref, l_ref, acc_ref, heads, tq, adjust):
    nk = s.shape[1] // LANES
    p_rows = []
    for h in range(heads):
        for c in range(tq // ROWS):
            t_off = c * ROWS
            r0 = h * tq + t_off
            cols = [s[r0:r0 + ROWS, k * LANES:(k + 1) * LANES] for k in range(nk)]
            cols = adjust(h, t_off, cols)
            mx = cols[0]
            for k in range(1, nk):
                mx = jnp.maximum(mx, cols[k])
            m_old = m_ref[r0:r0 + ROWS, :]
            m_new = jnp.maximum(m_old, jnp.max(mx, axis=-1, keepdims=True))
            alpha = jnp.exp2(m_old - m_new)
            pks = [jnp.exp2(cols[k] - m_new) for k in range(nk)]
            psum = pks[0]
            for k in range(1, nk):
                psum = psum + pks[k]
            p_rows.append(jnp.concatenate([pk.astype(BF16) for pk in pks], axis=1))
            l_ref[r0:r0 + ROWS, :] = alpha * l_ref[r0:r0 + ROWS, :] + psum
            acc_ref[r0:r0 + ROWS, :] = alpha * acc_ref[r0:r0 + ROWS, :]
            m_ref[r0:r0 + ROWS, :] = m_new
    acc_ref[...] += _dot(jnp.concatenate(p_rows, axis=0), v)


def _tile_loop(score_fn, process_fn, lo, hi):
    def body(kt, carry):
        process_fn(kt, score_fn(kt))
        return carry

    lax.fori_loop(lo, hi - 1, body, 0)
    return score_fn(hi - 1)


def _softmax_init(m_ref, l_ref, acc_ref):
    m_ref[...] = jnp.full(m_ref.shape, NEG, F32)
    l_ref[...] = jnp.zeros(l_ref.shape, F32)
    acc_ref[...] = jnp.zeros(acc_ref.shape, F32)


def _softmax_result(l_ref, acc_ref):
    return acc_ref[...] / jnp.sum(l_ref[...], axis=-1, keepdims=True)


def _softmax_scratch(rows):
    return [pltpu.VMEM((rows, LANES), F32), pltpu.VMEM((rows, LANES), F32), pltpu.VMEM((rows, LANES), F32)]


def _fox_kernel(q_ref, k_ref, v_ref, cq_ref, ckt_ref, gn_ref, o_ref,
                m_ref, l_ref, acc_ref, cqr_ref):
    tq = q_ref.shape[1]
    tk = TK_FOX
    t0 = pl.program_id(2) * tq
    q = q_ref[0]
    qall = jnp.concatenate([q[:, :LANES], q[:, LANES:]], axis=0)
    cq = cq_ref[0]
    cqr_ref[0:tq, :] = jnp.broadcast_to(cq[:, 0:1], (tq, LANES))
    cqr_ref[tq:2 * tq, :] = jnp.broadcast_to(cq[:, 1:2], (tq, LANES))
    _softmax_init(m_ref, l_ref, acc_ref)
    n_tiles = lax.div(t0 + tq - 1, tk) + 1
    diag = (lax.broadcasted_iota(jnp.int32, (ROWS, LANES), 1)
            - lax.broadcasted_iota(jnp.int32, (ROWS, LANES), 0))

    def scores(kt):
        return _dot_nt(qall, k_ref[0, pl.ds(pl.multiple_of(kt * tk, tk), tk), :])

    def tile(kt, s, masked):
        k0 = pl.multiple_of(kt * tk, tk)
        cks = [ckt_ref[0, 0, kt * (tk // TK_ATT) + j] for j in range(tk // TK_ATT)]

        def adjust(h, t_off, cols):
            out = []
            cqr = cqr_ref[h * tq + t_off:h * tq + t_off + ROWS, :]
            for kk in range(tk // LANES):
                lo = (kk * LANES) % TK_ATT
                ck = cks[(kk * LANES) // TK_ATT][h:h + 1, lo:lo + LANES]
                val = (cols[kk] - ck) + cqr
                if masked:
                    val = jnp.where(diag <= (t0 + t_off) - (k0 + kk * LANES), val, NEG)
                out.append(val)
            return out

        _softmax_tile(s, v_ref[0, pl.ds(k0, tk), :], m_ref, l_ref, acc_ref, 2, tq, adjust)

    s_last = _tile_loop(scores, lambda kt, s: tile(kt, s, False), 0, n_tiles)
    tile(n_tiles - 1, s_last, True)
    acc = _softmax_result(l_ref, acc_ref).reshape(2, tq, LANES)
    _pair_finish(acc, gn_ref, o_ref, tq)


def _fox_attention(u, cq, ckt, gn):
    b, s, _ = u.shape
    tq = min(TQ_PAIR, s)
    qb = COL_FOX // (2 * LANES)
    kb = (COL_FOX + FOX_HEADS * LANES) // LANES
    vb = kb + 2
    return pl.pallas_call(
        _fox_kernel,
        grid=(b, 2, s // tq),
        in_specs=[pl.BlockSpec((1, tq, 2 * LANES), lambda i, p, j: (i, j, qb + p)),
                  pl.BlockSpec((1, s, LANES), lambda i, p, j: (i, 0, kb + p)),
                  pl.BlockSpec((1, s, LANES), lambda i, p, j: (i, 0, vb + p)),
                  pl.BlockSpec((1, tq, LANES), lambda i, p, j: (i, j, p)),
                  pl.BlockSpec((1, 1, s // TK_ATT, 8, TK_ATT), lambda i, p, j: (i, p, 0, 0, 0)),
                  pl.BlockSpec((1, LANES), lambda i, p, j: (0, p))],
        out_specs=pl.BlockSpec((1, tq, LANES), lambda i, p, j: (i, j, p)),
        out_shape=jax.ShapeDtypeStruct((b, s, FOX_HEADS * HEAD_DIM), BF16),
        scratch_shapes=_softmax_scratch(2 * tq) + [pltpu.VMEM((2 * tq, LANES), F32)],
        compiler_params=_cp(3),
        name="fox_attention",
    )(u, u, u, cq, ckt, gn)


def _sb_kernel(q_ref, k_ref, v_ref, gn_ref, o_ref, rest_ref, acc_ref):
    tq = q_ref.shape[1]
    sub = TK_ATT
    tk = SB_SUBS * sub
    nks = sub // LANES
    rows = 2 * tq
    t0 = pl.program_id(2) * tq
    r = lax.broadcasted_iota(jnp.int32, (sub, sub), 0)
    c = lax.broadcasted_iota(jnp.int32, (sub, sub), 1)
    upper = jnp.where(r >= c, 1.0, 0.0).astype(BF16)
    n_tiles = lax.div(t0 + tq - 1, tk) + 1
    diag = (lax.broadcasted_iota(jnp.int32, (ROWS, LANES), 1)
            - lax.broadcasted_iota(jnp.int32, (ROWS, LANES), 0))
    rest_ref[...] = jnp.zeros(rest_ref.shape, F32)
    acc_ref[...] = jnp.zeros(acc_ref.shape, F32)
    hi_mask = jnp.uint32(0xFFFF0000)
    q = q_ref[0]
    qall = jnp.concatenate([q[:, :LANES], q[:, LANES:]], axis=0)
    chunks = [(h * tq + cc * ROWS, cc * ROWS) for h in range(2) for cc in range(tq // ROWS)]

    def tile(kt, masked):
        k0 = pl.multiple_of(kt * tk, tk)
        k = k_ref[0, pl.ds(k0, tk), :]
        v = v_ref[0, pl.ds(k0, tk), :]

        def strictly_before(t_off, col):
            return diag < (t0 + t_off) - (k0 + col)

        z = _dot_nt(qall, k)
        his = [[] for _ in range(SB_SUBS)]
        los = [[] for _ in range(SB_SUBS)]
        for r0, t_off in chunks:
            for sb in range(SB_SUBS):
                hi_c, lo_c = [], []
                for kk in range(nks):
                    col = sb * sub + kk * LANES
                    zc = z[r0:r0 + ROWS, col:col + LANES]
                    l = -(jnp.maximum(zc, 0.0) + jnp.log2(1.0 + jnp.exp2(-jnp.abs(zc))))
                    if masked:
                        l = jnp.where(strictly_before(t_off, col), l, 0.0)
                    hi = pltpu.bitcast(pltpu.bitcast(l, jnp.uint32) & hi_mask, F32)
                    hi_c.append(hi.astype(BF16))
                    lo_c.append((l - hi).astype(BF16))
                his[sb].append(jnp.concatenate(hi_c, axis=1))
                los[sb].append(jnp.concatenate(lo_c, axis=1))
        hi_all = jnp.concatenate([x for sb in range(SB_SUBS) for x in his[sb]], axis=0)
        lo_all = jnp.concatenate([x for sb in range(SB_SUBS) for x in los[sb]], axis=0)
        cum = _dot(hi_all, upper) + _dot(lo_all, upper)

        a_rows = []
        for r0, t_off in chunks:
            base = rest_ref[r0:r0 + ROWS, :]
            a_c = [None] * (SB_SUBS * nks)
            for sb in reversed(range(SB_SUBS)):
                cs = cum[sb * rows + r0:sb * rows + r0 + ROWS, :]
                for kk in range(nks):
                    col = sb * sub + kk * LANES
                    a = jnp.exp2(z[r0:r0 + ROWS, col:col + LANES] + cs[:, kk * LANES:(kk + 1) * LANES] + base)
                    if masked:
                        a = jnp.where(strictly_before(t_off, col), a, 0.0)
                    a_c[sb * nks + kk] = a.astype(BF16)
                base = base + jnp.broadcast_to(cs[:, 0:1], (ROWS, LANES))
            a_rows.append(jnp.concatenate(a_c, axis=1))
            rest_ref[r0:r0 + ROWS, :] = base
        acc_ref[...] += _dot(jnp.concatenate(a_rows, axis=0), v)

    tile(n_tiles - 1, True)

    def full_tile(i, carry):
        tile(n_tiles - 2 - i, False)
        return carry

    lax.fori_loop(0, n_tiles - 1, full_tile, 0)
    _pair_finish(acc_ref[...].reshape(2, tq, LANES), gn_ref, o_ref, tq)


def _sb_attention(u, gn):
    b, s, _ = u.shape
    tq = min(TQ_PAIR, s)
    qb = COL_SB // (2 * LANES)
    kb = (COL_SB + SB_HEADS * LANES) // LANES
    vb = kb + 2
    return pl.pallas_call(
        _sb_kernel,
        grid=(b, 2, s // tq),
        in_specs=[pl.BlockSpec((1, tq, 2 * LANES), lambda i, p, j: (i, j, qb + p)),
                  pl.BlockSpec((1, s, LANES), lambda i, p, j: (i, 0, kb + p)),
                  pl.BlockSpec((1, s, LANES), lambda i, p, j: (i, 0, vb + p)),
                  pl.BlockSpec((1, LANES), lambda i, p, j: (0, p))],
        out_specs=pl.BlockSpec((1, tq, LANES), lambda i, p, j: (i, j, p)),
        out_shape=jax.ShapeDtypeStruct((b, s, SB_HEADS * HEAD_DIM), BF16),
        scratch_shapes=[pltpu.VMEM((2 * tq, LANES), F32), pltpu.VMEM((2 * tq, LANES), F32)],
        compiler_params=_cp(3),
        name="sb_attention",
    )(u, u, u, gn)


def _outproj_kernel(oa_ref, ob_ref, oc_ref, x_ref, w_ref, g1_ref, n2_ref, sc_ref, sh_ref,
                    wrh_ref, wrl_ref, br_ref, xo_ref, h_ref, rw_ref, ri_ref):
    na = oa_ref.shape[2]
    nb = ob_ref.shape[2]
    y = _dot(oa_ref[0], w_ref[0:na, :])
    y = y + _dot(ob_ref[0], w_ref[na:na + nb, :])
    y = y + _dot(oc_ref[0], w_ref[na + nb:, :])
    x = x_ref[0] + g1_ref[0] * y
    xo_ref[0] = x
    ms = jnp.mean(x * x, axis=-1, keepdims=True)
    h = (x * lax.rsqrt(ms + EPS) * n2_ref[...]) * (1.0 + sc_ref[0]) + sh_ref[0]
    hb = h.astype(BF16)
    _store_row_tiles(h_ref, h)
    hl = (h - hb.astype(F32)).astype(BF16)
    logit = _dot(hb, wrh_ref[...]) + _dot(hl, wrh_ref[...]) + _dot(hb, wrl_ref[...]) + br_ref[...]

    tm = logit.shape[0]
    lane = lax.broadcasted_iota(jnp.int32, (tm, LANES), 1).astype(F32)
    big = float(LANES)
    is_g = lane < N_GROUPS
    lg = jnp.where(is_g, logit, NEG)
    mg = jnp.max(lg, axis=-1, keepdims=True)
    zg = jnp.sum(jnp.where(is_g, jnp.exp(lg - mg), 0.0), axis=-1, keepdims=True)
    pg = 1.0 / zg
    gi = jnp.min(jnp.where(is_g & (lg == mg), lane, big), axis=-1, keepdims=True)
    e_lane = lane - N_GROUPS
    in_grp = (e_lane >= gi * EXPERTS_PER_GROUP) & (e_lane < (gi + 1) * EXPERTS_PER_GROUP)
    le = jnp.where(in_grp, logit, NEG)
    m1 = jnp.max(le, axis=-1, keepdims=True)
    i1 = jnp.min(jnp.where(in_grp & (le == m1), lane, big), axis=-1, keepdims=True)
    rest = in_grp & (lane != i1)
    le2 = jnp.where(rest, logit, NEG)
    m2 = jnp.max(le2, axis=-1, keepdims=True)
    i2 = jnp.min(jnp.where(rest & (le2 == m2), lane, big), axis=-1, keepdims=True)
    ze = jnp.sum(jnp.where(in_grp, jnp.exp(le - m1), 0.0), axis=-1, keepdims=True)
    p1 = 1.0 / ze
    p2 = jnp.exp(m2 - m1) / ze
    den = p1 + p2
    w1 = pg * (p1 / den)
    w2 = pg * (p2 / den)
    rw_ref[0] = jnp.where(lane == 0.0, w1, jnp.where(lane == 1.0, w2, 0.0))
    ri_ref[0] = jnp.where(lane == 0.0, i1 - N_GROUPS, jnp.where(lane == 1.0, i2 - N_GROUPS, 0.0)).astype(jnp.int32)


def _out_projection(oa, ob, oc, x, w_out, g1, n2, sc, sh, wr_hi, wr_lo, br):
    b, s, d = x.shape
    tm = min(TM_OUT, s)
    row = lambda i, j: (i, j, 0)
    const2 = lambda i, j: (0, 0)
    per_b = lambda i, j: (i, 0, 0)
    return pl.pallas_call(
        _outproj_kernel,
        grid=(b, s // tm),
        in_specs=[pl.BlockSpec((1, tm, oa.shape[2]), row),
                  pl.BlockSpec((1, tm, ob.shape[2]), row),
                  pl.BlockSpec((1, tm, oc.shape[2]), row),
                  pl.BlockSpec((1, tm, d), row),
                  pl.BlockSpec(w_out.shape, const2),
                  pl.BlockSpec((1, 1, d), per_b),
                  pl.BlockSpec((1, d), const2),
                  pl.BlockSpec((1, 1, d), per_b),
                  pl.BlockSpec((1, 1, d), per_b),
                  pl.BlockSpec((d, LANES), const2),
                  pl.BlockSpec((d, LANES), const2),
                  pl.BlockSpec((1, LANES), const2)],
        out_specs=[pl.BlockSpec((1, tm, d), row),
                   pl.BlockSpec((tm * ROW_TILE, LANES), lambda i, j: (i * (s // tm) + j, 0)),
                   pl.BlockSpec((1, tm, LANES), row),
                   pl.BlockSpec((1, tm, LANES), row)],
        out_shape=[jax.ShapeDtypeStruct((b, s, d), F32),
                   jax.ShapeDtypeStruct((b * s * ROW_TILE, LANES), F32),
                   jax.ShapeDtypeStruct((b, s, LANES), F32),
                   jax.ShapeDtypeStruct((b, s, LANES), jnp.int32)],
        compiler_params=_cp(2),
        name="out_projection",
    )(oa, ob, oc, x, w_out, g1.reshape(b, 1, d), n2.reshape(1, d), sc.reshape(b, 1, d),
      sh.reshape(b, 1, d), wr_hi, wr_lo, br)


def _store_row_tiles(ref, val):
    tm = val.shape[0]
    for c in range(ROW_TILE):
        ref[pl.ds(c, tm, stride=ROW_TILE), :] = val[:, c * LANES:(c + 1) * LANES]


def _load_row_tiles(ref, tm):
    return [ref[pl.ds(c, tm, stride=ROW_TILE), :] for c in range(ROW_TILE)]


def _tile_rows(ref, n):
    return ref.at[pl.ds(pl.multiple_of(n * ROW_TILE, ROW_TILE), ROW_TILE), :]


def _dispatch_kernel(idx_ref, src_ref, init_hbm, dst_hbm, sem):
    del init_hbm

    def issue(tok, carry):
        for k in range(2):
            pltpu.make_async_copy(_tile_rows(src_ref, tok), _tile_rows(dst_hbm, idx_ref[0, 0, 2 * tok + k]),
                                  sem).start(priority=k)
        return carry

    lax.fori_loop(0, COPY_CHUNK // 2, issue, 0, unroll=4)
    for _ in range(2):
        pltpu.make_async_copy(src_ref, dst_hbm.at[pl.ds(0, src_ref.shape[0]), :], sem).wait()


def _dispatch(src, idx, n_dst, init):
    n = idx.shape[0]
    if init is None:
        init = jnp.zeros((n_dst * ROW_TILE, LANES), src.dtype)
    return pl.pallas_call(
        _dispatch_kernel,
        grid=(n // COPY_CHUNK,),
        in_specs=[pl.BlockSpec((1, 1, COPY_CHUNK), lambda i: (i, 0, 0), memory_space=pltpu.SMEM),
                  pl.BlockSpec((COPY_CHUNK // 2 * ROW_TILE, LANES), lambda i: (i, 0)),
                  pl.BlockSpec(memory_space=pl.ANY)],
        out_specs=pl.BlockSpec(memory_space=pl.ANY),
        out_shape=jax.ShapeDtypeStruct((n_dst * ROW_TILE, LANES), src.dtype),
        scratch_shapes=[pltpu.SemaphoreType.DMA],
        input_output_aliases={2: 0},
        compiler_params=_cp(1),
        name="moe_dispatch",
    )(idx.reshape(n // COPY_CHUNK, 1, COPY_CHUNK), src, init)


def _expert_kernel(te_ref, nu_ref, x_ref, w1_ref, w3_ref, w2_ref, y_ref, w1b_ref, w3b_ref, w2b_ref):
    i = pl.program_id(0)

    @pl.when((i == 0) | (te_ref[i] != te_ref[jnp.maximum(i - 1, 0)]))
    def _():
        w1b_ref[...] = w1_ref[0].astype(BF16)
        w3b_ref[...] = w3_ref[0].astype(BF16)
        w2b_ref[...] = w2_ref[0].astype(BF16)

    @pl.when(i < nu_ref[0])
    def _():
        x = jnp.concatenate(_load_row_tiles(x_ref, TM_EXP), axis=1).astype(BF16)
        a = _dot(x, w1b_ref[...])
        g = _dot(x, w3b_ref[...])
        act = (a * (1.0 / (1.0 + jnp.exp(-a)))) * g
        _store_row_tiles(y_ref, _dot(act.astype(BF16), w2b_ref[...]))

    @pl.when(i >= nu_ref[0])
    def _():
        y_ref[...] = jnp.zeros_like(y_ref)


def _expert_mlp(xs, tile_expert, n_used, w1, w3, w2, layer):
    d, de = w1.shape[2], w1.shape[3]
    n_tiles = xs.shape[0] // (TM_EXP * ROW_TILE)
    grid_spec = pltpu.PrefetchScalarGridSpec(
        num_scalar_prefetch=2,
        grid=(n_tiles,),
        in_specs=[pl.BlockSpec((TM_EXP * ROW_TILE, LANES), lambda i, te, nu: (i, 0)),
                  pl.BlockSpec((None, 1, d, de), lambda i, te, nu: (layer, te[i], 0, 0)),
                  pl.BlockSpec((None, 1, d, de), lambda i, te, nu: (layer, te[i], 0, 0)),
                  pl.BlockSpec((None, 1, de, d), lambda i, te, nu: (layer, te[i], 0, 0))],
        out_specs=pl.BlockSpec((TM_EXP * ROW_TILE, LANES), lambda i, te, nu: (i, 0)),
        scratch_shapes=[pltpu.VMEM((d, de), BF16), pltpu.VMEM((d, de), BF16), pltpu.VMEM((de, d), BF16)],
    )
    return pl.pallas_call(
        _expert_kernel,
        grid_spec=grid_spec,
        out_shape=jax.ShapeDtypeStruct(xs.shape, F32),
        compiler_params=_cp(1),
        name="expert_mlp",
    )(tile_expert, n_used, xs, w1, w3, w2)


def _combine_kernel(d0_ref, d1_ref, x_ref, ys_hbm, rw_ref, g2_ref, fg_ref, o_ref, y0_ref, y1_ref, sem,
                    *, final):
    tm = x_ref.shape[1]

    def issue(r, carry):
        pltpu.make_async_copy(_tile_rows(ys_hbm, d0_ref[0, 0, r]), _tile_rows(y0_ref, r),
                              sem.at[0]).start(priority=0)
        pltpu.make_async_copy(_tile_rows(ys_hbm, d1_ref[0, 0, r]), _tile_rows(y1_ref, r),
                              sem.at[1]).start(priority=1)
        return carry

    lax.fori_loop(0, tm, issue, 0, unroll=8)
    pltpu.make_async_copy(ys_hbm.at[pl.ds(0, tm * ROW_TILE), :], y0_ref, sem.at[0]).wait()
    pltpu.make_async_copy(ys_hbm.at[pl.ds(0, tm * ROW_TILE), :], y1_ref, sem.at[1]).wait()

    rw = rw_ref[0]
    w0 = jnp.broadcast_to(rw[:, 0:1], (tm, LANES))
    w1 = jnp.broadcast_to(rw[:, 1:2], (tm, LANES))
    y0 = _load_row_tiles(y0_ref, tm)
    y1 = _load_row_tiles(y1_ref, tm)
    cols = []
    for c in range(ROW_TILE):
        sl = slice(c * LANES, (c + 1) * LANES)
        cols.append(x_ref[0, :, sl] + g2_ref[0, :, sl] * (y0[c] * w0 + y1[c] * w1))
    if final:
        ssq = cols[0] * cols[0]
        for c in range(1, ROW_TILE):
            ssq = ssq + cols[c] * cols[c]
        inv = lax.rsqrt(jnp.sum(ssq, axis=-1, keepdims=True) * (1.0 / (ROW_TILE * LANES)) + EPS)
        cols = [cols[c] * inv * fg_ref[:, c * LANES:(c + 1) * LANES] for c in range(ROW_TILE)]
    for c in range(ROW_TILE):
        o_ref[0, :, c * LANES:(c + 1) * LANES] = cols[c]


def _combine(x, ys, dest0, dest1, rw, g2, final_g, final):
    b, s, d = x.shape
    tm = min(TM_CMB, s)
    row = lambda i, j: (i, j, 0)
    idx_spec = pl.BlockSpec((1, 1, tm), lambda i, j: (i * (s // tm) + j, 0, 0), memory_space=pltpu.SMEM)
    return pl.pallas_call(
        functools.partial(_combine_kernel, final=final),
        grid=(b, s // tm),
        in_specs=[idx_spec, idx_spec,
                  pl.BlockSpec((1, tm, d), row),
                  pl.BlockSpec(memory_space=pl.ANY),
                  pl.BlockSpec((1, tm, LANES), row),
                  pl.BlockSpec((1, 1, d), lambda i, j: (i, 0, 0)),
                  pl.BlockSpec((1, d), lambda i, j: (0, 0))],
        out_specs=pl.BlockSpec((1, tm, d), row),
        out_shape=jax.ShapeDtypeStruct((b, s, d), F32),
        scratch_shapes=[pltpu.VMEM((tm * ROW_TILE, LANES), F32), pltpu.VMEM((tm * ROW_TILE, LANES), F32),
                        pltpu.SemaphoreType.DMA((2,))],
        compiler_params=_cp(2),
        name="moe_combine_final" if final else "moe_combine",
    )(dest0.reshape(-1, 1, tm), dest1.reshape(-1, 1, tm), x, ys, rw, g2.reshape(b, 1, d),
      final_g.reshape(1, d))


def _pad_heads(w, n_heads, offsets):
    d = w.shape[0]
    w = w.reshape(d, n_heads, HEAD_DIM)
    z = jnp.zeros((d, n_heads, HEAD_DIM), w.dtype)
    off = jnp.asarray(offsets, jnp.int32).reshape(1, n_heads, 1)
    blk = jnp.where(off == 0, jnp.concatenate([w, z], axis=-1), jnp.concatenate([z, w], axis=-1))
    return blk.reshape(d, n_heads * LANES)


def _layout_w_in(w_in):
    d = w_in.shape[0]
    kvw = NSA_KV * HEAD_DIM
    sizes = (NSA_HEADS * HEAD_DIM, kvw, kvw, kvw, kvw, kvw, kvw, NSA_HEADS * 3,
             FOX_HEADS * HEAD_DIM, FOX_HEADS * HEAD_DIM, FOX_HEADS * HEAD_DIM, FOX_HEADS,
             SB_HEADS * HEAD_DIM, SB_HEADS * HEAD_DIM, SB_HEADS * HEAD_DIM)
    pts = np.cumsum(sizes)[:-1].tolist()
    (qa, kca, vca, ksa, vsa, kwa, vwa, ga, qb, kb, vb, fb, qc, kc, vc) = jnp.split(w_in, pts, axis=1)
    scale = HEAD_DIM ** -0.5 * LOG2E
    qa_p = _pad_heads(qa * scale, NSA_HEADS, [0] * NSA_GROUP + [HEAD_DIM] * NSA_GROUP)
    qb_p = _pad_heads(qb * scale, FOX_HEADS, [0, HEAD_DIM, 0, HEAD_DIM])
    qc_p = _pad_heads(qc * scale, SB_HEADS, [0, HEAD_DIM, 0, HEAD_DIM])
    main = jnp.concatenate([qa_p, ksa, kwa, vsa, vwa, qb_p, kb, vb, qc_p, kc, vc], axis=1).astype(BF16)
    cmp_w = jnp.concatenate([kca, vca], axis=1).astype(BF16)
    zpad = lambda n: jnp.zeros((d, n), w_in.dtype)
    per_grp = NSA_GROUP * 3
    small = jnp.concatenate([ga[:, :per_grp], zpad(LANES - per_grp), ga[:, per_grp:], zpad(LANES - per_grp),
                             fb[:, 0:2], zpad(LANES - 2), fb[:, 2:4], zpad(LANES - 2)], axis=1).astype(BF16)
    return main, cmp_w, small


def _layout_cmp(w1, w2):
    hid = w1.shape[1]
    w1r = w1.reshape(2, CMP_STRIDE, HEAD_DIM, hid)
    z = jnp.zeros((CMP_STRIDE, HEAD_DIM, hid), w1.dtype)
    cols = []
    for half in range(2):
        for g in range(NSA_KV):
            parts = [w1r[half] if gg == g else z for gg in range(NSA_KV)]
            cols.append(jnp.concatenate(parts, axis=1).reshape(CMP_STRIDE * LANES, hid))
    wcat = jnp.concatenate(cols, axis=1).astype(BF16)
    zz = jnp.zeros_like(w2)
    w2bd = jnp.concatenate([jnp.concatenate([w2, zz], axis=1),
                            jnp.concatenate([zz, w2], axis=1)], axis=0).astype(BF16)
    return wcat, w2bd


def _rope_tables(pos):
    inv = jnp.exp(jnp.arange(ROPE_HALF, dtype=F32) * (-2.0 * math.log(ROPE_THETA) / ROPE_DIM))
    ang = pos.astype(F32)[:, None] * inv[None, :]
    cos, sin = jnp.cos(ang), jnp.sin(ang)
    n = pos.shape[0]
    z8 = jnp.zeros((n, ROPE_HALF), F32)
    rest1 = jnp.ones((n, HEAD_DIM - ROPE_DIM), F32)
    rest0 = jnp.zeros((n, HEAD_DIM - ROPE_DIM), F32)
    c = jnp.concatenate([cos, cos, rest1], axis=1)
    s1 = jnp.concatenate([z8, sin, rest0], axis=1)
    s2 = jnp.concatenate([-sin, z8, rest0], axis=1)
    dup = lambda a: jnp.concatenate([a, a], axis=1)
    return dup(c), dup(s1), dup(s2)


def _static_tables(s):
    n_cmp_pad = s // CMP_STRIDE
    n = np.arange(n_cmp_pad)[:, None]
    j = np.arange(LANES)[None, :]
    n_sel = s // SEL_LEN
    cover = ((n * CMP_STRIDE < j * SEL_LEN + SEL_LEN) & (n * CMP_STRIDE + CMP_LEN > j * SEL_LEN)
             & (j < n_sel)).astype(np.float32)
    nt = s // TK_FOX
    key = np.arange(nt)[:, None, None] * TK_FOX + np.arange(TK_FOX)[None, None, :]
    expand = (key // SEL_LEN == np.arange(LANES)[None, :, None]).astype(np.float32)
    return jnp.asarray(cover, BF16), jnp.asarray(expand, BF16)


def _dispatch_plan(ri, t):
    eid = ri.reshape(t, LANES)[:, :2].reshape(-1)
    n_assign = eid.shape[0]
    onehot = (eid[:, None] == jnp.arange(N_EXPERTS, dtype=jnp.int32)[None, :]).astype(jnp.int32)
    csum = jnp.cumsum(onehot, axis=0)
    counts = csum[-1]
    rank = jnp.take_along_axis(csum, eid[:, None], axis=1)[:, 0] - 1
    padded = ((counts + TM_EXP - 1) // TM_EXP) * TM_EXP
    pends = jnp.cumsum(padded)
    pstarts = pends - padded
    dest = pstarts[eid] + rank
    n_tiles = -(-(n_assign + N_EXPERTS * (TM_EXP - 1)) // TM_EXP)
    tile_start = jnp.arange(n_tiles, dtype=jnp.int32) * TM_EXP
    tile_expert = jnp.minimum(jnp.sum((pends[None, :] <= tile_start[:, None]).astype(jnp.int32), axis=1),
                              N_EXPERTS - 1).astype(jnp.int32)
    n_used = (pends[-1] // TM_EXP).astype(jnp.int32).reshape(1)
    return dest.astype(jnp.int32), n_tiles * TM_EXP, tile_expert, n_used


def kernel(x, c, norm1_g, norm2_g, ada_w, ada_b, w_in, b_forget, cmp_pos_k, cmp_w1_k, cmp_w2_k,
           cmp_pos_v, cmp_w1_v, cmp_w2_v, out_norm_g, w_out, router_group_w, router_group_b,
           router_expert_w, router_expert_b, expert_w1, expert_w3, expert_w2, final_g):
    b, s, d = x.shape
    depth = ada_w.shape[0]
    t = b * s
    mod = _modulation(c, ada_w, ada_b)
    rope_c, rope_1, rope_2 = _rope_tables(jnp.arange(s))
    n_cmp_pad = s // CMP_STRIDE
    crc, cr1, cr2 = _rope_tables(jnp.arange(n_cmp_pad) * CMP_STRIDE + (CMP_LEN - 1))
    cover, expand = _static_tables(s)
    xs = None

    for l in range(depth):
        sh1, sc1, g1, sh2, sc2, g2 = [mod[l][:, i * d:(i + 1) * d] for i in range(6)]
        w_main, w_cmp, w_small = _layout_w_in(w_in[l])
        u, kc, vc, small = _in_projection(x, norm1_g[l], sc1, sh1, w_main, w_cmp, w_small,
                                          rope_c, rope_1, rope_2)
        bf = b_forget[l]
        zf = jnp.zeros((LANES - 2,), F32)
        b_pairs = jnp.stack([jnp.concatenate([bf[0:2], zf]), jnp.concatenate([bf[2:4], zf])]).reshape(2, 1, LANES)
        cq, ckt = _forget_cumsum(small, b_pairs)
        wk, w2k = _layout_cmp(cmp_w1_k[l], cmp_w2_k[l])
        wv, w2v = _layout_cmp(cmp_w1_v[l], cmp_w2_v[l])
        ck, cv = _compress(kc, vc, wk, wv, cmp_w1_k[l], cmp_w1_v[l],
                           jnp.broadcast_to(cmp_pos_k[l].reshape(1, -1), (8, CMP_LEN * HEAD_DIM)),
                           jnp.broadcast_to(cmp_pos_v[l].reshape(1, -1), (8, CMP_LEN * HEAD_DIM)),
                           w2k, w2v, crc, cr1, cr2)
        gn = out_norm_g[l].reshape(1, -1)
        o_a = _nsa_attention(u, ck, cv, small, gn[:, :NSA_HEADS * HEAD_DIM], cover, expand)
        o_b = _fox_attention(u, cq, ckt, gn[:, NSA_HEADS * HEAD_DIM:(NSA_HEADS + FOX_HEADS) * HEAD_DIM])
        o_c = _sb_attention(u, gn[:, (NSA_HEADS + FOX_HEADS) * HEAD_DIM:])

        wr = jnp.concatenate([router_group_w[l], router_expert_w[l],
                              jnp.zeros((d, LANES - N_GROUPS - N_EXPERTS), F32)], axis=1)
        wr_hi = wr.astype(BF16)
        wr_lo = (wr - wr_hi.astype(F32)).astype(BF16)
        br = jnp.concatenate([router_group_b[l], router_expert_b[l],
                              jnp.zeros((LANES - N_GROUPS - N_EXPERTS,), F32)]).reshape(1, LANES)
        x, h2, rw, ri = _out_projection(o_a, o_b, o_c, x, w_out[l].astype(BF16), g1, norm2_g[l],
                                        sc2, sh2, wr_hi, wr_lo, br)

        dest, p_rows, tile_expert, n_used = _dispatch_plan(ri, t)
        xs = _dispatch(h2, dest, p_rows, xs)
        ys = _expert_mlp(xs, tile_expert, n_used, expert_w1, expert_w3, expert_w2, l)
        x = _combine(x, ys, dest[0::2], dest[1::2], rw, g2, final_g, final=(l == depth - 1))
    return x
```

```python
import functools
import math

import numpy as np
import jax
import jax.numpy as jnp
from jax import lax
from jax.experimental import pallas as pl
from jax.experimental.pallas import tpu as pltpu

F32 = jnp.float32
BF16 = jnp.bfloat16

HEAD_DIM = 64
LANES = 128
N_HEADS = 16
NSA_HEADS = 8
NSA_KV = 2
NSA_GROUP = 4
FOX_HEADS = 4
SB_HEADS = 4
ROPE_DIM = 16
ROPE_HALF = 8
ROPE_THETA = 500000.0
CMP_LEN = 32
CMP_STRIDE = 16
CMP_HIDDEN = 128
SEL_LEN = 64
SEL_TOPN = 16
WINDOW = 512
FORCE_SCORE = 1.0e4
N_GROUPS = 4
EXPERTS_PER_GROUP = 8
N_EXPERTS = 32
EPS = 1e-6
LOG2E = math.log2(math.e)
NEG = -1e30

COL_QA = 0
COL_KS = 1024
COL_KW = 1152
ROPE_COLS = 1280
COL_VS = 1280
COL_VW = 1408
COL_FOX = 1536
COL_SB = 2560
N_MAIN = 3584
N_SMALL = 512
PROJ_CHUNK = 1280

TM_PROJ = 512
TQ_NSA = 128
TK_ATT = 256
TQ_PAIR = 256
TK_FOX = 512
ROWS = 32
SB_SUBS = 2
TM_OUT = 512
TM_EXP = 512
TM_CMB = 512
COPY_CHUNK = 512
ROW_TILE = 8
VMEM_LIMIT = 56 * 1024 * 1024


def _cp(n_axes, vmem=VMEM_LIMIT):
    return pltpu.CompilerParams(dimension_semantics=("arbitrary",) * n_axes, vmem_limit_bytes=vmem)


def _dot(a, b):
    return jnp.dot(a, b, preferred_element_type=F32)


def _dot_nt(a, b):
    return lax.dot_general(a, b, (((1,), (1,)), ((), ())), preferred_element_type=F32)


def _split_bf16(x, parts):
    out = []
    r = x
    for _ in range(parts):
        p = r.astype(BF16)
        out.append(p)
        r = r - p.astype(F32)
    return out


def _dot_split(x, m, parts):
    acc = None
    for p in _split_bf16(x, parts):
        d = _dot(p, m)
        acc = d if acc is None else acc + d
    return acc


def _rope(x, c, s1, s2):
    return x * c + pltpu.roll(x, ROPE_HALF, 1) * s1 + pltpu.roll(x, LANES - ROPE_HALF, 1) * s2


def _softplus(z):
    return jnp.maximum(z, 0.0) + jnp.log(1.0 + jnp.exp(-jnp.abs(z)))


def _mod_kernel(c_ref, w_ref, b_ref, o_ref):
    c = c_ref[...]
    cond = c * (1.0 / (1.0 + jnp.exp(-c)))
    o_ref[0] = _dot(cond, w_ref[0]) + b_ref[0]


def _modulation(c, ada_w, ada_b):
    depth, d, n = ada_w.shape
    b = c.shape[0]
    tn = 1024
    return pl.pallas_call(
        _mod_kernel,
        grid=(depth, n // tn),
        in_specs=[pl.BlockSpec((b, d), lambda l, j: (0, 0)),
                  pl.BlockSpec((1, d, tn), lambda l, j: (l, 0, j)),
                  pl.BlockSpec((1, 1, tn), lambda l, j: (l, 0, j))],
        out_specs=pl.BlockSpec((1, b, tn), lambda l, j: (l, 0, j)),
        out_shape=jax.ShapeDtypeStruct((depth, b, n), F32),
        compiler_params=_cp(2),
        name="modulation",
    )(c, ada_w, ada_b.reshape(depth, 1, n))


def _inproj_kernel(x_ref, g_ref, sc_ref, sh_ref, w_ref, wc_ref, ws_ref, rc_ref, r1_ref, r2_ref,
                   u_ref, kc_ref, vc_ref, sm_ref):
    x = x_ref[0]
    ms = jnp.mean(x * x, axis=-1, keepdims=True)
    h = (x * lax.rsqrt(ms + EPS) * g_ref[...]) * (1.0 + sc_ref[0]) + sh_ref[0]
    hb = h.astype(BF16)
    rc, r1, r2 = rc_ref[...], r1_ref[...], r2_ref[...]
    for j in range(N_MAIN // PROJ_CHUNK + (1 if N_MAIN % PROJ_CHUNK else 0)):
        lo = j * PROJ_CHUNK
        hi = min(lo + PROJ_CHUNK, N_MAIN)
        acc = _dot(hb, w_ref[:, lo:hi])
        if lo < ROPE_COLS:
            for k in range((hi - lo) // LANES):
                blk = acc[:, k * LANES:(k + 1) * LANES]
                u_ref[0, :, lo + k * LANES:lo + (k + 1) * LANES] = _rope(blk, rc, r1, r2).astype(BF16)
        else:
            u_ref[0, :, lo:hi] = acc.astype(BF16)
    cmp_in = _dot(hb, wc_ref[...])
    kc_ref[0] = cmp_in[:, :LANES].astype(BF16)
    vc_ref[0] = cmp_in[:, LANES:].astype(BF16)
    sm_ref[0] = _dot(hb, ws_ref[...])


def _in_projection(x, g, sc, sh, w_main, w_cmp, w_small, rope_c, rope_1, rope_2):
    b, s, d = x.shape
    tm = min(TM_PROJ, s)
    row = lambda i, j: (i, j, 0)
    const2 = lambda i, j: (0, 0)
    per_b = lambda i, j: (i, 0, 0)
    seq = lambda i, j: (j, 0)
    return pl.pallas_call(
        _inproj_kernel,
        grid=(b, s // tm),
        in_specs=[pl.BlockSpec((1, tm, d), row),
                  pl.BlockSpec((1, d), const2),
                  pl.BlockSpec((1, 1, d), per_b),
                  pl.BlockSpec((1, 1, d), per_b),
                  pl.BlockSpec((d, N_MAIN), const2),
                  pl.BlockSpec((d, 2 * LANES), const2),
                  pl.BlockSpec((d, N_SMALL), const2),
                  pl.BlockSpec((tm, LANES), seq),
                  pl.BlockSpec((tm, LANES), seq),
                  pl.BlockSpec((tm, LANES), seq)],
        out_specs=[pl.BlockSpec((1, tm, N_MAIN), row),
                   pl.BlockSpec((1, tm, LANES), row),
                   pl.BlockSpec((1, tm, LANES), row),
                   pl.BlockSpec((1, tm, N_SMALL), row)],
        out_shape=[jax.ShapeDtypeStruct((b, s, N_MAIN), BF16),
                   jax.ShapeDtypeStruct((b, s, LANES), BF16),
                   jax.ShapeDtypeStruct((b, s, LANES), BF16),
                   jax.ShapeDtypeStruct((b, s, N_SMALL), F32)],
        compiler_params=_cp(2),
        name="in_projection",
    )(x, g.reshape(1, d), sc.reshape(b, 1, d), sh.reshape(b, 1, d), w_main, w_cmp, w_small,
      rope_c, rope_1, rope_2)


def _cumf_kernel(f_ref, b_ref, cq_ref, ckt_ref):
    n_chunks = f_ref.shape[1] // TK_ATT
    r = lax.broadcasted_iota(jnp.int32, (TK_ATT, TK_ATT), 0)
    c = lax.broadcasted_iota(jnp.int32, (TK_ATT, TK_ATT), 1)
    tri = jnp.where(c <= r, 1.0, 0.0).astype(BF16)
    carry = jnp.zeros((1, LANES), F32)
    for j in range(n_chunks):
        f = f_ref[0, j * TK_ATT:(j + 1) * TK_ATT, :] + b_ref[0]
        ls = -_softplus(-f)
        acc = None
        for p in _split_bf16(ls, 3):
            dd = _dot(tri, p)
            acc = dd if acc is None else acc + dd
        cs = acc + carry
        cs2 = cs * LOG2E
        cq_ref[0, j * TK_ATT:(j + 1) * TK_ATT, :] = cs2
        ckt_ref[0, 0, j] = cs2.T[:8, :]
        carry = cs[TK_ATT - 1:TK_ATT, :]


def _forget_cumsum(small, b_pairs):
    b, s, _ = small.shape
    return pl.pallas_call(
        _cumf_kernel,
        grid=(b, 2),
        in_specs=[pl.BlockSpec((1, s, LANES), lambda i, p: (i, 0, 2 + p)),
                  pl.BlockSpec((1, 1, LANES), lambda i, p: (p, 0, 0))],
        out_specs=[pl.BlockSpec((1, s, LANES), lambda i, p: (i, 0, p)),
                   pl.BlockSpec((1, 1, s // TK_ATT, 8, TK_ATT), lambda i, p: (i, p, 0, 0, 0))],
        out_shape=[jax.ShapeDtypeStruct((b, s, 2 * LANES), F32),
                   jax.ShapeDtypeStruct((b, 2, s // TK_ATT, 8, TK_ATT), F32)],
        compiler_params=_cp(2),
        name="forget_cumsum",
    )(small, b_pairs)


def _compress_kernel(ks_ref, vs_ref, wk_ref, wv_ref, w1k_ref, w1v_ref, pek_ref, pev_ref,
                     w2k_ref, w2v_ref, rc_ref, r1_ref, r2_ref, ck_ref, cv_ref):
    def one(seg_ref, w_ref, w1_ref, pe_ref, w2_ref):
        p = _dot(seg_ref[0], w_ref[...])
        half = 2 * CMP_HIDDEN
        bias = _dot(pe_ref[...].astype(BF16), w1_ref[...].astype(BF16))[0:1, :]
        bias2 = jnp.concatenate([bias, bias], axis=1)
        n = p.shape[0]
        hid = p[:, :half] + pltpu.roll(p[:, half:], n - 1, 0) + bias2
        act = hid * (1.0 / (1.0 + jnp.exp(-hid)))
        return _dot(act.astype(BF16), w2_ref[...])

    ck = one(ks_ref, wk_ref, w1k_ref, pek_ref, w2k_ref)
    ck_ref[0] = _rope(ck, rc_ref[...], r1_ref[...], r2_ref[...]).astype(BF16)
    cv_ref[0] = one(vs_ref, wv_ref, w1v_ref, pev_ref, w2v_ref).astype(BF16)


def _compress(kc, vc, wk, wv, w1k, w1v, pek, pev, w2k, w2v, rc, r1, r2):
    b, s, _ = kc.shape
    n = s // CMP_STRIDE
    width = CMP_STRIDE * LANES
    kseg = kc.reshape(b, n, width)
    vseg = vc.reshape(b, n, width)
    seg = pl.BlockSpec((1, n, width), lambda i: (i, 0, 0))
    full = lambda a: pl.BlockSpec(a.shape, lambda i: (0,) * a.ndim)
    return pl.pallas_call(
        _compress_kernel,
        grid=(b,),
        in_specs=[seg, seg, full(wk), full(wv), full(w1k), full(w1v), full(pek), full(pev),
                  full(w2k), full(w2v), full(rc), full(r1), full(r2)],
        out_specs=[pl.BlockSpec((1, n, LANES), lambda i: (i, 0, 0)),
                   pl.BlockSpec((1, n, LANES), lambda i: (i, 0, 0))],
        out_shape=[jax.ShapeDtypeStruct((b, n, LANES), BF16),
                   jax.ShapeDtypeStruct((b, n, LANES), BF16)],
        compiler_params=_cp(1),
        name="nsa_compress",
    )(kseg, vseg, wk, wv, w1k, w1v, pek, pev, w2k, w2v, rc, r1, r2)


def _nsa_kernel(q_ref, ck_ref, cv_ref, ks_ref, vs_ref, kw_ref, vw_ref, gate_ref, gn_ref,
                cover_ref, expand_ref, o_ref, m_ref, l_ref, acc_ref):
    tq = q_ref.shape[1]
    g = pl.program_id(1)
    t0 = pl.program_id(2) * tq
    q = q_ref[0]
    qall = jnp.concatenate([q[:, h * LANES:(h + 1) * LANES] for h in range(NSA_GROUP)], axis=0)
    tpos = t0 + lax.broadcasted_iota(jnp.int32, (tq, 1), 0)
    lane = lax.broadcasted_iota(jnp.int32, (tq, LANES), 1)

    n_cmp = ck_ref.shape[1]
    s_t = _dot_nt(ck_ref[0], qall)
    valid = ((CMP_STRIDE * lax.broadcasted_iota(jnp.int32, (n_cmp, tq), 0) + (CMP_LEN - 1))
             <= t0 + lax.broadcasted_iota(jnp.int32, (n_cmp, tq), 1))
    p_cols, p_sum_t = [], None
    for h in range(NSA_GROUP):
        sm = jnp.where(valid, s_t[:, h * tq:(h + 1) * tq], NEG)
        e = jnp.where(valid, jnp.exp2(sm - jnp.max(sm, axis=0, keepdims=True)), 0.0)
        den = jnp.sum(e, axis=0, keepdims=True)
        p = e * jnp.where(den > 0.0, 1.0 / den, 0.0)
        p_cols.append(p.astype(BF16))
        p_sum_t = p if p_sum_t is None else p_sum_t + p
    o_cmp = lax.dot_general(jnp.concatenate(p_cols, axis=1), cv_ref[0], (((0,), (0,)), ((), ())),
                            preferred_element_type=F32).reshape(NSA_GROUP, tq, LANES)

    n_sel = expand_ref.shape[0] * (TK_FOX // SEL_LEN)
    imp_t = None
    for piece in _split_bf16(p_sum_t, 3):
        d_imp = _dot(cover_ref[...], piece)
        imp_t = d_imp if imp_t is None else imp_t + d_imp
    imp_t = imp_t[:n_sel, :]
    blk = lax.broadcasted_iota(jnp.int32, (n_sel, tq), 0)
    tcol = t0 + lax.broadcasted_iota(jnp.int32, (n_sel, tq), 1)
    cur = jnp.right_shift(tcol, int(math.log2(SEL_LEN)))
    forced = (blk == 0) | (blk == cur) | (blk == cur - 1)
    score = jnp.where(forced, FORCE_SCORE, jnp.where(blk * SEL_LEN <= tcol, imp_t, -1.0))
    cnts = [jnp.zeros((n_sel, tq), F32) for _ in range(4)]
    for j in range(n_sel):
        sj = score[j:j + 1, :]
        beats = (sj > score) | ((sj == score) & (blk > j))
        cnts[j % 4] = cnts[j % 4] + jnp.where(beats, 1.0, 0.0)
    cnt = (cnts[0] + cnts[1]) + (cnts[2] + cnts[3])
    sel_t = jnp.where(cnt < float(min(SEL_TOPN, n_sel)), 1.0, 0.0)
    sel = jnp.concatenate([sel_t, jnp.zeros((LANES - n_sel, tq), F32)], axis=0).T.astype(BF16)

    def biased(bias):
        def adjust(h, t_off, cols):
            return [cols[kk] + bias[t_off:t_off + ROWS, kk * LANES:(kk + 1) * LANES]
                    for kk in range(len(cols))]
        return adjust

    kcol_s = lax.broadcasted_iota(jnp.int32, (tq, TK_FOX), 1)

    def sel_tile(kt, carry):
        k0 = pl.multiple_of(kt * TK_FOX, TK_FOX)
        hit = _dot(sel, expand_ref[kt])
        ok = (hit > 0.5) & ((k0 + kcol_s) <= tpos)
        s_t = _dot_nt(qall, ks_ref[0, pl.ds(k0, TK_FOX), :])
        _softmax_tile(s_t, vs_ref[0, pl.ds(k0, TK_FOX), :], m_ref, l_ref, acc_ref, NSA_GROUP, tq,
                      biased(jnp.where(ok, 0.0, NEG)))
        return carry

    _softmax_init(m_ref, l_ref, acc_ref)
    lax.fori_loop(0, lax.div(t0 + tq - 1, TK_FOX) + 1, sel_tile, 0)
    o_sel = _softmax_result(l_ref, acc_ref).reshape(NSA_GROUP, tq, LANES)

    span = WINDOW + tq
    w0 = pl.multiple_of(jnp.maximum(t0 - WINDOW, 0), LANES)
    kp = w0 + lax.broadcasted_iota(jnp.int32, (tq, span), 1)
    bias_w = jnp.where((kp <= tpos) & (kp > tpos - WINDOW), 0.0, NEG)
    s_w = _dot_nt(qall, kw_ref[0, pl.ds(w0, span), :])
    nkw = span // LANES
    p_rows, inv_rows = [], []
    for h in range(NSA_GROUP):
        for c in range(tq // ROWS):
            r0 = h * tq + c * ROWS
            cols = [s_w[r0:r0 + ROWS, kk * LANES:(kk + 1) * LANES]
                    + bias_w[c * ROWS:(c + 1) * ROWS, kk * LANES:(kk + 1) * LANES] for kk in range(nkw)]
            mx = cols[0]
            for kk in range(1, nkw):
                mx = jnp.maximum(mx, cols[kk])
            mx = jnp.max(mx, axis=-1, keepdims=True)
            pks = [jnp.exp2(cols[kk] - mx) for kk in range(nkw)]
            psum = pks[0]
            for kk in range(1, nkw):
                psum = psum + pks[kk]
            inv_rows.append(jnp.broadcast_to(1.0 / jnp.sum(psum, axis=-1, keepdims=True), (ROWS, LANES)))
            p_rows.append(jnp.concatenate([pk.astype(BF16) for pk in pks], axis=1))
    o_win = _dot(jnp.concatenate(p_rows, axis=0), vw_ref[0, pl.ds(w0, span), :])
    o_win = (o_win * jnp.concatenate(inv_rows, axis=0)).reshape(NSA_GROUP, tq, LANES)

    gt = gate_ref[0]
    gt = 1.0 / (1.0 + jnp.exp(-gt))
    mine = (lane >= g * HEAD_DIM) & (lane < (g + 1) * HEAD_DIM)
    outs = []
    for h in range(NSA_GROUP):
        o = (gt[:, 3 * h:3 * h + 1] * o_cmp[h] + gt[:, 3 * h + 1:3 * h + 2] * o_sel[h]
             + gt[:, 3 * h + 2:3 * h + 3] * o_win[h])
        o = jnp.where(mine, o, 0.0)
        ms = jnp.sum(o * o, axis=-1, keepdims=True) * (1.0 / HEAD_DIM)
        o = o * lax.rsqrt(ms + EPS)
        outs.append(o + pltpu.roll(o, HEAD_DIM, 1))
    left = lane < HEAD_DIM
    o_ref[0, :, :LANES] = (jnp.where(left, outs[0], outs[1]) * gn_ref[:, :LANES]).astype(o_ref.dtype)
    o_ref[0, :, LANES:] = (jnp.where(left, outs[2], outs[3]) * gn_ref[:, LANES:]).astype(o_ref.dtype)


def _nsa_attention(u, ck, cv, small, gn, cover, expand):
    b, s, _ = u.shape
    tq = min(TQ_NSA, s)
    assert s >= WINDOW + tq and WINDOW % tq == 0
    n_cmp = ck.shape[1]
    blk = LANES
    kv = lambda col: pl.BlockSpec((1, s, LANES), lambda i, g, j, col=col: (i, 0, col // blk))
    return pl.pallas_call(
        _nsa_kernel,
        grid=(b, NSA_KV, s // tq),
        in_specs=[pl.BlockSpec((1, tq, NSA_GROUP * LANES), lambda i, g, j: (i, j, g)),
                  pl.BlockSpec((1, n_cmp, LANES), lambda i, g, j: (i, 0, 0)),
                  pl.BlockSpec((1, n_cmp, LANES), lambda i, g, j: (i, 0, 0)),
                  kv(COL_KS), kv(COL_VS), kv(COL_KW), kv(COL_VW),
                  pl.BlockSpec((1, tq, LANES), lambda i, g, j: (i, j, g)),
                  pl.BlockSpec((1, 2 * LANES), lambda i, g, j: (0, g)),
                  pl.BlockSpec(cover.shape, lambda i, g, j: (0, 0)),
                  pl.BlockSpec(expand.shape, lambda i, g, j: (0, 0, 0))],
        out_specs=pl.BlockSpec((1, tq, 2 * LANES), lambda i, g, j: (i, j, g)),
        out_shape=jax.ShapeDtypeStruct((b, s, NSA_HEADS * HEAD_DIM), BF16),
        scratch_shapes=_softmax_scratch(NSA_GROUP * tq),
        compiler_params=_cp(3),
        name="nsa_attention",
    )(u, ck, cv, u, u, u, u, small, gn, cover, expand)


def _pair_finish(acc, gn_ref, o_ref, tq):
    lane = lax.broadcasted_iota(jnp.int32, (tq, LANES), 1)
    left = lane < HEAD_DIM
    o = jnp.where(left, acc[0], acc[1])
    o2 = o * o
    ms_l = jnp.sum(jnp.where(left, o2, 0.0), axis=-1, keepdims=True) * (1.0 / HEAD_DIM)
    ms_r = jnp.sum(jnp.where(left, 0.0, o2), axis=-1, keepdims=True) * (1.0 / HEAD_DIM)
    inv = jnp.where(left, lax.rsqrt(ms_l + EPS), lax.rsqrt(ms_r + EPS))
    o_ref[0] = (o * inv * gn_ref[...]).astype(o_ref.dtype)


def _softmax_tile(s, v, m_ref, l_ref, acc_ref, heads, tq, adjust, p_ref=None):
    nk = s.shape[1] // LANES
    p_rows = []
    for h in range(heads):
        for c in range(tq // ROWS):
            t_off = c * ROWS
            r0 = h * tq + t_off
            cols = [s[r0:r0 + ROWS, k * LANES:(k + 1) * LANES] for k in range(nk)]
            cols = adjust(h, t_off, cols)
            mx = cols[0]
            for k in range(1, nk):
                mx = jnp.maximum(mx, cols[k])
            m_old = m_ref[r0:r0 + ROWS, :]
            m_new = jnp.maximum(m_old, jnp.max(mx, axis=-1, keepdims=True))
            alpha = jnp.exp2(m_old - m_new)
            pks = [jnp.exp2(cols[k] - m_new) for k in range(nk)]
            psum = pks[0]
            for k in range(1, nk):
                psum = psum + pks[k]
            p_chunk = jnp.concatenate([pk.astype(BF16) for pk in pks], axis=1)
            if p_ref is None:
                p_rows.append(p_chunk)
            else:
                p_ref[r0:r0 + ROWS, :] = p_chunk
            l_ref[r0:r0 + ROWS, :] = alpha * l_ref[r0:r0 + ROWS, :] + psum
            acc_ref[r0:r0 + ROWS, :] = alpha * acc_ref[r0:r0 + ROWS, :]
            m_ref[r0:r0 + ROWS, :] = m_new
    if p_ref is None:
        acc_ref[...] += _dot(jnp.concatenate(p_rows, axis=0), v)


def _tile_loop(score_fn, process_fn, lo, hi):
    def body(kt, carry):
        process_fn(kt, score_fn(kt))
        return carry

    lax.fori_loop(lo, hi - 1, body, 0)
    return score_fn(hi - 1)


def _softmax_init(m_ref, l_ref, acc_ref):
    m_ref[...] = jnp.full(m_ref.shape, NEG, F32)
    l_ref[...] = jnp.zeros(l_ref.shape, F32)
    acc_ref[...] = jnp.zeros(acc_ref.shape, F32)


def _softmax_result(l_ref, acc_ref):
    return acc_ref[...] / jnp.sum(l_ref[...], axis=-1, keepdims=True)


def _softmax_scratch(rows):
    return [pltpu.VMEM((rows, LANES), F32), pltpu.VMEM((rows, LANES), F32), pltpu.VMEM((rows, LANES), F32)]


def _fox_kernel(q_ref, k_ref, v_ref, cq_ref, ckt_ref, gn_ref, o_ref,
                m_ref, l_ref, acc_ref, cqr_ref, p_ref):
    tq = q_ref.shape[1]
    tk = TK_FOX
    t0 = pl.program_id(2) * tq
    q = q_ref[0]
    qall = jnp.concatenate([q[:, :LANES], q[:, LANES:]], axis=0)
    cq = cq_ref[0]
    cqr_ref[0:tq, :] = jnp.broadcast_to(cq[:, 0:1], (tq, LANES))
    cqr_ref[tq:2 * tq, :] = jnp.broadcast_to(cq[:, 1:2], (tq, LANES))
    _softmax_init(m_ref, l_ref, acc_ref)
    n_tiles = lax.div(t0 + tq - 1, tk) + 1
    diag = (lax.broadcasted_iota(jnp.int32, (ROWS, LANES), 1)
            - lax.broadcasted_iota(jnp.int32, (ROWS, LANES), 0))

    def scores(kt):
        return _dot_nt(qall, k_ref[0, pl.ds(pl.multiple_of(kt * tk, tk), tk), :])

    def tile(kt, s, masked):
        k0 = pl.multiple_of(kt * tk, tk)
        cks = [ckt_ref[0, 0, kt * (tk // TK_ATT) + j] for j in range(tk // TK_ATT)]

        def adjust(h, t_off, cols):
            out = []
            cqr = cqr_ref[h * tq + t_off:h * tq + t_off + ROWS, :]
            for kk in range(tk // LANES):
                lo = (kk * LANES) % TK_ATT
                ck = cks[(kk * LANES) // TK_ATT][h:h + 1, lo:lo + LANES]
                val = (cols[kk] - ck) + cqr
                if masked:
                    val = jnp.where(diag <= (t0 + t_off) - (k0 + kk * LANES), val, NEG)
                out.append(val)
            return out

        kp = pl.multiple_of(jnp.maximum(kt - 1, 0) * tk, tk)
        acc_ref[...] += _dot(p_ref[...], v_ref[0, pl.ds(kp, tk), :])
        _softmax_tile(s, None, m_ref, l_ref, acc_ref, 2, tq, adjust, p_ref=p_ref)

    p_ref[...] = jnp.zeros(p_ref.shape, BF16)
    s_last = _tile_loop(scores, lambda kt, s: tile(kt, s, False), 0, n_tiles)
    tile(n_tiles - 1, s_last, True)
    acc_ref[...] += _dot(p_ref[...], v_ref[0, pl.ds(pl.multiple_of((n_tiles - 1) * tk, tk), tk), :])
    acc = _softmax_result(l_ref, acc_ref).reshape(2, tq, LANES)
    _pair_finish(acc, gn_ref, o_ref, tq)


def _fox_attention(u, cq, ckt, gn):
    b, s, _ = u.shape
    tq = min(TQ_PAIR, s)
    qb = COL_FOX // (2 * LANES)
    kb = (COL_FOX + FOX_HEADS * LANES) // LANES
    vb = kb + 2
    return pl.pallas_call(
        _fox_kernel,
        grid=(b, 2, s // tq),
        in_specs=[pl.BlockSpec((1, tq, 2 * LANES), lambda i, p, j: (i, j, qb + p)),
                  pl.BlockSpec((1, s, LANES), lambda i, p, j: (i, 0, kb + p)),
                  pl.BlockSpec((1, s, LANES), lambda i, p, j: (i, 0, vb + p)),
                  pl.BlockSpec((1, tq, LANES), lambda i, p, j: (i, j, p)),
                  pl.BlockSpec((1, 1, s // TK_ATT, 8, TK_ATT), lambda i, p, j: (i, p, 0, 0, 0)),
                  pl.BlockSpec((1, LANES), lambda i, p, j: (0, p))],
        out_specs=pl.BlockSpec((1, tq, LANES), lambda i, p, j: (i, j, p)),
        out_shape=jax.ShapeDtypeStruct((b, s, FOX_HEADS * HEAD_DIM), BF16),
        scratch_shapes=_softmax_scratch(2 * tq) + [pltpu.VMEM((2 * tq, LANES), F32),
                                                   pltpu.VMEM((2 * tq, TK_FOX), BF16)],
        compiler_params=_cp(3),
        name="fox_attention",
    )(u, u, u, cq, ckt, gn)


def _sb_kernel(q_ref, k_ref, v_ref, gn_ref, o_ref, rest_ref, acc_ref):
    tq = q_ref.shape[1]
    sub = TK_ATT
    tk = SB_SUBS * sub
    nks = sub // LANES
    rows = 2 * tq
    t0 = pl.program_id(2) * tq
    r = lax.broadcasted_iota(jnp.int32, (sub, sub), 0)
    c = lax.broadcasted_iota(jnp.int32, (sub, sub), 1)
    upper = jnp.where(r >= c, 1.0, 0.0).astype(BF16)
    n_tiles = lax.div(t0 + tq - 1, tk) + 1
    diag = (lax.broadcasted_iota(jnp.int32, (ROWS, LANES), 1)
            - lax.broadcasted_iota(jnp.int32, (ROWS, LANES), 0))
    rest_ref[...] = jnp.zeros(rest_ref.shape, F32)
    acc_ref[...] = jnp.zeros(acc_ref.shape, F32)
    hi_mask = jnp.uint32(0xFFFF0000)
    q = q_ref[0]
    qall = jnp.concatenate([q[:, :LANES], q[:, LANES:]], axis=0)
    chunks = [(h * tq + cc * ROWS, cc * ROWS) for h in range(2) for cc in range(tq // ROWS)]

    def tile(kt, masked):
        k0 = pl.multiple_of(kt * tk, tk)
        k = k_ref[0, pl.ds(k0, tk), :]
        v = v_ref[0, pl.ds(k0, tk), :]

        def strictly_before(t_off, col):
            return diag < (t0 + t_off) - (k0 + col)

        z = _dot_nt(qall, k)
        his = [[] for _ in range(SB_SUBS)]
        los = [[] for _ in range(SB_SUBS)]
        for r0, t_off in chunks:
            for sb in range(SB_SUBS):
                hi_c, lo_c = [], []
                for kk in range(nks):
                    col = sb * sub + kk * LANES
                    zc = z[r0:r0 + ROWS, col:col + LANES]
                    l = -(jnp.maximum(zc, 0.0) + jnp.log2(1.0 + jnp.exp2(-jnp.abs(zc))))
                    if masked:
                        l = jnp.where(strictly_before(t_off, col), l, 0.0)
                    hi = pltpu.bitcast(pltpu.bitcast(l, jnp.uint32) & hi_mask, F32)
                    hi_c.append(hi.astype(BF16))
                    lo_c.append((l - hi).astype(BF16))
                his[sb].append(jnp.concatenate(hi_c, axis=1))
                los[sb].append(jnp.concatenate(lo_c, axis=1))
        hi_all = jnp.concatenate([x for sb in range(SB_SUBS) for x in his[sb]], axis=0)
        lo_all = jnp.concatenate([x for sb in range(SB_SUBS) for x in los[sb]], axis=0)
        cum = _dot(hi_all, upper) + _dot(lo_all, upper)

        a_rows = []
        for r0, t_off in chunks:
            base = rest_ref[r0:r0 + ROWS, :]
            a_c = [None] * (SB_SUBS * nks)
            for sb in reversed(range(SB_SUBS)):
                cs = cum[sb * rows + r0:sb * rows + r0 + ROWS, :]
                for kk in range(nks):
                    col = sb * sub + kk * LANES
                    a = jnp.exp2(z[r0:r0 + ROWS, col:col + LANES] + cs[:, kk * LANES:(kk + 1) * LANES] + base)
                    if masked:
                        a = jnp.where(strictly_before(t_off, col), a, 0.0)
                    a_c[sb * nks + kk] = a.astype(BF16)
                base = base + jnp.broadcast_to(cs[:, 0:1], (ROWS, LANES))
            a_rows.append(jnp.concatenate(a_c, axis=1))
            rest_ref[r0:r0 + ROWS, :] = base
        acc_ref[...] += _dot(jnp.concatenate(a_rows, axis=0), v)

    tile(n_tiles - 1, True)

    def full_tile(i, carry):
        tile(n_tiles - 2 - i, False)
        return carry

    lax.fori_loop(0, n_tiles - 1, full_tile, 0)
    _pair_finish(acc_ref[...].reshape(2, tq, LANES), gn_ref, o_ref, tq)


def _sb_attention(u, gn):
    b, s, _ = u.shape
    tq = min(TQ_PAIR, s)
    qb = COL_SB // (2 * LANES)
    kb = (COL_SB + SB_HEADS * LANES) // LANES
    vb = kb + 2
    return pl.pallas_call(
        _sb_kernel,
        grid=(b, 2, s // tq),
        in_specs=[pl.BlockSpec((1, tq, 2 * LANES), lambda i, p, j: (i, j, qb + p)),
                  pl.BlockSpec((1, s, LANES), lambda i, p, j: (i, 0, kb + p)),
                  pl.BlockSpec((1, s, LANES), lambda i, p, j: (i, 0, vb + p)),
                  pl.BlockSpec((1, LANES), lambda i, p, j: (0, p))],
        out_specs=pl.BlockSpec((1, tq, LANES), lambda i, p, j: (i, j, p)),
        out_shape=jax.ShapeDtypeStruct((b, s, SB_HEADS * HEAD_DIM), BF16),
        scratch_shapes=[pltpu.VMEM((2 * tq, LANES), F32), pltpu.VMEM((2 * tq, LANES), F32)],
        compiler_params=_cp(3),
        name="sb_attention",
    )(u, u, u, gn)


def _outproj_kernel(oa_ref, ob_ref, oc_ref, x_ref, w_ref, g1_ref, n2_ref, sc_ref, sh_ref,
                    wrh_ref, wrl_ref, br_ref, xo_ref, h_ref, rw_ref, ri_ref):
    na = oa_ref.shape[2]
    nb = ob_ref.shape[2]
    y = _dot(oa_ref[0], w_ref[0:na, :])
    y = y + _dot(ob_ref[0], w_ref[na:na + nb, :])
    y = y + _dot(oc_ref[0], w_ref[na + nb:, :])
    x = x_ref[0] + g1_ref[0] * y
    xo_ref[0] = x
    ms = jnp.mean(x * x, axis=-1, keepdims=True)
    h = (x * lax.rsqrt(ms + EPS) * n2_ref[...]) * (1.0 + sc_ref[0]) + sh_ref[0]
    hb = h.astype(BF16)
    _store_row_tiles(h_ref, h)
    hl = (h - hb.astype(F32)).astype(BF16)
    logit = _dot(hb, wrh_ref[...]) + _dot(hl, wrh_ref[...]) + _dot(hb, wrl_ref[...]) + br_ref[...]

    tm = logit.shape[0]
    lane = lax.broadcasted_iota(jnp.int32, (tm, LANES), 1).astype(F32)
    big = float(LANES)
    is_g = lane < N_GROUPS
    lg = jnp.where(is_g, logit, NEG)
    mg = jnp.max(lg, axis=-1, keepdims=True)
    zg = jnp.sum(jnp.where(is_g, jnp.exp(lg - mg), 0.0), axis=-1, keepdims=True)
    pg = 1.0 / zg
    gi = jnp.min(jnp.where(is_g & (lg == mg), lane, big), axis=-1, keepdims=True)
    e_lane = lane - N_GROUPS
    in_grp = (e_lane >= gi * EXPERTS_PER_GROUP) & (e_lane < (gi + 1) * EXPERTS_PER_GROUP)
    le = jnp.where(in_grp, logit, NEG)
    m1 = jnp.max(le, axis=-1, keepdims=True)
    i1 = jnp.min(jnp.where(in_grp & (le == m1), lane, big), axis=-1, keepdims=True)
    rest = in_grp & (lane != i1)
    le2 = jnp.where(rest, logit, NEG)
    m2 = jnp.max(le2, axis=-1, keepdims=True)
    i2 = jnp.min(jnp.where(rest & (le2 == m2), lane, big), axis=-1, keepdims=True)
    ze = jnp.sum(jnp.where(in_grp, jnp.exp(le - m1), 0.0), axis=-1, keepdims=True)
    p1 = 1.0 / ze
    p2 = jnp.exp(m2 - m1) / ze
    den = p1 + p2
    w1 = pg * (p1 / den)
    w2 = pg * (p2 / den)
    rw_ref[0] = jnp.where(lane == 0.0, w1, jnp.where(lane == 1.0, w2, 0.0))
    ri_ref[0] = jnp.where(lane == 0.0, i1 - N_GROUPS, jnp.where(lane == 1.0, i2 - N_GROUPS, 0.0)).astype(jnp.int32)


def _out_projection(oa, ob, oc, x, w_out, g1, n2, sc, sh, wr_hi, wr_lo, br):
    b, s, d = x.shape
    tm = min(TM_OUT, s)
    row = lambda i, j: (i, j, 0)
    const2 = lambda i, j: (0, 0)
    per_b = lambda i, j: (i, 0, 0)
    return pl.pallas_call(
        _outproj_kernel,
        grid=(b, s // tm),
        in_specs=[pl.BlockSpec((1, tm, oa.shape[2]), row),
                  pl.BlockSpec((1, tm, ob.shape[2]), row),
                  pl.BlockSpec((1, tm, oc.shape[2]), row),
                  pl.BlockSpec((1, tm, d), row),
                  pl.BlockSpec(w_out.shape, const2),
                  pl.BlockSpec((1, 1, d), per_b),
                  pl.BlockSpec((1, d), const2),
                  pl.BlockSpec((1, 1, d), per_b),
                  pl.BlockSpec((1, 1, d), per_b),
                  pl.BlockSpec((d, LANES), const2),
                  pl.BlockSpec((d, LANES), const2),
                  pl.BlockSpec((1, LANES), const2)],
        out_specs=[pl.BlockSpec((1, tm, d), row),
                   pl.BlockSpec((tm * ROW_TILE, LANES), lambda i, j: (i * (s // tm) + j, 0)),
                   pl.BlockSpec((1, tm, LANES), row),
                   pl.BlockSpec((1, tm, LANES), row)],
        out_shape=[jax.ShapeDtypeStruct((b, s, d), F32),
                   jax.ShapeDtypeStruct((b * s * ROW_TILE, LANES), F32),
                   jax.ShapeDtypeStruct((b, s, LANES), F32),
                   jax.ShapeDtypeStruct((b, s, LANES), jnp.int32)],
        compiler_params=_cp(2),
        name="out_projection",
    )(oa, ob, oc, x, w_out, g1.reshape(b, 1, d), n2.reshape(1, d), sc.reshape(b, 1, d),
      sh.reshape(b, 1, d), wr_hi, wr_lo, br)


def _store_row_tiles(ref, val):
    tm = val.shape[0]
    for c in range(ROW_TILE):
        ref[pl.ds(c, tm, stride=ROW_TILE), :] = val[:, c * LANES:(c + 1) * LANES]


def _load_row_tiles(ref, tm):
    return [ref[pl.ds(c, tm, stride=ROW_TILE), :] for c in range(ROW_TILE)]


def _tile_rows(ref, n):
    return ref.at[pl.ds(pl.multiple_of(n * ROW_TILE, ROW_TILE), ROW_TILE), :]


def _dispatch_kernel(idx_ref, src_ref, init_hbm, dst_hbm, sem):
    del init_hbm

    def issue(tok, carry):
        for k in range(2):
            pltpu.make_async_copy(_tile_rows(src_ref, tok), _tile_rows(dst_hbm, idx_ref[0, 0, 2 * tok + k]),
                                  sem).start(priority=k)
        return carry

    lax.fori_loop(0, COPY_CHUNK // 2, issue, 0, unroll=4)
    for _ in range(2):
        pltpu.make_async_copy(src_ref, dst_hbm.at[pl.ds(0, src_ref.shape[0]), :], sem).wait()


def _dispatch(src, idx, n_dst, init):
    n = idx.shape[0]
    if init is None:
        init = jnp.zeros((n_dst * ROW_TILE, LANES), src.dtype)
    return pl.pallas_call(
        _dispatch_kernel,
        grid=(n // COPY_CHUNK,),
        in_specs=[pl.BlockSpec((1, 1, COPY_CHUNK), lambda i: (i, 0, 0), memory_space=pltpu.SMEM),
                  pl.BlockSpec((COPY_CHUNK // 2 * ROW_TILE, LANES), lambda i: (i, 0)),
                  pl.BlockSpec(memory_space=pl.ANY)],
        out_specs=pl.BlockSpec(memory_space=pl.ANY),
        out_shape=jax.ShapeDtypeStruct((n_dst * ROW_TILE, LANES), src.dtype),
        scratch_shapes=[pltpu.SemaphoreType.DMA],
        input_output_aliases={2: 0},
        compiler_params=_cp(1),
        name="moe_dispatch",
    )(idx.reshape(n // COPY_CHUNK, 1, COPY_CHUNK), src, init)


def _expert_kernel(te_ref, nu_ref, x_ref, w1_ref, w3_ref, w2_ref, y_ref, w1b_ref, w3b_ref, w2b_ref):
    i = pl.program_id(0)

    @pl.when((i == 0) | (te_ref[i] != te_ref[jnp.maximum(i - 1, 0)]))
    def _():
        w1b_ref[...] = w1_ref[0].astype(BF16)
        w3b_ref[...] = w3_ref[0].astype(BF16)
        w2b_ref[...] = w2_ref[0].astype(BF16)

    @pl.when(i < nu_ref[0])
    def _():
        x = jnp.concatenate(_load_row_tiles(x_ref, TM_EXP), axis=1).astype(BF16)
        a = _dot(x, w1b_ref[...])
        g = _dot(x, w3b_ref[...])
        act = (a * (1.0 / (1.0 + jnp.exp(-a)))) * g
        _store_row_tiles(y_ref, _dot(act.astype(BF16), w2b_ref[...]))

    @pl.when(i >= nu_ref[0])
    def _():
        y_ref[...] = jnp.zeros_like(y_ref)


def _expert_mlp(xs, tile_expert, n_used, w1, w3, w2, layer):
    d, de = w1.shape[2], w1.shape[3]
    n_tiles = xs.shape[0] // (TM_EXP * ROW_TILE)
    grid_spec = pltpu.PrefetchScalarGridSpec(
        num_scalar_prefetch=2,
        grid=(n_tiles,),
        in_specs=[pl.BlockSpec((TM_EXP * ROW_TILE, LANES), lambda i, te, nu: (i, 0)),
                  pl.BlockSpec((None, 1, d, de), lambda i, te, nu: (layer, te[i], 0, 0)),
                  pl.BlockSpec((None, 1, d, de), lambda i, te, nu: (layer, te[i], 0, 0)),
                  pl.BlockSpec((None, 1, de, d), lambda i, te, nu: (layer, te[i], 0, 0))],
        out_specs=pl.BlockSpec((TM_EXP * ROW_TILE, LANES), lambda i, te, nu: (i, 0)),
        scratch_shapes=[pltpu.VMEM((d, de), BF16), pltpu.VMEM((d, de), BF16), pltpu.VMEM((de, d), BF16)],
    )
    return pl.pallas_call(
        _expert_kernel,
        grid_spec=grid_spec,
        out_shape=jax.ShapeDtypeStruct(xs.shape, F32),
        compiler_params=_cp(1),
        name="expert_mlp",
    )(tile_expert, n_used, xs, w1, w3, w2)


def _combine_kernel(d0_ref, d1_ref, x_ref, ys_hbm, rw_ref, g2_ref, fg_ref, o_ref, y0_ref, y1_ref, sem,
                    *, final):
    tm = x_ref.shape[1]

    def issue(r, carry):
        pltpu.make_async_copy(_tile_rows(ys_hbm, d0_ref[0, 0, r]), _tile_rows(y0_ref, r),
                              sem.at[0]).start(priority=0)
        pltpu.make_async_copy(_tile_rows(ys_hbm, d1_ref[0, 0, r]), _tile_rows(y1_ref, r),
                              sem.at[1]).start(priority=1)
        return carry

    lax.fori_loop(0, tm, issue, 0, unroll=8)
    pltpu.make_async_copy(ys_hbm.at[pl.ds(0, tm * ROW_TILE), :], y0_ref, sem.at[0]).wait()
    pltpu.make_async_copy(ys_hbm.at[pl.ds(0, tm * ROW_TILE), :], y1_ref, sem.at[1]).wait()

    rw = rw_ref[0]
    w0 = jnp.broadcast_to(rw[:, 0:1], (tm, LANES))
    w1 = jnp.broadcast_to(rw[:, 1:2], (tm, LANES))
    y0 = _load_row_tiles(y0_ref, tm)
    y1 = _load_row_tiles(y1_ref, tm)
    cols = []
    for c in range(ROW_TILE):
        sl = slice(c * LANES, (c + 1) * LANES)
        cols.append(x_ref[0, :, sl] + g2_ref[0, :, sl] * (y0[c] * w0 + y1[c] * w1))
    if final:
        ssq = cols[0] * cols[0]
        for c in range(1, ROW_TILE):
            ssq = ssq + cols[c] * cols[c]
        inv = lax.rsqrt(jnp.sum(ssq, axis=-1, keepdims=True) * (1.0 / (ROW_TILE * LANES)) + EPS)
        cols = [cols[c] * inv * fg_ref[:, c * LANES:(c + 1) * LANES] for c in range(ROW_TILE)]
    for c in range(ROW_TILE):
        o_ref[0, :, c * LANES:(c + 1) * LANES] = cols[c]


def _combine(x, ys, dest0, dest1, rw, g2, final_g, final):
    b, s, d = x.shape
    tm = min(TM_CMB, s)
    row = lambda i, j: (i, j, 0)
    idx_spec = pl.BlockSpec((1, 1, tm), lambda i, j: (i * (s // tm) + j, 0, 0), memory_space=pltpu.SMEM)
    return pl.pallas_call(
        functools.partial(_combine_kernel, final=final),
        grid=(b, s // tm),
        in_specs=[idx_spec, idx_spec,
                  pl.BlockSpec((1, tm, d), row),
                  pl.BlockSpec(memory_space=pl.ANY),
                  pl.BlockSpec((1, tm, LANES), row),
                  pl.BlockSpec((1, 1, d), lambda i, j: (i, 0, 0)),
                  pl.BlockSpec((1, d), lambda i, j: (0, 0))],
        out_specs=pl.BlockSpec((1, tm, d), row),
        out_shape=jax.ShapeDtypeStruct((b, s, d), F32),
        scratch_shapes=[pltpu.VMEM((tm * ROW_TILE, LANES), F32), pltpu.VMEM((tm * ROW_TILE, LANES), F32),
                        pltpu.SemaphoreType.DMA((2,))],
        compiler_params=_cp(2),
        name="moe_combine_final" if final else "moe_combine",
    )(dest0.reshape(-1, 1, tm), dest1.reshape(-1, 1, tm), x, ys, rw, g2.reshape(b, 1, d),
      final_g.reshape(1, d))


def _pad_heads(w, n_heads, offsets):
    d = w.shape[0]
    w = w.reshape(d, n_heads, HEAD_DIM)
    z = jnp.zeros((d, n_heads, HEAD_DIM), w.dtype)
    off = jnp.asarray(offsets, jnp.int32).reshape(1, n_heads, 1)
    blk = jnp.where(off == 0, jnp.concatenate([w, z], axis=-1), jnp.concatenate([z, w], axis=-1))
    return blk.reshape(d, n_heads * LANES)


def _layout_w_in(w_in):
    d = w_in.shape[0]
    kvw = NSA_KV * HEAD_DIM
    sizes = (NSA_HEADS * HEAD_DIM, kvw, kvw, kvw, kvw, kvw, kvw, NSA_HEADS * 3,
             FOX_HEADS * HEAD_DIM, FOX_HEADS * HEAD_DIM, FOX_HEADS * HEAD_DIM, FOX_HEADS,
             SB_HEADS * HEAD_DIM, SB_HEADS * HEAD_DIM, SB_HEADS * HEAD_DIM)
    pts = np.cumsum(sizes)[:-1].tolist()
    (qa, kca, vca, ksa, vsa, kwa, vwa, ga, qb, kb, vb, fb, qc, kc, vc) = jnp.split(w_in, pts, axis=1)
    scale = HEAD_DIM ** -0.5 * LOG2E
    qa_p = _pad_heads(qa * scale, NSA_HEADS, [0] * NSA_GROUP + [HEAD_DIM] * NSA_GROUP)
    qb_p = _pad_heads(qb * scale, FOX_HEADS, [0, HEAD_DIM, 0, HEAD_DIM])
    qc_p = _pad_heads(qc * scale, SB_HEADS, [0, HEAD_DIM, 0, HEAD_DIM])
    main = jnp.concatenate([qa_p, ksa, kwa, vsa, vwa, qb_p, kb, vb, qc_p, kc, vc], axis=1).astype(BF16)
    cmp_w = jnp.concatenate([kca, vca], axis=1).astype(BF16)
    zpad = lambda n: jnp.zeros((d, n), w_in.dtype)
    per_grp = NSA_GROUP * 3
    small = jnp.concatenate([ga[:, :per_grp], zpad(LANES - per_grp), ga[:, per_grp:], zpad(LANES - per_grp),
                             fb[:, 0:2], zpad(LANES - 2), fb[:, 2:4], zpad(LANES - 2)], axis=1).astype(BF16)
    return main, cmp_w, small


def _layout_cmp(w1, w2):
    hid = w1.shape[1]
    w1r = w1.reshape(2, CMP_STRIDE, HEAD_DIM, hid)
    z = jnp.zeros((CMP_STRIDE, HEAD_DIM, hid), w1.dtype)
    cols = []
    for half in range(2):
        for g in range(NSA_KV):
            parts = [w1r[half] if gg == g else z for gg in range(NSA_KV)]
            cols.append(jnp.concatenate(parts, axis=1).reshape(CMP_STRIDE * LANES, hid))
    wcat = jnp.concatenate(cols, axis=1).astype(BF16)
    zz = jnp.zeros_like(w2)
    w2bd = jnp.concatenate([jnp.concatenate([w2, zz], axis=1),
                            jnp.concatenate([zz, w2], axis=1)], axis=0).astype(BF16)
    return wcat, w2bd


def _rope_tables(pos):
    inv = jnp.exp(jnp.arange(ROPE_HALF, dtype=F32) * (-2.0 * math.log(ROPE_THETA) / ROPE_DIM))
    ang = pos.astype(F32)[:, None] * inv[None, :]
    cos, sin = jnp.cos(ang), jnp.sin(ang)
    n = pos.shape[0]
    z8 = jnp.zeros((n, ROPE_HALF), F32)
    rest1 = jnp.ones((n, HEAD_DIM - ROPE_DIM), F32)
    rest0 = jnp.zeros((n, HEAD_DIM - ROPE_DIM), F32)
    c = jnp.concatenate([cos, cos, rest1], axis=1)
    s1 = jnp.concatenate([z8, sin, rest0], axis=1)
    s2 = jnp.concatenate([-sin, z8, rest0], axis=1)
    dup = lambda a: jnp.concatenate([a, a], axis=1)
    return dup(c), dup(s1), dup(s2)


def _static_tables(s):
    n_cmp_pad = s // CMP_STRIDE
    n = np.arange(n_cmp_pad)[:, None]
    j = np.arange(LANES)[None, :]
    n_sel = s // SEL_LEN
    cover = ((n * CMP_STRIDE < j * SEL_LEN + SEL_LEN) & (n * CMP_STRIDE + CMP_LEN > j * SEL_LEN)
             & (j < n_sel)).astype(np.float32)
    nt = s // TK_FOX
    key = np.arange(nt)[:, None, None] * TK_FOX + np.arange(TK_FOX)[None, None, :]
    expand = (key // SEL_LEN == np.arange(LANES)[None, :, None]).astype(np.float32)
    return jnp.asarray(cover.T, BF16), jnp.asarray(expand, BF16)


def _dispatch_plan(ri, t):
    eid = ri.reshape(t, LANES)[:, :2].reshape(-1)
    n_assign = eid.shape[0]
    onehot = (eid[:, None] == jnp.arange(N_EXPERTS, dtype=jnp.int32)[None, :]).astype(jnp.int32)
    csum = jnp.cumsum(onehot, axis=0)
    counts = csum[-1]
    rank = jnp.take_along_axis(csum, eid[:, None], axis=1)[:, 0] - 1
    padded = ((counts + TM_EXP - 1) // TM_EXP) * TM_EXP
    pends = jnp.cumsum(padded)
    pstarts = pends - padded
    dest = pstarts[eid] + rank
    n_tiles = -(-(n_assign + N_EXPERTS * (TM_EXP - 1)) // TM_EXP)
    tile_start = jnp.arange(n_tiles, dtype=jnp.int32) * TM_EXP
    tile_expert = jnp.minimum(jnp.sum((pends[None, :] <= tile_start[:, None]).astype(jnp.int32), axis=1),
                              N_EXPERTS - 1).astype(jnp.int32)
    n_used = (pends[-1] // TM_EXP).astype(jnp.int32).reshape(1)
    return dest.astype(jnp.int32), n_tiles * TM_EXP, tile_expert, n_used


def kernel(x, c, norm1_g, norm2_g, ada_w, ada_b, w_in, b_forget, cmp_pos_k, cmp_w1_k, cmp_w2_k,
           cmp_pos_v, cmp_w1_v, cmp_w2_v, out_norm_g, w_out, router_group_w, router_group_b,
           router_expert_w, router_expert_b, expert_w1, expert_w3, expert_w2, final_g):
    b, s, d = x.shape
    depth = ada_w.shape[0]
    t = b * s
    mod = _modulation(c, ada_w, ada_b)
    rope_c, rope_1, rope_2 = _rope_tables(jnp.arange(s))
    n_cmp_pad = s // CMP_STRIDE
    crc, cr1, cr2 = _rope_tables(jnp.arange(n_cmp_pad) * CMP_STRIDE + (CMP_LEN - 1))
    cover, expand = _static_tables(s)
    xs = None

    for l in range(depth):
        sh1, sc1, g1, sh2, sc2, g2 = [mod[l][:, i * d:(i + 1) * d] for i in range(6)]
        w_main, w_cmp, w_small = _layout_w_in(w_in[l])
        u, kc, vc, small = _in_projection(x, norm1_g[l], sc1, sh1, w_main, w_cmp, w_small,
                                          rope_c, rope_1, rope_2)
        bf = b_forget[l]
        zf = jnp.zeros((LANES - 2,), F32)
        b_pairs = jnp.stack([jnp.concatenate([bf[0:2], zf]), jnp.concatenate([bf[2:4], zf])]).reshape(2, 1, LANES)
        cq, ckt = _forget_cumsum(small, b_pairs)
        wk, w2k = _layout_cmp(cmp_w1_k[l], cmp_w2_k[l])
        wv, w2v = _layout_cmp(cmp_w1_v[l], cmp_w2_v[l])
        ck, cv = _compress(kc, vc, wk, wv, cmp_w1_k[l], cmp_w1_v[l],
                           jnp.broadcast_to(cmp_pos_k[l].reshape(1, -1), (8, CMP_LEN * HEAD_DIM)),
                           jnp.broadcast_to(cmp_pos_v[l].reshape(1, -1), (8, CMP_LEN * HEAD_DIM)),
                           w2k, w2v, crc, cr1, cr2)
        gn = out_norm_g[l].reshape(1, -1)
        o_a = _nsa_attention(u, ck, cv, small, gn[:, :NSA_HEADS * HEAD_DIM], cover, expand)
        o_b = _fox_attention(u, cq, ckt, gn[:, NSA_HEADS * HEAD_DIM:(NSA_HEADS + FOX_HEADS) * HEAD_DIM])
        o_c = _sb_attention(u, gn[:, (NSA_HEADS + FOX_HEADS) * HEAD_DIM:])

        wr = jnp.concatenate([router_group_w[l], router_expert_w[l],
                              jnp.zeros((d, LANES - N_GROUPS - N_EXPERTS), F32)], axis=1)
        wr_hi = wr.astype(BF16)
        wr_lo = (wr - wr_hi.astype(F32)).astype(BF16)
        br = jnp.concatenate([router_group_b[l], router_expert_b[l],
                              jnp.zeros((LANES - N_GROUPS - N_EXPERTS,), F32)]).reshape(1, LANES)
        x, h2, rw, ri = _out_projection(o_a, o_b, o_c, x, w_out[l].astype(BF16), g1, norm2_g[l],
                                        sc2, sh2, wr_hi, wr_lo, br)

        dest, p_rows, tile_expert, n_used = _dispatch_plan(ri, t)
        xs = _dispatch(h2, dest, p_rows, xs)
        ys = _expert_mlp(xs, tile_expert, n_used, expert_w1, expert_w3, expert_w2, l)
        x = _combine(x, ys, dest[0::2], dest[1::2], rw, g2, final_g, final=(l == depth - 1))
    return x
```

```python
import functools
import math

import numpy as np
import jax
import jax.numpy as jnp
from jax import lax
from jax.experimental import pallas as pl
from jax.experimental.pallas import tpu as pltpu

F32 = jnp.float32
BF16 = jnp.bfloat16

HEAD_DIM = 64
LANES = 128
N_HEADS = 16
NSA_HEADS = 8
NSA_KV = 2
NSA_GROUP = 4
FOX_HEADS = 4
SB_HEADS = 4
ROPE_DIM = 16
ROPE_HALF = 8
ROPE_THETA = 500000.0
CMP_LEN = 32
CMP_STRIDE = 16
CMP_HIDDEN = 128
SEL_LEN = 64
SEL_TOPN = 16
WINDOW = 512
FORCE_SCORE = 1.0e4
N_GROUPS = 4
EXPERTS_PER_GROUP = 8
N_EXPERTS = 32
EPS = 1e-6
LOG2E = math.log2(math.e)
NEG = -1e30

COL_QA = 0
COL_KS = 1024
COL_KW = 1152
COL_VS = 1280
COL_VW = 1408
COL_FOX = 1536
COL_SB = 2560
N_MAIN = 3584
N_SMALL = 512
W_ROPE = (NSA_HEADS + 2 * NSA_KV) * HEAD_DIM
W_MAIN = W_ROPE + (2 * NSA_KV + 3 * FOX_HEADS + 3 * SB_HEADS) * HEAD_DIM

TM_PROJ = 512
TQ_NSA = 128
TK_ATT = 256
TQ_PAIR = 256
TK_FOX = 512
ROWS = 32
SB_SUBS = 2
TM_OUT = 512
TM_EXP = 512
TM_CMB = 512
COPY_CHUNK = 512
ROW_TILE = 8
VMEM_LIMIT = 56 * 1024 * 1024


def _cp(n_axes, vmem=VMEM_LIMIT):
    return pltpu.CompilerParams(dimension_semantics=("arbitrary",) * n_axes, vmem_limit_bytes=vmem)


def _dot(a, b):
    return jnp.dot(a, b, preferred_element_type=F32)


def _dot_nt(a, b):
    return lax.dot_general(a, b, (((1,), (1,)), ((), ())), preferred_element_type=F32)


def _split_bf16(x, parts):
    out = []
    r = x
    for _ in range(parts):
        p = r.astype(BF16)
        out.append(p)
        r = r - p.astype(F32)
    return out


def _dot_split(x, m, parts):
    acc = None
    for p in _split_bf16(x, parts):
        d = _dot(p, m)
        acc = d if acc is None else acc + d
    return acc


def _rope(x, c, s1, s2):
    return x * c + pltpu.roll(x, ROPE_HALF, 1) * s1 + pltpu.roll(x, LANES - ROPE_HALF, 1) * s2


def _softplus(z):
    return jnp.maximum(z, 0.0) + jnp.log(1.0 + jnp.exp(-jnp.abs(z)))


def _mod_kernel(c_ref, w_ref, b_ref, o_ref):
    c = c_ref[...]
    cond = c * (1.0 / (1.0 + jnp.exp(-c)))
    o_ref[0] = _dot(cond, w_ref[0]) + b_ref[0]


def _modulation(c, ada_w, ada_b):
    depth, d, n = ada_w.shape
    b = c.shape[0]
    tn = 1024
    return pl.pallas_call(
        _mod_kernel,
        grid=(depth, n // tn),
        in_specs=[pl.BlockSpec((b, d), lambda l, j: (0, 0)),
                  pl.BlockSpec((1, d, tn), lambda l, j: (l, 0, j)),
                  pl.BlockSpec((1, 1, tn), lambda l, j: (l, 0, j))],
        out_specs=pl.BlockSpec((1, b, tn), lambda l, j: (l, 0, j)),
        out_shape=jax.ShapeDtypeStruct((depth, b, n), F32),
        compiler_params=_cp(2),
        name="modulation",
    )(c, ada_w, ada_b.reshape(depth, 1, n))


def _inproj_kernel(x_ref, g_ref, sc_ref, sh_ref, w_ref, wc_ref, ws_ref, rc_ref, r1_ref, r2_ref,
                   u_ref, kc_ref, vc_ref, sm_ref):
    x = x_ref[0]
    ms = jnp.mean(x * x, axis=-1, keepdims=True)
    h = (x * lax.rsqrt(ms + EPS) * g_ref[...]) * (1.0 + sc_ref[0]) + sh_ref[0]
    hb = h.astype(BF16)
    rc, r1, r2 = rc_ref[...], r1_ref[...], r2_ref[...]
    lane = lax.broadcasted_iota(jnp.int32, (x.shape[0], LANES), 1)
    left = lane < HEAD_DIM

    def put(col, val):
        u_ref[0, :, col:col + LANES] = val.astype(BF16)

    def put_pair(col, blk, offsets):
        for i, off in enumerate(offsets):
            src = blk if off == i * HEAD_DIM else pltpu.roll(blk, HEAD_DIM, 1)
            put(col + i * LANES, jnp.where(left if off == 0 else ~left, src, 0.0))

    acc = _dot(hb, w_ref[:, :W_ROPE])
    for j in range(NSA_HEADS // 2):
        off = (2 * j // NSA_GROUP) * HEAD_DIM
        put_pair(COL_QA + 2 * j * LANES, _rope(acc[:, j * LANES:(j + 1) * LANES], rc, r1, r2), (off, off))
    nq = NSA_HEADS // 2
    put(COL_KS, _rope(acc[:, nq * LANES:(nq + 1) * LANES], rc, r1, r2))
    put(COL_KW, _rope(acc[:, (nq + 1) * LANES:(nq + 2) * LANES], rc, r1, r2))
    acc = _dot(hb, w_ref[:, W_ROPE:])
    u_ref[0, :, COL_VS:COL_VS + 2 * LANES] = acc[:, :2 * LANES].astype(BF16)
    c = 2 * LANES
    for base, heads in ((COL_FOX, FOX_HEADS), (COL_SB, SB_HEADS)):
        for j in range(heads // 2):
            put_pair(base + 2 * j * LANES, acc[:, c:c + LANES], (0, HEAD_DIM))
            c += LANES
        kv = 2 * heads * HEAD_DIM
        u_ref[0, :, base + heads * LANES:base + heads * LANES + kv] = acc[:, c:c + kv].astype(BF16)
        c += kv
    cmp_in = _dot(hb, wc_ref[...])
    kc_ref[0] = cmp_in[:, :LANES].astype(BF16)
    vc_ref[0] = cmp_in[:, LANES:].astype(BF16)
    sm_ref[0] = _dot(hb, ws_ref[...])


def _mod_spec(layer, which, d):
    return pl.BlockSpec((None, 1, None, 1, d), lambda i, j: (layer, i, which, 0, 0))


def _layer_spec(layer, shape):
    zeros = (0,) * len(shape)
    return pl.BlockSpec((None,) + tuple(shape), lambda *_: (layer,) + zeros)


def _in_projection(x, g, mod4, w_main, w_cmp, w_small, rope_c, rope_1, rope_2, layer):
    b, s, d = x.shape
    tm = min(TM_PROJ, s)
    row = lambda i, j: (i, j, 0)
    seq = lambda i, j: (j, 0)
    return pl.pallas_call(
        _inproj_kernel,
        grid=(b, s // tm),
        in_specs=[pl.BlockSpec((1, tm, d), row),
                  _layer_spec(layer, (1, d)),
                  _mod_spec(layer, 1, d),
                  _mod_spec(layer, 0, d),
                  _layer_spec(layer, (d, W_MAIN)),
                  _layer_spec(layer, (d, 2 * LANES)),
                  _layer_spec(layer, (d, N_SMALL)),
                  pl.BlockSpec((tm, LANES), seq),
                  pl.BlockSpec((tm, LANES), seq),
                  pl.BlockSpec((tm, LANES), seq)],
        out_specs=[pl.BlockSpec((1, tm, N_MAIN), row),
                   pl.BlockSpec((1, tm, LANES), row),
                   pl.BlockSpec((1, tm, LANES), row),
                   pl.BlockSpec((1, tm, N_SMALL), row)],
        out_shape=[jax.ShapeDtypeStruct((b, s, N_MAIN), BF16),
                   jax.ShapeDtypeStruct((b, s, LANES), BF16),
                   jax.ShapeDtypeStruct((b, s, LANES), BF16),
                   jax.ShapeDtypeStruct((b, s, N_SMALL), F32)],
        compiler_params=_cp(2),
        name="in_projection",
    )(x, g.reshape(-1, 1, d), mod4, mod4, w_main, w_cmp, w_small, rope_c, rope_1, rope_2)


def _cumf_kernel(f_ref, b_ref, cq_ref, ckt_ref):
    n_chunks = f_ref.shape[1] // TK_ATT
    r = lax.broadcasted_iota(jnp.int32, (TK_ATT, TK_ATT), 0)
    c = lax.broadcasted_iota(jnp.int32, (TK_ATT, TK_ATT), 1)
    tri = jnp.where(c <= r, 1.0, 0.0).astype(BF16)
    carry = jnp.zeros((1, LANES), F32)
    for j in range(n_chunks):
        f = f_ref[0, j * TK_ATT:(j + 1) * TK_ATT, :] + b_ref[0]
        ls = -_softplus(-f)
        acc = None
        for p in _split_bf16(ls, 3):
            dd = _dot(tri, p)
            acc = dd if acc is None else acc + dd
        cs = acc + carry
        cs2 = cs * LOG2E
        cq_ref[0, j * TK_ATT:(j + 1) * TK_ATT, :] = cs2
        ckt_ref[0, 0, j] = cs2.T[:8, :]
        carry = cs[TK_ATT - 1:TK_ATT, :]


def _forget_cumsum(small, b_pairs, layer):
    b, s, _ = small.shape
    return pl.pallas_call(
        _cumf_kernel,
        grid=(b, 2),
        in_specs=[pl.BlockSpec((1, s, LANES), lambda i, p: (i, 0, 2 + p)),
                  pl.BlockSpec((None, 1, 1, LANES), lambda i, p: (layer, p, 0, 0))],
        out_specs=[pl.BlockSpec((1, s, LANES), lambda i, p: (i, 0, p)),
                   pl.BlockSpec((1, 1, s // TK_ATT, 8, TK_ATT), lambda i, p: (i, p, 0, 0, 0))],
        out_shape=[jax.ShapeDtypeStruct((b, s, 2 * LANES), F32),
                   jax.ShapeDtypeStruct((b, 2, s // TK_ATT, 8, TK_ATT), F32)],
        compiler_params=_cp(2),
        name="forget_cumsum",
    )(small, b_pairs)


def _compress_kernel(ks_ref, vs_ref, wk_ref, wv_ref, w1k_ref, w1v_ref, pek_ref, pev_ref,
                     w2k_ref, w2v_ref, rc_ref, r1_ref, r2_ref, ck_ref, cv_ref):
    def one(seg_ref, w_ref, w1_ref, pe_ref, w2_ref):
        p = _dot(seg_ref[0], w_ref[...])
        half = 2 * CMP_HIDDEN
        bias = _dot(pe_ref[...].astype(BF16), w1_ref[...].astype(BF16))[0:1, :]
        bias2 = jnp.concatenate([bias, bias], axis=1)
        n = p.shape[0]
        hid = p[:, :half] + pltpu.roll(p[:, half:], n - 1, 0) + bias2
        act = hid * (1.0 / (1.0 + jnp.exp(-hid)))
        return _dot(act.astype(BF16), w2_ref[...])

    ck = one(ks_ref, wk_ref, w1k_ref, pek_ref, w2k_ref)
    ck_ref[0] = _rope(ck, rc_ref[...], r1_ref[...], r2_ref[...]).astype(BF16)
    cv_ref[0] = one(vs_ref, wv_ref, w1v_ref, pev_ref, w2v_ref).astype(BF16)


def _compress(kc, vc, wk, wv, w1k, w1v, pek, pev, w2k, w2v, rc, r1, r2):
    b, s, _ = kc.shape
    n = s // CMP_STRIDE
    width = CMP_STRIDE * LANES
    kseg = kc.reshape(b, n, width)
    vseg = vc.reshape(b, n, width)
    seg = pl.BlockSpec((1, n, width), lambda i: (i, 0, 0))
    full = lambda a: pl.BlockSpec(a.shape, lambda i: (0,) * a.ndim)
    return pl.pallas_call(
        _compress_kernel,
        grid=(b,),
        in_specs=[seg, seg, full(wk), full(wv), full(w1k), full(w1v), full(pek), full(pev),
                  full(w2k), full(w2v), full(rc), full(r1), full(r2)],
        out_specs=[pl.BlockSpec((1, n, LANES), lambda i: (i, 0, 0)),
                   pl.BlockSpec((1, n, LANES), lambda i: (i, 0, 0))],
        out_shape=[jax.ShapeDtypeStruct((b, n, LANES), BF16),
                   jax.ShapeDtypeStruct((b, n, LANES), BF16)],
        compiler_params=_cp(1),
        name="nsa_compress",
    )(kseg, vseg, wk, wv, w1k, w1v, pek, pev, w2k, w2v, rc, r1, r2)


def _nsa_kernel(q_ref, ck_ref, cv_ref, ks_ref, vs_ref, kw_ref, vw_ref, gate_ref, gn_ref,
                cover_ref, expand_ref, o_ref, m_ref, l_ref, acc_ref):
    tq = q_ref.shape[1]
    g = pl.program_id(1)
    t0 = pl.program_id(2) * tq
    q = q_ref[0]
    qall = jnp.concatenate([q[:, h * LANES:(h + 1) * LANES] for h in range(NSA_GROUP)], axis=0)
    tpos = t0 + lax.broadcasted_iota(jnp.int32, (tq, 1), 0)
    lane = lax.broadcasted_iota(jnp.int32, (tq, LANES), 1)

    n_cmp = ck_ref.shape[1]
    s_t = _dot_nt(ck_ref[0], qall)
    valid = ((CMP_STRIDE * lax.broadcasted_iota(jnp.int32, (n_cmp, tq), 0) + (CMP_LEN - 1))
             <= t0 + lax.broadcasted_iota(jnp.int32, (n_cmp, tq), 1))
    p_cols, p_sum_t = [], None
    for h in range(NSA_GROUP):
        sm = jnp.where(valid, s_t[:, h * tq:(h + 1) * tq], NEG)
        e = jnp.where(valid, jnp.exp2(sm - jnp.max(sm, axis=0, keepdims=True)), 0.0)
        den = jnp.sum(e, axis=0, keepdims=True)
        p = e * jnp.where(den > 0.0, 1.0 / den, 0.0)
        p_cols.append(p.astype(BF16))
        p_sum_t = p if p_sum_t is None else p_sum_t + p
    o_cmp = lax.dot_general(jnp.concatenate(p_cols, axis=1), cv_ref[0], (((0,), (0,)), ((), ())),
                            preferred_element_type=F32).reshape(NSA_GROUP, tq, LANES)

    n_sel = expand_ref.shape[0] * (TK_FOX // SEL_LEN)
    imp_t = None
    for piece in _split_bf16(p_sum_t, 3):
        d_imp = _dot(cover_ref[...], piece)
        imp_t = d_imp if imp_t is None else imp_t + d_imp
    imp_t = imp_t[:n_sel, :]
    blk = lax.broadcasted_iota(jnp.int32, (n_sel, tq), 0)
    tcol = t0 + lax.broadcasted_iota(jnp.int32, (n_sel, tq), 1)
    cur = jnp.right_shift(tcol, int(math.log2(SEL_LEN)))
    forced = (blk == 0) | (blk == cur) | (blk == cur - 1)
    score = jnp.where(forced, FORCE_SCORE, jnp.where(blk * SEL_LEN <= tcol, imp_t, -1.0))
    cnts = [jnp.zeros((n_sel, tq), F32) for _ in range(4)]
    for j in range(n_sel):
        sj = score[j:j + 1, :]
        beats = (sj > score) | ((sj == score) & (blk > j))
        cnts[j % 4] = cnts[j % 4] + jnp.where(beats, 1.0, 0.0)
    cnt = (cnts[0] + cnts[1]) + (cnts[2] + cnts[3])
    sel_t = jnp.where(cnt < float(min(SEL_TOPN, n_sel)), 1.0, 0.0)
    sel = jnp.concatenate([sel_t, jnp.zeros((LANES - n_sel, tq), F32)], axis=0).T.astype(BF16)

    def biased(bias):
        def adjust(h, t_off, cols):
            return [cols[kk] + bias[t_off:t_off + ROWS, kk * LANES:(kk + 1) * LANES]
                    for kk in range(len(cols))]
        return adjust

    kcol_s = lax.broadcasted_iota(jnp.int32, (tq, TK_FOX), 1)

    def sel_tile(kt, carry):
        k0 = pl.multiple_of(kt * TK_FOX, TK_FOX)
        hit = _dot(sel, expand_ref[kt])
        ok = (hit > 0.5) & ((k0 + kcol_s) <= tpos)
        s_t = _dot_nt(qall, ks_ref[0, pl.ds(k0, TK_FOX), :])
        _softmax_tile(s_t, vs_ref[0, pl.ds(k0, TK_FOX), :], m_ref, l_ref, acc_ref, NSA_GROUP, tq,
                      biased(jnp.where(ok, 0.0, NEG)))
        return carry

    _softmax_init(m_ref, l_ref, acc_ref)
    lax.fori_loop(0, lax.div(t0 + tq - 1, TK_FOX) + 1, sel_tile, 0)
    o_sel = _softmax_result(l_ref, acc_ref).reshape(NSA_GROUP, tq, LANES)

    span = WINDOW + tq
    w0 = pl.multiple_of(jnp.maximum(t0 - WINDOW, 0), LANES)
    kp = w0 + lax.broadcasted_iota(jnp.int32, (tq, span), 1)
    bias_w = jnp.where((kp <= tpos) & (kp > tpos - WINDOW), 0.0, NEG)
    s_w = _dot_nt(qall, kw_ref[0, pl.ds(w0, span), :])
    nkw = span // LANES
    p_rows, inv_rows = [], []
    for h in range(NSA_GROUP):
        for c in range(tq // ROWS):
            r0 = h * tq + c * ROWS
            cols = [s_w[r0:r0 + ROWS, kk * LANES:(kk + 1) * LANES]
                    + bias_w[c * ROWS:(c + 1) * ROWS, kk * LANES:(kk + 1) * LANES] for kk in range(nkw)]
            mx = cols[0]
            for kk in range(1, nkw):
                mx = jnp.maximum(mx, cols[kk])
            mx = jnp.max(mx, axis=-1, keepdims=True)
            pks = [jnp.exp2(cols[kk] - mx) for kk in range(nkw)]
            psum = pks[0]
            for kk in range(1, nkw):
                psum = psum + pks[kk]
            inv_rows.append(jnp.broadcast_to(1.0 / jnp.sum(psum, axis=-1, keepdims=True), (ROWS, LANES)))
            p_rows.append(jnp.concatenate([pk.astype(BF16) for pk in pks], axis=1))
    o_win = _dot(jnp.concatenate(p_rows, axis=0), vw_ref[0, pl.ds(w0, span), :])
    o_win = (o_win * jnp.concatenate(inv_rows, axis=0)).reshape(NSA_GROUP, tq, LANES)

    gt = gate_ref[0]
    gt = 1.0 / (1.0 + jnp.exp(-gt))
    mine = (lane >= g * HEAD_DIM) & (lane < (g + 1) * HEAD_DIM)
    outs = []
    for h in range(NSA_GROUP):
        o = (gt[:, 3 * h:3 * h + 1] * o_cmp[h] + gt[:, 3 * h + 1:3 * h + 2] * o_sel[h]
             + gt[:, 3 * h + 2:3 * h + 3] * o_win[h])
        o = jnp.where(mine, o, 0.0)
        ms = jnp.sum(o * o, axis=-1, keepdims=True) * (1.0 / HEAD_DIM)
        o = o * lax.rsqrt(ms + EPS)
        outs.append(o + pltpu.roll(o, HEAD_DIM, 1))
    left = lane < HEAD_DIM
    o_ref[0, :, :LANES] = (jnp.where(left, outs[0], outs[1]) * gn_ref[:, :LANES]).astype(o_ref.dtype)
    o_ref[0, :, LANES:] = (jnp.where(left, outs[2], outs[3]) * gn_ref[:, LANES:]).astype(o_ref.dtype)


def _nsa_attention(u, ck, cv, small, gn, cover, expand, layer):
    b, s, _ = u.shape
    tq = min(TQ_NSA, s)
    assert s >= WINDOW + tq and WINDOW % tq == 0
    n_cmp = ck.shape[1]
    blk = LANES
    kv = lambda col: pl.BlockSpec((1, s, LANES), lambda i, g, j, col=col: (i, 0, col // blk))
    return pl.pallas_call(
        _nsa_kernel,
        grid=(b, NSA_KV, s // tq),
        in_specs=[pl.BlockSpec((1, tq, NSA_GROUP * LANES), lambda i, g, j: (i, j, g)),
                  pl.BlockSpec((1, n_cmp, LANES), lambda i, g, j: (i, 0, 0)),
                  pl.BlockSpec((1, n_cmp, LANES), lambda i, g, j: (i, 0, 0)),
                  kv(COL_KS), kv(COL_VS), kv(COL_KW), kv(COL_VW),
                  pl.BlockSpec((1, tq, LANES), lambda i, g, j: (i, j, g)),
                  pl.BlockSpec((None, 1, 2 * LANES), lambda i, g, j: (layer, 0, g)),
                  pl.BlockSpec(cover.shape, lambda i, g, j: (0, 0)),
                  pl.BlockSpec(expand.shape, lambda i, g, j: (0, 0, 0))],
        out_specs=pl.BlockSpec((1, tq, 2 * LANES), lambda i, g, j: (i, j, g)),
        out_shape=jax.ShapeDtypeStruct((b, s, NSA_HEADS * HEAD_DIM), BF16),
        scratch_shapes=_softmax_scratch(NSA_GROUP * tq),
        compiler_params=_cp(3),
        name="nsa_attention",
    )(u, ck, cv, u, u, u, u, small, gn, cover, expand)


def _pair_finish(acc, gn_ref, o_ref, tq):
    lane = lax.broadcasted_iota(jnp.int32, (tq, LANES), 1)
    left = lane < HEAD_DIM
    o = jnp.where(left, acc[0], acc[1])
    o2 = o * o
    ms_l = jnp.sum(jnp.where(left, o2, 0.0), axis=-1, keepdims=True) * (1.0 / HEAD_DIM)
    ms_r = jnp.sum(jnp.where(left, 0.0, o2), axis=-1, keepdims=True) * (1.0 / HEAD_DIM)
    inv = jnp.where(left, lax.rsqrt(ms_l + EPS), lax.rsqrt(ms_r + EPS))
    o_ref[0] = (o * inv * gn_ref[...]).astype(o_ref.dtype)


def _softmax_tile(s, v, m_ref, l_ref, acc_ref, heads, tq, adjust, p_ref=None):
    nk = s.shape[1] // LANES
    p_rows = []
    for h in range(heads):
        for c in range(tq // ROWS):
            t_off = c * ROWS
            r0 = h * tq + t_off
            cols = [s[r0:r0 + ROWS, k * LANES:(k + 1) * LANES] for k in range(nk)]
            cols = adjust(h, t_off, cols)
            mx = cols[0]
            for k in range(1, nk):
                mx = jnp.maximum(mx, cols[k])
            m_old = m_ref[r0:r0 + ROWS, :]
            m_new = jnp.maximum(m_old, jnp.max(mx, axis=-1, keepdims=True))
            alpha = jnp.exp2(m_old - m_new)
            pks = [jnp.exp2(cols[k] - m_new) for k in range(nk)]
            psum = pks[0]
            for k in range(1, nk):
                psum = psum + pks[k]
            p_chunk = jnp.concatenate([pk.astype(BF16) for pk in pks], axis=1)
            if p_ref is None:
                p_rows.append(p_chunk)
            else:
                p_ref[r0:r0 + ROWS, :] = p_chunk
            l_ref[r0:r0 + ROWS, :] = alpha * l_ref[r0:r0 + ROWS, :] + psum
            acc_ref[r0:r0 + ROWS, :] = alpha * acc_ref[r0:r0 + ROWS, :]
            m_ref[r0:r0 + ROWS, :] = m_new
    if p_ref is None:
        acc_ref[...] += _dot(jnp.concatenate(p_rows, axis=0), v)


def _tile_loop(score_fn, process_fn, lo, hi):
    def body(kt, carry):
        process_fn(kt, score_fn(kt))
        return carry

    lax.fori_loop(lo, hi - 1, body, 0)
    return score_fn(hi - 1)


def _softmax_init(m_ref, l_ref, acc_ref):
    m_ref[...] = jnp.full(m_ref.shape, NEG, F32)
    l_ref[...] = jnp.zeros(l_ref.shape, F32)
    acc_ref[...] = jnp.zeros(acc_ref.shape, F32)


def _softmax_result(l_ref, acc_ref):
    return acc_ref[...] / jnp.sum(l_ref[...], axis=-1, keepdims=True)


def _softmax_scratch(rows):
    return [pltpu.VMEM((rows, LANES), F32), pltpu.VMEM((rows, LANES), F32), pltpu.VMEM((rows, LANES), F32)]


def _fox_kernel(q_ref, k_ref, v_ref, cq_ref, ckt_ref, gn_ref, o_ref,
                m_ref, l_ref, acc_ref, cqr_ref, p_ref):
    tq = q_ref.shape[1]
    tk = TK_FOX
    t0 = pl.program_id(2) * tq
    q = q_ref[0]
    qall = jnp.concatenate([q[:, :LANES], q[:, LANES:]], axis=0)
    cq = cq_ref[0]
    cqr_ref[0:tq, :] = jnp.broadcast_to(cq[:, 0:1], (tq, LANES))
    cqr_ref[tq:2 * tq, :] = jnp.broadcast_to(cq[:, 1:2], (tq, LANES))
    _softmax_init(m_ref, l_ref, acc_ref)
    n_tiles = lax.div(t0 + tq - 1, tk) + 1
    diag = (lax.broadcasted_iota(jnp.int32, (ROWS, LANES), 1)
            - lax.broadcasted_iota(jnp.int32, (ROWS, LANES), 0))

    def scores(kt):
        return _dot_nt(qall, k_ref[0, pl.ds(pl.multiple_of(kt * tk, tk), tk), :])

    def tile(kt, s, masked):
        k0 = pl.multiple_of(kt * tk, tk)
        cks = [ckt_ref[0, 0, kt * (tk // TK_ATT) + j] for j in range(tk // TK_ATT)]

        def adjust(h, t_off, cols):
            out = []
            cqr = cqr_ref[h * tq + t_off:h * tq + t_off + ROWS, :]
            for kk in range(tk // LANES):
                lo = (kk * LANES) % TK_ATT
                ck = cks[(kk * LANES) // TK_ATT][h:h + 1, lo:lo + LANES]
                val = (cols[kk] - ck) + cqr
                if masked:
                    val = jnp.where(diag <= (t0 + t_off) - (k0 + kk * LANES), val, NEG)
                out.append(val)
            return out

        kp = pl.multiple_of(jnp.maximum(kt - 1, 0) * tk, tk)
        acc_ref[...] += _dot(p_ref[...], v_ref[0, pl.ds(kp, tk), :])
        _softmax_tile(s, None, m_ref, l_ref, acc_ref, 2, tq, adjust, p_ref=p_ref)

    p_ref[...] = jnp.zeros(p_ref.shape, BF16)
    s_last = _tile_loop(scores, lambda kt, s: tile(kt, s, False), 0, n_tiles)
    tile(n_tiles - 1, s_last, True)
    acc_ref[...] += _dot(p_ref[...], v_ref[0, pl.ds(pl.multiple_of((n_tiles - 1) * tk, tk), tk), :])
    acc = _softmax_result(l_ref, acc_ref).reshape(2, tq, LANES)
    _pair_finish(acc, gn_ref, o_ref, tq)


def _fox_attention(u, cq, ckt, gn, layer):
    b, s, _ = u.shape
    tq = min(TQ_PAIR, s)
    gb = NSA_HEADS * HEAD_DIM // LANES
    qb = COL_FOX // (2 * LANES)
    kb = (COL_FOX + FOX_HEADS * LANES) // LANES
    vb = kb + 2
    return pl.pallas_call(
        _fox_kernel,
        grid=(b, 2, s // tq),
        in_specs=[pl.BlockSpec((1, tq, 2 * LANES), lambda i, p, j: (i, j, qb + p)),
                  pl.BlockSpec((1, s, LANES), lambda i, p, j: (i, 0, kb + p)),
                  pl.BlockSpec((1, s, LANES), lambda i, p, j: (i, 0, vb + p)),
                  pl.BlockSpec((1, tq, LANES), lambda i, p, j: (i, j, p)),
                  pl.BlockSpec((1, 1, s // TK_ATT, 8, TK_ATT), lambda i, p, j: (i, p, 0, 0, 0)),
                  pl.BlockSpec((None, 1, LANES), lambda i, p, j: (layer, 0, gb + p))],
        out_specs=pl.BlockSpec((1, tq, LANES), lambda i, p, j: (i, j, p)),
        out_shape=jax.ShapeDtypeStruct((b, s, FOX_HEADS * HEAD_DIM), BF16),
        scratch_shapes=_softmax_scratch(2 * tq) + [pltpu.VMEM((2 * tq, LANES), F32),
                                                   pltpu.VMEM((2 * tq, TK_FOX), BF16)],
        compiler_params=_cp(3),
        name="fox_attention",
    )(u, u, u, cq, ckt, gn)


def _sb_kernel(q_ref, k_ref, v_ref, gn_ref, o_ref, rest_ref, acc_ref):
    tq = q_ref.shape[1]
    sub = TK_ATT
    tk = SB_SUBS * sub
    nks = sub // LANES
    rows = 2 * tq
    t0 = pl.program_id(2) * tq
    r = lax.broadcasted_iota(jnp.int32, (sub, sub), 0)
    c = lax.broadcasted_iota(jnp.int32, (sub, sub), 1)
    upper = jnp.where(r >= c, 1.0, 0.0).astype(BF16)
    n_tiles = lax.div(t0 + tq - 1, tk) + 1
    diag = (lax.broadcasted_iota(jnp.int32, (ROWS, LANES), 1)
            - lax.broadcasted_iota(jnp.int32, (ROWS, LANES), 0))
    rest_ref[...] = jnp.zeros(rest_ref.shape, F32)
    acc_ref[...] = jnp.zeros(acc_ref.shape, F32)
    hi_mask = jnp.uint32(0xFFFF0000)
    q = q_ref[0]
    qall = jnp.concatenate([q[:, :LANES], q[:, LANES:]], axis=0)
    chunks = [(h * tq + cc * ROWS, cc * ROWS) for h in range(2) for cc in range(tq // ROWS)]

    def tile(kt, masked):
        k0 = pl.multiple_of(kt * tk, tk)
        k = k_ref[0, pl.ds(k0, tk), :]
        v = v_ref[0, pl.ds(k0, tk), :]

        def strictly_before(t_off, col):
            return diag < (t0 + t_off) - (k0 + col)

        z = _dot_nt(qall, k)
        his = [[] for _ in range(SB_SUBS)]
        los = [[] for _ in range(SB_SUBS)]
        for r0, t_off in chunks:
            for sb in range(SB_SUBS):
                hi_c, lo_c = [], []
                for kk in range(nks):
                    col = sb * sub + kk * LANES
                    zc = z[r0:r0 + ROWS, col:col + LANES]
                    l = -(jnp.maximum(zc, 0.0) + jnp.log2(1.0 + jnp.exp2(-jnp.abs(zc))))
                    if masked:
                        l = jnp.where(strictly_before(t_off, col), l, 0.0)
                    hi = pltpu.bitcast(pltpu.bitcast(l, jnp.uint32) & hi_mask, F32)
                    hi_c.append(hi.astype(BF16))
                    lo_c.append((l - hi).astype(BF16))
                his[sb].append(jnp.concatenate(hi_c, axis=1))
                los[sb].append(jnp.concatenate(lo_c, axis=1))
        hi_all = jnp.concatenate([x for sb in range(SB_SUBS) for x in his[sb]], axis=0)
        lo_all = jnp.concatenate([x for sb in range(SB_SUBS) for x in los[sb]], axis=0)
        cum = _dot(hi_all, upper) + _dot(lo_all, upper)

        a_rows = []
        for r0, t_off in chunks:
            base = rest_ref[r0:r0 + ROWS, :]
            a_c = [None] * (SB_SUBS * nks)
            for sb in reversed(range(SB_SUBS)):
                cs = cum[sb * rows + r0:sb * rows + r0 + ROWS, :]
                for kk in range(nks):
                    col = sb * sub + kk * LANES
                    a = jnp.exp2(z[r0:r0 + ROWS, col:col + LANES] + cs[:, kk * LANES:(kk + 1) * LANES] + base)
                    if masked:
                        a = jnp.where(strictly_before(t_off, col), a, 0.0)
                    a_c[sb * nks + kk] = a.astype(BF16)
                base = base + jnp.broadcast_to(cs[:, 0:1], (ROWS, LANES))
            a_rows.append(jnp.concatenate(a_c, axis=1))
            rest_ref[r0:r0 + ROWS, :] = base
        acc_ref[...] += _dot(jnp.concatenate(a_rows, axis=0), v)

    tile(n_tiles - 1, True)

    def full_tile(i, carry):
        tile(n_tiles - 2 - i, False)
        return carry

    lax.fori_loop(0, n_tiles - 1, full_tile, 0)
    _pair_finish(acc_ref[...].reshape(2, tq, LANES), gn_ref, o_ref, tq)


def _sb_attention(u, gn, layer):
    b, s, _ = u.shape
    tq = min(TQ_PAIR, s)
    gb = (NSA_HEADS + FOX_HEADS) * HEAD_DIM // LANES
    qb = COL_SB // (2 * LANES)
    kb = (COL_SB + SB_HEADS * LANES) // LANES
    vb = kb + 2
    return pl.pallas_call(
        _sb_kernel,
        grid=(b, 2, s // tq),
        in_specs=[pl.BlockSpec((1, tq, 2 * LANES), lambda i, p, j: (i, j, qb + p)),
                  pl.BlockSpec((1, s, LANES), lambda i, p, j: (i, 0, kb + p)),
                  pl.BlockSpec((1, s, LANES), lambda i, p, j: (i, 0, vb + p)),
                  pl.BlockSpec((None, 1, LANES), lambda i, p, j: (layer, 0, gb + p))],
        out_specs=pl.BlockSpec((1, tq, LANES), lambda i, p, j: (i, j, p)),
        out_shape=jax.ShapeDtypeStruct((b, s, SB_HEADS * HEAD_DIM), BF16),
        scratch_shapes=[pltpu.VMEM((2 * tq, LANES), F32), pltpu.VMEM((2 * tq, LANES), F32)],
        compiler_params=_cp(3),
        name="sb_attention",
    )(u, u, u, gn)


def _outproj_kernel(oa_ref, ob_ref, oc_ref, x_ref, w_ref, g1_ref, n2_ref, sc_ref, sh_ref,
                    wrh_ref, wrl_ref, br_ref, xo_ref, h_ref, rw_ref, ri_ref):
    na = oa_ref.shape[2]
    nb = ob_ref.shape[2]
    y = _dot(oa_ref[0], w_ref[0:na, :])
    y = y + _dot(ob_ref[0], w_ref[na:na + nb, :])
    y = y + _dot(oc_ref[0], w_ref[na + nb:, :])
    x = x_ref[0] + g1_ref[0] * y
    xo_ref[0] = x
    ms = jnp.mean(x * x, axis=-1, keepdims=True)
    h = (x * lax.rsqrt(ms + EPS) * n2_ref[...]) * (1.0 + sc_ref[0]) + sh_ref[0]
    hb = h.astype(BF16)
    _store_row_tiles(h_ref, h)
    hl = (h - hb.astype(F32)).astype(BF16)
    logit = _dot(hb, wrh_ref[...]) + _dot(hl, wrh_ref[...]) + _dot(hb, wrl_ref[...]) + br_ref[...]

    tm = logit.shape[0]
    lane = lax.broadcasted_iota(jnp.int32, (tm, LANES), 1).astype(F32)
    big = float(LANES)
    is_g = lane < N_GROUPS
    lg = jnp.where(is_g, logit, NEG)
    mg = jnp.max(lg, axis=-1, keepdims=True)
    zg = jnp.sum(jnp.where(is_g, jnp.exp(lg - mg), 0.0), axis=-1, keepdims=True)
    pg = 1.0 / zg
    gi = jnp.min(jnp.where(is_g & (lg == mg), lane, big), axis=-1, keepdims=True)
    e_lane = lane - N_GROUPS
    in_grp = (e_lane >= gi * EXPERTS_PER_GROUP) & (e_lane < (gi + 1) * EXPERTS_PER_GROUP)
    le = jnp.where(in_grp, logit, NEG)
    m1 = jnp.max(le, axis=-1, keepdims=True)
    i1 = jnp.min(jnp.where(in_grp & (le == m1), lane, big), axis=-1, keepdims=True)
    rest = in_grp & (lane != i1)
    le2 = jnp.where(rest, logit, NEG)
    m2 = jnp.max(le2, axis=-1, keepdims=True)
    i2 = jnp.min(jnp.where(rest & (le2 == m2), lane, big), axis=-1, keepdims=True)
    ze = jnp.sum(jnp.where(in_grp, jnp.exp(le - m1), 0.0), axis=-1, keepdims=True)
    p1 = 1.0 / ze
    p2 = jnp.exp(m2 - m1) / ze
    den = p1 + p2
    w1 = pg * (p1 / den)
    w2 = pg * (p2 / den)
    rw_ref[0] = jnp.where(lane == 0.0, w1, jnp.where(lane == 1.0, w2, 0.0))
    ri_ref[0] = jnp.where(lane == 0.0, i1 - N_GROUPS, jnp.where(lane == 1.0, i2 - N_GROUPS, 0.0)).astype(jnp.int32)


def _out_projection(oa, ob, oc, x, w_out, mod4, n2, wr_hi, wr_lo, br, layer):
    b, s, d = x.shape
    tm = min(TM_OUT, s)
    row = lambda i, j: (i, j, 0)
    return pl.pallas_call(
        _outproj_kernel,
        grid=(b, s // tm),
        in_specs=[pl.BlockSpec((1, tm, oa.shape[2]), row),
                  pl.BlockSpec((1, tm, ob.shape[2]), row),
                  pl.BlockSpec((1, tm, oc.shape[2]), row),
                  pl.BlockSpec((1, tm, d), row),
                  _layer_spec(layer, w_out.shape[1:]),
                  _mod_spec(layer, 2, d),
                  _layer_spec(layer, (1, d)),
                  _mod_spec(layer, 4, d),
                  _mod_spec(layer, 3, d),
                  _layer_spec(layer, (d, LANES)),
                  _layer_spec(layer, (d, LANES)),
                  _layer_spec(layer, (1, LANES))],
        out_specs=[pl.BlockSpec((1, tm, d), row),
                   pl.BlockSpec((tm * ROW_TILE, LANES), lambda i, j: (i * (s // tm) + j, 0)),
                   pl.BlockSpec((1, tm, LANES), row),
                   pl.BlockSpec((1, tm, LANES), row)],
        out_shape=[jax.ShapeDtypeStruct((b, s, d), F32),
                   jax.ShapeDtypeStruct((b * s * ROW_TILE, LANES), F32),
                   jax.ShapeDtypeStruct((b, s, LANES), F32),
                   jax.ShapeDtypeStruct((b, s, LANES), jnp.int32)],
        compiler_params=_cp(2),
        name="out_projection",
    )(oa, ob, oc, x, w_out, mod4, n2.reshape(-1, 1, d), mod4, mod4, wr_hi, wr_lo, br)


def _store_row_tiles(ref, val):
    tm = val.shape[0]
    for c in range(ROW_TILE):
        ref[pl.ds(c, tm, stride=ROW_TILE), :] = val[:, c * LANES:(c + 1) * LANES]


def _load_row_tiles(ref, tm):
    return [ref[pl.ds(c, tm, stride=ROW_TILE), :] for c in range(ROW_TILE)]


def _tile_rows(ref, n):
    return ref.at[pl.ds(pl.multiple_of(n * ROW_TILE, ROW_TILE), ROW_TILE), :]


def _dispatch_kernel(idx_ref, src_ref, init_hbm, dst_hbm, sem):
    del init_hbm

    def issue(tok, carry):
        for k in range(2):
            pltpu.make_async_copy(_tile_rows(src_ref, tok), _tile_rows(dst_hbm, idx_ref[0, 0, 2 * tok + k]),
                                  sem).start(priority=k)
        return carry

    lax.fori_loop(0, COPY_CHUNK // 2, issue, 0, unroll=4)
    for _ in range(2):
        pltpu.make_async_copy(src_ref, dst_hbm.at[pl.ds(0, src_ref.shape[0]), :], sem).wait()


def _dispatch(src, idx, n_dst, init):
    n = idx.shape[0]
    if init is None:
        init = jnp.zeros((n_dst * ROW_TILE, LANES), src.dtype)
    return pl.pallas_call(
        _dispatch_kernel,
        grid=(n // COPY_CHUNK,),
        in_specs=[pl.BlockSpec((1, 1, COPY_CHUNK), lambda i: (i, 0, 0), memory_space=pltpu.SMEM),
                  pl.BlockSpec((COPY_CHUNK // 2 * ROW_TILE, LANES), lambda i: (i, 0)),
                  pl.BlockSpec(memory_space=pl.ANY)],
        out_specs=pl.BlockSpec(memory_space=pl.ANY),
        out_shape=jax.ShapeDtypeStruct((n_dst * ROW_TILE, LANES), src.dtype),
        scratch_shapes=[pltpu.SemaphoreType.DMA],
        input_output_aliases={2: 0},
        compiler_params=_cp(1),
        name="moe_dispatch",
    )(idx.reshape(n // COPY_CHUNK, 1, COPY_CHUNK), src, init)


def _expert_kernel(te_ref, nu_ref, x_ref, w1_ref, w3_ref, w2_ref, y_ref, w1b_ref, w3b_ref, w2b_ref):
    i = pl.program_id(0)

    @pl.when((i == 0) | (te_ref[i] != te_ref[jnp.maximum(i - 1, 0)]))
    def _():
        w1b_ref[...] = w1_ref[0].astype(BF16)
        w3b_ref[...] = w3_ref[0].astype(BF16)
        w2b_ref[...] = w2_ref[0].astype(BF16)

    @pl.when(i < nu_ref[0])
    def _():
        x = jnp.concatenate(_load_row_tiles(x_ref, TM_EXP), axis=1).astype(BF16)
        a = _dot(x, w1b_ref[...])
        g = _dot(x, w3b_ref[...])
        act = (a * (1.0 / (1.0 + jnp.exp(-a)))) * g
        _store_row_tiles(y_ref, _dot(act.astype(BF16), w2b_ref[...]))

    @pl.when(i >= nu_ref[0])
    def _():
        y_ref[...] = jnp.zeros_like(y_ref)


def _expert_mlp(xs, tile_expert, n_used, w1, w3, w2, layer):
    d, de = w1.shape[2], w1.shape[3]
    n_tiles = xs.shape[0] // (TM_EXP * ROW_TILE)
    grid_spec = pltpu.PrefetchScalarGridSpec(
        num_scalar_prefetch=2,
        grid=(n_tiles,),
        in_specs=[pl.BlockSpec((TM_EXP * ROW_TILE, LANES), lambda i, te, nu: (i, 0)),
                  pl.BlockSpec((None, 1, d, de), lambda i, te, nu: (layer, te[i], 0, 0)),
                  pl.BlockSpec((None, 1, d, de), lambda i, te, nu: (layer, te[i], 0, 0)),
                  pl.BlockSpec((None, 1, de, d), lambda i, te, nu: (layer, te[i], 0, 0))],
        out_specs=pl.BlockSpec((TM_EXP * ROW_TILE, LANES), lambda i, te, nu: (i, 0)),
        scratch_shapes=[pltpu.VMEM((d, de), BF16), pltpu.VMEM((d, de), BF16), pltpu.VMEM((de, d), BF16)],
    )
    return pl.pallas_call(
        _expert_kernel,
        grid_spec=grid_spec,
        out_shape=jax.ShapeDtypeStruct(xs.shape, F32),
        compiler_params=_cp(1),
        name="expert_mlp",
    )(tile_expert, n_used, xs, w1, w3, w2)


def _combine_kernel(d0_ref, d1_ref, x_ref, ys_hbm, rw_ref, g2_ref, fg_ref, o_ref, y0_ref, y1_ref, sem,
                    *, final):
    tm = x_ref.shape[1]

    def issue(r, carry):
        pltpu.make_async_copy(_tile_rows(ys_hbm, d0_ref[0, 0, r]), _tile_rows(y0_ref, r),
                              sem.at[0]).start(priority=0)
        pltpu.make_async_copy(_tile_rows(ys_hbm, d1_ref[0, 0, r]), _tile_rows(y1_ref, r),
                              sem.at[1]).start(priority=1)
        return carry

    lax.fori_loop(0, tm, issue, 0, unroll=8)
    pltpu.make_async_copy(ys_hbm.at[pl.ds(0, tm * ROW_TILE), :], y0_ref, sem.at[0]).wait()
    pltpu.make_async_copy(ys_hbm.at[pl.ds(0, tm * ROW_TILE), :], y1_ref, sem.at[1]).wait()

    rw = rw_ref[0]
    w0 = jnp.broadcast_to(rw[:, 0:1], (tm, LANES))
    w1 = jnp.broadcast_to(rw[:, 1:2], (tm, LANES))
    y0 = _load_row_tiles(y0_ref, tm)
    y1 = _load_row_tiles(y1_ref, tm)
    cols = []
    for c in range(ROW_TILE):
        sl = slice(c * LANES, (c + 1) * LANES)
        cols.append(x_ref[0, :, sl] + g2_ref[0, :, sl] * (y0[c] * w0 + y1[c] * w1))
    if final:
        ssq = cols[0] * cols[0]
        for c in range(1, ROW_TILE):
            ssq = ssq + cols[c] * cols[c]
        inv = lax.rsqrt(jnp.sum(ssq, axis=-1, keepdims=True) * (1.0 / (ROW_TILE * LANES)) + EPS)
        cols = [cols[c] * inv * fg_ref[:, c * LANES:(c + 1) * LANES] for c in range(ROW_TILE)]
    for c in range(ROW_TILE):
        o_ref[0, :, c * LANES:(c + 1) * LANES] = cols[c]


def _combine(x, ys, dest0, dest1, rw, mod4, final_g, final, layer):
    b, s, d = x.shape
    tm = min(TM_CMB, s)
    row = lambda i, j: (i, j, 0)
    idx_spec = pl.BlockSpec((1, 1, tm), lambda i, j: (i * (s // tm) + j, 0, 0), memory_space=pltpu.SMEM)
    return pl.pallas_call(
        functools.partial(_combine_kernel, final=final),
        grid=(b, s // tm),
        in_specs=[idx_spec, idx_spec,
                  pl.BlockSpec((1, tm, d), row),
                  pl.BlockSpec(memory_space=pl.ANY),
                  pl.BlockSpec((1, tm, LANES), row),
                  _mod_spec(layer, 5, d),
                  pl.BlockSpec((1, d), lambda i, j: (0, 0))],
        out_specs=pl.BlockSpec((1, tm, d), row),
        out_shape=jax.ShapeDtypeStruct((b, s, d), F32),
        scratch_shapes=[pltpu.VMEM((tm * ROW_TILE, LANES), F32), pltpu.VMEM((tm * ROW_TILE, LANES), F32),
                        pltpu.SemaphoreType.DMA((2,))],
        compiler_params=_cp(2),
        name="moe_combine_final" if final else "moe_combine",
    )(dest0.reshape(-1, 1, tm), dest1.reshape(-1, 1, tm), x, ys, rw, mod4, final_g.reshape(1, d))


def _layout_w_in(w_in):
    d = w_in.shape[0]
    kvw = NSA_KV * HEAD_DIM
    sizes = (NSA_HEADS * HEAD_DIM, kvw, kvw, kvw, kvw, kvw, kvw, NSA_HEADS * 3,
             FOX_HEADS * HEAD_DIM, FOX_HEADS * HEAD_DIM, FOX_HEADS * HEAD_DIM, FOX_HEADS,
             SB_HEADS * HEAD_DIM, SB_HEADS * HEAD_DIM, SB_HEADS * HEAD_DIM)
    pts = np.cumsum(sizes)[:-1].tolist()
    (qa, kca, vca, ksa, vsa, kwa, vwa, ga, qb, kb, vb, fb, qc, kc, vc) = jnp.split(w_in, pts, axis=1)
    scale = HEAD_DIM ** -0.5 * LOG2E
    main = jnp.concatenate([qa * scale, ksa, kwa, vsa, vwa, qb * scale, kb, vb, qc * scale, kc, vc],
                           axis=1).astype(BF16)
    cmp_w = jnp.concatenate([kca, vca], axis=1).astype(BF16)
    zpad = lambda n: jnp.zeros((d, n), w_in.dtype)
    per_grp = NSA_GROUP * 3
    small = jnp.concatenate([ga[:, :per_grp], zpad(LANES - per_grp), ga[:, per_grp:], zpad(LANES - per_grp),
                             fb[:, 0:2], zpad(LANES - 2), fb[:, 2:4], zpad(LANES - 2)], axis=1).astype(BF16)
    return main, cmp_w, small


def _layout_cmp(w1, w2):
    hid = w1.shape[1]
    w1r = w1.reshape(2, CMP_STRIDE, HEAD_DIM, hid)
    z = jnp.zeros((CMP_STRIDE, HEAD_DIM, hid), w1.dtype)
    cols = []
    for half in range(2):
        for g in range(NSA_KV):
            parts = [w1r[half] if gg == g else z for gg in range(NSA_KV)]
            cols.append(jnp.concatenate(parts, axis=1).reshape(CMP_STRIDE * LANES, hid))
    wcat = jnp.concatenate(cols, axis=1).astype(BF16)
    zz = jnp.zeros_like(w2)
    w2bd = jnp.concatenate([jnp.concatenate([w2, zz], axis=1),
                            jnp.concatenate([zz, w2], axis=1)], axis=0).astype(BF16)
    return wcat, w2bd


def _rope_tables(pos):
    inv = jnp.exp(jnp.arange(ROPE_HALF, dtype=F32) * (-2.0 * math.log(ROPE_THETA) / ROPE_DIM))
    ang = pos.astype(F32)[:, None] * inv[None, :]
    cos, sin = jnp.cos(ang), jnp.sin(ang)
    n = pos.shape[0]
    z8 = jnp.zeros((n, ROPE_HALF), F32)
    rest1 = jnp.ones((n, HEAD_DIM - ROPE_DIM), F32)
    rest0 = jnp.zeros((n, HEAD_DIM - ROPE_DIM), F32)
    c = jnp.concatenate([cos, cos, rest1], axis=1)
    s1 = jnp.concatenate([z8, sin, rest0], axis=1)
    s2 = jnp.concatenate([-sin, z8, rest0], axis=1)
    dup = lambda a: jnp.concatenate([a, a], axis=1)
    return dup(c), dup(s1), dup(s2)


def _static_tables(s):
    n_cmp_pad = s // CMP_STRIDE
    n = np.arange(n_cmp_pad)[:, None]
    j = np.arange(LANES)[None, :]
    n_sel = s // SEL_LEN
    cover = ((n * CMP_STRIDE < j * SEL_LEN + SEL_LEN) & (n * CMP_STRIDE + CMP_LEN > j * SEL_LEN)
             & (j < n_sel)).astype(np.float32)
    nt = s // TK_FOX
    key = np.arange(nt)[:, None, None] * TK_FOX + np.arange(TK_FOX)[None, None, :]
    expand = (key // SEL_LEN == np.arange(LANES)[None, :, None]).astype(np.float32)
    return jnp.asarray(cover.T, BF16), jnp.asarray(expand, BF16)


def _dispatch_plan(ri, t):
    eid = ri.reshape(t, LANES)[:, :2].reshape(-1)
    n_assign = eid.shape[0]
    onehot = (eid[:, None] == jnp.arange(N_EXPERTS, dtype=jnp.int32)[None, :]).astype(jnp.int32)
    csum = jnp.cumsum(onehot, axis=0)
    counts = csum[-1]
    rank = jnp.take_along_axis(csum, eid[:, None], axis=1)[:, 0] - 1
    padded = ((counts + TM_EXP - 1) // TM_EXP) * TM_EXP
    pends = jnp.cumsum(padded)
    pstarts = pends - padded
    dest = pstarts[eid] + rank
    n_tiles = -(-(n_assign + N_EXPERTS * (TM_EXP - 1)) // TM_EXP)
    tile_start = jnp.arange(n_tiles, dtype=jnp.int32) * TM_EXP
    tile_expert = jnp.minimum(jnp.sum((pends[None, :] <= tile_start[:, None]).astype(jnp.int32), axis=1),
                              N_EXPERTS - 1).astype(jnp.int32)
    n_used = (pends[-1] // TM_EXP).astype(jnp.int32).reshape(1)
    return dest.astype(jnp.int32), n_tiles * TM_EXP, tile_expert, n_used


def kernel(x, c, norm1_g, norm2_g, ada_w, ada_b, w_in, b_forget, cmp_pos_k, cmp_w1_k, cmp_w2_k,
           cmp_pos_v, cmp_w1_v, cmp_w2_v, out_norm_g, w_out, router_group_w, router_group_b,
           router_expert_w, router_expert_b, expert_w1, expert_w3, expert_w2, final_g):
    b, s, d = x.shape
    depth = ada_w.shape[0]
    t = b * s
    mod = _modulation(c, ada_w, ada_b)
    rope_c, rope_1, rope_2 = _rope_tables(jnp.arange(s))
    n_cmp_pad = s // CMP_STRIDE
    crc, cr1, cr2 = _rope_tables(jnp.arange(n_cmp_pad) * CMP_STRIDE + (CMP_LEN - 1))
    cover, expand = _static_tables(s)
    xs = None

    mod4 = mod.reshape(depth, b, 6, 1, d)
    w_main, w_cmp, w_small = jax.vmap(_layout_w_in)(w_in)
    b_pairs = jnp.pad(b_forget.reshape(depth, 2, 1, 2), ((0, 0), (0, 0), (0, 0), (0, LANES - 2)))
    wk, w2k = jax.vmap(_layout_cmp)(cmp_w1_k, cmp_w2_k)
    wv, w2v = jax.vmap(_layout_cmp)(cmp_w1_v, cmp_w2_v)
    pek = jnp.broadcast_to(cmp_pos_k.reshape(depth, 1, -1), (depth, 8, CMP_LEN * HEAD_DIM))
    pev = jnp.broadcast_to(cmp_pos_v.reshape(depth, 1, -1), (depth, 8, CMP_LEN * HEAD_DIM))
    gn = out_norm_g.reshape(depth, 1, -1)
    wr = jnp.concatenate([router_group_w, router_expert_w,
                          jnp.zeros((depth, d, LANES - N_GROUPS - N_EXPERTS), F32)], axis=2)
    wr_hi = wr.astype(BF16)
    wr_lo = (wr - wr_hi.astype(F32)).astype(BF16)
    br = jnp.concatenate([router_group_b, router_expert_b,
                          jnp.zeros((depth, LANES - N_GROUPS - N_EXPERTS), F32)], axis=1).reshape(depth, 1, LANES)
    w_out_b = w_out.astype(BF16)

    for l in range(depth):
        u, kc, vc, small = _in_projection(x, norm1_g, mod4, w_main, w_cmp, w_small, rope_c, rope_1, rope_2, l)
        cq, ckt = _forget_cumsum(small, b_pairs, l)
        ck, cv = _compress(kc, vc, wk[l], wv[l], cmp_w1_k[l], cmp_w1_v[l], pek[l], pev[l],
                           w2k[l], w2v[l], crc, cr1, cr2)
        o_a = _nsa_attention(u, ck, cv, small, gn, cover, expand, l)
        o_b = _fox_attention(u, cq, ckt, gn, l)
        o_c = _sb_attention(u, gn, l)
        x, h2, rw, ri = _out_projection(o_a, o_b, o_c, x, w_out_b, mod4, norm2_g, wr_hi, wr_lo, br, l)

        dest, p_rows, tile_expert, n_used = _dispatch_plan(ri, t)
        xs = _dispatch(h2, dest, p_rows, xs)
        ys = _expert_mlp(xs, tile_expert, n_used, expert_w1, expert_w3, expert_w2, l)
        x = _combine(x, ys, dest[0::2], dest[1::2], rw, mod4, final_g, final=(l == depth - 1), layer=l)
    return x
```

```python
import functools
import math

import numpy as np
import jax
import jax.numpy as jnp
from jax import lax
from jax.experimental import pallas as pl
from jax.experimental.pallas import tpu as pltpu

F32 = jnp.float32
BF16 = jnp.bfloat16

HEAD_DIM = 64
LANES = 128
NSA_HEADS = 8
NSA_KV = 2
NSA_GROUP = 4
FOX_HEADS = 4
SB_HEADS = 4
ROPE_DIM = 16
ROPE_HALF = 8
ROPE_THETA = 500000.0
CMP_LEN = 32
CMP_STRIDE = 16
CMP_HIDDEN = 128
SEL_LEN = 64
SEL_TOPN = 16
WINDOW = 512
FORCE_SCORE = 1.0e4
N_GROUPS = 4
EXPERTS_PER_GROUP = 8
N_EXPERTS = 32
EPS = 1e-6
LOG2E = math.log2(math.e)
NEG = -1e30

COL_QA = 0
COL_KS = 1024
COL_KW = 1152
COL_VS = 1280
COL_VW = 1408
COL_FOX = 1536
COL_SB = 2560
N_MAIN = 3584
N_SMALL = 512
W_ROPE = (NSA_HEADS + 2 * NSA_KV) * HEAD_DIM
W_MAIN = W_ROPE + (2 * NSA_KV + 3 * FOX_HEADS + 3 * SB_HEADS) * HEAD_DIM

TM_PROJ = 512
TQ_NSA = 128
TK_ATT = 256
TQ_PAIR = 256
TK_FOX = 512
ROWS = 32
SB_SUBS = 2
TM_OUT = 512
TM_EXP = 512
TM_CMB = 512
COPY_CHUNK = 512
ROW_TILE = 8
VMEM_LIMIT = 56 * 1024 * 1024


def _cp(n_axes, vmem=VMEM_LIMIT):
    return pltpu.CompilerParams(dimension_semantics=("arbitrary",) * n_axes, vmem_limit_bytes=vmem)


def _dot(a, b):
    return jnp.dot(a, b, preferred_element_type=F32)


def _dot_nt(a, b):
    return lax.dot_general(a, b, (((1,), (1,)), ((), ())), preferred_element_type=F32)


def _split_bf16(x, parts):
    out = []
    r = x
    for _ in range(parts):
        p = r.astype(BF16)
        out.append(p)
        r = r - p.astype(F32)
    return out


def _rope(x, c, s1, s2):
    return x * c + pltpu.roll(x, ROPE_HALF, 1) * s1 + pltpu.roll(x, LANES - ROPE_HALF, 1) * s2


def _softplus(z):
    return jnp.maximum(z, 0.0) + jnp.log(1.0 + jnp.exp(-jnp.abs(z)))


def _mod_kernel(c_ref, w_ref, b_ref, o_ref):
    c = c_ref[...]
    cond = c * (1.0 / (1.0 + jnp.exp(-c)))
    o_ref[0] = _dot(cond, w_ref[0]) + b_ref[0]


def _modulation(c, ada_w, ada_b):
    depth, d, n = ada_w.shape
    b = c.shape[0]
    tn = 1024
    return pl.pallas_call(
        _mod_kernel,
        grid=(depth, n // tn),
        in_specs=[pl.BlockSpec((b, d), lambda l, j: (0, 0)),
                  pl.BlockSpec((1, d, tn), lambda l, j: (l, 0, j)),
                  pl.BlockSpec((1, 1, tn), lambda l, j: (l, 0, j))],
        out_specs=pl.BlockSpec((1, b, tn), lambda l, j: (l, 0, j)),
        out_shape=jax.ShapeDtypeStruct((depth, b, n), F32),
        compiler_params=_cp(2),
        name="modulation",
    )(c, ada_w, ada_b.reshape(depth, 1, n))


def _inproj_kernel(x_ref, g_ref, sc_ref, sh_ref, w_ref, wc_ref, ws_ref, rc_ref, r1_ref, r2_ref,
                   u_ref, kc_ref, vc_ref, sm_ref):
    x = x_ref[0]
    ms = jnp.mean(x * x, axis=-1, keepdims=True)
    h = (x * lax.rsqrt(ms + EPS) * g_ref[...]) * (1.0 + sc_ref[0]) + sh_ref[0]
    hb = h.astype(BF16)
    rc, r1, r2 = rc_ref[...], r1_ref[...], r2_ref[...]
    lane = lax.broadcasted_iota(jnp.int32, (x.shape[0], LANES), 1)
    left = lane < HEAD_DIM

    def put(col, val):
        u_ref[0, :, col:col + LANES] = val.astype(BF16)

    def put_pair(col, blk, offsets):
        for i, off in enumerate(offsets):
            src = blk if off == i * HEAD_DIM else pltpu.roll(blk, HEAD_DIM, 1)
            put(col + i * LANES, jnp.where(left if off == 0 else ~left, src, 0.0))

    acc = _dot(hb, w_ref[:, :W_ROPE])
    for j in range(NSA_HEADS // 2):
        off = (2 * j // NSA_GROUP) * HEAD_DIM
        put_pair(COL_QA + 2 * j * LANES, _rope(acc[:, j * LANES:(j + 1) * LANES], rc, r1, r2), (off, off))
    nq = NSA_HEADS // 2
    put(COL_KS, _rope(acc[:, nq * LANES:(nq + 1) * LANES], rc, r1, r2))
    put(COL_KW, _rope(acc[:, (nq + 1) * LANES:(nq + 2) * LANES], rc, r1, r2))
    acc = _dot(hb, w_ref[:, W_ROPE:])
    u_ref[0, :, COL_VS:COL_VS + 2 * LANES] = acc[:, :2 * LANES].astype(BF16)
    c = 2 * LANES
    for base, heads in ((COL_FOX, FOX_HEADS), (COL_SB, SB_HEADS)):
        for j in range(heads // 2):
            put_pair(base + 2 * j * LANES, acc[:, c:c + LANES], (0, HEAD_DIM))
            c += LANES
        kv = 2 * heads * HEAD_DIM
        u_ref[0, :, base + heads * LANES:base + heads * LANES + kv] = acc[:, c:c + kv].astype(BF16)
        c += kv
    cmp_in = _dot(hb, wc_ref[...])
    kc_ref[0] = cmp_in[:, :LANES].astype(BF16)
    vc_ref[0] = cmp_in[:, LANES:].astype(BF16)
    sm_ref[0] = _dot(hb, ws_ref[...])


def _mod_spec(layer, which, d):
    return pl.BlockSpec((None, 1, None, 1, d), lambda i, j: (layer, i, which, 0, 0))


def _layer_spec(layer, shape):
    zeros = (0,) * len(shape)
    return pl.BlockSpec((None,) + tuple(shape), lambda *_: (layer,) + zeros)


def _in_projection(x, g, mod4, w_main, w_cmp, w_small, rope_c, rope_1, rope_2, layer):
    b, s, d = x.shape
    tm = min(TM_PROJ, s)
    row = lambda i, j: (i, j, 0)
    seq = lambda i, j: (j, 0)
    return pl.pallas_call(
        _inproj_kernel,
        grid=(b, s // tm),
        in_specs=[pl.BlockSpec((1, tm, d), row),
                  _layer_spec(layer, (1, d)),
                  _mod_spec(layer, 1, d),
                  _mod_spec(layer, 0, d),
                  _layer_spec(layer, (d, W_MAIN)),
                  _layer_spec(layer, (d, 2 * LANES)),
                  _layer_spec(layer, (d, N_SMALL)),
                  pl.BlockSpec((tm, LANES), seq),
                  pl.BlockSpec((tm, LANES), seq),
                  pl.BlockSpec((tm, LANES), seq)],
        out_specs=[pl.BlockSpec((1, tm, N_MAIN), row),
                   pl.BlockSpec((1, tm, LANES), row),
                   pl.BlockSpec((1, tm, LANES), row),
                   pl.BlockSpec((1, tm, N_SMALL), row)],
        out_shape=[jax.ShapeDtypeStruct((b, s, N_MAIN), BF16),
                   jax.ShapeDtypeStruct((b, s, LANES), BF16),
                   jax.ShapeDtypeStruct((b, s, LANES), BF16),
                   jax.ShapeDtypeStruct((b, s, N_SMALL), F32)],
        compiler_params=_cp(2),
        name="in_projection",
    )(x, g.reshape(-1, 1, d), mod4, mod4, w_main, w_cmp, w_small, rope_c, rope_1, rope_2)


def _cumf_kernel(f_ref, b_ref, cq_ref, ckt_ref):
    n_chunks = f_ref.shape[1] // TK_ATT
    r = lax.broadcasted_iota(jnp.int32, (TK_ATT, TK_ATT), 0)
    c = lax.broadcasted_iota(jnp.int32, (TK_ATT, TK_ATT), 1)
    tri = jnp.where(c <= r, 1.0, 0.0).astype(BF16)
    carry = jnp.zeros((1, LANES), F32)
    for j in range(n_chunks):
        f = f_ref[0, j * TK_ATT:(j + 1) * TK_ATT, :] + b_ref[0]
        ls = -_softplus(-f)
        acc = None
        for p in _split_bf16(ls, 3):
            dd = _dot(tri, p)
            acc = dd if acc is None else acc + dd
        cs = acc + carry
        cs2 = cs * LOG2E
        cq_ref[0, j * TK_ATT:(j + 1) * TK_ATT, :] = cs2
        ckt_ref[0, 0, j] = cs2.T[:ROW_TILE, :]
        carry = cs[TK_ATT - 1:TK_ATT, :]


def _forget_cumsum(small, b_pairs, layer):
    b, s, _ = small.shape
    return pl.pallas_call(
        _cumf_kernel,
        grid=(b, 2),
        in_specs=[pl.BlockSpec((1, s, LANES), lambda i, p: (i, 0, 2 + p)),
                  pl.BlockSpec((None, 1, 1, LANES), lambda i, p: (layer, p, 0, 0))],
        out_specs=[pl.BlockSpec((1, s, LANES), lambda i, p: (i, 0, p)),
                   pl.BlockSpec((1, 1, s // TK_ATT, ROW_TILE, TK_ATT), lambda i, p: (i, p, 0, 0, 0))],
        out_shape=[jax.ShapeDtypeStruct((b, s, 2 * LANES), F32),
                   jax.ShapeDtypeStruct((b, 2, s // TK_ATT, ROW_TILE, TK_ATT), F32)],
        compiler_params=_cp(2),
        name="forget_cumsum",
    )(small, b_pairs)


def _compress_kernel(ks_ref, vs_ref, wk_ref, wv_ref, w1k_ref, w1v_ref, pek_ref, pev_ref,
                     w2k_ref, w2v_ref, rc_ref, r1_ref, r2_ref, ck_ref, cv_ref):
    def one(seg_ref, w_ref, w1_ref, pe_ref, w2_ref):
        p = _dot(seg_ref[0], w_ref[...])
        half = 2 * CMP_HIDDEN
        bias = _dot(pe_ref[...].astype(BF16), w1_ref[...].astype(BF16))[0:1, :]
        bias2 = jnp.concatenate([bias, bias], axis=1)
        n = p.shape[0]
        hid = p[:, :half] + pltpu.roll(p[:, half:], n - 1, 0) + bias2
        act = hid * (1.0 / (1.0 + jnp.exp(-hid)))
        return _dot(act.astype(BF16), w2_ref[...])

    ck = one(ks_ref, wk_ref, w1k_ref, pek_ref, w2k_ref)
    ck_ref[0] = _rope(ck, rc_ref[...], r1_ref[...], r2_ref[...]).astype(BF16)
    cv_ref[0] = one(vs_ref, wv_ref, w1v_ref, pev_ref, w2v_ref).astype(BF16)


def _compress(kc, vc, wk, wv, w1k, w1v, pek, pev, w2k, w2v, rc, r1, r2):
    b, s, _ = kc.shape
    n = s // CMP_STRIDE
    width = CMP_STRIDE * LANES
    kseg = kc.reshape(b, n, width)
    vseg = vc.reshape(b, n, width)
    seg = pl.BlockSpec((1, n, width), lambda i: (i, 0, 0))
    full = lambda a: pl.BlockSpec(a.shape, lambda i: (0,) * a.ndim)
    return pl.pallas_call(
        _compress_kernel,
        grid=(b,),
        in_specs=[seg, seg, full(wk), full(wv), full(w1k), full(w1v), full(pek), full(pev),
                  full(w2k), full(w2v), full(rc), full(r1), full(r2)],
        out_specs=[pl.BlockSpec((1, n, LANES), lambda i: (i, 0, 0)),
                   pl.BlockSpec((1, n, LANES), lambda i: (i, 0, 0))],
        out_shape=[jax.ShapeDtypeStruct((b, n, LANES), BF16),
                   jax.ShapeDtypeStruct((b, n, LANES), BF16)],
        compiler_params=_cp(1),
        name="nsa_compress",
    )(kseg, vseg, wk, wv, w1k, w1v, pek, pev, w2k, w2v, rc, r1, r2)


def _nsa_kernel(q_ref, ck_ref, cv_ref, ks_ref, vs_ref, kw_ref, vw_ref, gate_ref, gn_ref,
                cover_ref, expand_ref, o_ref, m_ref, l_ref, acc_ref):
    tq = q_ref.shape[1]
    g = pl.program_id(1)
    t0 = pl.program_id(2) * tq
    q = q_ref[0]
    qall = jnp.concatenate([q[:, h * LANES:(h + 1) * LANES] for h in range(NSA_GROUP)], axis=0)
    tpos = t0 + lax.broadcasted_iota(jnp.int32, (tq, 1), 0)
    lane = lax.broadcasted_iota(jnp.int32, (tq, LANES), 1)

    n_cmp = ck_ref.shape[1]
    s_t = _dot_nt(ck_ref[0], qall)
    valid = ((CMP_STRIDE * lax.broadcasted_iota(jnp.int32, (n_cmp, tq), 0) + (CMP_LEN - 1))
             <= t0 + lax.broadcasted_iota(jnp.int32, (n_cmp, tq), 1))
    p_cols, p_sum_t = [], None
    for h in range(NSA_GROUP):
        sm = jnp.where(valid, s_t[:, h * tq:(h + 1) * tq], NEG)
        e = jnp.where(valid, jnp.exp2(sm - jnp.max(sm, axis=0, keepdims=True)), 0.0)
        den = jnp.sum(e, axis=0, keepdims=True)
        p = e * jnp.where(den > 0.0, 1.0 / den, 0.0)
        p_cols.append(p.astype(BF16))
        p_sum_t = p if p_sum_t is None else p_sum_t + p
    o_cmp = lax.dot_general(jnp.concatenate(p_cols, axis=1), cv_ref[0], (((0,), (0,)), ((), ())),
                            preferred_element_type=F32).reshape(NSA_GROUP, tq, LANES)

    n_sel = expand_ref.shape[0] * (TK_FOX // SEL_LEN)
    imp_t = None
    for piece in _split_bf16(p_sum_t, 3):
        d_imp = _dot(cover_ref[...], piece)
        imp_t = d_imp if imp_t is None else imp_t + d_imp
    imp_t = imp_t[:n_sel, :]
    blk = lax.broadcasted_iota(jnp.int32, (n_sel, tq), 0)
    tcol = t0 + lax.broadcasted_iota(jnp.int32, (n_sel, tq), 1)
    cur = jnp.right_shift(tcol, int(math.log2(SEL_LEN)))
    forced = (blk == 0) | (blk == cur) | (blk == cur - 1)
    score = jnp.where(forced, FORCE_SCORE, jnp.where(blk * SEL_LEN <= tcol, imp_t, -1.0))
    cnt = jnp.zeros((n_sel, tq), F32)
    for j in range(n_sel):
        sj = score[j:j + 1, :]
        beats = (sj > score) | ((sj == score) & (blk > j))
        cnt = cnt + jnp.where(beats, 1.0, 0.0)
    sel_t = jnp.where(cnt < float(min(SEL_TOPN, n_sel)), 1.0, 0.0)
    sel = jnp.concatenate([sel_t, jnp.zeros((LANES - n_sel, tq), F32)], axis=0).T.astype(BF16)

    def biased(bias):
        def adjust(h, t_off, cols):
            return [cols[kk] + bias[t_off:t_off + ROWS, kk * LANES:(kk + 1) * LANES]
                    for kk in range(len(cols))]
        return adjust

    kcol_s = lax.broadcasted_iota(jnp.int32, (tq, TK_FOX), 1)

    def sel_tile(kt, carry):
        k0 = pl.multiple_of(kt * TK_FOX, TK_FOX)
        hit = _dot(sel, expand_ref[kt])
        ok = (hit > 0.5) & ((k0 + kcol_s) <= tpos)
        s_t = _dot_nt(qall, ks_ref[0, pl.ds(k0, TK_FOX), :])
        _softmax_tile(s_t, vs_ref[0, pl.ds(k0, TK_FOX), :], m_ref, l_ref, acc_ref, NSA_GROUP, tq,
                      biased(jnp.where(ok, 0.0, NEG)))
        return carry

    _softmax_init(m_ref, l_ref, acc_ref)
    lax.fori_loop(0, lax.div(t0 + tq - 1, TK_FOX) + 1, sel_tile, 0)
    o_sel = _softmax_result(l_ref, acc_ref).reshape(NSA_GROUP, tq, LANES)

    span = WINDOW + tq
    w0 = pl.multiple_of(jnp.maximum(t0 - WINDOW, 0), LANES)
    kp = w0 + lax.broadcasted_iota(jnp.int32, (tq, span), 1)
    bias_w = jnp.where((kp <= tpos) & (kp > tpos - WINDOW), 0.0, NEG)
    s_w = _dot_nt(qall, kw_ref[0, pl.ds(w0, span), :])
    nkw = span // LANES
    p_rows, inv_rows = [], []
    for h in range(NSA_GROUP):
        for c in range(tq // ROWS):
            r0 = h * tq + c * ROWS
            cols = [s_w[r0:r0 + ROWS, kk * LANES:(kk + 1) * LANES]
                    + bias_w[c * ROWS:(c + 1) * ROWS, kk * LANES:(kk + 1) * LANES] for kk in range(nkw)]
            mx = cols[0]
            for kk in range(1, nkw):
                mx = jnp.maximum(mx, cols[kk])
            mx = jnp.max(mx, axis=-1, keepdims=True)
            pks = [jnp.exp2(cols[kk] - mx) for kk in range(nkw)]
            psum = pks[0]
            for kk in range(1, nkw):
                psum = psum + pks[kk]
            inv_rows.append(jnp.broadcast_to(1.0 / jnp.sum(psum, axis=-1, keepdims=True), (ROWS, LANES)))
            p_rows.append(jnp.concatenate([pk.astype(BF16) for pk in pks], axis=1))
    o_win = _dot(jnp.concatenate(p_rows, axis=0), vw_ref[0, pl.ds(w0, span), :])
    o_win = (o_win * jnp.concatenate(inv_rows, axis=0)).reshape(NSA_GROUP, tq, LANES)

    gt = gate_ref[0]
    gt = 1.0 / (1.0 + jnp.exp(-gt))
    mine = (lane >= g * HEAD_DIM) & (lane < (g + 1) * HEAD_DIM)
    outs = []
    for h in range(NSA_GROUP):
        o = (gt[:, 3 * h:3 * h + 1] * o_cmp[h] + gt[:, 3 * h + 1:3 * h + 2] * o_sel[h]
             + gt[:, 3 * h + 2:3 * h + 3] * o_win[h])
        o = jnp.where(mine, o, 0.0)
        ms = jnp.sum(o * o, axis=-1, keepdims=True) * (1.0 / HEAD_DIM)
        o = o * lax.rsqrt(ms + EPS)
        outs.append(o + pltpu.roll(o, HEAD_DIM, 1))
    left = lane < HEAD_DIM
    o_ref[0, :, :LANES] = (jnp.where(left, outs[0], outs[1]) * gn_ref[:, :LANES]).astype(o_ref.dtype)
    o_ref[0, :, LANES:] = (jnp.where(left, outs[2], outs[3]) * gn_ref[:, LANES:]).astype(o_ref.dtype)


def _nsa_attention(u, ck, cv, small, gn, cover, expand, layer):
    b, s, _ = u.shape
    tq = min(TQ_NSA, s)
    assert s >= WINDOW + tq and WINDOW % tq == 0
    n_cmp = ck.shape[1]
    blk = LANES
    kv = lambda col: pl.BlockSpec((1, s, LANES), lambda i, g, j, col=col: (i, 0, col // blk))
    return pl.pallas_call(
        _nsa_kernel,
        grid=(b, NSA_KV, s // tq),
        in_specs=[pl.BlockSpec((1, tq, NSA_GROUP * LANES), lambda i, g, j: (i, j, g)),
                  pl.BlockSpec((1, n_cmp, LANES), lambda i, g, j: (i, 0, 0)),
                  pl.BlockSpec((1, n_cmp, LANES), lambda i, g, j: (i, 0, 0)),
                  kv(COL_KS), kv(COL_VS), kv(COL_KW), kv(COL_VW),
                  pl.BlockSpec((1, tq, LANES), lambda i, g, j: (i, j, g)),
                  pl.BlockSpec((None, 1, 2 * LANES), lambda i, g, j: (layer, 0, g)),
                  pl.BlockSpec(cover.shape, lambda i, g, j: (0, 0)),
                  pl.BlockSpec(expand.shape, lambda i, g, j: (0, 0, 0))],
        out_specs=pl.BlockSpec((1, tq, 2 * LANES), lambda i, g, j: (i, j, g)),
        out_shape=jax.ShapeDtypeStruct((b, s, NSA_HEADS * HEAD_DIM), BF16),
        scratch_shapes=_softmax_scratch(NSA_GROUP * tq),
        compiler_params=_cp(3),
        name="nsa_attention",
    )(u, ck, cv, u, u, u, u, small, gn, cover, expand)


def _pair_finish(acc, gn_ref, o_ref, tq):
    lane = lax.broadcasted_iota(jnp.int32, (tq, LANES), 1)
    left = lane < HEAD_DIM
    o = jnp.where(left, acc[0], acc[1])
    o2 = o * o
    ms_l = jnp.sum(jnp.where(left, o2, 0.0), axis=-1, keepdims=True) * (1.0 / HEAD_DIM)
    ms_r = jnp.sum(jnp.where(left, 0.0, o2), axis=-1, keepdims=True) * (1.0 / HEAD_DIM)
    inv = jnp.where(left, lax.rsqrt(ms_l + EPS), lax.rsqrt(ms_r + EPS))
    o_ref[0] = (o * inv * gn_ref[...]).astype(o_ref.dtype)


def _softmax_tile(s, v, m_ref, l_ref, acc_ref, heads, tq, adjust, p_ref=None):
    nk = s.shape[1] // LANES
    p_rows = []
    for h in range(heads):
        for c in range(tq // ROWS):
            t_off = c * ROWS
            r0 = h * tq + t_off
            cols = [s[r0:r0 + ROWS, k * LANES:(k + 1) * LANES] for k in range(nk)]
            cols = adjust(h, t_off, cols)
            mx = cols[0]
            for k in range(1, nk):
                mx = jnp.maximum(mx, cols[k])
            m_old = m_ref[r0:r0 + ROWS, :]
            m_new = jnp.maximum(m_old, jnp.max(mx, axis=-1, keepdims=True))
            alpha = jnp.exp2(m_old - m_new)
            pks = [jnp.exp2(cols[k] - m_new) for k in range(nk)]
            psum = pks[0]
            for k in range(1, nk):
                psum = psum + pks[k]
            p_chunk = jnp.concatenate([pk.astype(BF16) for pk in pks], axis=1)
            if p_ref is None:
                p_rows.append(p_chunk)
            else:
                p_ref[r0:r0 + ROWS, :] = p_chunk
            l_ref[r0:r0 + ROWS, :] = alpha * l_ref[r0:r0 + ROWS, :] + psum
            acc_ref[r0:r0 + ROWS, :] = alpha * acc_ref[r0:r0 + ROWS, :]
            m_ref[r0:r0 + ROWS, :] = m_new
    if p_ref is None:
        acc_ref[...] += _dot(jnp.concatenate(p_rows, axis=0), v)


def _tile_loop(score_fn, process_fn, lo, hi):
    def body(kt, carry):
        process_fn(kt, score_fn(kt))
        return carry

    lax.fori_loop(lo, hi - 1, body, 0)
    return score_fn(hi - 1)


def _softmax_init(m_ref, l_ref, acc_ref):
    m_ref[...] = jnp.full(m_ref.shape, NEG, F32)
    l_ref[...] = jnp.zeros(l_ref.shape, F32)
    acc_ref[...] = jnp.zeros(acc_ref.shape, F32)


def _softmax_result(l_ref, acc_ref):
    return acc_ref[...] / jnp.sum(l_ref[...], axis=-1, keepdims=True)


def _softmax_scratch(rows):
    return [pltpu.VMEM((rows, LANES), F32), pltpu.VMEM((rows, LANES), F32), pltpu.VMEM((rows, LANES), F32)]


def _fox_kernel(q_ref, k_ref, v_ref, cq_ref, ckt_ref, gn_ref, o_ref,
                m_ref, l_ref, acc_ref, cqr_ref, p_ref):
    tq = q_ref.shape[1]
    tk = TK_FOX
    t0 = pl.program_id(2) * tq
    q = q_ref[0]
    qall = jnp.concatenate([q[:, :LANES], q[:, LANES:]], axis=0)
    cq = cq_ref[0]
    cqr_ref[0:tq, :] = jnp.broadcast_to(cq[:, 0:1], (tq, LANES))
    cqr_ref[tq:2 * tq, :] = jnp.broadcast_to(cq[:, 1:2], (tq, LANES))
    _softmax_init(m_ref, l_ref, acc_ref)
    n_tiles = lax.div(t0 + tq - 1, tk) + 1
    diag = (lax.broadcasted_iota(jnp.int32, (ROWS, LANES), 1)
            - lax.broadcasted_iota(jnp.int32, (ROWS, LANES), 0))

    def scores(kt):
        return _dot_nt(qall, k_ref[0, pl.ds(pl.multiple_of(kt * tk, tk), tk), :])

    def tile(kt, s, masked):
        k0 = pl.multiple_of(kt * tk, tk)
        cks = [ckt_ref[0, 0, kt * (tk // TK_ATT) + j] for j in range(tk // TK_ATT)]

        def adjust(h, t_off, cols):
            out = []
            cqr = cqr_ref[h * tq + t_off:h * tq + t_off + ROWS, :]
            for kk in range(tk // LANES):
                lo = (kk * LANES) % TK_ATT
                ck = cks[(kk * LANES) // TK_ATT][h:h + 1, lo:lo + LANES]
                val = (cols[kk] - ck) + cqr
                if masked:
                    val = jnp.where(diag <= (t0 + t_off) - (k0 + kk * LANES), val, NEG)
                out.append(val)
            return out

        kp = pl.multiple_of(jnp.maximum(kt - 1, 0) * tk, tk)
        acc_ref[...] += _dot(p_ref[...], v_ref[0, pl.ds(kp, tk), :])
        _softmax_tile(s, None, m_ref, l_ref, acc_ref, 2, tq, adjust, p_ref=p_ref)

    p_ref[...] = jnp.zeros(p_ref.shape, BF16)
    s_last = _tile_loop(scores, lambda kt, s: tile(kt, s, False), 0, n_tiles)
    tile(n_tiles - 1, s_last, True)
    acc_ref[...] += _dot(p_ref[...], v_ref[0, pl.ds(pl.multiple_of((n_tiles - 1) * tk, tk), tk), :])
    acc = _softmax_result(l_ref, acc_ref).reshape(2, tq, LANES)
    _pair_finish(acc, gn_ref, o_ref, tq)


def _fox_attention(u, cq, ckt, gn, layer):
    b, s, _ = u.shape
    tq = min(TQ_PAIR, s)
    gb = NSA_HEADS * HEAD_DIM // LANES
    qb = COL_FOX // (2 * LANES)
    kb = (COL_FOX + FOX_HEADS * LANES) // LANES
    vb = kb + 2
    return pl.pallas_call(
        _fox_kernel,
        grid=(b, 2, s // tq),
        in_specs=[pl.BlockSpec((1, tq, 2 * LANES), lambda i, p, j: (i, j, qb + p)),
                  pl.BlockSpec((1, s, LANES), lambda i, p, j: (i, 0, kb + p)),
                  pl.BlockSpec((1, s, LANES), lambda i, p, j: (i, 0, vb + p)),
                  pl.BlockSpec((1, tq, LANES), lambda i, p, j: (i, j, p)),
                  pl.BlockSpec((1, 1, s // TK_ATT, ROW_TILE, TK_ATT), lambda i, p, j: (i, p, 0, 0, 0)),
                  pl.BlockSpec((None, 1, LANES), lambda i, p, j: (layer, 0, gb + p))],
        out_specs=pl.BlockSpec((1, tq, LANES), lambda i, p, j: (i, j, p)),
        out_shape=jax.ShapeDtypeStruct((b, s, FOX_HEADS * HEAD_DIM), BF16),
        scratch_shapes=_softmax_scratch(2 * tq) + [pltpu.VMEM((2 * tq, LANES), F32),
                                                   pltpu.VMEM((2 * tq, TK_FOX), BF16)],
        compiler_params=_cp(3),
        name="fox_attention",
    )(u, u, u, cq, ckt, gn)


def _sb_kernel(q_ref, k_ref, v_ref, gn_ref, o_ref, rest_ref, acc_ref):
    tq = q_ref.shape[1]
    sub = TK_ATT
    tk = SB_SUBS * sub
    nks = sub // LANES
    rows = 2 * tq
    t0 = pl.program_id(2) * tq
    r = lax.broadcasted_iota(jnp.int32, (sub, sub), 0)
    c = lax.broadcasted_iota(jnp.int32, (sub, sub), 1)
    upper = jnp.where(r >= c, 1.0, 0.0).astype(BF16)
    n_tiles = lax.div(t0 + tq - 1, tk) + 1
    diag = (lax.broadcasted_iota(jnp.int32, (ROWS, LANES), 1)
            - lax.broadcasted_iota(jnp.int32, (ROWS, LANES), 0))
    rest_ref[...] = jnp.zeros(rest_ref.shape, F32)
    acc_ref[...] = jnp.zeros(acc_ref.shape, F32)
    hi_mask = jnp.uint32(0xFFFF0000)
    q = q_ref[0]
    qall = jnp.concatenate([q[:, :LANES], q[:, LANES:]], axis=0)
    chunks = [(h * tq + cc * ROWS, cc * ROWS) for h in range(2) for cc in range(tq // ROWS)]

    def tile(kt, masked):
        k0 = pl.multiple_of(kt * tk, tk)
        k = k_ref[0, pl.ds(k0, tk), :]
        v = v_ref[0, pl.ds(k0, tk), :]

        def strictly_before(t_off, col):
            return diag < (t0 + t_off) - (k0 + col)

        z = _dot_nt(qall, k)
        his = [[] for _ in range(SB_SUBS)]
        los = [[] for _ in range(SB_SUBS)]
        for r0, t_off in chunks:
            for sb in range(SB_SUBS):
                hi_c, lo_c = [], []
                for kk in range(nks):
                    col = sb * sub + kk * LANES
                    zc = z[r0:r0 + ROWS, col:col + LANES]
                    l = -(jnp.maximum(zc, 0.0) + jnp.log2(1.0 + jnp.exp2(-jnp.abs(zc))))
                    if masked:
                        l = jnp.where(strictly_before(t_off, col), l, 0.0)
                    hi = pltpu.bitcast(pltpu.bitcast(l, jnp.uint32) & hi_mask, F32)
                    hi_c.append(hi.astype(BF16))
                    lo_c.append((l - hi).astype(BF16))
                his[sb].append(jnp.concatenate(hi_c, axis=1))
                los[sb].append(jnp.concatenate(lo_c, axis=1))
        hi_all = jnp.concatenate([x for sb in range(SB_SUBS) for x in his[sb]], axis=0)
        lo_all = jnp.concatenate([x for sb in range(SB_SUBS) for x in los[sb]], axis=0)
        cum = _dot(hi_all, upper) + _dot(lo_all, upper)

        a_rows = []
        for r0, t_off in chunks:
            base = rest_ref[r0:r0 + ROWS, :]
            a_c = [None] * (SB_SUBS * nks)
            for sb in reversed(range(SB_SUBS)):
                cs = cum[sb * rows + r0:sb * rows + r0 + ROWS, :]
                for kk in range(nks):
                    col = sb * sub + kk * LANES
                    a = jnp.exp2(z[r0:r0 + ROWS, col:col + LANES] + cs[:, kk * LANES:(kk + 1) * LANES] + base)
                    if masked:
                        a = jnp.where(strictly_before(t_off, col), a, 0.0)
                    a_c[sb * nks + kk] = a.astype(BF16)
                base = base + jnp.broadcast_to(cs[:, 0:1], (ROWS, LANES))
            a_rows.append(jnp.concatenate(a_c, axis=1))
            rest_ref[r0:r0 + ROWS, :] = base
        acc_ref[...] += _dot(jnp.concatenate(a_rows, axis=0), v)

    tile(n_tiles - 1, True)

    def full_tile(i, carry):
        tile(n_tiles - 2 - i, False)
        return carry

    lax.fori_loop(0, n_tiles - 1, full_tile, 0)
    _pair_finish(acc_ref[...].reshape(2, tq, LANES), gn_ref, o_ref, tq)


def _sb_attention(u, gn, layer):
    b, s, _ = u.shape
    tq = min(TQ_PAIR, s)
    gb = (NSA_HEADS + FOX_HEADS) * HEAD_DIM // LANES
    qb = COL_SB // (2 * LANES)
    kb = (COL_SB + SB_HEADS * LANES) // LANES
    vb = kb + 2
    return pl.pallas_call(
        _sb_kernel,
        grid=(b, 2, s // tq),
        in_specs=[pl.BlockSpec((1, tq, 2 * LANES), lambda i, p, j: (i, j, qb + p)),
                  pl.BlockSpec((1, s, LANES), lambda i, p, j: (i, 0, kb + p)),
                  pl.BlockSpec((1, s, LANES), lambda i, p, j: (i, 0, vb + p)),
                  pl.BlockSpec((None, 1, LANES), lambda i, p, j: (layer, 0, gb + p))],
        out_specs=pl.BlockSpec((1, tq, LANES), lambda i, p, j: (i, j, p)),
        out_shape=jax.ShapeDtypeStruct((b, s, SB_HEADS * HEAD_DIM), BF16),
        scratch_shapes=[pltpu.VMEM((2 * tq, LANES), F32), pltpu.VMEM((2 * tq, LANES), F32)],
        compiler_params=_cp(3),
        name="sb_attention",
    )(u, u, u, gn)


def _outproj_kernel(oa_ref, ob_ref, oc_ref, x_ref, w_ref, g1_ref, n2_ref, sc_ref, sh_ref,
                    wrh_ref, wrl_ref, br_ref, xo_ref, h_ref, rw_ref, ri_ref):
    na = oa_ref.shape[2]
    nb = ob_ref.shape[2]
    y = _dot(oa_ref[0], w_ref[0:na, :])
    y = y + _dot(ob_ref[0], w_ref[na:na + nb, :])
    y = y + _dot(oc_ref[0], w_ref[na + nb:, :])
    x = x_ref[0] + g1_ref[0] * y
    xo_ref[0] = x
    ms = jnp.mean(x * x, axis=-1, keepdims=True)
    h = (x * lax.rsqrt(ms + EPS) * n2_ref[...]) * (1.0 + sc_ref[0]) + sh_ref[0]
    hb = h.astype(BF16)
    _store_row_tiles(h_ref, h)
    hl = (h - hb.astype(F32)).astype(BF16)
    logit = _dot(hb, wrh_ref[...]) + _dot(hl, wrh_ref[...]) + _dot(hb, wrl_ref[...]) + br_ref[...]

    tm = logit.shape[0]
    lane = lax.broadcasted_iota(jnp.int32, (tm, LANES), 1).astype(F32)
    big = float(LANES)
    is_g = lane < N_GROUPS
    lg = jnp.where(is_g, logit, NEG)
    mg = jnp.max(lg, axis=-1, keepdims=True)
    zg = jnp.sum(jnp.where(is_g, jnp.exp(lg - mg), 0.0), axis=-1, keepdims=True)
    pg = 1.0 / zg
    gi = jnp.min(jnp.where(is_g & (lg == mg), lane, big), axis=-1, keepdims=True)
    e_lane = lane - N_GROUPS
    in_grp = (e_lane >= gi * EXPERTS_PER_GROUP) & (e_lane < (gi + 1) * EXPERTS_PER_GROUP)
    le = jnp.where(in_grp, logit, NEG)
    m1 = jnp.max(le, axis=-1, keepdims=True)
    i1 = jnp.min(jnp.where(in_grp & (le == m1), lane, big), axis=-1, keepdims=True)
    rest = in_grp & (lane != i1)
    le2 = jnp.where(rest, logit, NEG)
    m2 = jnp.max(le2, axis=-1, keepdims=True)
    i2 = jnp.min(jnp.where(rest & (le2 == m2), lane, big), axis=-1, keepdims=True)
    ze = jnp.sum(jnp.where(in_grp, jnp.exp(le - m1), 0.0), axis=-1, keepdims=True)
    p1 = 1.0 / ze
    p2 = jnp.exp(m2 - m1) / ze
    den = p1 + p2
    w1 = pg * (p1 / den)
    w2 = pg * (p2 / den)
    rw_ref[0] = jnp.where(lane == 0.0, w1, jnp.where(lane == 1.0, w2, 0.0))
    ri_ref[0] = jnp.where(lane == 0.0, i1 - N_GROUPS, jnp.where(lane == 1.0, i2 - N_GROUPS, 0.0)).astype(jnp.int32)


def _out_projection(oa, ob, oc, x, w_out, mod4, n2, wr_hi, wr_lo, br, layer):
    b, s, d = x.shape
    tm = min(TM_OUT, s)
    row = lambda i, j: (i, j, 0)
    return pl.pallas_call(
        _outproj_kernel,
        grid=(b, s // tm),
        in_specs=[pl.BlockSpec((1, tm, oa.shape[2]), row),
                  pl.BlockSpec((1, tm, ob.shape[2]), row),
                  pl.BlockSpec((1, tm, oc.shape[2]), row),
                  pl.BlockSpec((1, tm, d), row),
                  _layer_spec(layer, w_out.shape[1:]),
                  _mod_spec(layer, 2, d),
                  _layer_spec(layer, (1, d)),
                  _mod_spec(layer, 4, d),
                  _mod_spec(layer, 3, d),
                  _layer_spec(layer, (d, LANES)),
                  _layer_spec(layer, (d, LANES)),
                  _layer_spec(layer, (1, LANES))],
        out_specs=[pl.BlockSpec((1, tm, d), row),
                   pl.BlockSpec((tm * ROW_TILE, LANES), lambda i, j: (i * (s // tm) + j, 0)),
                   pl.BlockSpec((1, tm, LANES), row),
                   pl.BlockSpec((1, tm, LANES), row)],
        out_shape=[jax.ShapeDtypeStruct((b, s, d), F32),
                   jax.ShapeDtypeStruct((b * s * ROW_TILE, LANES), F32),
                   jax.ShapeDtypeStruct((b, s, LANES), F32),
                   jax.ShapeDtypeStruct((b, s, LANES), jnp.int32)],
        compiler_params=_cp(2),
        name="out_projection",
    )(oa, ob, oc, x, w_out, mod4, n2.reshape(-1, 1, d), mod4, mod4, wr_hi, wr_lo, br)


def _store_row_tiles(ref, val):
    tm = val.shape[0]
    for c in range(ROW_TILE):
        ref[pl.ds(c, tm, stride=ROW_TILE), :] = val[:, c * LANES:(c + 1) * LANES]


def _load_row_tiles(ref, tm):
    return [ref[pl.ds(c, tm, stride=ROW_TILE), :] for c in range(ROW_TILE)]


def _tile_rows(ref, n):
    return ref.at[pl.ds(pl.multiple_of(n * ROW_TILE, ROW_TILE), ROW_TILE), :]


def _dispatch_kernel(idx_ref, src_ref, init_hbm, dst_hbm, sem):
    del init_hbm

    def issue(tok, carry):
        for k in range(2):
            pltpu.make_async_copy(_tile_rows(src_ref, tok), _tile_rows(dst_hbm, idx_ref[0, 0, 2 * tok + k]),
                                  sem).start(priority=k)
        return carry

    lax.fori_loop(0, COPY_CHUNK // 2, issue, 0, unroll=4)
    for _ in range(2):
        pltpu.make_async_copy(src_ref, dst_hbm.at[pl.ds(0, src_ref.shape[0]), :], sem).wait()


def _dispatch(src, idx, n_dst, init):
    n = idx.shape[0]
    if init is None:
        init = jnp.zeros((n_dst * ROW_TILE, LANES), src.dtype)
    return pl.pallas_call(
        _dispatch_kernel,
        grid=(n // COPY_CHUNK,),
        in_specs=[pl.BlockSpec((1, 1, COPY_CHUNK), lambda i: (i, 0, 0), memory_space=pltpu.SMEM),
                  pl.BlockSpec((COPY_CHUNK // 2 * ROW_TILE, LANES), lambda i: (i, 0)),
                  pl.BlockSpec(memory_space=pl.ANY)],
        out_specs=pl.BlockSpec(memory_space=pl.ANY),
        out_shape=jax.ShapeDtypeStruct((n_dst * ROW_TILE, LANES), src.dtype),
        scratch_shapes=[pltpu.SemaphoreType.DMA],
        input_output_aliases={2: 0},
        compiler_params=_cp(1),
        name="moe_dispatch",
    )(idx.reshape(n // COPY_CHUNK, 1, COPY_CHUNK), src, init)


def _expert_kernel(te_ref, nu_ref, x_ref, w1_ref, w3_ref, w2_ref, y_ref, w1b_ref, w3b_ref, w2b_ref):
    i = pl.program_id(0)

    @pl.when((i == 0) | (te_ref[i] != te_ref[jnp.maximum(i - 1, 0)]))
    def _():
        w1b_ref[...] = w1_ref[0].astype(BF16)
        w3b_ref[...] = w3_ref[0].astype(BF16)
        w2b_ref[...] = w2_ref[0].astype(BF16)

    @pl.when(i < nu_ref[0])
    def _():
        x = jnp.concatenate(_load_row_tiles(x_ref, TM_EXP), axis=1).astype(BF16)
        a = _dot(x, w1b_ref[...])
        g = _dot(x, w3b_ref[...])
        act = (a * (1.0 / (1.0 + jnp.exp(-a)))) * g
        _store_row_tiles(y_ref, _dot(act.astype(BF16), w2b_ref[...]))

    @pl.when(i >= nu_ref[0])
    def _():
        y_ref[...] = jnp.zeros_like(y_ref)


def _expert_mlp(xs, tile_expert, n_used, w1, w3, w2, layer):
    d, de = w1.shape[2], w1.shape[3]
    n_tiles = xs.shape[0] // (TM_EXP * ROW_TILE)
    grid_spec = pltpu.PrefetchScalarGridSpec(
        num_scalar_prefetch=2,
        grid=(n_tiles,),
        in_specs=[pl.BlockSpec((TM_EXP * ROW_TILE, LANES), lambda i, te, nu: (i, 0)),
                  pl.BlockSpec((None, 1, d, de), lambda i, te, nu: (layer, te[i], 0, 0)),
                  pl.BlockSpec((None, 1, d, de), lambda i, te, nu: (layer, te[i], 0, 0)),
                  pl.BlockSpec((None, 1, de, d), lambda i, te, nu: (layer, te[i], 0, 0))],
        out_specs=pl.BlockSpec((TM_EXP * ROW_TILE, LANES), lambda i, te, nu: (i, 0)),
        scratch_shapes=[pltpu.VMEM((d, de), BF16), pltpu.VMEM((d, de), BF16), pltpu.VMEM((de, d), BF16)],
    )
    return pl.pallas_call(
        _expert_kernel,
        grid_spec=grid_spec,
        out_shape=jax.ShapeDtypeStruct(xs.shape, F32),
        compiler_params=_cp(1),
        name="expert_mlp",
    )(tile_expert, n_used, xs, w1, w3, w2)


def _combine_kernel(d0_ref, d1_ref, x_ref, ys_hbm, rw_ref, g2_ref, fg_ref, o_ref, y0_ref, y1_ref, sem,
                    *, final):
    tm = x_ref.shape[1]

    def issue(r, carry):
        pltpu.make_async_copy(_tile_rows(ys_hbm, d0_ref[0, 0, r]), _tile_rows(y0_ref, r),
                              sem.at[0]).start(priority=0)
        pltpu.make_async_copy(_tile_rows(ys_hbm, d1_ref[0, 0, r]), _tile_rows(y1_ref, r),
                              sem.at[1]).start(priority=1)
        return carry

    lax.fori_loop(0, tm, issue, 0, unroll=8)
    pltpu.make_async_copy(ys_hbm.at[pl.ds(0, tm * ROW_TILE), :], y0_ref, sem.at[0]).wait()
    pltpu.make_async_copy(ys_hbm.at[pl.ds(0, tm * ROW_TILE), :], y1_ref, sem.at[1]).wait()

    rw = rw_ref[0]
    w0 = jnp.broadcast_to(rw[:, 0:1], (tm, LANES))
    w1 = jnp.broadcast_to(rw[:, 1:2], (tm, LANES))
    y0 = _load_row_tiles(y0_ref, tm)
    y1 = _load_row_tiles(y1_ref, tm)
    cols = []
    for c in range(ROW_TILE):
        sl = slice(c * LANES, (c + 1) * LANES)
        cols.append(x_ref[0, :, sl] + g2_ref[0, :, sl] * (y0[c] * w0 + y1[c] * w1))
    if final:
        ssq = cols[0] * cols[0]
        for c in range(1, ROW_TILE):
            ssq = ssq + cols[c] * cols[c]
        inv = lax.rsqrt(jnp.sum(ssq, axis=-1, keepdims=True) * (1.0 / (ROW_TILE * LANES)) + EPS)
        cols = [cols[c] * inv * fg_ref[:, c * LANES:(c + 1) * LANES] for c in range(ROW_TILE)]
    for c in range(ROW_TILE):
        o_ref[0, :, c * LANES:(c + 1) * LANES] = cols[c]


def _combine(x, ys, dest0, dest1, rw, mod4, final_g, final, layer):
    b, s, d = x.shape
    tm = min(TM_CMB, s)
    row = lambda i, j: (i, j, 0)
    idx_spec = pl.BlockSpec((1, 1, tm), lambda i, j: (i * (s // tm) + j, 0, 0), memory_space=pltpu.SMEM)
    return pl.pallas_call(
        functools.partial(_combine_kernel, final=final),
        grid=(b, s // tm),
        in_specs=[idx_spec, idx_spec,
                  pl.BlockSpec((1, tm, d), row),
                  pl.BlockSpec(memory_space=pl.ANY),
                  pl.BlockSpec((1, tm, LANES), row),
                  _mod_spec(layer, 5, d),
                  pl.BlockSpec((1, d), lambda i, j: (0, 0))],
        out_specs=pl.BlockSpec((1, tm, d), row),
        out_shape=jax.ShapeDtypeStruct((b, s, d), F32),
        scratch_shapes=[pltpu.VMEM((tm * ROW_TILE, LANES), F32), pltpu.VMEM((tm * ROW_TILE, LANES), F32),
                        pltpu.SemaphoreType.DMA((2,))],
        compiler_params=_cp(2),
        name="moe_combine_final" if final else "moe_combine",
    )(dest0.reshape(-1, 1, tm), dest1.reshape(-1, 1, tm), x, ys, rw, mod4, final_g.reshape(1, d))


def _layout_w_in(w_in):
    d = w_in.shape[0]
    kvw = NSA_KV * HEAD_DIM
    sizes = (NSA_HEADS * HEAD_DIM, kvw, kvw, kvw, kvw, kvw, kvw, NSA_HEADS * 3,
             FOX_HEADS * HEAD_DIM, FOX_HEADS * HEAD_DIM, FOX_HEADS * HEAD_DIM, FOX_HEADS,
             SB_HEADS * HEAD_DIM, SB_HEADS * HEAD_DIM, SB_HEADS * HEAD_DIM)
    pts = np.cumsum(sizes)[:-1].tolist()
    (qa, kca, vca, ksa, vsa, kwa, vwa, ga, qb, kb, vb, fb, qc, kc, vc) = jnp.split(w_in, pts, axis=1)
    scale = HEAD_DIM ** -0.5 * LOG2E
    main = jnp.concatenate([qa * scale, ksa, kwa, vsa, vwa, qb * scale, kb, vb, qc * scale, kc, vc],
                           axis=1).astype(BF16)
    cmp_w = jnp.concatenate([kca, vca], axis=1).astype(BF16)
    zpad = lambda n: jnp.zeros((d, n), w_in.dtype)
    per_grp = NSA_GROUP * 3
    small = jnp.concatenate([ga[:, :per_grp], zpad(LANES - per_grp), ga[:, per_grp:], zpad(LANES - per_grp),
                             fb[:, 0:2], zpad(LANES - 2), fb[:, 2:4], zpad(LANES - 2)], axis=1).astype(BF16)
    return main, cmp_w, small


def _layout_cmp(w1, w2):
    hid = w1.shape[1]
    w1r = w1.reshape(2, CMP_STRIDE, HEAD_DIM, hid)
    z = jnp.zeros((CMP_STRIDE, HEAD_DIM, hid), w1.dtype)
    cols = []
    for half in range(2):
        for g in range(NSA_KV):
            parts = [w1r[half] if gg == g else z for gg in range(NSA_KV)]
            cols.append(jnp.concatenate(parts, axis=1).reshape(CMP_STRIDE * LANES, hid))
    wcat = jnp.concatenate(cols, axis=1).astype(BF16)
    zz = jnp.zeros_like(w2)
    w2bd = jnp.concatenate([jnp.concatenate([w2, zz], axis=1),
                            jnp.concatenate([zz, w2], axis=1)], axis=0).astype(BF16)
    return wcat, w2bd


def _rope_tables(pos):
    inv = jnp.exp(jnp.arange(ROPE_HALF, dtype=F32) * (-2.0 * math.log(ROPE_THETA) / ROPE_DIM))
    ang = pos.astype(F32)[:, None] * inv[None, :]
    cos, sin = jnp.cos(ang), jnp.sin(ang)
    n = pos.shape[0]
    z8 = jnp.zeros((n, ROPE_HALF), F32)
    rest1 = jnp.ones((n, HEAD_DIM - ROPE_DIM), F32)
    rest0 = jnp.zeros((n, HEAD_DIM - ROPE_DIM), F32)
    c = jnp.concatenate([cos, cos, rest1], axis=1)
    s1 = jnp.concatenate([z8, sin, rest0], axis=1)
    s2 = jnp.concatenate([-sin, z8, rest0], axis=1)
    dup = lambda a: jnp.concatenate([a, a], axis=1)
    return dup(c), dup(s1), dup(s2)


def _static_tables(s):
    n_cmp_pad = s // CMP_STRIDE
    n = np.arange(n_cmp_pad)[:, None]
    j = np.arange(LANES)[None, :]
    n_sel = s // SEL_LEN
    cover = ((n * CMP_STRIDE < j * SEL_LEN + SEL_LEN) & (n * CMP_STRIDE + CMP_LEN > j * SEL_LEN)
             & (j < n_sel)).astype(np.float32)
    nt = s // TK_FOX
    key = np.arange(nt)[:, None, None] * TK_FOX + np.arange(TK_FOX)[None, None, :]
    expand = (key // SEL_LEN == np.arange(LANES)[None, :, None]).astype(np.float32)
    return jnp.asarray(cover.T, BF16), jnp.asarray(expand, BF16)


def _plan_kernel(ri_ref, dest_ref, te_ref, cnt_ref, pst_ref):
    phase = pl.program_id(0)
    j = pl.program_id(1)
    tm = ri_ref.shape[1]
    lane = lax.broadcasted_iota(jnp.int32, (tm, LANES), 1)
    ri = ri_ref[0]
    hot0 = jnp.where(lane == ri[:, 0:1], 1.0, 0.0)
    hot1 = jnp.where(lane == ri[:, 1:2], 1.0, 0.0)
    both = hot0 + hot1

    @pl.when((phase == 0) & (j == 0))
    def _():
        cnt_ref[...] = jnp.zeros(cnt_ref.shape, F32)

    @pl.when(phase == 0)
    def _():
        cnt_ref[...] += jnp.sum(both, axis=0, keepdims=True)

    @pl.when((phase == 0) & (j == pl.num_programs(1) - 1))
    def _():
        cnt = cnt_ref[...]
        padded = jnp.ceil(cnt * (1.0 / TM_EXP)) * TM_EXP
        r = lax.broadcasted_iota(jnp.int32, (LANES, LANES), 0)
        c = lax.broadcasted_iota(jnp.int32, (LANES, LANES), 1)
        incl = jnp.where(r <= c, 1.0, 0.0).astype(BF16)
        rows = jnp.broadcast_to(padded, (LANES, LANES)).astype(BF16)
        pends = _dot(rows, incl)
        pst_ref[...] = pends[0:1, :] - padded
        ends = pends.T
        tile_start = (c * TM_EXP).astype(F32)
        done = jnp.where((r < N_EXPERTS) & (ends <= tile_start), 1.0, 0.0)
        te = jnp.minimum(jnp.sum(done, axis=0, keepdims=True), float(N_EXPERTS - 1))
        used = pends[0:1, N_EXPERTS - 1:N_EXPERTS] * (1.0 / TM_EXP)
        lane1 = lax.broadcasted_iota(jnp.int32, (1, LANES), 1)
        te_ref[...] = jnp.concatenate(
            [te, jnp.where(lane1 == 0, used, 0.0)] + [jnp.zeros((ROW_TILE - 2, LANES), F32)], axis=0
        ).astype(jnp.int32)
        cnt_ref[...] = jnp.zeros(cnt_ref.shape, F32)

    @pl.when(phase == 1)
    def _():
        r = lax.broadcasted_iota(jnp.int32, (tm, tm), 0)
        c = lax.broadcasted_iota(jnp.int32, (tm, tm), 1)
        before = jnp.where(c < r, 1.0, 0.0).astype(BF16)
        earlier = _dot(before, both.astype(BF16)) + cnt_ref[...]
        slot = earlier + pst_ref[...]
        d0 = jnp.sum(hot0 * slot, axis=-1, keepdims=True)
        d1 = jnp.sum(hot1 * slot, axis=-1, keepdims=True)
        dest_ref[0] = jnp.where(lane == 0, d0, jnp.where(lane == 1, d1, 0.0)).astype(jnp.int32)
        cnt_ref[...] += jnp.sum(both, axis=0, keepdims=True)


def _dispatch_plan(ri, t):
    b, s, _ = ri.shape
    tm = min(TM_CMB, s)
    n_tiles = -(-(2 * t + N_EXPERTS * (TM_EXP - 1)) // TM_EXP)
    assert n_tiles <= LANES and N_EXPERTS <= LANES
    chunks = s // tm
    dest, te = pl.pallas_call(
        _plan_kernel,
        grid=(2, b * chunks),
        in_specs=[pl.BlockSpec((1, tm, LANES), lambda ph, j: (j // chunks, j % chunks, 0))],
        out_specs=[pl.BlockSpec((1, tm, LANES), lambda ph, j: (ph * (j // chunks), ph * (j % chunks), 0)),
                   pl.BlockSpec((ROW_TILE, LANES), lambda ph, j: (0, 0))],
        out_shape=[jax.ShapeDtypeStruct((b, s, LANES), jnp.int32),
                   jax.ShapeDtypeStruct((ROW_TILE, LANES), jnp.int32)],
        scratch_shapes=[pltpu.VMEM((1, LANES), F32), pltpu.VMEM((1, LANES), F32)],
        compiler_params=_cp(2),
        name="moe_plan",
    )(ri)
    dest = dest.reshape(t, LANES)[:, :2].reshape(-1)
    return dest, n_tiles * TM_EXP, te[0, :n_tiles], te[1, 0:1]


def kernel(x, c, norm1_g, norm2_g, ada_w, ada_b, w_in, b_forget, cmp_pos_k, cmp_w1_k, cmp_w2_k,
           cmp_pos_v, cmp_w1_v, cmp_w2_v, out_norm_g, w_out, router_group_w, router_group_b,
           router_expert_w, router_expert_b, expert_w1, expert_w3, expert_w2, final_g):
    b, s, d = x.shape
    depth = ada_w.shape[0]
    t = b * s
    mod = _modulation(c, ada_w, ada_b)
    rope_c, rope_1, rope_2 = _rope_tables(jnp.arange(s))
    n_cmp_pad = s // CMP_STRIDE
    crc, cr1, cr2 = _rope_tables(jnp.arange(n_cmp_pad) * CMP_STRIDE + (CMP_LEN - 1))
    cover, expand = _static_tables(s)
    xs = None

    mod4 = mod.reshape(depth, b, 6, 1, d)
    w_main, w_cmp, w_small = jax.vmap(_layout_w_in)(w_in)
    b_pairs = jnp.pad(b_forget.reshape(depth, 2, 1, 2), ((0, 0), (0, 0), (0, 0), (0, LANES - 2)))
    wk, w2k = jax.vmap(_layout_cmp)(cmp_w1_k, cmp_w2_k)
    wv, w2v = jax.vmap(_layout_cmp)(cmp_w1_v, cmp_w2_v)
    pek = jnp.broadcast_to(cmp_pos_k.reshape(depth, 1, -1), (depth, ROW_TILE, CMP_LEN * HEAD_DIM))
    pev = jnp.broadcast_to(cmp_pos_v.reshape(depth, 1, -1), (depth, ROW_TILE, CMP_LEN * HEAD_DIM))
    gn = out_norm_g.reshape(depth, 1, -1)
    wr = jnp.concatenate([router_group_w, router_expert_w,
                          jnp.zeros((depth, d, LANES - N_GROUPS - N_EXPERTS), F32)], axis=2)
    wr_hi = wr.astype(BF16)
    wr_lo = (wr - wr_hi.astype(F32)).astype(BF16)
    br = jnp.concatenate([router_group_b, router_expert_b,
                          jnp.zeros((depth, LANES - N_GROUPS - N_EXPERTS), F32)], axis=1).reshape(depth, 1, LANES)
    w_out_b = w_out.astype(BF16)

    for l in range(depth):
        u, kc, vc, small = _in_projection(x, norm1_g, mod4, w_main, w_cmp, w_small, rope_c, rope_1, rope_2, l)
        cq, ckt = _forget_cumsum(small, b_pairs, l)
        ck, cv = _compress(kc, vc, wk[l], wv[l], cmp_w1_k[l], cmp_w1_v[l], pek[l], pev[l],
                           w2k[l], w2v[l], crc, cr1, cr2)
        o_a = _nsa_attention(u, ck, cv, small, gn, cover, expand, l)
        o_b = _fox_attention(u, cq, ckt, gn, l)
        o_c = _sb_attention(u, gn, l)
        x, h2, rw, ri = _out_projection(o_a, o_b, o_c, x, w_out_b, mod4, norm2_g, wr_hi, wr_lo, br, l)

        dest, p_rows, tile_expert, n_used = _dispatch_plan(ri, t)
        xs = _dispatch(h2, dest, p_rows, xs)
        ys = _expert_mlp(xs, tile_expert, n_used, expert_w1, expert_w3, expert_w2, l)
        x = _combine(x, ys, dest[0::2], dest[1::2], rw, mod4, final_g, final=(l == depth - 1), layer=l)
    return x
```

```python
import functools
import math

import numpy as np
import jax
import jax.numpy as jnp
from jax import lax
from jax.experimental import pallas as pl
from jax.experimental.pallas import tpu as pltpu

F32 = jnp.float32
BF16 = jnp.bfloat16

HEAD_DIM = 64
LANES = 128
NSA_HEADS = 8
NSA_KV = 2
NSA_GROUP = 4
FOX_HEADS = 4
SB_HEADS = 4
ROPE_DIM = 16
ROPE_HALF = 8
ROPE_THETA = 500000.0
CMP_LEN = 32
CMP_STRIDE = 16
CMP_HIDDEN = 128
SEL_LEN = 64
SEL_TOPN = 16
WINDOW = 512
FORCE_SCORE = 1.0e4
N_GROUPS = 4
EXPERTS_PER_GROUP = 8
N_EXPERTS = 32
EPS = 1e-6
LOG2E = math.log2(math.e)
NEG = -1e30

COL_QA = 0
COL_KS = 1024
COL_KW = 1152
COL_VS = 1280
COL_VW = 1408
COL_FOX = 1536
COL_SB = 2560
N_MAIN = 3584
N_SMALL = 512
W_ROPE = (NSA_HEADS + 2 * NSA_KV) * HEAD_DIM
W_MAIN = W_ROPE + (2 * NSA_KV + 3 * FOX_HEADS + 3 * SB_HEADS) * HEAD_DIM

TM_PROJ = 512
TQ_NSA = 128
TK_ATT = 256
TQ_PAIR = 256
TK_FOX = 512
ROWS = 32
SB_SUBS = 2
TM_OUT = 512
TM_EXP = 512
TM_CMB = 512
COPY_CHUNK = 512
ROW_TILE = 8
VMEM_LIMIT = 56 * 1024 * 1024


def _cp(n_axes, vmem=VMEM_LIMIT):
    return pltpu.CompilerParams(dimension_semantics=("arbitrary",) * n_axes, vmem_limit_bytes=vmem)


def _dot(a, b):
    return jnp.dot(a, b, preferred_element_type=F32)


def _dot_nt(a, b):
    return lax.dot_general(a, b, (((1,), (1,)), ((), ())), preferred_element_type=F32)


def _split_bf16(x, parts):
    out = []
    r = x
    for _ in range(parts):
        p = r.astype(BF16)
        out.append(p)
        r = r - p.astype(F32)
    return out


def _rope(x, c, s1, s2):
    return x * c + pltpu.roll(x, ROPE_HALF, 1) * s1 + pltpu.roll(x, LANES - ROPE_HALF, 1) * s2


def _softplus(z):
    return jnp.maximum(z, 0.0) + jnp.log(1.0 + jnp.exp(-jnp.abs(z)))


def _mod_kernel(c_ref, w_ref, b_ref, o_ref):
    c = c_ref[...]
    cond = c * (1.0 / (1.0 + jnp.exp(-c)))
    o_ref[0] = _dot(cond, w_ref[0]) + b_ref[0]


def _modulation(c, ada_w, ada_b):
    depth, d, n = ada_w.shape
    b = c.shape[0]
    tn = 1024
    return pl.pallas_call(
        _mod_kernel,
        grid=(depth, n // tn),
        in_specs=[pl.BlockSpec((b, d), lambda l, j: (0, 0)),
                  pl.BlockSpec((1, d, tn), lambda l, j: (l, 0, j)),
                  pl.BlockSpec((1, 1, tn), lambda l, j: (l, 0, j))],
        out_specs=pl.BlockSpec((1, b, tn), lambda l, j: (l, 0, j)),
        out_shape=jax.ShapeDtypeStruct((depth, b, n), F32),
        compiler_params=_cp(2),
        name="modulation",
    )(c, ada_w, ada_b.reshape(depth, 1, n))


def _inproj_kernel(x_ref, g_ref, sc_ref, sh_ref, w_ref, wc_ref, ws_ref, rc_ref, r1_ref, r2_ref,
                   u_ref, kc_ref, vc_ref, sm_ref):
    x = x_ref[0]
    ms = jnp.mean(x * x, axis=-1, keepdims=True)
    h = (x * lax.rsqrt(ms + EPS) * g_ref[...]) * (1.0 + sc_ref[0]) + sh_ref[0]
    hb = h.astype(BF16)
    rc, r1, r2 = rc_ref[...], r1_ref[...], r2_ref[...]
    lane = lax.broadcasted_iota(jnp.int32, (x.shape[0], LANES), 1)
    left = lane < HEAD_DIM

    def put(col, val):
        u_ref[0, :, col:col + LANES] = val.astype(BF16)

    def put_pair(col, blk, offsets):
        for i, off in enumerate(offsets):
            src = blk if off == i * HEAD_DIM else pltpu.roll(blk, HEAD_DIM, 1)
            put(col + i * LANES, jnp.where(left if off == 0 else ~left, src, 0.0))

    acc = _dot(hb, w_ref[:, :W_ROPE])
    for j in range(NSA_HEADS // 2):
        off = (2 * j // NSA_GROUP) * HEAD_DIM
        put_pair(COL_QA + 2 * j * LANES, _rope(acc[:, j * LANES:(j + 1) * LANES], rc, r1, r2), (off, off))
    nq = NSA_HEADS // 2
    put(COL_KS, _rope(acc[:, nq * LANES:(nq + 1) * LANES], rc, r1, r2))
    put(COL_KW, _rope(acc[:, (nq + 1) * LANES:(nq + 2) * LANES], rc, r1, r2))
    acc = _dot(hb, w_ref[:, W_ROPE:])
    u_ref[0, :, COL_VS:COL_VS + 2 * LANES] = acc[:, :2 * LANES].astype(BF16)
    c = 2 * LANES
    for base, heads in ((COL_FOX, FOX_HEADS), (COL_SB, SB_HEADS)):
        for j in range(heads // 2):
            put_pair(base + 2 * j * LANES, acc[:, c:c + LANES], (0, HEAD_DIM))
            c += LANES
        kv = 2 * heads * HEAD_DIM
        u_ref[0, :, base + heads * LANES:base + heads * LANES + kv] = acc[:, c:c + kv].astype(BF16)
        c += kv
    cmp_in = _dot(hb, wc_ref[...])
    kc_ref[0] = cmp_in[:, :LANES].astype(BF16)
    vc_ref[0] = cmp_in[:, LANES:].astype(BF16)
    sm_ref[0] = _dot(hb, ws_ref[...])


def _mod_spec(layer, which, d):
    return pl.BlockSpec((None, 1, None, 1, d), lambda i, j: (layer, i, which, 0, 0))


def _layer_spec(layer, shape):
    zeros = (0,) * len(shape)
    return pl.BlockSpec((None,) + tuple(shape), lambda *_: (layer,) + zeros)


def _in_projection(x, g, mod4, w_main, w_cmp, w_small, rope_c, rope_1, rope_2, layer):
    b, s, d = x.shape
    tm = min(TM_PROJ, s)
    row = lambda i, j: (i, j, 0)
    seq = lambda i, j: (j, 0)
    return pl.pallas_call(
        _inproj_kernel,
        grid=(b, s // tm),
        in_specs=[pl.BlockSpec((1, tm, d), row),
                  _layer_spec(layer, (1, d)),
                  _mod_spec(layer, 1, d),
                  _mod_spec(layer, 0, d),
                  _layer_spec(layer, (d, W_MAIN)),
                  _layer_spec(layer, (d, 2 * LANES)),
                  _layer_spec(layer, (d, N_SMALL)),
                  pl.BlockSpec((tm, LANES), seq),
                  pl.BlockSpec((tm, LANES), seq),
                  pl.BlockSpec((tm, LANES), seq)],
        out_specs=[pl.BlockSpec((1, tm, N_MAIN), row),
                   pl.BlockSpec((1, tm, LANES), row),
                   pl.BlockSpec((1, tm, LANES), row),
                   pl.BlockSpec((1, tm, N_SMALL), row)],
        out_shape=[jax.ShapeDtypeStruct((b, s, N_MAIN), BF16),
                   jax.ShapeDtypeStruct((b, s, LANES), BF16),
                   jax.ShapeDtypeStruct((b, s, LANES), BF16),
                   jax.ShapeDtypeStruct((b, s, N_SMALL), F32)],
        compiler_params=_cp(2),
        name="in_projection",
    )(x, g.reshape(-1, 1, d), mod4, mod4, w_main, w_cmp, w_small, rope_c, rope_1, rope_2)


def _cumf_kernel(f_ref, b_ref, cq_ref, ckt_ref):
    n_chunks = f_ref.shape[1] // TK_ATT
    r = lax.broadcasted_iota(jnp.int32, (TK_ATT, TK_ATT), 0)
    c = lax.broadcasted_iota(jnp.int32, (TK_ATT, TK_ATT), 1)
    tri = jnp.where(c <= r, 1.0, 0.0).astype(BF16)
    carry = jnp.zeros((1, LANES), F32)
    for j in range(n_chunks):
        f = f_ref[0, j * TK_ATT:(j + 1) * TK_ATT, :] + b_ref[0]
        ls = -_softplus(-f)
        acc = None
        for p in _split_bf16(ls, 3):
            dd = _dot(tri, p)
            acc = dd if acc is None else acc + dd
        cs = acc + carry
        cs2 = cs * LOG2E
        cq_ref[0, j * TK_ATT:(j + 1) * TK_ATT, :] = cs2
        ckt_ref[0, 0, j] = cs2.T[:ROW_TILE, :]
        carry = cs[TK_ATT - 1:TK_ATT, :]


def _forget_cumsum(small, b_pairs, layer):
    b, s, _ = small.shape
    return pl.pallas_call(
        _cumf_kernel,
        grid=(b, 2),
        in_specs=[pl.BlockSpec((1, s, LANES), lambda i, p: (i, 0, 2 + p)),
                  pl.BlockSpec((None, 1, 1, LANES), lambda i, p: (layer, p, 0, 0))],
        out_specs=[pl.BlockSpec((1, s, LANES), lambda i, p: (i, 0, p)),
                   pl.BlockSpec((1, 1, s // TK_ATT, ROW_TILE, TK_ATT), lambda i, p: (i, p, 0, 0, 0))],
        out_shape=[jax.ShapeDtypeStruct((b, s, 2 * LANES), F32),
                   jax.ShapeDtypeStruct((b, 2, s // TK_ATT, ROW_TILE, TK_ATT), F32)],
        compiler_params=_cp(2),
        name="forget_cumsum",
    )(small, b_pairs)


def _compress_kernel(ks_ref, vs_ref, wk_ref, wv_ref, w1k_ref, w1v_ref, pek_ref, pev_ref,
                     w2k_ref, w2v_ref, rc_ref, r1_ref, r2_ref, ck_ref, cv_ref):
    def one(seg_ref, w_ref, w1_ref, pe_ref, w2_ref):
        p = _dot(seg_ref[0], w_ref[...])
        half = 2 * CMP_HIDDEN
        bias = _dot(pe_ref[...].astype(BF16), w1_ref[...].astype(BF16))[0:1, :]
        bias2 = jnp.concatenate([bias, bias], axis=1)
        n = p.shape[0]
        hid = p[:, :half] + pltpu.roll(p[:, half:], n - 1, 0) + bias2
        act = hid * (1.0 / (1.0 + jnp.exp(-hid)))
        return _dot(act.astype(BF16), w2_ref[...])

    ck = one(ks_ref, wk_ref, w1k_ref, pek_ref, w2k_ref)
    ck_ref[0] = _rope(ck, rc_ref[...], r1_ref[...], r2_ref[...]).astype(BF16)
    cv_ref[0] = one(vs_ref, wv_ref, w1v_ref, pev_ref, w2v_ref).astype(BF16)


def _compress(kc, vc, wk, wv, w1k, w1v, pek, pev, w2k, w2v, rc, r1, r2):
    b, s, _ = kc.shape
    n = s // CMP_STRIDE
    width = CMP_STRIDE * LANES
    kseg = kc.reshape(b, n, width)
    vseg = vc.reshape(b, n, width)
    seg = pl.BlockSpec((1, n, width), lambda i: (i, 0, 0))
    full = lambda a: pl.BlockSpec(a.shape, lambda i: (0,) * a.ndim)
    return pl.pallas_call(
        _compress_kernel,
        grid=(b,),
        in_specs=[seg, seg, full(wk), full(wv), full(w1k), full(w1v), full(pek), full(pev),
                  full(w2k), full(w2v), full(rc), full(r1), full(r2)],
        out_specs=[pl.BlockSpec((1, n, LANES), lambda i: (i, 0, 0)),
                   pl.BlockSpec((1, n, LANES), lambda i: (i, 0, 0))],
        out_shape=[jax.ShapeDtypeStruct((b, n, LANES), BF16),
                   jax.ShapeDtypeStruct((b, n, LANES), BF16)],
        compiler_params=_cp(1),
        name="nsa_compress",
    )(kseg, vseg, wk, wv, w1k, w1v, pek, pev, w2k, w2v, rc, r1, r2)


def _nsa_kernel(q_ref, ck_ref, cv_ref, ks_ref, vs_ref, kw_ref, vw_ref, gate_ref, gn_ref,
                cover_ref, expand_ref, o_ref, m_ref, l_ref, acc_ref):
    for g in range(NSA_KV):
        _nsa_group(g, q_ref, ck_ref, cv_ref, ks_ref, vs_ref, kw_ref, vw_ref, gate_ref, gn_ref,
                   cover_ref, expand_ref, o_ref, m_ref, l_ref, acc_ref)


def _nsa_group(g, q_ref, ck_ref, cv_ref, ks_ref, vs_ref, kw_ref, vw_ref, gate_ref, gn_ref,
               cover_ref, expand_ref, o_ref, m_ref, l_ref, acc_ref):
    tq = q_ref.shape[1]
    t0 = pl.program_id(1) * tq
    q = q_ref[0, :, g * NSA_GROUP * LANES:(g + 1) * NSA_GROUP * LANES]
    qall = jnp.concatenate([q[:, h * LANES:(h + 1) * LANES] for h in range(NSA_GROUP)], axis=0)
    tpos = t0 + lax.broadcasted_iota(jnp.int32, (tq, 1), 0)
    lane = lax.broadcasted_iota(jnp.int32, (tq, LANES), 1)

    n_cmp = ck_ref.shape[1]
    s_t = _dot_nt(ck_ref[0], qall)
    valid = ((CMP_STRIDE * lax.broadcasted_iota(jnp.int32, (n_cmp, tq), 0) + (CMP_LEN - 1))
             <= t0 + lax.broadcasted_iota(jnp.int32, (n_cmp, tq), 1))
    p_cols, p_sum_t = [], None
    for h in range(NSA_GROUP):
        sm = jnp.where(valid, s_t[:, h * tq:(h + 1) * tq], NEG)
        e = jnp.where(valid, jnp.exp2(sm - jnp.max(sm, axis=0, keepdims=True)), 0.0)
        den = jnp.sum(e, axis=0, keepdims=True)
        p = e * jnp.where(den > 0.0, 1.0 / den, 0.0)
        p_cols.append(p.astype(BF16))
        p_sum_t = p if p_sum_t is None else p_sum_t + p
    o_cmp = lax.dot_general(jnp.concatenate(p_cols, axis=1), cv_ref[0], (((0,), (0,)), ((), ())),
                            preferred_element_type=F32).reshape(NSA_GROUP, tq, LANES)

    n_sel = expand_ref.shape[0] * (TK_FOX // SEL_LEN)
    imp_t = None
    for piece in _split_bf16(p_sum_t, 3):
        d_imp = _dot(cover_ref[...], piece)
        imp_t = d_imp if imp_t is None else imp_t + d_imp
    imp_t = imp_t[:n_sel, :]
    blk = lax.broadcasted_iota(jnp.int32, (n_sel, tq), 0)
    tcol = t0 + lax.broadcasted_iota(jnp.int32, (n_sel, tq), 1)
    cur = jnp.right_shift(tcol, int(math.log2(SEL_LEN)))
    forced = (blk == 0) | (blk == cur) | (blk == cur - 1)
    score = jnp.where(forced, FORCE_SCORE, jnp.where(blk * SEL_LEN <= tcol, imp_t, -1.0))
    cnt = jnp.zeros((n_sel, tq), F32)
    for j in range(n_sel):
        sj = score[j:j + 1, :]
        beats = (sj > score) | ((sj == score) & (blk > j))
        cnt = cnt + jnp.where(beats, 1.0, 0.0)
    sel_t = jnp.where(cnt < float(min(SEL_TOPN, n_sel)), 1.0, 0.0)
    sel = jnp.concatenate([sel_t, jnp.zeros((LANES - n_sel, tq), F32)], axis=0).T.astype(BF16)

    def biased(bias):
        def adjust(h, t_off, cols):
            return [cols[kk] + bias[t_off:t_off + ROWS, kk * LANES:(kk + 1) * LANES]
                    for kk in range(len(cols))]
        return adjust

    kcol_s = lax.broadcasted_iota(jnp.int32, (tq, TK_FOX), 1)

    def sel_tile(kt, carry):
        k0 = pl.multiple_of(kt * TK_FOX, TK_FOX)
        hit = _dot(sel, expand_ref[kt])
        ok = (hit > 0.5) & ((k0 + kcol_s) <= tpos)
        s_t = _dot_nt(qall, ks_ref[0, pl.ds(k0, TK_FOX), :])
        _softmax_tile(s_t, vs_ref[0, pl.ds(k0, TK_FOX), :], m_ref, l_ref, acc_ref, NSA_GROUP, tq,
                      biased(jnp.where(ok, 0.0, NEG)))
        return carry

    _softmax_init(m_ref, l_ref, acc_ref)
    lax.fori_loop(0, lax.div(t0 + tq - 1, TK_FOX) + 1, sel_tile, 0)
    o_sel = _softmax_result(l_ref, acc_ref).reshape(NSA_GROUP, tq, LANES)

    span = WINDOW + tq
    w0 = pl.multiple_of(jnp.maximum(t0 - WINDOW, 0), LANES)
    kp = w0 + lax.broadcasted_iota(jnp.int32, (tq, span), 1)
    bias_w = jnp.where((kp <= tpos) & (kp > tpos - WINDOW), 0.0, NEG)
    s_w = _dot_nt(qall, kw_ref[0, pl.ds(w0, span), :])
    nkw = span // LANES
    p_rows, inv_rows = [], []
    for h in range(NSA_GROUP):
        for c in range(tq // ROWS):
            r0 = h * tq + c * ROWS
            cols = [s_w[r0:r0 + ROWS, kk * LANES:(kk + 1) * LANES]
                    + bias_w[c * ROWS:(c + 1) * ROWS, kk * LANES:(kk + 1) * LANES] for kk in range(nkw)]
            mx = cols[0]
            for kk in range(1, nkw):
                mx = jnp.maximum(mx, cols[kk])
            mx = jnp.max(mx, axis=-1, keepdims=True)
            pks = [jnp.exp2(cols[kk] - mx) for kk in range(nkw)]
            psum = pks[0]
            for kk in range(1, nkw):
                psum = psum + pks[kk]
            inv_rows.append(jnp.broadcast_to(1.0 / jnp.sum(psum, axis=-1, keepdims=True), (ROWS, LANES)))
            p_rows.append(jnp.concatenate([pk.astype(BF16) for pk in pks], axis=1))
    o_win = _dot(jnp.concatenate(p_rows, axis=0), vw_ref[0, pl.ds(w0, span), :])
    o_win = (o_win * jnp.concatenate(inv_rows, axis=0)).reshape(NSA_GROUP, tq, LANES)

    gt = gate_ref[0, :, g * LANES:(g + 1) * LANES]
    gt = 1.0 / (1.0 + jnp.exp(-gt))
    mine = (lane >= g * HEAD_DIM) & (lane < (g + 1) * HEAD_DIM)
    outs = []
    for h in range(NSA_GROUP):
        o = (gt[:, 3 * h:3 * h + 1] * o_cmp[h] + gt[:, 3 * h + 1:3 * h + 2] * o_sel[h]
             + gt[:, 3 * h + 2:3 * h + 3] * o_win[h])
        o = jnp.where(mine, o, 0.0)
        ms = jnp.sum(o * o, axis=-1, keepdims=True) * (1.0 / HEAD_DIM)
        o = o * lax.rsqrt(ms + EPS)
        outs.append(o + pltpu.roll(o, HEAD_DIM, 1))
    left = lane < HEAD_DIM
    for pair in range(NSA_GROUP // 2):
        col = (g * NSA_GROUP // 2 + pair) * LANES
        o_ref[0, :, col:col + LANES] = (jnp.where(left, outs[2 * pair], outs[2 * pair + 1])
                                        * gn_ref[:, col:col + LANES]).astype(o_ref.dtype)


def _nsa_attention(u, ck, cv, small, gn, cover, expand, layer):
    b, s, _ = u.shape
    tq = min(TQ_NSA, s)
    assert s >= WINDOW + tq and WINDOW % tq == 0
    n_cmp = ck.shape[1]
    blk = LANES
    kv = lambda col: pl.BlockSpec((1, s, LANES), lambda i, j, col=col: (i, 0, col // blk))
    return pl.pallas_call(
        _nsa_kernel,
        grid=(b, s // tq),
        in_specs=[pl.BlockSpec((1, tq, NSA_HEADS * LANES), lambda i, j: (i, j, 0)),
                  pl.BlockSpec((1, n_cmp, LANES), lambda i, j: (i, 0, 0)),
                  pl.BlockSpec((1, n_cmp, LANES), lambda i, j: (i, 0, 0)),
                  kv(COL_KS), kv(COL_VS), kv(COL_KW), kv(COL_VW),
                  pl.BlockSpec((1, tq, NSA_KV * LANES), lambda i, j: (i, j, 0)),
                  pl.BlockSpec((None, 1, NSA_HEADS * HEAD_DIM), lambda i, j: (layer, 0, 0)),
                  pl.BlockSpec(cover.shape, lambda i, j: (0, 0)),
                  pl.BlockSpec(expand.shape, lambda i, j: (0, 0, 0))],
        out_specs=pl.BlockSpec((1, tq, NSA_HEADS * HEAD_DIM), lambda i, j: (i, j, 0)),
        out_shape=jax.ShapeDtypeStruct((b, s, NSA_HEADS * HEAD_DIM), BF16),
        scratch_shapes=_softmax_scratch(NSA_GROUP * tq),
        compiler_params=_cp(2),
        name="nsa_attention",
    )(u, ck, cv, u, u, u, u, small, gn, cover, expand)


def _pair_finish(acc, gn_ref, o_ref, tq):
    lane = lax.broadcasted_iota(jnp.int32, (tq, LANES), 1)
    left = lane < HEAD_DIM
    o = jnp.where(left, acc[0], acc[1])
    o2 = o * o
    ms_l = jnp.sum(jnp.where(left, o2, 0.0), axis=-1, keepdims=True) * (1.0 / HEAD_DIM)
    ms_r = jnp.sum(jnp.where(left, 0.0, o2), axis=-1, keepdims=True) * (1.0 / HEAD_DIM)
    inv = jnp.where(left, lax.rsqrt(ms_l + EPS), lax.rsqrt(ms_r + EPS))
    o_ref[0] = (o * inv * gn_ref[...]).astype(o_ref.dtype)


def _softmax_tile(s, v, m_ref, l_ref, acc_ref, heads, tq, adjust, p_ref=None):
    nk = s.shape[1] // LANES
    p_rows = []
    for h in range(heads):
        for c in range(tq // ROWS):
            t_off = c * ROWS
            r0 = h * tq + t_off
            cols = [s[r0:r0 + ROWS, k * LANES:(k + 1) * LANES] for k in range(nk)]
            cols = adjust(h, t_off, cols)
            mx = cols[0]
            for k in range(1, nk):
                mx = jnp.maximum(mx, cols[k])
            m_old = m_ref[r0:r0 + ROWS, :]
            m_new = jnp.maximum(m_old, jnp.max(mx, axis=-1, keepdims=True))
            alpha = jnp.exp2(m_old - m_new)
            pks = [jnp.exp2(cols[k] - m_new) for k in range(nk)]
            psum = pks[0]
            for k in range(1, nk):
                psum = psum + pks[k]
            p_chunk = jnp.concatenate([pk.astype(BF16) for pk in pks], axis=1)
            if p_ref is None:
                p_rows.append(p_chunk)
            else:
                p_ref[r0:r0 + ROWS, :] = p_chunk
            l_ref[r0:r0 + ROWS, :] = alpha * l_ref[r0:r0 + ROWS, :] + psum
            acc_ref[r0:r0 + ROWS, :] = alpha * acc_ref[r0:r0 + ROWS, :]
            m_ref[r0:r0 + ROWS, :] = m_new
    if p_ref is None:
        acc_ref[...] += _dot(jnp.concatenate(p_rows, axis=0), v)


def _tile_loop(score_fn, process_fn, lo, hi):
    def body(kt, carry):
        process_fn(kt, score_fn(kt))
        return carry

    lax.fori_loop(lo, hi - 1, body, 0)
    return score_fn(hi - 1)


def _softmax_init(m_ref, l_ref, acc_ref):
    m_ref[...] = jnp.full(m_ref.shape, NEG, F32)
    l_ref[...] = jnp.zeros(l_ref.shape, F32)
    acc_ref[...] = jnp.zeros(acc_ref.shape, F32)


def _softmax_result(l_ref, acc_ref):
    return acc_ref[...] / jnp.sum(l_ref[...], axis=-1, keepdims=True)


def _softmax_scratch(rows):
    return [pltpu.VMEM((rows, LANES), F32), pltpu.VMEM((rows, LANES), F32), pltpu.VMEM((rows, LANES), F32)]


def _pair_views(p, q_ref, k_ref, v_ref, gn_ref, o_ref):
    lanes = slice(p * LANES, (p + 1) * LANES)
    return (q_ref.at[:, :, 2 * p * LANES:2 * (p + 1) * LANES], k_ref.at[:, :, lanes], v_ref.at[:, :, lanes],
            gn_ref.at[:, lanes], o_ref.at[:, :, lanes])


def _fox_kernel(q_ref, k_ref, v_ref, cq_ref, ckt_ref, gn_ref, o_ref,
                m_ref, l_ref, acc_ref, cqr_ref, p_ref):
    for p in range(FOX_HEADS // 2):
        q_p, k_p, v_p, gn_p, o_p = _pair_views(p, q_ref, k_ref, v_ref, gn_ref, o_ref)
        _fox_pair(q_p, k_p, v_p, cq_ref.at[:, :, p * LANES:(p + 1) * LANES], ckt_ref.at[:, p:p + 1],
                  gn_p, o_p, m_ref, l_ref, acc_ref, cqr_ref, p_ref)


def _fox_pair(q_ref, k_ref, v_ref, cq_ref, ckt_ref, gn_ref, o_ref,
              m_ref, l_ref, acc_ref, cqr_ref, p_ref):
    tq = q_ref.shape[1]
    tk = TK_FOX
    t0 = pl.program_id(1) * tq
    q = q_ref[0]
    qall = jnp.concatenate([q[:, :LANES], q[:, LANES:]], axis=0)
    cq = cq_ref[0]
    cqr_ref[0:tq, :] = jnp.broadcast_to(cq[:, 0:1], (tq, LANES))
    cqr_ref[tq:2 * tq, :] = jnp.broadcast_to(cq[:, 1:2], (tq, LANES))
    _softmax_init(m_ref, l_ref, acc_ref)
    n_tiles = lax.div(t0 + tq - 1, tk) + 1
    diag = (lax.broadcasted_iota(jnp.int32, (ROWS, LANES), 1)
            - lax.broadcasted_iota(jnp.int32, (ROWS, LANES), 0))

    def scores(kt):
        return _dot_nt(qall, k_ref[0, pl.ds(pl.multiple_of(kt * tk, tk), tk), :])

    def tile(kt, s, masked):
        k0 = pl.multiple_of(kt * tk, tk)
        cks = [ckt_ref[0, 0, kt * (tk // TK_ATT) + j] for j in range(tk // TK_ATT)]

        def adjust(h, t_off, cols):
            out = []
            cqr = cqr_ref[h * tq + t_off:h * tq + t_off + ROWS, :]
            for kk in range(tk // LANES):
                lo = (kk * LANES) % TK_ATT
                ck = cks[(kk * LANES) // TK_ATT][h:h + 1, lo:lo + LANES]
                val = (cols[kk] - ck) + cqr
                if masked:
                    val = jnp.where(diag <= (t0 + t_off) - (k0 + kk * LANES), val, NEG)
                out.append(val)
            return out

        kp = pl.multiple_of(jnp.maximum(kt - 1, 0) * tk, tk)
        acc_ref[...] += _dot(p_ref[...], v_ref[0, pl.ds(kp, tk), :])
        _softmax_tile(s, None, m_ref, l_ref, acc_ref, 2, tq, adjust, p_ref=p_ref)

    p_ref[...] = jnp.zeros(p_ref.shape, BF16)
    s_last = _tile_loop(scores, lambda kt, s: tile(kt, s, False), 0, n_tiles)
    tile(n_tiles - 1, s_last, True)
    acc_ref[...] += _dot(p_ref[...], v_ref[0, pl.ds(pl.multiple_of((n_tiles - 1) * tk, tk), tk), :])
    acc = _softmax_result(l_ref, acc_ref).reshape(2, tq, LANES)
    _pair_finish(acc, gn_ref, o_ref, tq)


def _fox_attention(u, cq, ckt, gn, layer):
    b, s, _ = u.shape
    tq = min(TQ_PAIR, s)
    qw, kvw = FOX_HEADS * LANES, FOX_HEADS * HEAD_DIM
    qb = COL_FOX // qw
    kb = (COL_FOX + qw) // kvw
    gb = NSA_HEADS * HEAD_DIM // kvw
    return pl.pallas_call(
        _fox_kernel,
        grid=(b, s // tq),
        in_specs=[pl.BlockSpec((1, tq, qw), lambda i, j: (i, j, qb)),
                  pl.BlockSpec((1, s, kvw), lambda i, j: (i, 0, kb)),
                  pl.BlockSpec((1, s, kvw), lambda i, j: (i, 0, kb + 1)),
                  pl.BlockSpec((1, tq, 2 * LANES), lambda i, j: (i, j, 0)),
                  pl.BlockSpec((1, 2, s // TK_ATT, ROW_TILE, TK_ATT), lambda i, j: (i, 0, 0, 0, 0)),
                  pl.BlockSpec((None, 1, kvw), lambda i, j: (layer, 0, gb))],
        out_specs=pl.BlockSpec((1, tq, kvw), lambda i, j: (i, j, 0)),
        out_shape=jax.ShapeDtypeStruct((b, s, FOX_HEADS * HEAD_DIM), BF16),
        scratch_shapes=_softmax_scratch(2 * tq) + [pltpu.VMEM((2 * tq, LANES), F32),
                                                   pltpu.VMEM((2 * tq, TK_FOX), BF16)],
        compiler_params=_cp(2),
        name="fox_attention",
    )(u, u, u, cq, ckt, gn)


def _sb_kernel(q_ref, k_ref, v_ref, gn_ref, o_ref, rest_ref, acc_ref):
    for p in range(SB_HEADS // 2):
        _sb_pair(*_pair_views(p, q_ref, k_ref, v_ref, gn_ref, o_ref), rest_ref, acc_ref)


def _sb_pair(q_ref, k_ref, v_ref, gn_ref, o_ref, rest_ref, acc_ref):
    tq = q_ref.shape[1]
    sub = TK_ATT
    tk = SB_SUBS * sub
    nks = sub // LANES
    rows = 2 * tq
    t0 = pl.program_id(1) * tq
    r = lax.broadcasted_iota(jnp.int32, (sub, sub), 0)
    c = lax.broadcasted_iota(jnp.int32, (sub, sub), 1)
    upper = jnp.where(r >= c, 1.0, 0.0).astype(BF16)
    n_tiles = lax.div(t0 + tq - 1, tk) + 1
    diag = (lax.broadcasted_iota(jnp.int32, (ROWS, LANES), 1)
            - lax.broadcasted_iota(jnp.int32, (ROWS, LANES), 0))
    rest_ref[...] = jnp.zeros(rest_ref.shape, F32)
    acc_ref[...] = jnp.zeros(acc_ref.shape, F32)
    hi_mask = jnp.uint32(0xFFFF0000)
    q = q_ref[0]
    qall = jnp.concatenate([q[:, :LANES], q[:, LANES:]], axis=0)
    chunks = [(h * tq + cc * ROWS, cc * ROWS) for h in range(2) for cc in range(tq // ROWS)]

    def tile(kt, masked):
        k0 = pl.multiple_of(kt * tk, tk)
        k = k_ref[0, pl.ds(k0, tk), :]
        v = v_ref[0, pl.ds(k0, tk), :]

        def strictly_before(t_off, col):
            return diag < (t0 + t_off) - (k0 + col)

        z = _dot_nt(qall, k)
        his = [[] for _ in range(SB_SUBS)]
        los = [[] for _ in range(SB_SUBS)]
        for r0, t_off in chunks:
            for sb in range(SB_SUBS):
                hi_c, lo_c = [], []
                for kk in range(nks):
                    col = sb * sub + kk * LANES
                    zc = z[r0:r0 + ROWS, col:col + LANES]
                    l = -(jnp.maximum(zc, 0.0) + jnp.log2(1.0 + jnp.exp2(-jnp.abs(zc))))
                    if masked:
                        l = jnp.where(strictly_before(t_off, col), l, 0.0)
                    hi = pltpu.bitcast(pltpu.bitcast(l, jnp.uint32) & hi_mask, F32)
                    hi_c.append(hi.astype(BF16))
                    lo_c.append((l - hi).astype(BF16))
                his[sb].append(jnp.concatenate(hi_c, axis=1))
                los[sb].append(jnp.concatenate(lo_c, axis=1))
        hi_all = jnp.concatenate([x for sb in range(SB_SUBS) for x in his[sb]], axis=0)
        lo_all = jnp.concatenate([x for sb in range(SB_SUBS) for x in los[sb]], axis=0)
        cum = _dot(hi_all, upper) + _dot(lo_all, upper)

        a_rows = []
        for r0, t_off in chunks:
            base = rest_ref[r0:r0 + ROWS, :]
            a_c = [None] * (SB_SUBS * nks)
            for sb in reversed(range(SB_SUBS)):
                cs = cum[sb * rows + r0:sb * rows + r0 + ROWS, :]
                for kk in range(nks):
                    col = sb * sub + kk * LANES
                    a = jnp.exp2(z[r0:r0 + ROWS, col:col + LANES] + cs[:, kk * LANES:(kk + 1) * LANES] + base)
                    if masked:
                        a = jnp.where(strictly_before(t_off, col), a, 0.0)
                    a_c[sb * nks + kk] = a.astype(BF16)
                base = base + jnp.broadcast_to(cs[:, 0:1], (ROWS, LANES))
            a_rows.append(jnp.concatenate(a_c, axis=1))
            rest_ref[r0:r0 + ROWS, :] = base
        acc_ref[...] += _dot(jnp.concatenate(a_rows, axis=0), v)

    tile(n_tiles - 1, True)

    def full_tile(i, carry):
        tile(n_tiles - 2 - i, False)
        return carry

    lax.fori_loop(0, n_tiles - 1, full_tile, 0)
    _pair_finish(acc_ref[...].reshape(2, tq, LANES), gn_ref, o_ref, tq)


def _sb_attention(u, gn, layer):
    b, s, _ = u.shape
    tq = min(TQ_PAIR, s)
    qw, kvw = SB_HEADS * LANES, SB_HEADS * HEAD_DIM
    qb = COL_SB // qw
    kb = (COL_SB + qw) // kvw
    gb = (NSA_HEADS + FOX_HEADS) * HEAD_DIM // kvw
    return pl.pallas_call(
        _sb_kernel,
        grid=(b, s // tq),
        in_specs=[pl.BlockSpec((1, tq, qw), lambda i, j: (i, j, qb)),
                  pl.BlockSpec((1, s, kvw), lambda i, j: (i, 0, kb)),
                  pl.BlockSpec((1, s, kvw), lambda i, j: (i, 0, kb + 1)),
                  pl.BlockSpec((None, 1, kvw), lambda i, j: (layer, 0, gb))],
        out_specs=pl.BlockSpec((1, tq, kvw), lambda i, j: (i, j, 0)),
        out_shape=jax.ShapeDtypeStruct((b, s, SB_HEADS * HEAD_DIM), BF16),
        scratch_shapes=[pltpu.VMEM((2 * tq, LANES), F32), pltpu.VMEM((2 * tq, LANES), F32)],
        compiler_params=_cp(2),
        name="sb_attention",
    )(u, u, u, gn)


def _outproj_kernel(oa_ref, ob_ref, oc_ref, x_ref, w_ref, g1_ref, n2_ref, sc_ref, sh_ref,
                    wrh_ref, wrl_ref, br_ref, xo_ref, h_ref, rw_ref, ri_ref):
    na = oa_ref.shape[2]
    nb = ob_ref.shape[2]
    y = _dot(oa_ref[0], w_ref[0:na, :])
    y = y + _dot(ob_ref[0], w_ref[na:na + nb, :])
    y = y + _dot(oc_ref[0], w_ref[na + nb:, :])
    x = x_ref[0] + g1_ref[0] * y
    xo_ref[0] = x
    ms = jnp.mean(x * x, axis=-1, keepdims=True)
    h = (x * lax.rsqrt(ms + EPS) * n2_ref[...]) * (1.0 + sc_ref[0]) + sh_ref[0]
    hb = h.astype(BF16)
    _store_row_tiles(h_ref, h)
    hl = (h - hb.astype(F32)).astype(BF16)
    logit = _dot(hb, wrh_ref[...]) + _dot(hl, wrh_ref[...]) + _dot(hb, wrl_ref[...]) + br_ref[...]

    tm = logit.shape[0]
    lane = lax.broadcasted_iota(jnp.int32, (tm, LANES), 1).astype(F32)
    big = float(LANES)
    is_g = lane < N_GROUPS
    lg = jnp.where(is_g, logit, NEG)
    mg = jnp.max(lg, axis=-1, keepdims=True)
    zg = jnp.sum(jnp.where(is_g, jnp.exp(lg - mg), 0.0), axis=-1, keepdims=True)
    pg = 1.0 / zg
    gi = jnp.min(jnp.where(is_g & (lg == mg), lane, big), axis=-1, keepdims=True)
    e_lane = lane - N_GROUPS
    in_grp = (e_lane >= gi * EXPERTS_PER_GROUP) & (e_lane < (gi + 1) * EXPERTS_PER_GROUP)
    le = jnp.where(in_grp, logit, NEG)
    m1 = jnp.max(le, axis=-1, keepdims=True)
    i1 = jnp.min(jnp.where(in_grp & (le == m1), lane, big), axis=-1, keepdims=True)
    rest = in_grp & (lane != i1)
    le2 = jnp.where(rest, logit, NEG)
    m2 = jnp.max(le2, axis=-1, keepdims=True)
    i2 = jnp.min(jnp.where(rest & (le2 == m2), lane, big), axis=-1, keepdims=True)
    ze = jnp.sum(jnp.where(in_grp, jnp.exp(le - m1), 0.0), axis=-1, keepdims=True)
    p1 = 1.0 / ze
    p2 = jnp.exp(m2 - m1) / ze
    den = p1 + p2
    w1 = pg * (p1 / den)
    w2 = pg * (p2 / den)
    rw_ref[0] = jnp.where(lane == 0.0, w1, jnp.where(lane == 1.0, w2, 0.0))
    ri_ref[0] = jnp.where(lane == 0.0, i1 - N_GROUPS, jnp.where(lane == 1.0, i2 - N_GROUPS, 0.0)).astype(jnp.int32)


def _out_projection(oa, ob, oc, x, w_out, mod4, n2, wr_hi, wr_lo, br, layer):
    b, s, d = x.shape
    tm = min(TM_OUT, s)
    row = lambda i, j: (i, j, 0)
    return pl.pallas_call(
        _outproj_kernel,
        grid=(b, s // tm),
        in_specs=[pl.BlockSpec((1, tm, oa.shape[2]), row),
                  pl.BlockSpec((1, tm, ob.shape[2]), row),
                  pl.BlockSpec((1, tm, oc.shape[2]), row),
                  pl.BlockSpec((1, tm, d), row),
                  _layer_spec(layer, w_out.shape[1:]),
                  _mod_spec(layer, 2, d),
                  _layer_spec(layer, (1, d)),
                  _mod_spec(layer, 4, d),
                  _mod_spec(layer, 3, d),
                  _layer_spec(layer, (d, LANES)),
                  _layer_spec(layer, (d, LANES)),
                  _layer_spec(layer, (1, LANES))],
        out_specs=[pl.BlockSpec((1, tm, d), row),
                   pl.BlockSpec((tm * ROW_TILE, LANES), lambda i, j: (i * (s // tm) + j, 0)),
                   pl.BlockSpec((1, tm, LANES), row),
                   pl.BlockSpec((1, tm, LANES), row)],
        out_shape=[jax.ShapeDtypeStruct((b, s, d), F32),
                   jax.ShapeDtypeStruct((b * s * ROW_TILE, LANES), F32),
                   jax.ShapeDtypeStruct((b, s, LANES), F32),
                   jax.ShapeDtypeStruct((b, s, LANES), jnp.int32)],
        compiler_params=_cp(2),
        name="out_projection",
    )(oa, ob, oc, x, w_out, mod4, n2.reshape(-1, 1, d), mod4, mod4, wr_hi, wr_lo, br)


def _store_row_tiles(ref, val):
    tm = val.shape[0]
    for c in range(ROW_TILE):
        ref[pl.ds(c, tm, stride=ROW_TILE), :] = val[:, c * LANES:(c + 1) * LANES]


def _load_row_tiles(ref, tm):
    return [ref[pl.ds(c, tm, stride=ROW_TILE), :] for c in range(ROW_TILE)]


def _tile_rows(ref, n):
    return ref.at[pl.ds(pl.multiple_of(n * ROW_TILE, ROW_TILE), ROW_TILE), :]


def _dispatch_kernel(idx_ref, src_ref, init_hbm, dst_hbm, sem):
    del init_hbm

    def issue(tok, carry):
        for k in range(2):
            pltpu.make_async_copy(_tile_rows(src_ref, tok), _tile_rows(dst_hbm, idx_ref[0, 0, 2 * tok + k]),
                                  sem).start(priority=k)
        return carry

    lax.fori_loop(0, COPY_CHUNK // 2, issue, 0, unroll=4)
    for _ in range(2):
        pltpu.make_async_copy(src_ref, dst_hbm.at[pl.ds(0, src_ref.shape[0]), :], sem).wait()


def _dispatch(src, idx, n_dst, init):
    n = idx.shape[0]
    if init is None:
        init = jnp.zeros((n_dst * ROW_TILE, LANES), src.dtype)
    return pl.pallas_call(
        _dispatch_kernel,
        grid=(n // COPY_CHUNK,),
        in_specs=[pl.BlockSpec((1, 1, COPY_CHUNK), lambda i: (i, 0, 0), memory_space=pltpu.SMEM),
                  pl.BlockSpec((COPY_CHUNK // 2 * ROW_TILE, LANES), lambda i: (i, 0)),
                  pl.BlockSpec(memory_space=pl.ANY)],
        out_specs=pl.BlockSpec(memory_space=pl.ANY),
        out_shape=jax.ShapeDtypeStruct((n_dst * ROW_TILE, LANES), src.dtype),
        scratch_shapes=[pltpu.SemaphoreType.DMA],
        input_output_aliases={2: 0},
        compiler_params=_cp(1),
        name="moe_dispatch",
    )(idx.reshape(n // COPY_CHUNK, 1, COPY_CHUNK), src, init)


def _expert_kernel(te_ref, nu_ref, x_ref, w1_ref, w3_ref, w2_ref, y_ref, w1b_ref, w3b_ref, w2b_ref):
    i = pl.program_id(0)

    @pl.when((i == 0) | (te_ref[i] != te_ref[jnp.maximum(i - 1, 0)]))
    def _():
        w1b_ref[...] = w1_ref[0].astype(BF16)
        w3b_ref[...] = w3_ref[0].astype(BF16)
        w2b_ref[...] = w2_ref[0].astype(BF16)

    @pl.when(i < nu_ref[0])
    def _():
        x = jnp.concatenate(_load_row_tiles(x_ref, TM_EXP), axis=1).astype(BF16)
        a = _dot(x, w1b_ref[...])
        g = _dot(x, w3b_ref[...])
        act = (a * (1.0 / (1.0 + jnp.exp(-a)))) * g
        _store_row_tiles(y_ref, _dot(act.astype(BF16), w2b_ref[...]))

    @pl.when(i >= nu_ref[0])
    def _():
        y_ref[...] = jnp.zeros_like(y_ref)


def _expert_mlp(xs, tile_expert, n_used, w1, w3, w2, layer):
    d, de = w1.shape[2], w1.shape[3]
    n_tiles = xs.shape[0] // (TM_EXP * ROW_TILE)
    grid_spec = pltpu.PrefetchScalarGridSpec(
        num_scalar_prefetch=2,
        grid=(n_tiles,),
        in_specs=[pl.BlockSpec((TM_EXP * ROW_TILE, LANES), lambda i, te, nu: (i, 0)),
                  pl.BlockSpec((None, 1, d, de), lambda i, te, nu: (layer, te[i], 0, 0)),
                  pl.BlockSpec((None, 1, d, de), lambda i, te, nu: (layer, te[i], 0, 0)),
                  pl.BlockSpec((None, 1, de, d), lambda i, te, nu: (layer, te[i], 0, 0))],
        out_specs=pl.BlockSpec((TM_EXP * ROW_TILE, LANES), lambda i, te, nu: (i, 0)),
        scratch_shapes=[pltpu.VMEM((d, de), BF16), pltpu.VMEM((d, de), BF16), pltpu.VMEM((de, d), BF16)],
    )
    return pl.pallas_call(
        _expert_kernel,
        grid_spec=grid_spec,
        out_shape=jax.ShapeDtypeStruct(xs.shape, F32),
        compiler_params=_cp(1),
        name="expert_mlp",
    )(tile_expert, n_used, xs, w1, w3, w2)


def _combine_kernel(d0_ref, d1_ref, x_ref, ys_hbm, rw_ref, g2_ref, fg_ref, o_ref, y0_ref, y1_ref, sem,
                    *, final):
    tm = x_ref.shape[1]

    def issue(r, carry):
        pltpu.make_async_copy(_tile_rows(ys_hbm, d0_ref[0, 0, r]), _tile_rows(y0_ref, r),
                              sem.at[0]).start(priority=0)
        pltpu.make_async_copy(_tile_rows(ys_hbm, d1_ref[0, 0, r]), _tile_rows(y1_ref, r),
                              sem.at[1]).start(priority=1)
        return carry

    lax.fori_loop(0, tm, issue, 0, unroll=8)
    pltpu.make_async_copy(ys_hbm.at[pl.ds(0, tm * ROW_TILE), :], y0_ref, sem.at[0]).wait()
    pltpu.make_async_copy(ys_hbm.at[pl.ds(0, tm * ROW_TILE), :], y1_ref, sem.at[1]).wait()

    rw = rw_ref[0]
    w0 = jnp.broadcast_to(rw[:, 0:1], (tm, LANES))
    w1 = jnp.broadcast_to(rw[:, 1:2], (tm, LANES))
    y0 = _load_row_tiles(y0_ref, tm)
    y1 = _load_row_tiles(y1_ref, tm)
    cols = []
    for c in range(ROW_TILE):
        sl = slice(c * LANES, (c + 1) * LANES)
        cols.append(x_ref[0, :, sl] + g2_ref[0, :, sl] * (y0[c] * w0 + y1[c] * w1))
    if final:
        ssq = cols[0] * cols[0]
        for c in range(1, ROW_TILE):
            ssq = ssq + cols[c] * cols[c]
        inv = lax.rsqrt(jnp.sum(ssq, axis=-1, keepdims=True) * (1.0 / (ROW_TILE * LANES)) + EPS)
        cols = [cols[c] * inv * fg_ref[:, c * LANES:(c + 1) * LANES] for c in range(ROW_TILE)]
    for c in range(ROW_TILE):
        o_ref[0, :, c * LANES:(c + 1) * LANES] = cols[c]


def _combine(x, ys, dest0, dest1, rw, mod4, final_g, final, layer):
    b, s, d = x.shape
    tm = min(TM_CMB, s)
    row = lambda i, j: (i, j, 0)
    idx_spec = pl.BlockSpec((1, 1, tm), lambda i, j: (i * (s // tm) + j, 0, 0), memory_space=pltpu.SMEM)
    return pl.pallas_call(
        functools.partial(_combine_kernel, final=final),
        grid=(b, s // tm),
        in_specs=[idx_spec, idx_spec,
                  pl.BlockSpec((1, tm, d), row),
                  pl.BlockSpec(memory_space=pl.ANY),
                  pl.BlockSpec((1, tm, LANES), row),
                  _mod_spec(layer, 5, d),
                  pl.BlockSpec((1, d), lambda i, j: (0, 0))],
        out_specs=pl.BlockSpec((1, tm, d), row),
        out_shape=jax.ShapeDtypeStruct((b, s, d), F32),
        scratch_shapes=[pltpu.VMEM((tm * ROW_TILE, LANES), F32), pltpu.VMEM((tm * ROW_TILE, LANES), F32),
                        pltpu.SemaphoreType.DMA((2,))],
        compiler_params=_cp(2),
        name="moe_combine_final" if final else "moe_combine",
    )(dest0.reshape(-1, 1, tm), dest1.reshape(-1, 1, tm), x, ys, rw, mod4, final_g.reshape(1, d))


def _layout_w_in(w_in):
    d = w_in.shape[0]
    kvw = NSA_KV * HEAD_DIM
    sizes = (NSA_HEADS * HEAD_DIM, kvw, kvw, kvw, kvw, kvw, kvw, NSA_HEADS * 3,
             FOX_HEADS * HEAD_DIM, FOX_HEADS * HEAD_DIM, FOX_HEADS * HEAD_DIM, FOX_HEADS,
             SB_HEADS * HEAD_DIM, SB_HEADS * HEAD_DIM, SB_HEADS * HEAD_DIM)
    pts = np.cumsum(sizes)[:-1].tolist()
    (qa, kca, vca, ksa, vsa, kwa, vwa, ga, qb, kb, vb, fb, qc, kc, vc) = jnp.split(w_in, pts, axis=1)
    scale = HEAD_DIM ** -0.5 * LOG2E
    main = jnp.concatenate([qa * scale, ksa, kwa, vsa, vwa, qb * scale, kb, vb, qc * scale, kc, vc],
                           axis=1).astype(BF16)
    cmp_w = jnp.concatenate([kca, vca], axis=1).astype(BF16)
    zpad = lambda n: jnp.zeros((d, n), w_in.dtype)
    per_grp = NSA_GROUP * 3
    small = jnp.concatenate([ga[:, :per_grp], zpad(LANES - per_grp), ga[:, per_grp:], zpad(LANES - per_grp),
                             fb[:, 0:2], zpad(LANES - 2), fb[:, 2:4], zpad(LANES - 2)], axis=1).astype(BF16)
    return main, cmp_w, small


def _layout_cmp(w1, w2):
    hid = w1.shape[1]
    w1r = w1.reshape(2, CMP_STRIDE, HEAD_DIM, hid)
    z = jnp.zeros((CMP_STRIDE, HEAD_DIM, hid), w1.dtype)
    cols = []
    for half in range(2):
        for g in range(NSA_KV):
            parts = [w1r[half] if gg == g else z for gg in range(NSA_KV)]
            cols.append(jnp.concatenate(parts, axis=1).reshape(CMP_STRIDE * LANES, hid))
    wcat = jnp.concatenate(cols, axis=1).astype(BF16)
    zz = jnp.zeros_like(w2)
    w2bd = jnp.concatenate([jnp.concatenate([w2, zz], axis=1),
                            jnp.concatenate([zz, w2], axis=1)], axis=0).astype(BF16)
    return wcat, w2bd


def _rope_tables(pos):
    inv = jnp.exp(jnp.arange(ROPE_HALF, dtype=F32) * (-2.0 * math.log(ROPE_THETA) / ROPE_DIM))
    ang = pos.astype(F32)[:, None] * inv[None, :]
    cos, sin = jnp.cos(ang), jnp.sin(ang)
    n = pos.shape[0]
    z8 = jnp.zeros((n, ROPE_HALF), F32)
    rest1 = jnp.ones((n, HEAD_DIM - ROPE_DIM), F32)
    rest0 = jnp.zeros((n, HEAD_DIM - ROPE_DIM), F32)
    c = jnp.concatenate([cos, cos, rest1], axis=1)
    s1 = jnp.concatenate([z8, sin, rest0], axis=1)
    s2 = jnp.concatenate([-sin, z8, rest0], axis=1)
    dup = lambda a: jnp.concatenate([a, a], axis=1)
    return dup(c), dup(s1), dup(s2)


def _static_tables(s):
    n_cmp_pad = s // CMP_STRIDE
    n = np.arange(n_cmp_pad)[:, None]
    j = np.arange(LANES)[None, :]
    n_sel = s // SEL_LEN
    cover = ((n * CMP_STRIDE < j * SEL_LEN + SEL_LEN) & (n * CMP_STRIDE + CMP_LEN > j * SEL_LEN)
             & (j < n_sel)).astype(np.float32)
    nt = s // TK_FOX
    key = np.arange(nt)[:, None, None] * TK_FOX + np.arange(TK_FOX)[None, None, :]
    expand = (key // SEL_LEN == np.arange(LANES)[None, :, None]).astype(np.float32)
    return jnp.asarray(cover.T, BF16), jnp.asarray(expand, BF16)


def _plan_kernel(ri_ref, dest_ref, te_ref, cnt_ref, pst_ref):
    phase = pl.program_id(0)
    j = pl.program_id(1)
    tm = ri_ref.shape[1]
    lane = lax.broadcasted_iota(jnp.int32, (tm, LANES), 1)
    ri = ri_ref[0]
    hot0 = jnp.where(lane == ri[:, 0:1], 1.0, 0.0)
    hot1 = jnp.where(lane == ri[:, 1:2], 1.0, 0.0)
    both = hot0 + hot1

    @pl.when((phase == 0) & (j == 0))
    def _():
        cnt_ref[...] = jnp.zeros(cnt_ref.shape, F32)

    @pl.when(phase == 0)
    def _():
        cnt_ref[...] += jnp.sum(both, axis=0, keepdims=True)

    @pl.when((phase == 0) & (j == pl.num_programs(1) - 1))
    def _():
        cnt = cnt_ref[...]
        padded = jnp.ceil(cnt * (1.0 / TM_EXP)) * TM_EXP
        r = lax.broadcasted_iota(jnp.int32, (LANES, LANES), 0)
        c = lax.broadcasted_iota(jnp.int32, (LANES, LANES), 1)
        incl = jnp.where(r <= c, 1.0, 0.0).astype(BF16)
        rows = jnp.broadcast_to(padded, (LANES, LANES)).astype(BF16)
        pends = _dot(rows, incl)
        pst_ref[...] = pends[0:1, :] - padded
        ends = pends.T
        tile_start = (c * TM_EXP).astype(F32)
        done = jnp.where((r < N_EXPERTS) & (ends <= tile_start), 1.0, 0.0)
        te = jnp.minimum(jnp.sum(done, axis=0, keepdims=True), float(N_EXPERTS - 1))
        used = pends[0:1, N_EXPERTS - 1:N_EXPERTS] * (1.0 / TM_EXP)
        lane1 = lax.broadcasted_iota(jnp.int32, (1, LANES), 1)
        te_ref[...] = jnp.concatenate(
            [te, jnp.where(lane1 == 0, used, 0.0)] + [jnp.zeros((ROW_TILE - 2, LANES), F32)], axis=0
        ).astype(jnp.int32)
        cnt_ref[...] = jnp.zeros(cnt_ref.shape, F32)

    @pl.when(phase == 1)
    def _():
        r = lax.broadcasted_iota(jnp.int32, (tm, tm), 0)
        c = lax.broadcasted_iota(jnp.int32, (tm, tm), 1)
        before = jnp.where(c < r, 1.0, 0.0).astype(BF16)
        earlier = _dot(before, both.astype(BF16)) + cnt_ref[...]
        slot = earlier + pst_ref[...]
        d0 = jnp.sum(hot0 * slot, axis=-1, keepdims=True)
        d1 = jnp.sum(hot1 * slot, axis=-1, keepdims=True)
        dest_ref[0] = jnp.where(lane == 0, d0, jnp.where(lane == 1, d1, 0.0)).astype(jnp.int32)
        cnt_ref[...] += jnp.sum(both, axis=0, keepdims=True)


def _dispatch_plan(ri, t):
    b, s, _ = ri.shape
    tm = min(TM_CMB, s)
    n_tiles = -(-(2 * t + N_EXPERTS * (TM_EXP - 1)) // TM_EXP)
    assert n_tiles <= LANES and N_EXPERTS <= LANES
    chunks = s // tm
    dest, te = pl.pallas_call(
        _plan_kernel,
        grid=(2, b * chunks),
        in_specs=[pl.BlockSpec((1, tm, LANES), lambda ph, j: (j // chunks, j % chunks, 0))],
        out_specs=[pl.BlockSpec((1, tm, LANES), lambda ph, j: (ph * (j // chunks), ph * (j % chunks), 0)),
                   pl.BlockSpec((ROW_TILE, LANES), lambda ph, j: (0, 0))],
        out_shape=[jax.ShapeDtypeStruct((b, s, LANES), jnp.int32),
                   jax.ShapeDtypeStruct((ROW_TILE, LANES), jnp.int32)],
        scratch_shapes=[pltpu.VMEM((1, LANES), F32), pltpu.VMEM((1, LANES), F32)],
        compiler_params=_cp(2),
        name="moe_plan",
    )(ri)
    dest = dest.reshape(t, LANES)[:, :2].reshape(-1)
    return dest, n_tiles * TM_EXP, te[0, :n_tiles], te[1, 0:1]


def kernel(x, c, norm1_g, norm2_g, ada_w, ada_b, w_in, b_forget, cmp_pos_k, cmp_w1_k, cmp_w2_k,
           cmp_pos_v, cmp_w1_v, cmp_w2_v, out_norm_g, w_out, router_group_w, router_group_b,
           router_expert_w, router_expert_b, expert_w1, expert_w3, expert_w2, final_g):
    b, s, d = x.shape
    depth = ada_w.shape[0]
    t = b * s
    mod = _modulation(c, ada_w, ada_b)
    rope_c, rope_1, rope_2 = _rope_tables(jnp.arange(s))
    n_cmp_pad = s // CMP_STRIDE
    crc, cr1, cr2 = _rope_tables(jnp.arange(n_cmp_pad) * CMP_STRIDE + (CMP_LEN - 1))
    cover, expand = _static_tables(s)
    xs = None

    mod4 = mod.reshape(depth, b, 6, 1, d)
    w_main, w_cmp, w_small = jax.vmap(_layout_w_in)(w_in)
    b_pairs = jnp.pad(b_forget.reshape(depth, 2, 1, 2), ((0, 0), (0, 0), (0, 0), (0, LANES - 2)))
    wk, w2k = jax.vmap(_layout_cmp)(cmp_w1_k, cmp_w2_k)
    wv, w2v = jax.vmap(_layout_cmp)(cmp_w1_v, cmp_w2_v)
    pek = jnp.broadcast_to(cmp_pos_k.reshape(depth, 1, -1), (depth, ROW_TILE, CMP_LEN * HEAD_DIM))
    pev = jnp.broadcast_to(cmp_pos_v.reshape(depth, 1, -1), (depth, ROW_TILE, CMP_LEN * HEAD_DIM))
    gn = out_norm_g.reshape(depth, 1, -1)
    wr = jnp.concatenate([router_group_w, router_expert_w,
                          jnp.zeros((depth, d, LANES - N_GROUPS - N_EXPERTS), F32)], axis=2)
    wr_hi = wr.astype(BF16)
    wr_lo = (wr - wr_hi.astype(F32)).astype(BF16)
    br = jnp.concatenate([router_group_b, router_expert_b,
                          jnp.zeros((depth, LANES - N_GROUPS - N_EXPERTS), F32)], axis=1).reshape(depth, 1, LANES)
    w_out_b = w_out.astype(BF16)

    for l in range(depth):
        u, kc, vc, small = _in_projection(x, norm1_g, mod4, w_main, w_cmp, w_small, rope_c, rope_1, rope_2, l)
        cq, ckt = _forget_cumsum(small, b_pairs, l)
        ck, cv = _compress(kc, vc, wk[l], wv[l], cmp_w1_k[l], cmp_w1_v[l], pek[l], pev[l],
                           w2k[l], w2v[l], crc, cr1, cr2)
        o_a = _nsa_attention(u, ck, cv, small, gn, cover, expand, l)
        o_b = _fox_attention(u, cq, ckt, gn, l)
        o_c = _sb_attention(u, gn, l)
        x, h2, rw, ri = _out_projection(o_a, o_b, o_c, x, w_out_b, mod4, norm2_g, wr_hi, wr_lo, br, l)

        dest, p_rows, tile_expert, n_used = _dispatch_plan(ri, t)
        xs = _dispatch(h2, dest, p_rows, xs)
        ys = _expert_mlp(xs, tile_expert, n_used, expert_w1, expert_w3, expert_w2, l)
        x = _combine(x, ys, dest[0::2], dest[1::2], rw, mod4, final_g, final=(l == depth - 1), layer=l)
    return x
```

```python
import functools
import math

import numpy as np
import jax
import jax.numpy as jnp
from jax import lax
from jax.experimental import pallas as pl
from jax.experimental.pallas import tpu as pltpu

F32 = jnp.float32
BF16 = jnp.bfloat16

HEAD_DIM = 64
LANES = 128
NSA_HEADS = 8
NSA_KV = 2
NSA_GROUP = 4
FOX_HEADS = 4
SB_HEADS = 4
ROPE_DIM = 16
ROPE_HALF = 8
ROPE_THETA = 500000.0
CMP_LEN = 32
CMP_STRIDE = 16
CMP_HIDDEN = 128
SEL_LEN = 64
SEL_TOPN = 16
WINDOW = 512
FORCE_SCORE = 1.0e4
N_GROUPS = 4
EXPERTS_PER_GROUP = 8
N_EXPERTS = 32
EPS = 1e-6
LOG2E = math.log2(math.e)
NEG = -1e30

COL_QA = 0
COL_KS = 1024
COL_KW = 1152
COL_VS = 1280
COL_VW = 1408
COL_FOX = 1536
COL_SB = 2560
N_MAIN = 3584
N_SMALL = 512
W_ROPE = (NSA_HEADS + 2 * NSA_KV) * HEAD_DIM
W_MAIN = W_ROPE + (2 * NSA_KV + 3 * FOX_HEADS + 3 * SB_HEADS) * HEAD_DIM

TM_PROJ = 512
TQ_NSA = 128
TK_ATT = 256
TQ_PAIR = 256
TK_FOX = 512
ROWS = 32
SB_SUBS = 2
TM_OUT = 512
TM_EXP = 512
TM_CMB = 512
COPY_CHUNK = 512
ROW_TILE = 8
VMEM_LIMIT = 56 * 1024 * 1024


def _cp(n_axes, vmem=VMEM_LIMIT):
    return pltpu.CompilerParams(dimension_semantics=("arbitrary",) * n_axes, vmem_limit_bytes=vmem)


def _dot(a, b):
    return jnp.dot(a, b, preferred_element_type=F32)


def _dot_nt(a, b):
    return lax.dot_general(a, b, (((1,), (1,)), ((), ())), preferred_element_type=F32)


def _split_bf16(x, parts):
    out = []
    r = x
    for _ in range(parts):
        p = r.astype(BF16)
        out.append(p)
        r = r - p.astype(F32)
    return out


def _rope(x, c, s1, s2):
    return x * c + pltpu.roll(x, ROPE_HALF, 1) * s1 + pltpu.roll(x, LANES - ROPE_HALF, 1) * s2


def _softplus(z):
    return jnp.maximum(z, 0.0) + jnp.log(1.0 + jnp.exp(-jnp.abs(z)))


def _mod_kernel(c_ref, w_ref, b_ref, o_ref):
    c = c_ref[...]
    cond = c * (1.0 / (1.0 + jnp.exp(-c)))
    o_ref[0] = _dot(cond, w_ref[0]) + b_ref[0]


def _modulation(c, ada_w, ada_b):
    depth, d, n = ada_w.shape
    b = c.shape[0]
    tn = 1024
    return pl.pallas_call(
        _mod_kernel,
        grid=(depth, n // tn),
        in_specs=[pl.BlockSpec((b, d), lambda l, j: (0, 0)),
                  pl.BlockSpec((1, d, tn), lambda l, j: (l, 0, j)),
                  pl.BlockSpec((1, 1, tn), lambda l, j: (l, 0, j))],
        out_specs=pl.BlockSpec((1, b, tn), lambda l, j: (l, 0, j)),
        out_shape=jax.ShapeDtypeStruct((depth, b, n), F32),
        compiler_params=_cp(2),
        name="modulation",
    )(c, ada_w, ada_b.reshape(depth, 1, n))


def _inproj_kernel(x_ref, g_ref, sc_ref, sh_ref, w_ref, wc_ref, ws_ref, rc_ref, r1_ref, r2_ref,
                   u_ref, kc_ref, vc_ref, sm_ref):
    x = x_ref[0]
    ms = jnp.mean(x * x, axis=-1, keepdims=True)
    h = (x * lax.rsqrt(ms + EPS) * g_ref[...]) * (1.0 + sc_ref[0]) + sh_ref[0]
    hb = h.astype(BF16)
    rc, r1, r2 = rc_ref[...], r1_ref[...], r2_ref[...]
    lane = lax.broadcasted_iota(jnp.int32, (x.shape[0], LANES), 1)
    left = lane < HEAD_DIM

    def put(col, val):
        u_ref[0, :, col:col + LANES] = val.astype(BF16)

    def put_pair(col, blk, offsets):
        for i, off in enumerate(offsets):
            src = blk if off == i * HEAD_DIM else pltpu.roll(blk, HEAD_DIM, 1)
            put(col + i * LANES, jnp.where(left if off == 0 else ~left, src, 0.0))

    acc = _dot(hb, w_ref[:, :W_ROPE])
    for j in range(NSA_HEADS // 2):
        off = (2 * j // NSA_GROUP) * HEAD_DIM
        put_pair(COL_QA + 2 * j * LANES, _rope(acc[:, j * LANES:(j + 1) * LANES], rc, r1, r2), (off, off))
    nq = NSA_HEADS // 2
    put(COL_KS, _rope(acc[:, nq * LANES:(nq + 1) * LANES], rc, r1, r2))
    put(COL_KW, _rope(acc[:, (nq + 1) * LANES:(nq + 2) * LANES], rc, r1, r2))
    acc = _dot(hb, w_ref[:, W_ROPE:])
    u_ref[0, :, COL_VS:COL_VS + 2 * LANES] = acc[:, :2 * LANES].astype(BF16)
    c = 2 * LANES
    for base, heads in ((COL_FOX, FOX_HEADS), (COL_SB, SB_HEADS)):
        for j in range(heads // 2):
            put_pair(base + 2 * j * LANES, acc[:, c:c + LANES], (0, HEAD_DIM))
            c += LANES
        kv = 2 * heads * HEAD_DIM
        u_ref[0, :, base + heads * LANES:base + heads * LANES + kv] = acc[:, c:c + kv].astype(BF16)
        c += kv
    cmp_in = _dot(hb, wc_ref[...])
    kc_ref[0] = cmp_in[:, :LANES].astype(BF16)
    vc_ref[0] = cmp_in[:, LANES:].astype(BF16)
    sm_ref[0] = _dot(hb, ws_ref[...])


def _mod_spec(layer, which, d):
    return pl.BlockSpec((None, 1, None, 1, d), lambda i, j: (layer, i, which, 0, 0))


def _layer_spec(layer, shape):
    zeros = (0,) * len(shape)
    return pl.BlockSpec((None,) + tuple(shape), lambda *_: (layer,) + zeros)


def _in_projection(x, g, mod4, w_main, w_cmp, w_small, rope_c, rope_1, rope_2, layer):
    b, s, d = x.shape
    tm = min(TM_PROJ, s)
    row = lambda i, j: (i, j, 0)
    seq = lambda i, j: (j, 0)
    return pl.pallas_call(
        _inproj_kernel,
        grid=(b, s // tm),
        in_specs=[pl.BlockSpec((1, tm, d), row),
                  _layer_spec(layer, (1, d)),
                  _mod_spec(layer, 1, d),
                  _mod_spec(layer, 0, d),
                  _layer_spec(layer, (d, W_MAIN)),
                  _layer_spec(layer, (d, 2 * LANES)),
                  _layer_spec(layer, (d, N_SMALL)),
                  pl.BlockSpec((tm, LANES), seq),
                  pl.BlockSpec((tm, LANES), seq),
                  pl.BlockSpec((tm, LANES), seq)],
        out_specs=[pl.BlockSpec((1, tm, N_MAIN), row),
                   pl.BlockSpec((1, tm, LANES), row),
                   pl.BlockSpec((1, tm, LANES), row),
                   pl.BlockSpec((1, tm, N_SMALL), row)],
        out_shape=[jax.ShapeDtypeStruct((b, s, N_MAIN), BF16),
                   jax.ShapeDtypeStruct((b, s, LANES), BF16),
                   jax.ShapeDtypeStruct((b, s, LANES), BF16),
                   jax.ShapeDtypeStruct((b, s, N_SMALL), F32)],
        compiler_params=_cp(2),
        name="in_projection",
    )(x, g.reshape(-1, 1, d), mod4, mod4, w_main, w_cmp, w_small, rope_c, rope_1, rope_2)


def _cumf_kernel(f_ref, b_ref, cq_ref, ckt_ref):
    n_chunks = f_ref.shape[1] // TK_ATT
    r = lax.broadcasted_iota(jnp.int32, (TK_ATT, TK_ATT), 0)
    c = lax.broadcasted_iota(jnp.int32, (TK_ATT, TK_ATT), 1)
    tri = jnp.where(c <= r, 1.0, 0.0).astype(BF16)
    carry = jnp.zeros((1, LANES), F32)
    for j in range(n_chunks):
        f = f_ref[0, j * TK_ATT:(j + 1) * TK_ATT, :] + b_ref[0]
        ls = -_softplus(-f)
        acc = None
        for p in _split_bf16(ls, 3):
            dd = _dot(tri, p)
            acc = dd if acc is None else acc + dd
        cs = acc + carry
        cs2 = cs * LOG2E
        cq_ref[0, j * TK_ATT:(j + 1) * TK_ATT, :] = cs2
        ckt_ref[0, 0, j] = cs2.T[:ROW_TILE, :]
        carry = cs[TK_ATT - 1:TK_ATT, :]


def _forget_cumsum(small, b_pairs, layer):
    b, s, _ = small.shape
    return pl.pallas_call(
        _cumf_kernel,
        grid=(b, 2),
        in_specs=[pl.BlockSpec((1, s, LANES), lambda i, p: (i, 0, 2 + p)),
                  pl.BlockSpec((None, 1, 1, LANES), lambda i, p: (layer, p, 0, 0))],
        out_specs=[pl.BlockSpec((1, s, LANES), lambda i, p: (i, 0, p)),
                   pl.BlockSpec((1, 1, s // TK_ATT, ROW_TILE, TK_ATT), lambda i, p: (i, p, 0, 0, 0))],
        out_shape=[jax.ShapeDtypeStruct((b, s, 2 * LANES), F32),
                   jax.ShapeDtypeStruct((b, 2, s // TK_ATT, ROW_TILE, TK_ATT), F32)],
        compiler_params=_cp(2),
        name="forget_cumsum",
    )(small, b_pairs)


def _compress_kernel(ks_ref, vs_ref, wk_ref, wv_ref, w1k_ref, w1v_ref, pek_ref, pev_ref,
                     w2k_ref, w2v_ref, rc_ref, r1_ref, r2_ref, ck_ref, cv_ref):
    def one(seg_ref, w_ref, w1_ref, pe_ref, w2_ref):
        p = _dot(seg_ref[0], w_ref[...])
        half = 2 * CMP_HIDDEN
        bias = _dot(pe_ref[...].astype(BF16), w1_ref[...].astype(BF16))[0:1, :]
        bias2 = jnp.concatenate([bias, bias], axis=1)
        n = p.shape[0]
        hid = p[:, :half] + pltpu.roll(p[:, half:], n - 1, 0) + bias2
        act = hid * (1.0 / (1.0 + jnp.exp(-hid)))
        return _dot(act.astype(BF16), w2_ref[...])

    ck = one(ks_ref, wk_ref, w1k_ref, pek_ref, w2k_ref)
    ck_ref[0] = _rope(ck, rc_ref[...], r1_ref[...], r2_ref[...]).astype(BF16)
    cv_ref[0] = one(vs_ref, wv_ref, w1v_ref, pev_ref, w2v_ref).astype(BF16)


def _compress(kc, vc, wk, wv, w1k, w1v, pek, pev, w2k, w2v, rc, r1, r2):
    b, s, _ = kc.shape
    n = s // CMP_STRIDE
    width = CMP_STRIDE * LANES
    kseg = kc.reshape(b, n, width)
    vseg = vc.reshape(b, n, width)
    seg = pl.BlockSpec((1, n, width), lambda i: (i, 0, 0))
    full = lambda a: pl.BlockSpec(a.shape, lambda i: (0,) * a.ndim)
    return pl.pallas_call(
        _compress_kernel,
        grid=(b,),
        in_specs=[seg, seg, full(wk), full(wv), full(w1k), full(w1v), full(pek), full(pev),
                  full(w2k), full(w2v), full(rc), full(r1), full(r2)],
        out_specs=[pl.BlockSpec((1, n, LANES), lambda i: (i, 0, 0)),
                   pl.BlockSpec((1, n, LANES), lambda i: (i, 0, 0))],
        out_shape=[jax.ShapeDtypeStruct((b, n, LANES), BF16),
                   jax.ShapeDtypeStruct((b, n, LANES), BF16)],
        compiler_params=_cp(1),
        name="nsa_compress",
    )(kseg, vseg, wk, wv, w1k, w1v, pek, pev, w2k, w2v, rc, r1, r2)


def _nsa_kernel(q_ref, ck_ref, cv_ref, ks_ref, vs_ref, kw_ref, vw_ref, gate_ref, gn_ref,
                cover_ref, expand_ref, o_ref, *scratch):
    for g in range(NSA_KV):
        _nsa_group(g, q_ref, ck_ref, cv_ref, ks_ref, vs_ref, kw_ref, vw_ref, gate_ref, gn_ref,
                   cover_ref, expand_ref, o_ref, *scratch[3 * g:3 * g + 3])


def _nsa_group(g, q_ref, ck_ref, cv_ref, ks_ref, vs_ref, kw_ref, vw_ref, gate_ref, gn_ref,
               cover_ref, expand_ref, o_ref, m_ref, l_ref, acc_ref):
    tq = q_ref.shape[1]
    t0 = pl.program_id(1) * tq
    q = q_ref[0, :, g * NSA_GROUP * LANES:(g + 1) * NSA_GROUP * LANES]
    qall = jnp.concatenate([q[:, h * LANES:(h + 1) * LANES] for h in range(NSA_GROUP)], axis=0)
    tpos = t0 + lax.broadcasted_iota(jnp.int32, (tq, 1), 0)
    lane = lax.broadcasted_iota(jnp.int32, (tq, LANES), 1)

    n_cmp = ck_ref.shape[1]
    s_t = _dot_nt(ck_ref[0], qall)
    valid = ((CMP_STRIDE * lax.broadcasted_iota(jnp.int32, (n_cmp, tq), 0) + (CMP_LEN - 1))
             <= t0 + lax.broadcasted_iota(jnp.int32, (n_cmp, tq), 1))
    p_cols, p_sum_t = [], None
    for h in range(NSA_GROUP):
        sm = jnp.where(valid, s_t[:, h * tq:(h + 1) * tq], NEG)
        e = jnp.where(valid, jnp.exp2(sm - jnp.max(sm, axis=0, keepdims=True)), 0.0)
        den = jnp.sum(e, axis=0, keepdims=True)
        p = e * jnp.where(den > 0.0, 1.0 / den, 0.0)
        p_cols.append(p.astype(BF16))
        p_sum_t = p if p_sum_t is None else p_sum_t + p
    o_cmp = lax.dot_general(jnp.concatenate(p_cols, axis=1), cv_ref[0], (((0,), (0,)), ((), ())),
                            preferred_element_type=F32).reshape(NSA_GROUP, tq, LANES)

    n_sel = expand_ref.shape[0] * (TK_FOX // SEL_LEN)
    imp_t = None
    for piece in _split_bf16(p_sum_t, 3):
        d_imp = _dot(cover_ref[...], piece)
        imp_t = d_imp if imp_t is None else imp_t + d_imp
    imp_t = imp_t[:n_sel, :]
    blk = lax.broadcasted_iota(jnp.int32, (n_sel, tq), 0)
    tcol = t0 + lax.broadcasted_iota(jnp.int32, (n_sel, tq), 1)
    cur = jnp.right_shift(tcol, int(math.log2(SEL_LEN)))
    forced = (blk == 0) | (blk == cur) | (blk == cur - 1)
    score = jnp.where(forced, FORCE_SCORE, jnp.where(blk * SEL_LEN <= tcol, imp_t, -1.0))
    cnt = jnp.zeros((n_sel, tq), F32)
    for j in range(n_sel):
        sj = score[j:j + 1, :]
        beats = (sj > score) | ((sj == score) & (blk > j))
        cnt = cnt + jnp.where(beats, 1.0, 0.0)
    sel_t = jnp.where(cnt < float(min(SEL_TOPN, n_sel)), 1.0, 0.0)
    sel = jnp.concatenate([sel_t, jnp.zeros((LANES - n_sel, tq), F32)], axis=0).T.astype(BF16)

    def biased(bias):
        def adjust(h, t_off, cols):
            return [cols[kk] + bias[t_off:t_off + ROWS, kk * LANES:(kk + 1) * LANES]
                    for kk in range(len(cols))]
        return adjust

    kcol_s = lax.broadcasted_iota(jnp.int32, (tq, TK_FOX), 1)

    def sel_tile(kt, carry):
        k0 = pl.multiple_of(kt * TK_FOX, TK_FOX)
        hit = _dot(sel, expand_ref[kt])
        ok = (hit > 0.5) & ((k0 + kcol_s) <= tpos)
        s_t = _dot_nt(qall, ks_ref[0, pl.ds(k0, TK_FOX), :])
        _softmax_tile(s_t, vs_ref[0, pl.ds(k0, TK_FOX), :], m_ref, l_ref, acc_ref, NSA_GROUP, tq,
                      biased(jnp.where(ok, 0.0, NEG)))
        return carry

    _softmax_init(m_ref, l_ref, acc_ref)
    lax.fori_loop(0, lax.div(t0 + tq - 1, TK_FOX) + 1, sel_tile, 0)
    o_sel = _softmax_result(l_ref, acc_ref).reshape(NSA_GROUP, tq, LANES)

    span = WINDOW + tq
    w0 = pl.multiple_of(jnp.maximum(t0 - WINDOW, 0), LANES)
    kp = w0 + lax.broadcasted_iota(jnp.int32, (tq, span), 1)
    bias_w = jnp.where((kp <= tpos) & (kp > tpos - WINDOW), 0.0, NEG)
    s_w = _dot_nt(qall, kw_ref[0, pl.ds(w0, span), :])
    nkw = span // LANES
    p_rows, inv_rows = [], []
    for h in range(NSA_GROUP):
        for c in range(tq // ROWS):
            r0 = h * tq + c * ROWS
            cols = [s_w[r0:r0 + ROWS, kk * LANES:(kk + 1) * LANES]
                    + bias_w[c * ROWS:(c + 1) * ROWS, kk * LANES:(kk + 1) * LANES] for kk in range(nkw)]
            mx = cols[0]
            for kk in range(1, nkw):
                mx = jnp.maximum(mx, cols[kk])
            mx = jnp.max(mx, axis=-1, keepdims=True)
            pks = [jnp.exp2(cols[kk] - mx) for kk in range(nkw)]
            psum = pks[0]
            for kk in range(1, nkw):
                psum = psum + pks[kk]
            inv_rows.append(jnp.broadcast_to(1.0 / jnp.sum(psum, axis=-1, keepdims=True), (ROWS, LANES)))
            p_rows.append(jnp.concatenate([pk.astype(BF16) for pk in pks], axis=1))
    o_win = _dot(jnp.concatenate(p_rows, axis=0), vw_ref[0, pl.ds(w0, span), :])
    o_win = (o_win * jnp.concatenate(inv_rows, axis=0)).reshape(NSA_GROUP, tq, LANES)

    gt = gate_ref[0, :, g * LANES:(g + 1) * LANES]
    gt = 1.0 / (1.0 + jnp.exp(-gt))
    mine = (lane >= g * HEAD_DIM) & (lane < (g + 1) * HEAD_DIM)
    outs = []
    for h in range(NSA_GROUP):
        o = (gt[:, 3 * h:3 * h + 1] * o_cmp[h] + gt[:, 3 * h + 1:3 * h + 2] * o_sel[h]
             + gt[:, 3 * h + 2:3 * h + 3] * o_win[h])
        o = jnp.where(mine, o, 0.0)
        ms = jnp.sum(o * o, axis=-1, keepdims=True) * (1.0 / HEAD_DIM)
        o = o * lax.rsqrt(ms + EPS)
        outs.append(o + pltpu.roll(o, HEAD_DIM, 1))
    left = lane < HEAD_DIM
    for pair in range(NSA_GROUP // 2):
        col = (g * NSA_GROUP // 2 + pair) * LANES
        o_ref[0, :, col:col + LANES] = (jnp.where(left, outs[2 * pair], outs[2 * pair + 1])
                                        * gn_ref[...][:, col:col + LANES]).astype(o_ref.dtype)


def _nsa_attention(u, ck, cv, small, gn, cover, expand, layer):
    b, s, _ = u.shape
    tq = min(TQ_NSA, s)
    assert s >= WINDOW + tq and WINDOW % tq == 0
    n_cmp = ck.shape[1]
    blk = LANES
    kv = lambda col: pl.BlockSpec((1, s, LANES), lambda i, j, col=col: (i, 0, col // blk))
    return pl.pallas_call(
        _nsa_kernel,
        grid=(b, s // tq),
        in_specs=[pl.BlockSpec((1, tq, NSA_HEADS * LANES), lambda i, j: (i, j, 0)),
                  pl.BlockSpec((1, n_cmp, LANES), lambda i, j: (i, 0, 0)),
                  pl.BlockSpec((1, n_cmp, LANES), lambda i, j: (i, 0, 0)),
                  kv(COL_KS), kv(COL_VS), kv(COL_KW), kv(COL_VW),
                  pl.BlockSpec((1, tq, NSA_KV * LANES), lambda i, j: (i, j, 0)),
                  pl.BlockSpec((None, 1, NSA_HEADS * HEAD_DIM), lambda i, j: (layer, 0, 0)),
                  pl.BlockSpec(cover.shape, lambda i, j: (0, 0)),
                  pl.BlockSpec(expand.shape, lambda i, j: (0, 0, 0))],
        out_specs=pl.BlockSpec((1, tq, NSA_HEADS * HEAD_DIM), lambda i, j: (i, j, 0)),
        out_shape=jax.ShapeDtypeStruct((b, s, NSA_HEADS * HEAD_DIM), BF16),
        scratch_shapes=_softmax_scratch(NSA_GROUP * tq) * NSA_KV,
        compiler_params=_cp(2),
        name="nsa_attention",
    )(u, ck, cv, u, u, u, u, small, gn, cover, expand)


def _pair_finish(acc, gn_ref, o_ref, lanes, tq):
    lane = lax.broadcasted_iota(jnp.int32, (tq, LANES), 1)
    left = lane < HEAD_DIM
    o = jnp.where(left, acc[0], acc[1])
    o2 = o * o
    ms_l = jnp.sum(jnp.where(left, o2, 0.0), axis=-1, keepdims=True) * (1.0 / HEAD_DIM)
    ms_r = jnp.sum(jnp.where(left, 0.0, o2), axis=-1, keepdims=True) * (1.0 / HEAD_DIM)
    inv = jnp.where(left, lax.rsqrt(ms_l + EPS), lax.rsqrt(ms_r + EPS))
    o_ref[0, :, lanes] = (o * inv * gn_ref[...][:, lanes]).astype(o_ref.dtype)


def _softmax_tile(s, v, m_ref, l_ref, acc_ref, heads, tq, adjust, p_ref=None):
    nk = s.shape[1] // LANES
    p_rows = []
    for h in range(heads):
        for c in range(tq // ROWS):
            t_off = c * ROWS
            r0 = h * tq + t_off
            cols = [s[r0:r0 + ROWS, k * LANES:(k + 1) * LANES] for k in range(nk)]
            cols = adjust(h, t_off, cols)
            mx = cols[0]
            for k in range(1, nk):
                mx = jnp.maximum(mx, cols[k])
            m_old = m_ref[r0:r0 + ROWS, :]
            m_new = jnp.maximum(m_old, jnp.max(mx, axis=-1, keepdims=True))
            alpha = jnp.exp2(m_old - m_new)
            pks = [jnp.exp2(cols[k] - m_new) for k in range(nk)]
            psum = pks[0]
            for k in range(1, nk):
                psum = psum + pks[k]
            p_chunk = jnp.concatenate([pk.astype(BF16) for pk in pks], axis=1)
            if p_ref is None:
                p_rows.append(p_chunk)
            else:
                p_ref[r0:r0 + ROWS, :] = p_chunk
            l_ref[r0:r0 + ROWS, :] = alpha * l_ref[r0:r0 + ROWS, :] + psum
            acc_ref[r0:r0 + ROWS, :] = alpha * acc_ref[r0:r0 + ROWS, :]
            m_ref[r0:r0 + ROWS, :] = m_new
    if p_ref is None:
        acc_ref[...] += _dot(jnp.concatenate(p_rows, axis=0), v)


def _tile_loop(score_fn, process_fn, lo, hi):
    def body(kt, carry):
        process_fn(kt, score_fn(kt))
        return carry

    lax.fori_loop(lo, hi - 1, body, 0)
    return score_fn(hi - 1)


def _softmax_init(m_ref, l_ref, acc_ref):
    m_ref[...] = jnp.full(m_ref.shape, NEG, F32)
    l_ref[...] = jnp.zeros(l_ref.shape, F32)
    acc_ref[...] = jnp.zeros(acc_ref.shape, F32)


def _softmax_result(l_ref, acc_ref):
    return acc_ref[...] / jnp.sum(l_ref[...], axis=-1, keepdims=True)


def _softmax_scratch(rows):
    return [pltpu.VMEM((rows, LANES), F32), pltpu.VMEM((rows, LANES), F32), pltpu.VMEM((rows, LANES), F32)]


def _fox_kernel(q_ref, k_ref, v_ref, cq_ref, ckt_ref, gn_ref, o_ref, *scratch):
    for pair in range(FOX_HEADS // 2):
        _fox_pair(pair, q_ref, k_ref, v_ref, cq_ref, ckt_ref, gn_ref, o_ref, *scratch[5 * pair:5 * pair + 5])


def _fox_pair(pair, q_ref, k_ref, v_ref, cq_ref, ckt_ref, gn_ref, o_ref,
              m_ref, l_ref, acc_ref, cqr_ref, p_ref):
    lanes = slice(pair * LANES, (pair + 1) * LANES)
    tq = q_ref.shape[1]
    tk = TK_FOX
    t0 = pl.program_id(1) * tq
    qall = jnp.concatenate([q_ref[0, :, (2 * pair + h) * LANES:(2 * pair + h + 1) * LANES] for h in range(2)],
                           axis=0)
    cq = cq_ref[0, :, lanes]
    cqr_ref[0:tq, :] = jnp.broadcast_to(cq[:, 0:1], (tq, LANES))
    cqr_ref[tq:2 * tq, :] = jnp.broadcast_to(cq[:, 1:2], (tq, LANES))
    _softmax_init(m_ref, l_ref, acc_ref)
    n_tiles = lax.div(t0 + tq - 1, tk) + 1
    diag = (lax.broadcasted_iota(jnp.int32, (ROWS, LANES), 1)
            - lax.broadcasted_iota(jnp.int32, (ROWS, LANES), 0))

    def scores(kt):
        return _dot_nt(qall, k_ref[0, pl.ds(pl.multiple_of(kt * tk, tk), tk), lanes])

    def tile(kt, s, masked):
        k0 = pl.multiple_of(kt * tk, tk)
        cks = [ckt_ref[0, pair, kt * (tk // TK_ATT) + j] for j in range(tk // TK_ATT)]

        def adjust(h, t_off, cols):
            out = []
            cqr = cqr_ref[h * tq + t_off:h * tq + t_off + ROWS, :]
            for kk in range(tk // LANES):
                lo = (kk * LANES) % TK_ATT
                ck = cks[(kk * LANES) // TK_ATT][h:h + 1, lo:lo + LANES]
                val = (cols[kk] - ck) + cqr
                if masked:
                    val = jnp.where(diag <= (t0 + t_off) - (k0 + kk * LANES), val, NEG)
                out.append(val)
            return out

        kp = pl.multiple_of(jnp.maximum(kt - 1, 0) * tk, tk)
        acc_ref[...] += _dot(p_ref[...], v_ref[0, pl.ds(kp, tk), lanes])
        _softmax_tile(s, None, m_ref, l_ref, acc_ref, 2, tq, adjust, p_ref=p_ref)

    p_ref[...] = jnp.zeros(p_ref.shape, BF16)
    s_last = _tile_loop(scores, lambda kt, s: tile(kt, s, False), 0, n_tiles)
    tile(n_tiles - 1, s_last, True)
    acc_ref[...] += _dot(p_ref[...], v_ref[0, pl.ds(pl.multiple_of((n_tiles - 1) * tk, tk), tk), lanes])
    acc = _softmax_result(l_ref, acc_ref).reshape(2, tq, LANES)
    _pair_finish(acc, gn_ref, o_ref, lanes, tq)


def _fox_attention(u, cq, ckt, gn, layer):
    b, s, _ = u.shape
    tq = min(TQ_PAIR, s)
    qw, kvw = FOX_HEADS * LANES, FOX_HEADS * HEAD_DIM
    qb = COL_FOX // qw
    kb = (COL_FOX + qw) // kvw
    gb = NSA_HEADS * HEAD_DIM // kvw
    return pl.pallas_call(
        _fox_kernel,
        grid=(b, s // tq),
        in_specs=[pl.BlockSpec((1, tq, qw), lambda i, j: (i, j, qb)),
                  pl.BlockSpec((1, s, kvw), lambda i, j: (i, 0, kb)),
                  pl.BlockSpec((1, s, kvw), lambda i, j: (i, 0, kb + 1)),
                  pl.BlockSpec((1, tq, 2 * LANES), lambda i, j: (i, j, 0)),
                  pl.BlockSpec((1, 2, s // TK_ATT, ROW_TILE, TK_ATT), lambda i, j: (i, 0, 0, 0, 0)),
                  pl.BlockSpec((None, 1, kvw), lambda i, j: (layer, 0, gb))],
        out_specs=pl.BlockSpec((1, tq, kvw), lambda i, j: (i, j, 0)),
        out_shape=jax.ShapeDtypeStruct((b, s, FOX_HEADS * HEAD_DIM), BF16),
        scratch_shapes=(_softmax_scratch(2 * tq) + [pltpu.VMEM((2 * tq, LANES), F32),
                                                    pltpu.VMEM((2 * tq, TK_FOX), BF16)]) * (FOX_HEADS // 2),
        compiler_params=_cp(2),
        name="fox_attention",
    )(u, u, u, cq, ckt, gn)


def _sb_kernel(q_ref, k_ref, v_ref, gn_ref, o_ref, *scratch):
    for pair in range(SB_HEADS // 2):
        _sb_pair(pair, q_ref, k_ref, v_ref, gn_ref, o_ref, *scratch[2 * pair:2 * pair + 2])


def _sb_pair(pair, q_ref, k_ref, v_ref, gn_ref, o_ref, rest_ref, acc_ref):
    tq = q_ref.shape[1]
    sub = TK_ATT
    tk = SB_SUBS * sub
    nks = sub // LANES
    rows = 2 * tq
    t0 = pl.program_id(1) * tq
    r = lax.broadcasted_iota(jnp.int32, (sub, sub), 0)
    c = lax.broadcasted_iota(jnp.int32, (sub, sub), 1)
    upper = jnp.where(r >= c, 1.0, 0.0).astype(BF16)
    n_tiles = lax.div(t0 + tq - 1, tk) + 1
    diag = (lax.broadcasted_iota(jnp.int32, (ROWS, LANES), 1)
            - lax.broadcasted_iota(jnp.int32, (ROWS, LANES), 0))
    rest_ref[...] = jnp.zeros(rest_ref.shape, F32)
    acc_ref[...] = jnp.zeros(acc_ref.shape, F32)
    hi_mask = jnp.uint32(0xFFFF0000)
    lanes = slice(pair * LANES, (pair + 1) * LANES)
    qall = jnp.concatenate([q_ref[0, :, (2 * pair + h) * LANES:(2 * pair + h + 1) * LANES] for h in range(2)],
                           axis=0)
    chunks = [(h * tq + cc * ROWS, cc * ROWS) for h in range(2) for cc in range(tq // ROWS)]

    def tile(kt, masked):
        k0 = pl.multiple_of(kt * tk, tk)
        k = k_ref[0, pl.ds(k0, tk), lanes]
        v = v_ref[0, pl.ds(k0, tk), lanes]

        def strictly_before(t_off, col):
            return diag < (t0 + t_off) - (k0 + col)

        z = _dot_nt(qall, k)
        his = [[] for _ in range(SB_SUBS)]
        los = [[] for _ in range(SB_SUBS)]
        for r0, t_off in chunks:
            for sb in range(SB_SUBS):
                hi_c, lo_c = [], []
                for kk in range(nks):
                    col = sb * sub + kk * LANES
                    zc = z[r0:r0 + ROWS, col:col + LANES]
                    l = -(jnp.maximum(zc, 0.0) + jnp.log2(1.0 + jnp.exp2(-jnp.abs(zc))))
                    if masked:
                        l = jnp.where(strictly_before(t_off, col), l, 0.0)
                    hi = pltpu.bitcast(pltpu.bitcast(l, jnp.uint32) & hi_mask, F32)
                    hi_c.append(hi.astype(BF16))
                    lo_c.append((l - hi).astype(BF16))
                his[sb].append(jnp.concatenate(hi_c, axis=1))
                los[sb].append(jnp.concatenate(lo_c, axis=1))
        hi_all = jnp.concatenate([x for sb in range(SB_SUBS) for x in his[sb]], axis=0)
        lo_all = jnp.concatenate([x for sb in range(SB_SUBS) for x in los[sb]], axis=0)
        cum = _dot(hi_all, upper) + _dot(lo_all, upper)

        a_rows = []
        for r0, t_off in chunks:
            base = rest_ref[r0:r0 + ROWS, :]
            a_c = [None] * (SB_SUBS * nks)
            for sb in reversed(range(SB_SUBS)):
                cs = cum[sb * rows + r0:sb * rows + r0 + ROWS, :]
                for kk in range(nks):
                    col = sb * sub + kk * LANES
                    a = jnp.exp2(z[r0:r0 + ROWS, col:col + LANES] + cs[:, kk * LANES:(kk + 1) * LANES] + base)
                    if masked:
                        a = jnp.where(strictly_before(t_off, col), a, 0.0)
                    a_c[sb * nks + kk] = a.astype(BF16)
                base = base + jnp.broadcast_to(cs[:, 0:1], (ROWS, LANES))
            a_rows.append(jnp.concatenate(a_c, axis=1))
            rest_ref[r0:r0 + ROWS, :] = base
        acc_ref[...] += _dot(jnp.concatenate(a_rows, axis=0), v)

    tile(n_tiles - 1, True)

    def full_tile(i, carry):
        tile(n_tiles - 2 - i, False)
        return carry

    lax.fori_loop(0, n_tiles - 1, full_tile, 0)
    _pair_finish(acc_ref[...].reshape(2, tq, LANES), gn_ref, o_ref, lanes, tq)


def _sb_attention(u, gn, layer):
    b, s, _ = u.shape
    tq = min(TQ_PAIR, s)
    qw, kvw = SB_HEADS * LANES, SB_HEADS * HEAD_DIM
    qb = COL_SB // qw
    kb = (COL_SB + qw) // kvw
    gb = (NSA_HEADS + FOX_HEADS) * HEAD_DIM // kvw
    return pl.pallas_call(
        _sb_kernel,
        grid=(b, s // tq),
        in_specs=[pl.BlockSpec((1, tq, qw), lambda i, j: (i, j, qb)),
                  pl.BlockSpec((1, s, kvw), lambda i, j: (i, 0, kb)),
                  pl.BlockSpec((1, s, kvw), lambda i, j: (i, 0, kb + 1)),
                  pl.BlockSpec((None, 1, kvw), lambda i, j: (layer, 0, gb))],
        out_specs=pl.BlockSpec((1, tq, kvw), lambda i, j: (i, j, 0)),
        out_shape=jax.ShapeDtypeStruct((b, s, SB_HEADS * HEAD_DIM), BF16),
        scratch_shapes=[pltpu.VMEM((2 * tq, LANES), F32), pltpu.VMEM((2 * tq, LANES), F32)] * (SB_HEADS // 2),
        compiler_params=_cp(2),
        name="sb_attention",
    )(u, u, u, gn)


def _outproj_kernel(oa_ref, ob_ref, oc_ref, x_ref, w_ref, g1_ref, n2_ref, sc_ref, sh_ref,
                    wrh_ref, wrl_ref, br_ref, xo_ref, h_ref, rw_ref, ri_ref):
    na = oa_ref.shape[2]
    nb = ob_ref.shape[2]
    y = _dot(oa_ref[0], w_ref[0:na, :])
    y = y + _dot(ob_ref[0], w_ref[na:na + nb, :])
    y = y + _dot(oc_ref[0], w_ref[na + nb:, :])
    x = x_ref[0] + g1_ref[0] * y
    xo_ref[0] = x
    ms = jnp.mean(x * x, axis=-1, keepdims=True)
    h = (x * lax.rsqrt(ms + EPS) * n2_ref[...]) * (1.0 + sc_ref[0]) + sh_ref[0]
    hb = h.astype(BF16)
    _store_row_tiles(h_ref, h)
    hl = (h - hb.astype(F32)).astype(BF16)
    logit = _dot(hb, wrh_ref[...]) + _dot(hl, wrh_ref[...]) + _dot(hb, wrl_ref[...]) + br_ref[...]

    tm = logit.shape[0]
    lane = lax.broadcasted_iota(jnp.int32, (tm, LANES), 1).astype(F32)
    big = float(LANES)
    is_g = lane < N_GROUPS
    lg = jnp.where(is_g, logit, NEG)
    mg = jnp.max(lg, axis=-1, keepdims=True)
    zg = jnp.sum(jnp.where(is_g, jnp.exp(lg - mg), 0.0), axis=-1, keepdims=True)
    pg = 1.0 / zg
    gi = jnp.min(jnp.where(is_g & (lg == mg), lane, big), axis=-1, keepdims=True)
    e_lane = lane - N_GROUPS
    in_grp = (e_lane >= gi * EXPERTS_PER_GROUP) & (e_lane < (gi + 1) * EXPERTS_PER_GROUP)
    le = jnp.where(in_grp, logit, NEG)
    m1 = jnp.max(le, axis=-1, keepdims=True)
    i1 = jnp.min(jnp.where(in_grp & (le == m1), lane, big), axis=-1, keepdims=True)
    rest = in_grp & (lane != i1)
    le2 = jnp.where(rest, logit, NEG)
    m2 = jnp.max(le2, axis=-1, keepdims=True)
    i2 = jnp.min(jnp.where(rest & (le2 == m2), lane, big), axis=-1, keepdims=True)
    ze = jnp.sum(jnp.where(in_grp, jnp.exp(le - m1), 0.0), axis=-1, keepdims=True)
    p1 = 1.0 / ze
    p2 = jnp.exp(m2 - m1) / ze
    den = p1 + p2
    w1 = pg * (p1 / den)
    w2 = pg * (p2 / den)
    rw_ref[0] = jnp.where(lane == 0.0, w1, jnp.where(lane == 1.0, w2, 0.0))
    ri_ref[0] = jnp.where(lane == 0.0, i1 - N_GROUPS, jnp.where(lane == 1.0, i2 - N_GROUPS, 0.0)).astype(jnp.int32)


def _out_projection(oa, ob, oc, x, w_out, mod4, n2, wr_hi, wr_lo, br, layer):
    b, s, d = x.shape
    tm = min(TM_OUT, s)
    row = lambda i, j: (i, j, 0)
    return pl.pallas_call(
        _outproj_kernel,
        grid=(b, s // tm),
        in_specs=[pl.BlockSpec((1, tm, oa.shape[2]), row),
                  pl.BlockSpec((1, tm, ob.shape[2]), row),
                  pl.BlockSpec((1, tm, oc.shape[2]), row),
                  pl.BlockSpec((1, tm, d), row),
                  _layer_spec(layer, w_out.shape[1:]),
                  _mod_spec(layer, 2, d),
                  _layer_spec(layer, (1, d)),
                  _mod_spec(layer, 4, d),
                  _mod_spec(layer, 3, d),
                  _layer_spec(layer, (d, LANES)),
                  _layer_spec(layer, (d, LANES)),
                  _layer_spec(layer, (1, LANES))],
        out_specs=[pl.BlockSpec((1, tm, d), row),
                   pl.BlockSpec((tm * ROW_TILE, LANES), lambda i, j: (i * (s // tm) + j, 0)),
                   pl.BlockSpec((1, tm, LANES), row),
                   pl.BlockSpec((1, tm, LANES), row)],
        out_shape=[jax.ShapeDtypeStruct((b, s, d), F32),
                   jax.ShapeDtypeStruct((b * s * ROW_TILE, LANES), F32),
                   jax.ShapeDtypeStruct((b, s, LANES), F32),
                   jax.ShapeDtypeStruct((b, s, LANES), jnp.int32)],
        compiler_params=_cp(2),
        name="out_projection",
    )(oa, ob, oc, x, w_out, mod4, n2.reshape(-1, 1, d), mod4, mod4, wr_hi, wr_lo, br)


def _store_row_tiles(ref, val):
    tm = val.shape[0]
    for c in range(ROW_TILE):
        ref[pl.ds(c, tm, stride=ROW_TILE), :] = val[:, c * LANES:(c + 1) * LANES]


def _load_row_tiles(ref, tm):
    return [ref[pl.ds(c, tm, stride=ROW_TILE), :] for c in range(ROW_TILE)]


def _tile_rows(ref, n):
    return ref.at[pl.ds(pl.multiple_of(n * ROW_TILE, ROW_TILE), ROW_TILE), :]


def _dispatch_kernel(idx_ref, src_ref, init_hbm, dst_hbm, sem):
    del init_hbm

    def issue(tok, carry):
        for k in range(2):
            pltpu.make_async_copy(_tile_rows(src_ref, tok), _tile_rows(dst_hbm, idx_ref[0, 0, 2 * tok + k]),
                                  sem).start(priority=k)
        return carry

    lax.fori_loop(0, COPY_CHUNK // 2, issue, 0, unroll=4)
    for _ in range(2):
        pltpu.make_async_copy(src_ref, dst_hbm.at[pl.ds(0, src_ref.shape[0]), :], sem).wait()


def _dispatch(src, idx, n_dst, init):
    n = idx.shape[0]
    if init is None:
        init = jnp.zeros((n_dst * ROW_TILE, LANES), src.dtype)
    return pl.pallas_call(
        _dispatch_kernel,
        grid=(n // COPY_CHUNK,),
        in_specs=[pl.BlockSpec((1, 1, COPY_CHUNK), lambda i: (i, 0, 0), memory_space=pltpu.SMEM),
                  pl.BlockSpec((COPY_CHUNK // 2 * ROW_TILE, LANES), lambda i: (i, 0)),
                  pl.BlockSpec(memory_space=pl.ANY)],
        out_specs=pl.BlockSpec(memory_space=pl.ANY),
        out_shape=jax.ShapeDtypeStruct((n_dst * ROW_TILE, LANES), src.dtype),
        scratch_shapes=[pltpu.SemaphoreType.DMA],
        input_output_aliases={2: 0},
        compiler_params=_cp(1),
        name="moe_dispatch",
    )(idx.reshape(n // COPY_CHUNK, 1, COPY_CHUNK), src, init)


def _expert_kernel(te_ref, nu_ref, x_ref, w1_ref, w3_ref, w2_ref, y_ref, w1b_ref, w3b_ref, w2b_ref):
    i = pl.program_id(0)

    @pl.when((i == 0) | (te_ref[i] != te_ref[jnp.maximum(i - 1, 0)]))
    def _():
        w1b_ref[...] = w1_ref[0].astype(BF16)
        w3b_ref[...] = w3_ref[0].astype(BF16)
        w2b_ref[...] = w2_ref[0].astype(BF16)

    @pl.when(i < nu_ref[0])
    def _():
        x = jnp.concatenate(_load_row_tiles(x_ref, TM_EXP), axis=1).astype(BF16)
        a = _dot(x, w1b_ref[...])
        g = _dot(x, w3b_ref[...])
        act = (a * (1.0 / (1.0 + jnp.exp(-a)))) * g
        _store_row_tiles(y_ref, _dot(act.astype(BF16), w2b_ref[...]))

    @pl.when(i >= nu_ref[0])
    def _():
        y_ref[...] = jnp.zeros_like(y_ref)


def _expert_mlp(xs, tile_expert, n_used, w1, w3, w2, layer):
    d, de = w1.shape[2], w1.shape[3]
    n_tiles = xs.shape[0] // (TM_EXP * ROW_TILE)
    grid_spec = pltpu.PrefetchScalarGridSpec(
        num_scalar_prefetch=2,
        grid=(n_tiles,),
        in_specs=[pl.BlockSpec((TM_EXP * ROW_TILE, LANES), lambda i, te, nu: (i, 0)),
                  pl.BlockSpec((None, 1, d, de), lambda i, te, nu: (layer, te[i], 0, 0)),
                  pl.BlockSpec((None, 1, d, de), lambda i, te, nu: (layer, te[i], 0, 0)),
                  pl.BlockSpec((None, 1, de, d), lambda i, te, nu: (layer, te[i], 0, 0))],
        out_specs=pl.BlockSpec((TM_EXP * ROW_TILE, LANES), lambda i, te, nu: (i, 0)),
        scratch_shapes=[pltpu.VMEM((d, de), BF16), pltpu.VMEM((d, de), BF16), pltpu.VMEM((de, d), BF16)],
    )
    return pl.pallas_call(
        _expert_kernel,
        grid_spec=grid_spec,
        out_shape=jax.ShapeDtypeStruct(xs.shape, F32),
        compiler_params=_cp(1),
        name="expert_mlp",
    )(tile_expert, n_used, xs, w1, w3, w2)


def _combine_kernel(d0_ref, d1_ref, x_ref, ys_hbm, rw_ref, g2_ref, fg_ref, o_ref, y0_ref, y1_ref, sem,
                    *, final):
    tm = x_ref.shape[1]

    def issue(r, carry):
        pltpu.make_async_copy(_tile_rows(ys_hbm, d0_ref[0, 0, r]), _tile_rows(y0_ref, r),
                              sem.at[0]).start(priority=0)
        pltpu.make_async_copy(_tile_rows(ys_hbm, d1_ref[0, 0, r]), _tile_rows(y1_ref, r),
                              sem.at[1]).start(priority=1)
        return carry

    lax.fori_loop(0, tm, issue, 0, unroll=8)
    pltpu.make_async_copy(ys_hbm.at[pl.ds(0, tm * ROW_TILE), :], y0_ref, sem.at[0]).wait()
    pltpu.make_async_copy(ys_hbm.at[pl.ds(0, tm * ROW_TILE), :], y1_ref, sem.at[1]).wait()

    rw = rw_ref[0]
    w0 = jnp.broadcast_to(rw[:, 0:1], (tm, LANES))
    w1 = jnp.broadcast_to(rw[:, 1:2], (tm, LANES))
    y0 = _load_row_tiles(y0_ref, tm)
    y1 = _load_row_tiles(y1_ref, tm)
    cols = []
    for c in range(ROW_TILE):
        sl = slice(c * LANES, (c + 1) * LANES)
        cols.append(x_ref[0, :, sl] + g2_ref[0, :, sl] * (y0[c] * w0 + y1[c] * w1))
    if final:
        ssq = cols[0] * cols[0]
        for c in range(1, ROW_TILE):
            ssq = ssq + cols[c] * cols[c]
        inv = lax.rsqrt(jnp.sum(ssq, axis=-1, keepdims=True) * (1.0 / (ROW_TILE * LANES)) + EPS)
        cols = [cols[c] * inv * fg_ref[:, c * LANES:(c + 1) * LANES] for c in range(ROW_TILE)]
    for c in range(ROW_TILE):
        o_ref[0, :, c * LANES:(c + 1) * LANES] = cols[c]


def _combine(x, ys, dest0, dest1, rw, mod4, final_g, final, layer):
    b, s, d = x.shape
    tm = min(TM_CMB, s)
    row = lambda i, j: (i, j, 0)
    idx_spec = pl.BlockSpec((1, 1, tm), lambda i, j: (i * (s // tm) + j, 0, 0), memory_space=pltpu.SMEM)
    return pl.pallas_call(
        functools.partial(_combine_kernel, final=final),
        grid=(b, s // tm),
        in_specs=[idx_spec, idx_spec,
                  pl.BlockSpec((1, tm, d), row),
                  pl.BlockSpec(memory_space=pl.ANY),
                  pl.BlockSpec((1, tm, LANES), row),
                  _mod_spec(layer, 5, d),
                  pl.BlockSpec((1, d), lambda i, j: (0, 0))],
        out_specs=pl.BlockSpec((1, tm, d), row),
        out_shape=jax.ShapeDtypeStruct((b, s, d), F32),
        scratch_shapes=[pltpu.VMEM((tm * ROW_TILE, LANES), F32), pltpu.VMEM((tm * ROW_TILE, LANES), F32),
                        pltpu.SemaphoreType.DMA((2,))],
        compiler_params=_cp(2),
        name="moe_combine_final" if final else "moe_combine",
    )(dest0.reshape(-1, 1, tm), dest1.reshape(-1, 1, tm), x, ys, rw, mod4, final_g.reshape(1, d))


def _layout_w_in(w_in):
    d = w_in.shape[0]
    kvw = NSA_KV * HEAD_DIM
    sizes = (NSA_HEADS * HEAD_DIM, kvw, kvw, kvw, kvw, kvw, kvw, NSA_HEADS * 3,
             FOX_HEADS * HEAD_DIM, FOX_HEADS * HEAD_DIM, FOX_HEADS * HEAD_DIM, FOX_HEADS,
             SB_HEADS * HEAD_DIM, SB_HEADS * HEAD_DIM, SB_HEADS * HEAD_DIM)
    pts = np.cumsum(sizes)[:-1].tolist()
    (qa, kca, vca, ksa, vsa, kwa, vwa, ga, qb, kb, vb, fb, qc, kc, vc) = jnp.split(w_in, pts, axis=1)
    scale = HEAD_DIM ** -0.5 * LOG2E
    main = jnp.concatenate([qa * scale, ksa, kwa, vsa, vwa, qb * scale, kb, vb, qc * scale, kc, vc],
                           axis=1).astype(BF16)
    cmp_w = jnp.concatenate([kca, vca], axis=1).astype(BF16)
    zpad = lambda n: jnp.zeros((d, n), w_in.dtype)
    per_grp = NSA_GROUP * 3
    small = jnp.concatenate([ga[:, :per_grp], zpad(LANES - per_grp), ga[:, per_grp:], zpad(LANES - per_grp),
                             fb[:, 0:2], zpad(LANES - 2), fb[:, 2:4], zpad(LANES - 2)], axis=1).astype(BF16)
    return main, cmp_w, small


def _layout_cmp(w1, w2):
    hid = w1.shape[1]
    w1r = w1.reshape(2, CMP_STRIDE, HEAD_DIM, hid)
    z = jnp.zeros((CMP_STRIDE, HEAD_DIM, hid), w1.dtype)
    cols = []
    for half in range(2):
        for g in range(NSA_KV):
            parts = [w1r[half] if gg == g else z for gg in range(NSA_KV)]
            cols.append(jnp.concatenate(parts, axis=1).reshape(CMP_STRIDE * LANES, hid))
    wcat = jnp.concatenate(cols, axis=1).astype(BF16)
    zz = jnp.zeros_like(w2)
    w2bd = jnp.concatenate([jnp.concatenate([w2, zz], axis=1),
                            jnp.concatenate([zz, w2], axis=1)], axis=0).astype(BF16)
    return wcat, w2bd


def _rope_tables(pos):
    inv = jnp.exp(jnp.arange(ROPE_HALF, dtype=F32) * (-2.0 * math.log(ROPE_THETA) / ROPE_DIM))
    ang = pos.astype(F32)[:, None] * inv[None, :]
    cos, sin = jnp.cos(ang), jnp.sin(ang)
    n = pos.shape[0]
    z8 = jnp.zeros((n, ROPE_HALF), F32)
    rest1 = jnp.ones((n, HEAD_DIM - ROPE_DIM), F32)
    rest0 = jnp.zeros((n, HEAD_DIM - ROPE_DIM), F32)
    c = jnp.concatenate([cos, cos, rest1], axis=1)
    s1 = jnp.concatenate([z8, sin, rest0], axis=1)
    s2 = jnp.concatenate([-sin, z8, rest0], axis=1)
    dup = lambda a: jnp.concatenate([a, a], axis=1)
    return dup(c), dup(s1), dup(s2)


def _static_tables(s):
    n_cmp_pad = s // CMP_STRIDE
    n = np.arange(n_cmp_pad)[:, None]
    j = np.arange(LANES)[None, :]
    n_sel = s // SEL_LEN
    cover = ((n * CMP_STRIDE < j * SEL_LEN + SEL_LEN) & (n * CMP_STRIDE + CMP_LEN > j * SEL_LEN)
             & (j < n_sel)).astype(np.float32)
    nt = s // TK_FOX
    key = np.arange(nt)[:, None, None] * TK_FOX + np.arange(TK_FOX)[None, None, :]
    expand = (key // SEL_LEN == np.arange(LANES)[None, :, None]).astype(np.float32)
    return jnp.asarray(cover.T, BF16), jnp.asarray(expand, BF16)


def _plan_kernel(ri_ref, dest_ref, te_ref, cnt_ref, pst_ref):
    phase = pl.program_id(0)
    j = pl.program_id(1)
    tm = ri_ref.shape[1]
    lane = lax.broadcasted_iota(jnp.int32, (tm, LANES), 1)
    ri = ri_ref[0]
    hot0 = jnp.where(lane == ri[:, 0:1], 1.0, 0.0)
    hot1 = jnp.where(lane == ri[:, 1:2], 1.0, 0.0)
    both = hot0 + hot1

    @pl.when((phase == 0) & (j == 0))
    def _():
        cnt_ref[...] = jnp.zeros(cnt_ref.shape, F32)

    @pl.when(phase == 0)
    def _():
        cnt_ref[...] += jnp.sum(both, axis=0, keepdims=True)

    @pl.when((phase == 0) & (j == pl.num_programs(1) - 1))
    def _():
        cnt = cnt_ref[...]
        padded = jnp.ceil(cnt * (1.0 / TM_EXP)) * TM_EXP
        r = lax.broadcasted_iota(jnp.int32, (LANES, LANES), 0)
        c = lax.broadcasted_iota(jnp.int32, (LANES, LANES), 1)
        incl = jnp.where(r <= c, 1.0, 0.0).astype(BF16)
        rows = jnp.broadcast_to(padded, (LANES, LANES)).astype(BF16)
        pends = _dot(rows, incl)
        pst_ref[...] = pends[0:1, :] - padded
        ends = pends.T
        tile_start = (c * TM_EXP).astype(F32)
        done = jnp.where((r < N_EXPERTS) & (ends <= tile_start), 1.0, 0.0)
        te = jnp.minimum(jnp.sum(done, axis=0, keepdims=True), float(N_EXPERTS - 1))
        used = pends[0:1, N_EXPERTS - 1:N_EXPERTS] * (1.0 / TM_EXP)
        lane1 = lax.broadcasted_iota(jnp.int32, (1, LANES), 1)
        te_ref[...] = jnp.concatenate(
            [te, jnp.where(lane1 == 0, used, 0.0)] + [jnp.zeros((ROW_TILE - 2, LANES), F32)], axis=0
        ).astype(jnp.int32)
        cnt_ref[...] = jnp.zeros(cnt_ref.shape, F32)

    @pl.when(phase == 1)
    def _():
        r = lax.broadcasted_iota(jnp.int32, (tm, tm), 0)
        c = lax.broadcasted_iota(jnp.int32, (tm, tm), 1)
        before = jnp.where(c < r, 1.0, 0.0).astype(BF16)
        earlier = _dot(before, both.astype(BF16)) + cnt_ref[...]
        slot = earlier + pst_ref[...]
        d0 = jnp.sum(hot0 * slot, axis=-1, keepdims=True)
        d1 = jnp.sum(hot1 * slot, axis=-1, keepdims=True)
        dest_ref[0] = jnp.where(lane == 0, d0, jnp.where(lane == 1, d1, 0.0)).astype(jnp.int32)
        cnt_ref[...] += jnp.sum(both, axis=0, keepdims=True)


def _dispatch_plan(ri, t):
    b, s, _ = ri.shape
    tm = min(TM_CMB, s)
    n_tiles = -(-(2 * t + N_EXPERTS * (TM_EXP - 1)) // TM_EXP)
    assert n_tiles <= LANES and N_EXPERTS <= LANES
    chunks = s // tm
    dest, te = pl.pallas_call(
        _plan_kernel,
        grid=(2, b * chunks),
        in_specs=[pl.BlockSpec((1, tm, LANES), lambda ph, j: (j // chunks, j % chunks, 0))],
        out_specs=[pl.BlockSpec((1, tm, LANES), lambda ph, j: (ph * (j // chunks), ph * (j % chunks), 0)),
                   pl.BlockSpec((ROW_TILE, LANES), lambda ph, j: (0, 0))],
        out_shape=[jax.ShapeDtypeStruct((b, s, LANES), jnp.int32),
                   jax.ShapeDtypeStruct((ROW_TILE, LANES), jnp.int32)],
        scratch_shapes=[pltpu.VMEM((1, LANES), F32), pltpu.VMEM((1, LANES), F32)],
        compiler_params=_cp(2),
        name="moe_plan",
    )(ri)
    dest = dest.reshape(t, LANES)[:, :2].reshape(-1)
    return dest, n_tiles * TM_EXP, te[0, :n_tiles], te[1, 0:1]


def kernel(x, c, norm1_g, norm2_g, ada_w, ada_b, w_in, b_forget, cmp_pos_k, cmp_w1_k, cmp_w2_k,
           cmp_pos_v, cmp_w1_v, cmp_w2_v, out_norm_g, w_out, router_group_w, router_group_b,
           router_expert_w, router_expert_b, expert_w1, expert_w3, expert_w2, final_g):
    b, s, d = x.shape
    depth = ada_w.shape[0]
    t = b * s
    mod = _modulation(c, ada_w, ada_b)
    rope_c, rope_1, rope_2 = _rope_tables(jnp.arange(s))
    n_cmp_pad = s // CMP_STRIDE
    crc, cr1, cr2 = _rope_tables(jnp.arange(n_cmp_pad) * CMP_STRIDE + (CMP_LEN - 1))
    cover, expand = _static_tables(s)
    xs = None

    mod4 = mod.reshape(depth, b, 6, 1, d)
    w_main, w_cmp, w_small = jax.vmap(_layout_w_in)(w_in)
    b_pairs = jnp.pad(b_forget.reshape(depth, 2, 1, 2), ((0, 0), (0, 0), (0, 0), (0, LANES - 2)))
    wk, w2k = jax.vmap(_layout_cmp)(cmp_w1_k, cmp_w2_k)
    wv, w2v = jax.vmap(_layout_cmp)(cmp_w1_v, cmp_w2_v)
    pek = jnp.broadcast_to(cmp_pos_k.reshape(depth, 1, -1), (depth, ROW_TILE, CMP_LEN * HEAD_DIM))
    pev = jnp.broadcast_to(cmp_pos_v.reshape(depth, 1, -1), (depth, ROW_TILE, CMP_LEN * HEAD_DIM))
    gn = out_norm_g.reshape(depth, 1, -1)
    wr = jnp.concatenate([router_group_w, router_expert_w,
                          jnp.zeros((depth, d, LANES - N_GROUPS - N_EXPERTS), F32)], axis=2)
    wr_hi = wr.astype(BF16)
    wr_lo = (wr - wr_hi.astype(F32)).astype(BF16)
    br = jnp.concatenate([router_group_b, router_expert_b,
                          jnp.zeros((depth, LANES - N_GROUPS - N_EXPERTS), F32)], axis=1).reshape(depth, 1, LANES)
    w_out_b = w_out.astype(BF16)

    for l in range(depth):
        u, kc, vc, small = _in_projection(x, norm1_g, mod4, w_main, w_cmp, w_small, rope_c, rope_1, rope_2, l)
        cq, ckt = _forget_cumsum(small, b_pairs, l)
        ck, cv = _compress(kc, vc, wk[l], wv[l], cmp_w1_k[l], cmp_w1_v[l], pek[l], pev[l],
                           w2k[l], w2v[l], crc, cr1, cr2)
        o_a = _nsa_attention(u, ck, cv, small, gn, cover, expand, l)
        o_b = _fox_attention(u, cq, ckt, gn, l)
        o_c = _sb_attention(u, gn, l)
        x, h2, rw, ri = _out_projection(o_a, o_b, o_c, x, w_out_b, mod4, norm2_g, wr_hi, wr_lo, br, l)

        dest, p_rows, tile_expert, n_used = _dispatch_plan(ri, t)
        xs = _dispatch(h2, dest, p_rows, xs)
        ys = _expert_mlp(xs, tile_expert, n_used, expert_w1, expert_w3, expert_w2, l)
        x = _combine(x, ys, dest[0::2], dest[1::2], rw, mod4, final_g, final=(l == depth - 1), layer=l)
    return x
```

```python
import functools
import math

import numpy as np
import jax
import jax.numpy as jnp
from jax import lax
from jax.experimental import pallas as pl
from jax.experimental.pallas import tpu as pltpu

F32 = jnp.float32
BF16 = jnp.bfloat16

HEAD_DIM = 64
LANES = 128
NSA_HEADS = 8
NSA_KV = 2
NSA_GROUP = 4
FOX_HEADS = 4
SB_HEADS = 4
ROPE_DIM = 16
ROPE_HALF = 8
ROPE_THETA = 500000.0
CMP_LEN = 32
CMP_STRIDE = 16
CMP_HIDDEN = 128
SEL_LEN = 64
SEL_TOPN = 16
WINDOW = 512
FORCE_SCORE = 1.0e4
N_GROUPS = 4
EXPERTS_PER_GROUP = 8
N_EXPERTS = 32
EPS = 1e-6
LOG2E = math.log2(math.e)
NEG = -1e30

COL_QA = 0
COL_KS = 1024
COL_KW = 1152
COL_VS = 1280
COL_VW = 1408
COL_FOX = 1536
COL_SB = 2560
N_MAIN = 3584
N_SMALL = 512
W_ROPE = (NSA_HEADS + 2 * NSA_KV) * HEAD_DIM
W_MAIN = W_ROPE + (2 * NSA_KV + 3 * FOX_HEADS + 3 * SB_HEADS) * HEAD_DIM

TM_PROJ = 512
TQ_NSA = 128
TK_ATT = 256
TQ_PAIR = 256
TK_FOX = 512
ROWS = 32
SB_SUBS = 2
TM_OUT = 512
TM_EXP = 512
TM_CMB = 512
COPY_CHUNK = 512
ROW_TILE = 8
VMEM_LIMIT = 56 * 1024 * 1024


def _cp(n_axes, vmem=VMEM_LIMIT):
    return pltpu.CompilerParams(dimension_semantics=("arbitrary",) * n_axes, vmem_limit_bytes=vmem)


def _dot(a, b):
    return jnp.dot(a, b, preferred_element_type=F32)


def _dot_nt(a, b):
    return lax.dot_general(a, b, (((1,), (1,)), ((), ())), preferred_element_type=F32)


def _split_bf16(x, parts):
    out = []
    r = x
    for _ in range(parts):
        p = r.astype(BF16)
        out.append(p)
        r = r - p.astype(F32)
    return out


def _rope(x, c, s1, s2):
    return x * c + pltpu.roll(x, ROPE_HALF, 1) * s1 + pltpu.roll(x, LANES - ROPE_HALF, 1) * s2


def _softplus(z):
    return jnp.maximum(z, 0.0) + jnp.log(1.0 + jnp.exp(-jnp.abs(z)))


def _mod_kernel(c_ref, w_ref, b_ref, o_ref):
    c = c_ref[...]
    cond = c * (1.0 / (1.0 + jnp.exp(-c)))
    o_ref[0] = _dot(cond, w_ref[0]) + b_ref[0]


def _modulation(c, ada_w, ada_b):
    depth, d, n = ada_w.shape
    b = c.shape[0]
    tn = 1024
    return pl.pallas_call(
        _mod_kernel,
        grid=(depth, n // tn),
        in_specs=[pl.BlockSpec((b, d), lambda l, j: (0, 0)),
                  pl.BlockSpec((1, d, tn), lambda l, j: (l, 0, j)),
                  pl.BlockSpec((1, 1, tn), lambda l, j: (l, 0, j))],
        out_specs=pl.BlockSpec((1, b, tn), lambda l, j: (l, 0, j)),
        out_shape=jax.ShapeDtypeStruct((depth, b, n), F32),
        compiler_params=_cp(2),
        name="modulation",
    )(c, ada_w, ada_b.reshape(depth, 1, n))


def _inproj_kernel(x_ref, g_ref, sc_ref, sh_ref, w_ref, wc_ref, ws_ref, rc_ref, r1_ref, r2_ref,
                   u_ref, kc_ref, vc_ref, sm_ref):
    x = x_ref[0]
    ms = jnp.mean(x * x, axis=-1, keepdims=True)
    h = (x * lax.rsqrt(ms + EPS) * g_ref[...]) * (1.0 + sc_ref[0]) + sh_ref[0]
    hb = h.astype(BF16)
    rc, r1, r2 = rc_ref[...], r1_ref[...], r2_ref[...]
    lane = lax.broadcasted_iota(jnp.int32, (x.shape[0], LANES), 1)
    left = lane < HEAD_DIM

    def put(col, val):
        u_ref[0, :, col:col + LANES] = val.astype(BF16)

    def put_pair(col, blk, offsets):
        for i, off in enumerate(offsets):
            src = blk if off == i * HEAD_DIM else pltpu.roll(blk, HEAD_DIM, 1)
            put(col + i * LANES, jnp.where(left if off == 0 else ~left, src, 0.0))

    acc = _dot(hb, w_ref[:, :W_ROPE])
    for j in range(NSA_HEADS // 2):
        off = (2 * j // NSA_GROUP) * HEAD_DIM
        put_pair(COL_QA + 2 * j * LANES, _rope(acc[:, j * LANES:(j + 1) * LANES], rc, r1, r2), (off, off))
    nq = NSA_HEADS // 2
    put(COL_KS, _rope(acc[:, nq * LANES:(nq + 1) * LANES], rc, r1, r2))
    put(COL_KW, _rope(acc[:, (nq + 1) * LANES:(nq + 2) * LANES], rc, r1, r2))
    acc = _dot(hb, w_ref[:, W_ROPE:])
    u_ref[0, :, COL_VS:COL_VS + 2 * LANES] = acc[:, :2 * LANES].astype(BF16)
    c = 2 * LANES
    for base, heads in ((COL_FOX, FOX_HEADS), (COL_SB, SB_HEADS)):
        for j in range(heads // 2):
            put_pair(base + 2 * j * LANES, acc[:, c:c + LANES], (0, HEAD_DIM))
            c += LANES
        kv = 2 * heads * HEAD_DIM
        u_ref[0, :, base + heads * LANES:base + heads * LANES + kv] = acc[:, c:c + kv].astype(BF16)
        c += kv
    cmp_in = _dot(hb, wc_ref[...])
    kc_ref[0] = cmp_in[:, :LANES].astype(BF16)
    vc_ref[0] = cmp_in[:, LANES:].astype(BF16)
    sm_ref[0] = _dot(hb, ws_ref[...])


def _mod_spec(layer, which, d):
    return pl.BlockSpec((None, 1, None, 1, d), lambda i, j: (layer, i, which, 0, 0))


def _layer_spec(layer, shape):
    zeros = (0,) * len(shape)
    return pl.BlockSpec((None,) + tuple(shape), lambda *_: (layer,) + zeros)


def _in_projection(x, g, mod4, w_main, w_cmp, w_small, rope_c, rope_1, rope_2, layer):
    b, s, d = x.shape
    tm = min(TM_PROJ, s)
    row = lambda i, j: (i, j, 0)
    seq = lambda i, j: (j, 0)
    return pl.pallas_call(
        _inproj_kernel,
        grid=(b, s // tm),
        in_specs=[pl.BlockSpec((1, tm, d), row),
                  _layer_spec(layer, (1, d)),
                  _mod_spec(layer, 1, d),
                  _mod_spec(layer, 0, d),
                  _layer_spec(layer, (d, W_MAIN)),
                  _layer_spec(layer, (d, 2 * LANES)),
                  _layer_spec(layer, (d, N_SMALL)),
                  pl.BlockSpec((tm, LANES), seq),
                  pl.BlockSpec((tm, LANES), seq),
                  pl.BlockSpec((tm, LANES), seq)],
        out_specs=[pl.BlockSpec((1, tm, N_MAIN), row),
                   pl.BlockSpec((1, tm, LANES), row),
                   pl.BlockSpec((1, tm, LANES), row),
                   pl.BlockSpec((1, tm, N_SMALL), row)],
        out_shape=[jax.ShapeDtypeStruct((b, s, N_MAIN), BF16),
                   jax.ShapeDtypeStruct((b, s, LANES), BF16),
                   jax.ShapeDtypeStruct((b, s, LANES), BF16),
                   jax.ShapeDtypeStruct((b, s, N_SMALL), F32)],
        compiler_params=_cp(2),
        name="in_projection",
    )(x, g.reshape(-1, 1, d), mod4, mod4, w_main, w_cmp, w_small, rope_c, rope_1, rope_2)


def _cumf_kernel(f_ref, b_ref, cq_ref, ckt_ref):
    n_chunks = f_ref.shape[1] // TK_ATT
    r = lax.broadcasted_iota(jnp.int32, (TK_ATT, TK_ATT), 0)
    c = lax.broadcasted_iota(jnp.int32, (TK_ATT, TK_ATT), 1)
    tri = jnp.where(c <= r, 1.0, 0.0).astype(BF16)
    carry = jnp.zeros((1, LANES), F32)
    for j in range(n_chunks):
        f = f_ref[0, j * TK_ATT:(j + 1) * TK_ATT, :] + b_ref[0]
        ls = -_softplus(-f)
        acc = None
        for p in _split_bf16(ls, 3):
            dd = _dot(tri, p)
            acc = dd if acc is None else acc + dd
        cs = acc + carry
        cs2 = cs * LOG2E
        cq_ref[0, j * TK_ATT:(j + 1) * TK_ATT, :] = cs2
        ckt_ref[0, 0, j] = cs2.T[:ROW_TILE, :]
        carry = cs[TK_ATT - 1:TK_ATT, :]


def _forget_cumsum(small, b_pairs, layer):
    b, s, _ = small.shape
    return pl.pallas_call(
        _cumf_kernel,
        grid=(b, 2),
        in_specs=[pl.BlockSpec((1, s, LANES), lambda i, p: (i, 0, 2 + p)),
                  pl.BlockSpec((None, 1, 1, LANES), lambda i, p: (layer, p, 0, 0))],
        out_specs=[pl.BlockSpec((1, s, LANES), lambda i, p: (i, 0, p)),
                   pl.BlockSpec((1, 1, s // TK_ATT, ROW_TILE, TK_ATT), lambda i, p: (i, p, 0, 0, 0))],
        out_shape=[jax.ShapeDtypeStruct((b, s, 2 * LANES), F32),
                   jax.ShapeDtypeStruct((b, 2, s // TK_ATT, ROW_TILE, TK_ATT), F32)],
        compiler_params=_cp(2),
        name="forget_cumsum",
    )(small, b_pairs)


def _compress_kernel(ks_ref, vs_ref, wk_ref, wv_ref, w1k_ref, w1v_ref, pek_ref, pev_ref,
                     w2k_ref, w2v_ref, rc_ref, r1_ref, r2_ref, ck_ref, cv_ref):
    def one(seg_ref, w_ref, w1_ref, pe_ref, w2_ref):
        p = _dot(seg_ref[0], w_ref[...])
        half = 2 * CMP_HIDDEN
        bias = _dot(pe_ref[...].astype(BF16), w1_ref[...].astype(BF16))[0:1, :]
        bias2 = jnp.concatenate([bias, bias], axis=1)
        n = p.shape[0]
        hid = p[:, :half] + pltpu.roll(p[:, half:], n - 1, 0) + bias2
        act = hid * (1.0 / (1.0 + jnp.exp(-hid)))
        return _dot(act.astype(BF16), w2_ref[...])

    ck = one(ks_ref, wk_ref, w1k_ref, pek_ref, w2k_ref)
    ck_ref[0] = _rope(ck, rc_ref[...], r1_ref[...], r2_ref[...]).astype(BF16)
    cv_ref[0] = one(vs_ref, wv_ref, w1v_ref, pev_ref, w2v_ref).astype(BF16)


def _compress(kc, vc, wk, wv, w1k, w1v, pek, pev, w2k, w2v, rc, r1, r2):
    b, s, _ = kc.shape
    n = s // CMP_STRIDE
    width = CMP_STRIDE * LANES
    kseg = kc.reshape(b, n, width)
    vseg = vc.reshape(b, n, width)
    seg = pl.BlockSpec((1, n, width), lambda i: (i, 0, 0))
    full = lambda a: pl.BlockSpec(a.shape, lambda i: (0,) * a.ndim)
    return pl.pallas_call(
        _compress_kernel,
        grid=(b,),
        in_specs=[seg, seg, full(wk), full(wv), full(w1k), full(w1v), full(pek), full(pev),
                  full(w2k), full(w2v), full(rc), full(r1), full(r2)],
        out_specs=[pl.BlockSpec((1, n, LANES), lambda i: (i, 0, 0)),
                   pl.BlockSpec((1, n, LANES), lambda i: (i, 0, 0))],
        out_shape=[jax.ShapeDtypeStruct((b, n, LANES), BF16),
                   jax.ShapeDtypeStruct((b, n, LANES), BF16)],
        compiler_params=_cp(1),
        name="nsa_compress",
    )(kseg, vseg, wk, wv, w1k, w1v, pek, pev, w2k, w2v, rc, r1, r2)


def _nsa_kernel(q_ref, ck_ref, cv_ref, ks_ref, vs_ref, kw_ref, vw_ref, gate_ref, gn_ref,
                cover_ref, expand_ref, o_ref, *scratch):
    for g in range(NSA_KV):
        _nsa_group(g, q_ref, ck_ref, cv_ref, ks_ref, vs_ref, kw_ref, vw_ref, gate_ref, gn_ref,
                   cover_ref, expand_ref, o_ref, *scratch[3 * g:3 * g + 3])


def _nsa_group(g, q_ref, ck_ref, cv_ref, ks_ref, vs_ref, kw_ref, vw_ref, gate_ref, gn_ref,
               cover_ref, expand_ref, o_ref, m_ref, l_ref, acc_ref):
    tq = q_ref.shape[1]
    t0 = pl.program_id(1) * tq
    q = q_ref[0, :, g * NSA_GROUP * LANES:(g + 1) * NSA_GROUP * LANES]
    qall = jnp.concatenate([q[:, h * LANES:(h + 1) * LANES] for h in range(NSA_GROUP)], axis=0)
    tpos = t0 + lax.broadcasted_iota(jnp.int32, (tq, 1), 0)
    lane = lax.broadcasted_iota(jnp.int32, (tq, LANES), 1)

    n_cmp = ck_ref.shape[1]
    s_t = _dot_nt(ck_ref[0], qall)
    valid = ((CMP_STRIDE * lax.broadcasted_iota(jnp.int32, (n_cmp, tq), 0) + (CMP_LEN - 1))
             <= t0 + lax.broadcasted_iota(jnp.int32, (n_cmp, tq), 1))
    p_cols, p_sum_t = [], None
    for h in range(NSA_GROUP):
        sm = jnp.where(valid, s_t[:, h * tq:(h + 1) * tq], NEG)
        e = jnp.where(valid, jnp.exp2(sm - jnp.max(sm, axis=0, keepdims=True)), 0.0)
        den = jnp.sum(e, axis=0, keepdims=True)
        p = e * jnp.where(den > 0.0, 1.0 / den, 0.0)
        p_cols.append(p.astype(BF16))
        p_sum_t = p if p_sum_t is None else p_sum_t + p
    o_cmp = lax.dot_general(jnp.concatenate(p_cols, axis=1), cv_ref[0], (((0,), (0,)), ((), ())),
                            preferred_element_type=F32).reshape(NSA_GROUP, tq, LANES)

    n_sel = expand_ref.shape[0] * (TK_FOX // SEL_LEN)
    imp_t = None
    for piece in _split_bf16(p_sum_t, 3):
        d_imp = _dot(cover_ref[...], piece)
        imp_t = d_imp if imp_t is None else imp_t + d_imp
    imp_t = imp_t[:n_sel, :]
    blk = lax.broadcasted_iota(jnp.int32, (n_sel, tq), 0)
    tcol = t0 + lax.broadcasted_iota(jnp.int32, (n_sel, tq), 1)
    cur = jnp.right_shift(tcol, int(math.log2(SEL_LEN)))
    forced = (blk == 0) | (blk == cur) | (blk == cur - 1)
    score = jnp.where(forced, FORCE_SCORE, jnp.where(blk * SEL_LEN <= tcol, imp_t, -1.0))
    cnt = jnp.zeros((n_sel, tq), F32)
    for j in range(n_sel):
        sj = score[j:j + 1, :]
        beats = (sj > score) | ((sj == score) & (blk > j))
        cnt = cnt + jnp.where(beats, 1.0, 0.0)
    sel_t = jnp.where(cnt < float(min(SEL_TOPN, n_sel)), 1.0, 0.0)
    sel = jnp.concatenate([sel_t, jnp.zeros((LANES - n_sel, tq), F32)], axis=0).T.astype(BF16)

    def biased(bias):
        def adjust(h, t_off, cols):
            return [cols[kk] + bias[t_off:t_off + ROWS, kk * LANES:(kk + 1) * LANES]
                    for kk in range(len(cols))]
        return adjust

    kcol_s = lax.broadcasted_iota(jnp.int32, (tq, TK_FOX), 1)

    def sel_tile(kt, carry):
        k0 = pl.multiple_of(kt * TK_FOX, TK_FOX)
        hit = _dot(sel, expand_ref[kt])
        ok = (hit > 0.5) & ((k0 + kcol_s) <= tpos)
        s_t = _dot_nt(qall, ks_ref[0, pl.ds(k0, TK_FOX), :])
        _softmax_tile(s_t, vs_ref[0, pl.ds(k0, TK_FOX), :], m_ref, l_ref, acc_ref, NSA_GROUP, tq,
                      biased(jnp.where(ok, 0.0, NEG)))
        return carry

    _softmax_init(m_ref, l_ref, acc_ref)
    lax.fori_loop(0, lax.div(t0 + tq - 1, TK_FOX) + 1, sel_tile, 0)
    o_sel = _softmax_result(l_ref, acc_ref).reshape(NSA_GROUP, tq, LANES)

    span = WINDOW + tq
    w0 = pl.multiple_of(jnp.maximum(t0 - WINDOW, 0), LANES)
    kp = w0 + lax.broadcasted_iota(jnp.int32, (tq, span), 1)
    bias_w = jnp.where((kp <= tpos) & (kp > tpos - WINDOW), 0.0, NEG)
    s_w = _dot_nt(qall, kw_ref[0, pl.ds(w0, span), :])
    nkw = span // LANES
    p_rows, inv_rows = [], []
    for h in range(NSA_GROUP):
        for c in range(tq // ROWS):
            r0 = h * tq + c * ROWS
            cols = [s_w[r0:r0 + ROWS, kk * LANES:(kk + 1) * LANES]
                    + bias_w[c * ROWS:(c + 1) * ROWS, kk * LANES:(kk + 1) * LANES] for kk in range(nkw)]
            mx = cols[0]
            for kk in range(1, nkw):
                mx = jnp.maximum(mx, cols[kk])
            mx = jnp.max(mx, axis=-1, keepdims=True)
            pks = [jnp.exp2(cols[kk] - mx) for kk in range(nkw)]
            psum = pks[0]
            for kk in range(1, nkw):
                psum = psum + pks[kk]
            inv_rows.append(jnp.broadcast_to(1.0 / jnp.sum(psum, axis=-1, keepdims=True), (ROWS, LANES)))
            p_rows.append(jnp.concatenate([pk.astype(BF16) for pk in pks], axis=1))
    o_win = _dot(jnp.concatenate(p_rows, axis=0), vw_ref[0, pl.ds(w0, span), :])
    o_win = (o_win * jnp.concatenate(inv_rows, axis=0)).reshape(NSA_GROUP, tq, LANES)

    gt = gate_ref[0, :, g * LANES:(g + 1) * LANES]
    gt = 1.0 / (1.0 + jnp.exp(-gt))
    mine = (lane >= g * HEAD_DIM) & (lane < (g + 1) * HEAD_DIM)
    outs = []
    for h in range(NSA_GROUP):
        o = (gt[:, 3 * h:3 * h + 1] * o_cmp[h] + gt[:, 3 * h + 1:3 * h + 2] * o_sel[h]
             + gt[:, 3 * h + 2:3 * h + 3] * o_win[h])
        o = jnp.where(mine, o, 0.0)
        ms = jnp.sum(o * o, axis=-1, keepdims=True) * (1.0 / HEAD_DIM)
        o = o * lax.rsqrt(ms + EPS)
        outs.append(o + pltpu.roll(o, HEAD_DIM, 1))
    left = lane < HEAD_DIM
    for pair in range(NSA_GROUP // 2):
        col = (g * NSA_GROUP // 2 + pair) * LANES
        o_ref[0, :, col:col + LANES] = (jnp.where(left, outs[2 * pair], outs[2 * pair + 1])
                                        * gn_ref[...][:, col:col + LANES]).astype(o_ref.dtype)


def _nsa_attention(u, ck, cv, small, gn, cover, expand, layer):
    b, s, _ = u.shape
    tq = min(TQ_NSA, s)
    assert s >= WINDOW + tq and WINDOW % tq == 0
    n_cmp = ck.shape[1]
    blk = LANES
    kv = lambda col: pl.BlockSpec((1, s, LANES), lambda i, j, col=col: (i, 0, col // blk))
    return pl.pallas_call(
        _nsa_kernel,
        grid=(b, s // tq),
        in_specs=[pl.BlockSpec((1, tq, NSA_HEADS * LANES), lambda i, j: (i, j, 0)),
                  pl.BlockSpec((1, n_cmp, LANES), lambda i, j: (i, 0, 0)),
                  pl.BlockSpec((1, n_cmp, LANES), lambda i, j: (i, 0, 0)),
                  kv(COL_KS), kv(COL_VS), kv(COL_KW), kv(COL_VW),
                  pl.BlockSpec((1, tq, NSA_KV * LANES), lambda i, j: (i, j, 0)),
                  pl.BlockSpec((None, 1, NSA_HEADS * HEAD_DIM), lambda i, j: (layer, 0, 0)),
                  pl.BlockSpec(cover.shape, lambda i, j: (0, 0)),
                  pl.BlockSpec(expand.shape, lambda i, j: (0, 0, 0))],
        out_specs=pl.BlockSpec((1, tq, NSA_HEADS * HEAD_DIM), lambda i, j: (i, j, 0)),
        out_shape=jax.ShapeDtypeStruct((b, s, NSA_HEADS * HEAD_DIM), BF16),
        scratch_shapes=_softmax_scratch(NSA_GROUP * tq) * NSA_KV,
        compiler_params=_cp(2),
        name="nsa_attention",
    )(u, ck, cv, u, u, u, u, small, gn, cover, expand)


def _pair_finish(acc, gn_ref, o_ref, lanes, tq):
    lane = lax.broadcasted_iota(jnp.int32, (tq, LANES), 1)
    left = lane < HEAD_DIM
    o = jnp.where(left, acc[0], acc[1])
    o2 = o * o
    ms_l = jnp.sum(jnp.where(left, o2, 0.0), axis=-1, keepdims=True) * (1.0 / HEAD_DIM)
    ms_r = jnp.sum(jnp.where(left, 0.0, o2), axis=-1, keepdims=True) * (1.0 / HEAD_DIM)
    inv = jnp.where(left, lax.rsqrt(ms_l + EPS), lax.rsqrt(ms_r + EPS))
    o_ref[0, :, lanes] = (o * inv * gn_ref[...][:, lanes]).astype(o_ref.dtype)


def _softmax_tile(s, v, m_ref, l_ref, acc_ref, heads, tq, adjust, p_ref=None):
    nk = s.shape[1] // LANES
    p_rows = []
    for h in range(heads):
        for c in range(tq // ROWS):
            t_off = c * ROWS
            r0 = h * tq + t_off
            cols = [s[r0:r0 + ROWS, k * LANES:(k + 1) * LANES] for k in range(nk)]
            cols = adjust(h, t_off, cols)
            mx = cols[0]
            for k in range(1, nk):
                mx = jnp.maximum(mx, cols[k])
            m_old = m_ref[r0:r0 + ROWS, :]
            m_new = jnp.maximum(m_old, jnp.max(mx, axis=-1, keepdims=True))
            alpha = jnp.exp2(m_old - m_new)
            pks = [jnp.exp2(cols[k] - m_new) for k in range(nk)]
            psum = pks[0]
            for k in range(1, nk):
                psum = psum + pks[k]
            p_chunk = jnp.concatenate([pk.astype(BF16) for pk in pks], axis=1)
            if p_ref is None:
                p_rows.append(p_chunk)
            else:
                p_ref[r0:r0 + ROWS, :] = p_chunk
            l_ref[r0:r0 + ROWS, :] = alpha * l_ref[r0:r0 + ROWS, :] + psum
            acc_ref[r0:r0 + ROWS, :] = alpha * acc_ref[r0:r0 + ROWS, :]
            m_ref[r0:r0 + ROWS, :] = m_new
    if p_ref is None:
        acc_ref[...] += _dot(jnp.concatenate(p_rows, axis=0), v)


def _tile_loop(score_fn, process_fn, lo, hi):
    def body(kt, carry):
        process_fn(kt, score_fn(kt))
        return carry

    lax.fori_loop(lo, hi - 1, body, 0)
    return score_fn(hi - 1)


def _softmax_init(m_ref, l_ref, acc_ref):
    m_ref[...] = jnp.full(m_ref.shape, NEG, F32)
    l_ref[...] = jnp.zeros(l_ref.shape, F32)
    acc_ref[...] = jnp.zeros(acc_ref.shape, F32)


def _softmax_result(l_ref, acc_ref):
    return acc_ref[...] / jnp.sum(l_ref[...], axis=-1, keepdims=True)


def _softmax_scratch(rows):
    return [pltpu.VMEM((rows, LANES), F32), pltpu.VMEM((rows, LANES), F32), pltpu.VMEM((rows, LANES), F32)]


def _fox_kernel(q_ref, k_ref, v_ref, cq_ref, ckt_ref, gn_ref, o_ref, *scratch):
    for pair in range(FOX_HEADS // 2):
        _fox_pair(pair, q_ref, k_ref, v_ref, cq_ref, ckt_ref, gn_ref, o_ref, *scratch[5 * pair:5 * pair + 5])


def _fox_pair(pair, q_ref, k_ref, v_ref, cq_ref, ckt_ref, gn_ref, o_ref,
              m_ref, l_ref, acc_ref, cqr_ref, p_ref):
    lanes = slice(pair * LANES, (pair + 1) * LANES)
    tq = q_ref.shape[1]
    tk = TK_FOX
    t0 = pl.program_id(1) * tq
    qall = jnp.concatenate([q_ref[0, :, (2 * pair + h) * LANES:(2 * pair + h + 1) * LANES] for h in range(2)],
                           axis=0)
    cq = cq_ref[0, :, lanes]
    cqr_ref[0:tq, :] = jnp.broadcast_to(cq[:, 0:1], (tq, LANES))
    cqr_ref[tq:2 * tq, :] = jnp.broadcast_to(cq[:, 1:2], (tq, LANES))
    _softmax_init(m_ref, l_ref, acc_ref)
    n_tiles = lax.div(t0 + tq - 1, tk) + 1
    diag = (lax.broadcasted_iota(jnp.int32, (ROWS, LANES), 1)
            - lax.broadcasted_iota(jnp.int32, (ROWS, LANES), 0))

    def scores(kt):
        return _dot_nt(qall, k_ref[0, pl.ds(pl.multiple_of(kt * tk, tk), tk), lanes])

    def tile(kt, s, masked):
        k0 = pl.multiple_of(kt * tk, tk)
        cks = [ckt_ref[0, pair, kt * (tk // TK_ATT) + j] for j in range(tk // TK_ATT)]

        def adjust(h, t_off, cols):
            out = []
            cqr = cqr_ref[h * tq + t_off:h * tq + t_off + ROWS, :]
            for kk in range(tk // LANES):
                lo = (kk * LANES) % TK_ATT
                ck = cks[(kk * LANES) // TK_ATT][h:h + 1, lo:lo + LANES]
                val = (cols[kk] - ck) + cqr
                if masked:
                    val = jnp.where(diag <= (t0 + t_off) - (k0 + kk * LANES), val, NEG)
                out.append(val)
            return out

        kp = pl.multiple_of(jnp.maximum(kt - 1, 0) * tk, tk)
        acc_ref[...] += _dot(p_ref[...], v_ref[0, pl.ds(kp, tk), lanes])
        _softmax_tile(s, None, m_ref, l_ref, acc_ref, 2, tq, adjust, p_ref=p_ref)

    p_ref[...] = jnp.zeros(p_ref.shape, BF16)
    s_last = _tile_loop(scores, lambda kt, s: tile(kt, s, False), 0, n_tiles)
    tile(n_tiles - 1, s_last, True)
    acc_ref[...] += _dot(p_ref[...], v_ref[0, pl.ds(pl.multiple_of((n_tiles - 1) * tk, tk), tk), lanes])
    acc = _softmax_result(l_ref, acc_ref).reshape(2, tq, LANES)
    _pair_finish(acc, gn_ref, o_ref, lanes, tq)


def _fox_attention(u, cq, ckt, gn, layer):
    b, s, _ = u.shape
    tq = min(TQ_PAIR, s)
    qw, kvw = FOX_HEADS * LANES, FOX_HEADS * HEAD_DIM
    qb = COL_FOX // qw
    kb = (COL_FOX + qw) // kvw
    gb = NSA_HEADS * HEAD_DIM // kvw
    return pl.pallas_call(
        _fox_kernel,
        grid=(b, s // tq),
        in_specs=[pl.BlockSpec((1, tq, qw), lambda i, j: (i, j, qb)),
                  pl.BlockSpec((1, s, kvw), lambda i, j: (i, 0, kb)),
                  pl.BlockSpec((1, s, kvw), lambda i, j: (i, 0, kb + 1)),
                  pl.BlockSpec((1, tq, 2 * LANES), lambda i, j: (i, j, 0)),
                  pl.BlockSpec((1, 2, s // TK_ATT, ROW_TILE, TK_ATT), lambda i, j: (i, 0, 0, 0, 0)),
                  pl.BlockSpec((None, 1, kvw), lambda i, j: (layer, 0, gb))],
        out_specs=pl.BlockSpec((1, tq, kvw), lambda i, j: (i, j, 0)),
        out_shape=jax.ShapeDtypeStruct((b, s, FOX_HEADS * HEAD_DIM), BF16),
        scratch_shapes=(_softmax_scratch(2 * tq) + [pltpu.VMEM((2 * tq, LANES), F32),
                                                    pltpu.VMEM((2 * tq, TK_FOX), BF16)]) * (FOX_HEADS // 2),
        compiler_params=_cp(2),
        name="fox_attention",
    )(u, u, u, cq, ckt, gn)


def _sb_kernel(q_ref, k_ref, v_ref, gn_ref, o_ref, *scratch):
    for pair in range(SB_HEADS // 2):
        _sb_pair(pair, q_ref, k_ref, v_ref, gn_ref, o_ref, *scratch[2 * pair:2 * pair + 2])


def _sb_pair(pair, q_ref, k_ref, v_ref, gn_ref, o_ref, rest_ref, acc_ref):
    tq = q_ref.shape[1]
    sub = TK_ATT
    tk = SB_SUBS * sub
    nks = sub // LANES
    rows = 2 * tq
    t0 = pl.program_id(1) * tq
    r = lax.broadcasted_iota(jnp.int32, (sub, sub), 0)
    c = lax.broadcasted_iota(jnp.int32, (sub, sub), 1)
    upper = jnp.where(r >= c, 1.0, 0.0).astype(BF16)
    n_tiles = lax.div(t0 + tq - 1, tk) + 1
    diag = (lax.broadcasted_iota(jnp.int32, (ROWS, LANES), 1)
            - lax.broadcasted_iota(jnp.int32, (ROWS, LANES), 0))
    rest_ref[...] = jnp.zeros(rest_ref.shape, F32)
    acc_ref[...] = jnp.zeros(acc_ref.shape, F32)
    hi_mask = jnp.uint32(0xFFFF0000)
    lanes = slice(pair * LANES, (pair + 1) * LANES)
    qall = jnp.concatenate([q_ref[0, :, (2 * pair + h) * LANES:(2 * pair + h + 1) * LANES] for h in range(2)],
                           axis=0)
    chunks = [(h * tq + cc * ROWS, cc * ROWS) for h in range(2) for cc in range(tq // ROWS)]

    def tile(kt, masked):
        k0 = pl.multiple_of(kt * tk, tk)
        k = k_ref[0, pl.ds(k0, tk), lanes]
        v = v_ref[0, pl.ds(k0, tk), lanes]

        def strictly_before(t_off, col):
            return diag < (t0 + t_off) - (k0 + col)

        z = _dot_nt(qall, k)
        his = [[] for _ in range(SB_SUBS)]
        los = [[] for _ in range(SB_SUBS)]
        for r0, t_off in chunks:
            for sb in range(SB_SUBS):
                hi_c, lo_c = [], []
                for kk in range(nks):
                    col = sb * sub + kk * LANES
                    zc = z[r0:r0 + ROWS, col:col + LANES]
                    l = -(jnp.maximum(zc, 0.0) + jnp.log2(1.0 + jnp.exp2(-jnp.abs(zc))))
                    if masked:
                        l = jnp.where(strictly_before(t_off, col), l, 0.0)
                    hi = pltpu.bitcast(pltpu.bitcast(l, jnp.uint32) & hi_mask, F32)
                    hi_c.append(hi.astype(BF16))
                    lo_c.append((l - hi).astype(BF16))
                his[sb].append(jnp.concatenate(hi_c, axis=1))
                los[sb].append(jnp.concatenate(lo_c, axis=1))
        hi_all = jnp.concatenate([x for sb in range(SB_SUBS) for x in his[sb]], axis=0)
        lo_all = jnp.concatenate([x for sb in range(SB_SUBS) for x in los[sb]], axis=0)
        cum = _dot(hi_all, upper) + _dot(lo_all, upper)

        a_rows = []
        for r0, t_off in chunks:
            base = rest_ref[r0:r0 + ROWS, :]
            a_c = [None] * (SB_SUBS * nks)
            for sb in reversed(range(SB_SUBS)):
                cs = cum[sb * rows + r0:sb * rows + r0 + ROWS, :]
                for kk in range(nks):
                    col = sb * sub + kk * LANES
                    a = jnp.exp2(z[r0:r0 + ROWS, col:col + LANES] + cs[:, kk * LANES:(kk + 1) * LANES] + base)
                    if masked:
                        a = jnp.where(strictly_before(t_off, col), a, 0.0)
                    a_c[sb * nks + kk] = a.astype(BF16)
                base = base + jnp.broadcast_to(cs[:, 0:1], (ROWS, LANES))
            a_rows.append(jnp.concatenate(a_c, axis=1))
            rest_ref[r0:r0 + ROWS, :] = base
        acc_ref[...] += _dot(jnp.concatenate(a_rows, axis=0), v)

    tile(n_tiles - 1, True)

    def full_tile(i, carry):
        tile(n_tiles - 2 - i, False)
        return carry

    lax.fori_loop(0, n_tiles - 1, full_tile, 0)
    _pair_finish(acc_ref[...].reshape(2, tq, LANES), gn_ref, o_ref, lanes, tq)


def _sb_attention(u, gn, layer):
    b, s, _ = u.shape
    tq = min(TQ_PAIR, s)
    qw, kvw = SB_HEADS * LANES, SB_HEADS * HEAD_DIM
    qb = COL_SB // qw
    kb = (COL_SB + qw) // kvw
    gb = (NSA_HEADS + FOX_HEADS) * HEAD_DIM // kvw
    return pl.pallas_call(
        _sb_kernel,
        grid=(b, s // tq),
        in_specs=[pl.BlockSpec((1, tq, qw), lambda i, j: (i, j, qb)),
                  pl.BlockSpec((1, s, kvw), lambda i, j: (i, 0, kb)),
                  pl.BlockSpec((1, s, kvw), lambda i, j: (i, 0, kb + 1)),
                  pl.BlockSpec((None, 1, kvw), lambda i, j: (layer, 0, gb))],
        out_specs=pl.BlockSpec((1, tq, kvw), lambda i, j: (i, j, 0)),
        out_shape=jax.ShapeDtypeStruct((b, s, SB_HEADS * HEAD_DIM), BF16),
        scratch_shapes=[pltpu.VMEM((2 * tq, LANES), F32), pltpu.VMEM((2 * tq, LANES), F32)] * (SB_HEADS // 2),
        compiler_params=_cp(2),
        name="sb_attention",
    )(u, u, u, gn)


def _outproj_kernel(oa_ref, ob_ref, oc_ref, x_ref, w_ref, g1_ref, n2_ref, sc_ref, sh_ref,
                    wrh_ref, wrl_ref, br_ref, xo_ref, h_ref, rw_ref, ri_ref):
    na = oa_ref.shape[2]
    nb = ob_ref.shape[2]
    y = _dot(oa_ref[0], w_ref[0:na, :])
    y = y + _dot(ob_ref[0], w_ref[na:na + nb, :])
    y = y + _dot(oc_ref[0], w_ref[na + nb:, :])
    x = x_ref[0] + g1_ref[0] * y
    xo_ref[0] = x
    ms = jnp.mean(x * x, axis=-1, keepdims=True)
    h = (x * lax.rsqrt(ms + EPS) * n2_ref[...]) * (1.0 + sc_ref[0]) + sh_ref[0]
    hb = h.astype(BF16)
    _store_row_tiles(h_ref, h)
    hl = (h - hb.astype(F32)).astype(BF16)
    logit = _dot(hb, wrh_ref[...]) + _dot(hl, wrh_ref[...]) + _dot(hb, wrl_ref[...]) + br_ref[...]

    tm = logit.shape[0]
    lane = lax.broadcasted_iota(jnp.int32, (tm, LANES), 1).astype(F32)
    big = float(LANES)
    is_g = lane < N_GROUPS
    lg = jnp.where(is_g, logit, NEG)
    mg = jnp.max(lg, axis=-1, keepdims=True)
    zg = jnp.sum(jnp.where(is_g, jnp.exp(lg - mg), 0.0), axis=-1, keepdims=True)
    pg = 1.0 / zg
    gi = jnp.min(jnp.where(is_g & (lg == mg), lane, big), axis=-1, keepdims=True)
    e_lane = lane - N_GROUPS
    in_grp = (e_lane >= gi * EXPERTS_PER_GROUP) & (e_lane < (gi + 1) * EXPERTS_PER_GROUP)
    le = jnp.where(in_grp, logit, NEG)
    m1 = jnp.max(le, axis=-1, keepdims=True)
    i1 = jnp.min(jnp.where(in_grp & (le == m1), lane, big), axis=-1, keepdims=True)
    rest = in_grp & (lane != i1)
    le2 = jnp.where(rest, logit, NEG)
    m2 = jnp.max(le2, axis=-1, keepdims=True)
    i2 = jnp.min(jnp.where(rest & (le2 == m2), lane, big), axis=-1, keepdims=True)
    ze = jnp.sum(jnp.where(in_grp, jnp.exp(le - m1), 0.0), axis=-1, keepdims=True)
    p1 = 1.0 / ze
    p2 = jnp.exp(m2 - m1) / ze
    den = p1 + p2
    w1 = pg * (p1 / den)
    w2 = pg * (p2 / den)
    rw_ref[0] = jnp.where(lane == 0.0, w1, jnp.where(lane == 1.0, w2, 0.0))
    ri_ref[0] = jnp.where(lane == 0.0, i1 - N_GROUPS, jnp.where(lane == 1.0, i2 - N_GROUPS, 0.0)).astype(jnp.int32)


def _out_projection(oa, ob, oc, x, w_out, mod4, n2, wr_hi, wr_lo, br, layer):
    b, s, d = x.shape
    tm = min(TM_OUT, s)
    row = lambda i, j: (i, j, 0)
    return pl.pallas_call(
        _outproj_kernel,
        grid=(b, s // tm),
        in_specs=[pl.BlockSpec((1, tm, oa.shape[2]), row),
                  pl.BlockSpec((1, tm, ob.shape[2]), row),
                  pl.BlockSpec((1, tm, oc.shape[2]), row),
                  pl.BlockSpec((1, tm, d), row),
                  _layer_spec(layer, w_out.shape[1:]),
                  _mod_spec(layer, 2, d),
                  _layer_spec(layer, (1, d)),
                  _mod_spec(layer, 4, d),
                  _mod_spec(layer, 3, d),
                  _layer_spec(layer, (d, LANES)),
                  _layer_spec(layer, (d, LANES)),
                  _layer_spec(layer, (1, LANES))],
        out_specs=[pl.BlockSpec((1, tm, d), row),
                   pl.BlockSpec((tm * ROW_TILE, LANES), lambda i, j: (i * (s // tm) + j, 0)),
                   pl.BlockSpec((1, tm, LANES), row),
                   pl.BlockSpec((1, tm, LANES), row)],
        out_shape=[jax.ShapeDtypeStruct((b, s, d), F32),
                   jax.ShapeDtypeStruct((b * s * ROW_TILE, LANES), F32),
                   jax.ShapeDtypeStruct((b, s, LANES), F32),
                   jax.ShapeDtypeStruct((b, s, LANES), jnp.int32)],
        compiler_params=_cp(2),
        name="out_projection",
    )(oa, ob, oc, x, w_out, mod4, n2.reshape(-1, 1, d), mod4, mod4, wr_hi, wr_lo, br)


def _store_row_tiles(ref, val):
    tm = val.shape[0]
    for c in range(ROW_TILE):
        ref[pl.ds(c, tm, stride=ROW_TILE), :] = val[:, c * LANES:(c + 1) * LANES]


def _load_row_tiles(ref, tm):
    return [ref[pl.ds(c, tm, stride=ROW_TILE), :] for c in range(ROW_TILE)]


def _tile_rows(ref, n):
    return ref.at[pl.ds(pl.multiple_of(n * ROW_TILE, ROW_TILE), ROW_TILE), :]


def _dispatch_kernel(idx_ref, src_ref, init_hbm, dst_hbm, sem):
    del init_hbm

    def issue(tok, carry):
        for k in range(2):
            pltpu.make_async_copy(_tile_rows(src_ref, tok), _tile_rows(dst_hbm, idx_ref[0, 0, 2 * tok + k]),
                                  sem).start(priority=k)
        return carry

    lax.fori_loop(0, COPY_CHUNK // 2, issue, 0, unroll=4)
    for _ in range(2):
        pltpu.make_async_copy(src_ref, dst_hbm.at[pl.ds(0, src_ref.shape[0]), :], sem).wait()


def _dispatch(src, idx, n_dst, init):
    n = idx.shape[0]
    if init is None:
        init = jnp.zeros((n_dst * ROW_TILE, LANES), src.dtype)
    return pl.pallas_call(
        _dispatch_kernel,
        grid=(n // COPY_CHUNK,),
        in_specs=[pl.BlockSpec((1, 1, COPY_CHUNK), lambda i: (i, 0, 0), memory_space=pltpu.SMEM),
                  pl.BlockSpec((COPY_CHUNK // 2 * ROW_TILE, LANES), lambda i: (i, 0)),
                  pl.BlockSpec(memory_space=pl.ANY)],
        out_specs=pl.BlockSpec(memory_space=pl.ANY),
        out_shape=jax.ShapeDtypeStruct((n_dst * ROW_TILE, LANES), src.dtype),
        scratch_shapes=[pltpu.SemaphoreType.DMA],
        input_output_aliases={2: 0},
        compiler_params=_cp(1),
        name="moe_dispatch",
    )(idx.reshape(n // COPY_CHUNK, 1, COPY_CHUNK), src, init)


def _expert_kernel(te_ref, nu_ref, x_ref, w1_ref, w3_ref, w2_ref, y_ref, w1b_ref, w3b_ref, w2b_ref):
    i = pl.program_id(0)

    @pl.when((i == 0) | (te_ref[i] != te_ref[jnp.maximum(i - 1, 0)]))
    def _():
        w1b_ref[...] = w1_ref[0].astype(BF16)
        w3b_ref[...] = w3_ref[0].astype(BF16)
        w2b_ref[...] = w2_ref[0].astype(BF16)

    @pl.when(i < nu_ref[0])
    def _():
        x = jnp.concatenate(_load_row_tiles(x_ref, TM_EXP), axis=1).astype(BF16)
        a = _dot(x, w1b_ref[...])
        g = _dot(x, w3b_ref[...])
        act = (a * (1.0 / (1.0 + jnp.exp(-a)))) * g
        _store_row_tiles(y_ref, _dot(act.astype(BF16), w2b_ref[...]))

    @pl.when(i >= nu_ref[0])
    def _():
        y_ref[...] = jnp.zeros_like(y_ref)


def _expert_mlp(xs, tile_expert, n_used, w1, w3, w2, layer):
    d, de = w1.shape[2], w1.shape[3]
    n_tiles = xs.shape[0] // (TM_EXP * ROW_TILE)
    grid_spec = pltpu.PrefetchScalarGridSpec(
        num_scalar_prefetch=2,
        grid=(n_tiles,),
        in_specs=[pl.BlockSpec((TM_EXP * ROW_TILE, LANES), lambda i, te, nu: (i, 0)),
                  pl.BlockSpec((None, 1, d, de), lambda i, te, nu: (layer, te[i], 0, 0)),
                  pl.BlockSpec((None, 1, d, de), lambda i, te, nu: (layer, te[i], 0, 0)),
                  pl.BlockSpec((None, 1, de, d), lambda i, te, nu: (layer, te[i], 0, 0))],
        out_specs=pl.BlockSpec((TM_EXP * ROW_TILE, LANES), lambda i, te, nu: (i, 0)),
        scratch_shapes=[pltpu.VMEM((d, de), BF16), pltpu.VMEM((d, de), BF16), pltpu.VMEM((de, d), BF16)],
    )
    return pl.pallas_call(
        _expert_kernel,
        grid_spec=grid_spec,
        out_shape=jax.ShapeDtypeStruct(xs.shape, F32),
        compiler_params=_cp(1),
        name="expert_mlp",
    )(tile_expert, n_used, xs, w1, w3, w2)


def _combine_kernel(d0_ref, d1_ref, x_ref, ys_hbm, rw_ref, g2_ref, fg_ref, o_ref, y0_ref, y1_ref, sem,
                    *, final):
    tm = x_ref.shape[1]

    def issue(r, carry):
        pltpu.make_async_copy(_tile_rows(ys_hbm, d0_ref[0, 0, r]), _tile_rows(y0_ref, r),
                              sem.at[0]).start(priority=0)
        pltpu.make_async_copy(_tile_rows(ys_hbm, d1_ref[0, 0, r]), _tile_rows(y1_ref, r),
                              sem.at[1]).start(priority=1)
        return carry

    lax.fori_loop(0, tm, issue, 0, unroll=8)
    pltpu.make_async_copy(ys_hbm.at[pl.ds(0, tm * ROW_TILE), :], y0_ref, sem.at[0]).wait()
    pltpu.make_async_copy(ys_hbm.at[pl.ds(0, tm * ROW_TILE), :], y1_ref, sem.at[1]).wait()

    rw = rw_ref[0]
    w0 = jnp.broadcast_to(rw[:, 0:1], (tm, LANES))
    w1 = jnp.broadcast_to(rw[:, 1:2], (tm, LANES))
    y0 = _load_row_tiles(y0_ref, tm)
    y1 = _load_row_tiles(y1_ref, tm)
    cols = []
    for c in range(ROW_TILE):
        sl = slice(c * LANES, (c + 1) * LANES)
        cols.append(x_ref[0, :, sl] + g2_ref[0, :, sl] * (y0[c] * w0 + y1[c] * w1))
    if final:
        ssq = cols[0] * cols[0]
        for c in range(1, ROW_TILE):
            ssq = ssq + cols[c] * cols[c]
        inv = lax.rsqrt(jnp.sum(ssq, axis=-1, keepdims=True) * (1.0 / (ROW_TILE * LANES)) + EPS)
        cols = [cols[c] * inv * fg_ref[:, c * LANES:(c + 1) * LANES] for c in range(ROW_TILE)]
    for c in range(ROW_TILE):
        o_ref[0, :, c * LANES:(c + 1) * LANES] = cols[c]


def _combine(x, ys, dest0, dest1, rw, mod4, final_g, final, layer):
    b, s, d = x.shape
    tm = min(TM_CMB, s)
    row = lambda i, j: (i, j, 0)
    idx_spec = pl.BlockSpec((1, 1, tm), lambda i, j: (i * (s // tm) + j, 0, 0), memory_space=pltpu.SMEM)
    return pl.pallas_call(
        functools.partial(_combine_kernel, final=final),
        grid=(b, s // tm),
        in_specs=[idx_spec, idx_spec,
                  pl.BlockSpec((1, tm, d), row),
                  pl.BlockSpec(memory_space=pl.ANY),
                  pl.BlockSpec((1, tm, LANES), row),
                  _mod_spec(layer, 5, d),
                  pl.BlockSpec((1, d), lambda i, j: (0, 0))],
        out_specs=pl.BlockSpec((1, tm, d), row),
        out_shape=jax.ShapeDtypeStruct((b, s, d), F32),
        scratch_shapes=[pltpu.VMEM((tm * ROW_TILE, LANES), F32), pltpu.VMEM((tm * ROW_TILE, LANES), F32),
                        pltpu.SemaphoreType.DMA((2,))],
        compiler_params=_cp(2),
        name="moe_combine_final" if final else "moe_combine",
    )(dest0.reshape(-1, 1, tm), dest1.reshape(-1, 1, tm), x, ys, rw, mod4, final_g.reshape(1, d))


def _layout_w_in(w_in):
    d = w_in.shape[0]
    kvw = NSA_KV * HEAD_DIM
    sizes = (NSA_HEADS * HEAD_DIM, kvw, kvw, kvw, kvw, kvw, kvw, NSA_HEADS * 3,
             FOX_HEADS * HEAD_DIM, FOX_HEADS * HEAD_DIM, FOX_HEADS * HEAD_DIM, FOX_HEADS,
             SB_HEADS * HEAD_DIM, SB_HEADS * HEAD_DIM, SB_HEADS * HEAD_DIM)
    pts = np.cumsum(sizes)[:-1].tolist()
    (qa, kca, vca, ksa, vsa, kwa, vwa, ga, qb, kb, vb, fb, qc, kc, vc) = jnp.split(w_in, pts, axis=1)
    scale = HEAD_DIM ** -0.5 * LOG2E
    main = jnp.concatenate([qa * scale, ksa, kwa, vsa, vwa, qb * scale, kb, vb, qc * scale, kc, vc],
                           axis=1).astype(BF16)
    cmp_w = jnp.concatenate([kca, vca], axis=1).astype(BF16)
    zpad = lambda n: jnp.zeros((d, n), w_in.dtype)
    per_grp = NSA_GROUP * 3
    small = jnp.concatenate([ga[:, :per_grp], zpad(LANES - per_grp), ga[:, per_grp:], zpad(LANES - per_grp),
                             fb[:, 0:2], zpad(LANES - 2), fb[:, 2:4], zpad(LANES - 2)], axis=1).astype(BF16)
    return main, cmp_w, small


def _layout_cmp(w1, w2):
    hid = w1.shape[1]
    w1r = w1.reshape(2, CMP_STRIDE, HEAD_DIM, hid)
    z = jnp.zeros((CMP_STRIDE, HEAD_DIM, hid), w1.dtype)
    cols = []
    for half in range(2):
        for g in range(NSA_KV):
            parts = [w1r[half] if gg == g else z for gg in range(NSA_KV)]
            cols.append(jnp.concatenate(parts, axis=1).reshape(CMP_STRIDE * LANES, hid))
    wcat = jnp.concatenate(cols, axis=1).astype(BF16)
    zz = jnp.zeros_like(w2)
    w2bd = jnp.concatenate([jnp.concatenate([w2, zz], axis=1),
                            jnp.concatenate([zz, w2], axis=1)], axis=0).astype(BF16)
    return wcat, w2bd


def _rope_tables(pos):
    inv = jnp.exp(jnp.arange(ROPE_HALF, dtype=F32) * (-2.0 * math.log(ROPE_THETA) / ROPE_DIM))
    ang = pos.astype(F32)[:, None] * inv[None, :]
    cos, sin = jnp.cos(ang), jnp.sin(ang)
    n = pos.shape[0]
    z8 = jnp.zeros((n, ROPE_HALF), F32)
    rest1 = jnp.ones((n, HEAD_DIM - ROPE_DIM), F32)
    rest0 = jnp.zeros((n, HEAD_DIM - ROPE_DIM), F32)
    c = jnp.concatenate([cos, cos, rest1], axis=1)
    s1 = jnp.concatenate([z8, sin, rest0], axis=1)
    s2 = jnp.concatenate([-sin, z8, rest0], axis=1)
    dup = lambda a: jnp.concatenate([a, a], axis=1)
    return dup(c), dup(s1), dup(s2)


def _static_tables(s):
    n_cmp_pad = s // CMP_STRIDE
    n = np.arange(n_cmp_pad)[:, None]
    j = np.arange(LANES)[None, :]
    n_sel = s // SEL_LEN
    cover = ((n * CMP_STRIDE < j * SEL_LEN + SEL_LEN) & (n * CMP_STRIDE + CMP_LEN > j * SEL_LEN)
             & (j < n_sel)).astype(np.float32)
    nt = s // TK_FOX
    key = np.arange(nt)[:, None, None] * TK_FOX + np.arange(TK_FOX)[None, None, :]
    expand = (key // SEL_LEN == np.arange(LANES)[None, :, None]).astype(np.float32)
    return jnp.asarray(cover.T, BF16), jnp.asarray(expand, BF16)


def _plan_kernel(ri_ref, dest_ref, te_ref, cnt_ref, pst_ref, before_ref):
    phase = pl.program_id(0)
    j = pl.program_id(1)
    tm = ri_ref.shape[1]
    lane = lax.broadcasted_iota(jnp.int32, (tm, LANES), 1)
    ri = ri_ref[0]
    hot0 = jnp.where(lane == ri[:, 0:1], 1.0, 0.0)
    hot1 = jnp.where(lane == ri[:, 1:2], 1.0, 0.0)
    both = hot0 + hot1

    @pl.when((phase == 0) & (j == 0))
    def _():
        cnt_ref[...] = jnp.zeros(cnt_ref.shape, F32)

    @pl.when(phase == 0)
    def _():
        cnt_ref[...] += jnp.sum(both, axis=0, keepdims=True)

    @pl.when((phase == 0) & (j == pl.num_programs(1) - 1))
    def _():
        cnt = cnt_ref[...]
        padded = jnp.ceil(cnt * (1.0 / TM_EXP)) * TM_EXP
        r = lax.broadcasted_iota(jnp.int32, (LANES, LANES), 0)
        c = lax.broadcasted_iota(jnp.int32, (LANES, LANES), 1)
        incl = jnp.where(r <= c, 1.0, 0.0).astype(BF16)
        rows = jnp.broadcast_to(padded, (LANES, LANES)).astype(BF16)
        pends = _dot(rows, incl)
        pst_ref[...] = pends[0:1, :] - padded
        ends = pends.T
        tile_start = (c * TM_EXP).astype(F32)
        done = jnp.where((r < N_EXPERTS) & (ends <= tile_start), 1.0, 0.0)
        te = jnp.minimum(jnp.sum(done, axis=0, keepdims=True), float(N_EXPERTS - 1))
        used = pends[0:1, N_EXPERTS - 1:N_EXPERTS] * (1.0 / TM_EXP)
        lane1 = lax.broadcasted_iota(jnp.int32, (1, LANES), 1)
        te_ref[...] = jnp.concatenate(
            [te, jnp.where(lane1 == 0, used, 0.0)] + [jnp.zeros((ROW_TILE - 2, LANES), F32)], axis=0
        ).astype(jnp.int32)
        cnt_ref[...] = jnp.zeros(cnt_ref.shape, F32)

    @pl.when((phase == 1) & (j == 0))
    def _():
        r = lax.broadcasted_iota(jnp.int32, (tm, tm), 0)
        c = lax.broadcasted_iota(jnp.int32, (tm, tm), 1)
        before_ref[...] = jnp.where(c < r, 1.0, 0.0).astype(BF16)

    @pl.when(phase == 1)
    def _():
        earlier = _dot(before_ref[...], both.astype(BF16)) + cnt_ref[...]
        slot = earlier + pst_ref[...]
        d0 = jnp.sum(hot0 * slot, axis=-1, keepdims=True)
        d1 = jnp.sum(hot1 * slot, axis=-1, keepdims=True)
        dest_ref[0] = jnp.where(lane == 0, d0, jnp.where(lane == 1, d1, 0.0)).astype(jnp.int32)
        cnt_ref[...] += jnp.sum(both, axis=0, keepdims=True)


def _dispatch_plan(ri, t):
    b, s, _ = ri.shape
    tm = min(TM_CMB, s)
    n_tiles = -(-(2 * t + N_EXPERTS * (TM_EXP - 1)) // TM_EXP)
    assert n_tiles <= LANES and N_EXPERTS <= LANES
    chunks = s // tm
    dest, te = pl.pallas_call(
        _plan_kernel,
        grid=(2, b * chunks),
        in_specs=[pl.BlockSpec((1, tm, LANES), lambda ph, j: (j // chunks, j % chunks, 0))],
        out_specs=[pl.BlockSpec((1, tm, LANES), lambda ph, j: (ph * (j // chunks), ph * (j % chunks), 0)),
                   pl.BlockSpec((ROW_TILE, LANES), lambda ph, j: (0, 0))],
        out_shape=[jax.ShapeDtypeStruct((b, s, LANES), jnp.int32),
                   jax.ShapeDtypeStruct((ROW_TILE, LANES), jnp.int32)],
        scratch_shapes=[pltpu.VMEM((1, LANES), F32), pltpu.VMEM((1, LANES), F32), pltpu.VMEM((tm, tm), BF16)],
        compiler_params=_cp(2),
        name="moe_plan",
    )(ri)
    dest = dest.reshape(t, LANES)[:, :2].reshape(-1)
    return dest, n_tiles * TM_EXP, te[0, :n_tiles], te[1, 0:1]


def kernel(x, c, norm1_g, norm2_g, ada_w, ada_b, w_in, b_forget, cmp_pos_k, cmp_w1_k, cmp_w2_k,
           cmp_pos_v, cmp_w1_v, cmp_w2_v, out_norm_g, w_out, router_group_w, router_group_b,
           router_expert_w, router_expert_b, expert_w1, expert_w3, expert_w2, final_g):
    b, s, d = x.shape
    depth = ada_w.shape[0]
    t = b * s
    mod = _modulation(c, ada_w, ada_b)
    rope_c, rope_1, rope_2 = _rope_tables(jnp.arange(s))
    n_cmp_pad = s // CMP_STRIDE
    crc, cr1, cr2 = _rope_tables(jnp.arange(n_cmp_pad) * CMP_STRIDE + (CMP_LEN - 1))
    cover, expand = _static_tables(s)
    xs = None

    mod4 = mod.reshape(depth, b, 6, 1, d)
    w_main, w_cmp, w_small = jax.vmap(_layout_w_in)(w_in)
    b_pairs = jnp.pad(b_forget.reshape(depth, 2, 1, 2), ((0, 0), (0, 0), (0, 0), (0, LANES - 2)))
    wk, w2k = jax.vmap(_layout_cmp)(cmp_w1_k, cmp_w2_k)
    wv, w2v = jax.vmap(_layout_cmp)(cmp_w1_v, cmp_w2_v)
    pek = jnp.broadcast_to(cmp_pos_k.reshape(depth, 1, -1), (depth, ROW_TILE, CMP_LEN * HEAD_DIM))
    pev = jnp.broadcast_to(cmp_pos_v.reshape(depth, 1, -1), (depth, ROW_TILE, CMP_LEN * HEAD_DIM))
    gn = out_norm_g.reshape(depth, 1, -1)
    wr = jnp.concatenate([router_group_w, router_expert_w,
                          jnp.zeros((depth, d, LANES - N_GROUPS - N_EXPERTS), F32)], axis=2)
    wr_hi = wr.astype(BF16)
    wr_lo = (wr - wr_hi.astype(F32)).astype(BF16)
    br = jnp.concatenate([router_group_b, router_expert_b,
                          jnp.zeros((depth, LANES - N_GROUPS - N_EXPERTS), F32)], axis=1).reshape(depth, 1, LANES)
    w_out_b = w_out.astype(BF16)

    for l in range(depth):
        u, kc, vc, small = _in_projection(x, norm1_g, mod4, w_main, w_cmp, w_small, rope_c, rope_1, rope_2, l)
        cq, ckt = _forget_cumsum(small, b_pairs, l)
        ck, cv = _compress(kc, vc, wk[l], wv[l], cmp_w1_k[l], cmp_w1_v[l], pek[l], pev[l],
                           w2k[l], w2v[l], crc, cr1, cr2)
        o_a = _nsa_attention(u, ck, cv, small, gn, cover, expand, l)
        o_b = _fox_attention(u, cq, ckt, gn, l)
        o_c = _sb_attention(u, gn, l)
        x, h2, rw, ri = _out_projection(o_a, o_b, o_c, x, w_out_b, mod4, norm2_g, wr_hi, wr_lo, br, l)

        dest, p_rows, tile_expert, n_used = _dispatch_plan(ri, t)
        xs = _dispatch(h2, dest, p_rows, xs)
        ys = _expert_mlp(xs, tile_expert, n_used, expert_w1, expert_w3, expert_w2, l)
        x = _combine(x, ys, dest[0::2], dest[1::2], rw, mod4, final_g, final=(l == depth - 1), layer=l)
    return x
```
